```python
import math
import jax
import jax.numpy as jnp
from jax import lax
import numpy as np

D_MODEL = 1024
BATCH = 8
SEQ = 4096
DEPTH = 1

D_SSM = 512
SSM_CH = 16
SSM_GROUPS = D_SSM // SSM_CH
SSM_STATE = 64
N_HEADS = 8
N_KV = 2
HEAD_DIM = 64
GROUP = N_HEADS // N_KV
ATT_W = N_HEADS * HEAD_DIM
WINDOW = 128
BLOCK = 128
NUM_BUCKETS = 32
MAX_DIST = 128
D_IN = D_SSM + ATT_W + 2 * N_KV * HEAD_DIM
N_GROUPS = 4
EXPERTS_PER_GROUP = 8
N_EXPERTS = N_GROUPS * EXPERTS_PER_GROUP
TOP_K = 2
D_FF_EXPERT = 512
MOE_BLK = 256
PLE_DIM = 256
EPS = 1e-6

kernel_name = 'hybrid_s5_swa_sink_hiermoe_block'


def rmsnorm(x, g):
    xf = x.astype(jnp.float32)
    y = xf * lax.rsqrt(jnp.mean(xf * xf, axis=-1, keepdims=True) + EPS)
    return (y * g.astype(jnp.float32)).astype(x.dtype)


def t5_bucket(rel):
    max_exact = NUM_BUCKETS // 2
    relf = jnp.maximum(rel, 1).astype(jnp.float32)
    large = max_exact + (jnp.log(relf / max_exact) / math.log(MAX_DIST / max_exact)
                         * (NUM_BUCKETS - max_exact)).astype(jnp.int32)
    large = jnp.minimum(large, NUM_BUCKETS - 1)
    return jnp.where(rel < max_exact, rel, large)


def s5_mixer(u, a_re, a_im, log_dt, b_re, b_im, c_re, c_im, d_skip, w_glu, b_glu):
    bsz, seq, _ = u.shape
    f32 = jnp.float32
    uf = u.astype(f32).reshape(bsz, seq, SSM_GROUPS, SSM_CH)
    lam = lax.complex(a_re.astype(f32), a_im.astype(f32))
    dt = jnp.exp(log_dt.astype(f32))[:, None]
    a_bar = jnp.exp(lam * dt)
    b_mat = lax.complex(b_re.astype(f32), b_im.astype(f32))
    b_bar = ((a_bar - 1.0) / lam)[..., None] * b_mat
    bu = jnp.einsum('gpc,bsgc->bsgp', b_bar, uf.astype(jnp.complex64))
    a_elems = jnp.broadcast_to(a_bar, (1, seq, SSM_GROUPS, SSM_STATE))

    def combine(e1, e2):
        a1, s1 = e1
        a2, s2 = e2
        return a2 * a1, a2 * s1 + s2

    _, states = lax.associative_scan(combine, (a_elems, bu), axis=1)
    c_mat = lax.complex(c_re.astype(f32), c_im.astype(f32))
    y = jnp.real(jnp.einsum('gcp,bsgp->bsgc', c_mat, states))
    y = y + d_skip.astype(f32).reshape(SSM_GROUPS, SSM_CH) * uf
    y = jax.nn.gelu(y.reshape(bsz, seq, D_SSM))
    y = y * jax.nn.sigmoid(y @ w_glu.astype(f32) + b_glu.astype(f32))
    return y.astype(u.dtype)


def window_attention(q, k, v, rel_bias, sinks):
    bsz, seq = q.shape[0], q.shape[1]
    nb = seq // BLOCK
    qb = q.reshape(bsz, nb, BLOCK, N_KV, GROUP, HEAD_DIM)
    kb = k.reshape(bsz, nb, BLOCK, N_KV, HEAD_DIM)
    vb = v.reshape(bsz, nb, BLOCK, N_KV, HEAD_DIM)
    pad = ((0, 0), (1, 0), (0, 0), (0, 0), (0, 0))
    kwin = jnp.concatenate([jnp.pad(kb, pad)[:, :-1], kb], axis=2)
    vwin = jnp.concatenate([jnp.pad(vb, pad)[:, :-1], vb], axis=2)
    s = jnp.einsum('bnqkgd,bnckd->bnkgqc', qb, kwin).astype(jnp.float32) * (HEAD_DIM ** -0.5)
    q_loc = jnp.arange(BLOCK)[:, None]
    c_loc = jnp.arange(2 * BLOCK)[None, :]
    rel = q_loc + BLOCK - c_loc
    bias = rel_bias.astype(jnp.float32)[t5_bucket(jnp.maximum(rel, 0))]
    bias = bias.transpose(2, 0, 1).reshape(N_KV, GROUP, BLOCK, 2 * BLOCK)
    valid = (rel >= 0) & (rel < WINDOW)
    blk = jnp.arange(nb)[:, None, None]
    mask = valid[None] & ((blk > 0) | (c_loc[None] >= BLOCK))
    s = jnp.where(mask[None, :, None, None], s + bias, -jnp.inf)
    sink = sinks.astype(jnp.float32).reshape(N_KV, GROUP)[None, None, :, :, None, None]
    m = jnp.maximum(jnp.max(s, axis=-1, keepdims=True), sink)
    e = jnp.exp(s - m)
    pr = e / (jnp.sum(e, axis=-1, keepdims=True) + jnp.exp(sink - m))
    o = jnp.einsum('bnkgqc,bnckd->bnqkgd', pr.astype(v.dtype), vwin)
    return o.reshape(bsz, seq, ATT_W)


def hier_moe(h, w_rg, b_rg, w_re, b_re, w_eg, w_eu, w_ed):
    bsz, seq, dm = h.shape
    n_tok = bsz * seq
    ht = h.reshape(n_tok, dm)
    g_prob = jax.nn.softmax((ht @ w_rg + b_rg).astype(jnp.float32), axis=-1)
    g_top_p, g_idx = lax.top_k(g_prob, 1)
    e_logits = (ht @ w_re + b_re).astype(jnp.float32).reshape(n_tok, N_GROUPS, EXPERTS_PER_GROUP)
    idx = jnp.broadcast_to(g_idx[:, :, None], (n_tok, 1, EXPERTS_PER_GROUP))
    e_in = jnp.take_along_axis(e_logits, idx, axis=1)[:, 0]
    e_top_l, e_top_i = lax.top_k(e_in, TOP_K)
    e_w = jax.nn.softmax(e_top_l, axis=-1) * g_top_p
    expert = g_idx * EXPERTS_PER_GROUP + e_top_i
    n_slots = n_tok * TOP_K
    flat_e = expert.reshape(-1)
    flat_w = e_w.reshape(-1)
    flat_tok = jnp.arange(n_slots) // TOP_K
    order = jnp.argsort(flat_e)
    se = flat_e[order]
    tok_sorted = flat_tok[order]
    counts = jnp.bincount(flat_e, length=N_EXPERTS)
    padded = ((counts + MOE_BLK - 1) // MOE_BLK) * MOE_BLK
    pad_end = jnp.cumsum(padded)
    pad_start = pad_end - padded
    start = jnp.cumsum(counts) - counts
    dest = pad_start[se] + (jnp.arange(n_slots) - start[se])
    n_rows = n_slots + N_EXPERTS * MOE_BLK
    n_blk = n_rows // MOE_BLK
    xd = jnp.zeros((n_rows, dm), ht.dtype).at[dest].set(ht[tok_sorted])
    blk_e = jnp.minimum(jnp.searchsorted(pad_end, jnp.arange(n_blk) * MOE_BLK, side='right'),
                        N_EXPERTS - 1)

    def expert_block(args):
        xb, e = args
        return (jax.nn.silu(xb @ w_eg[e]) * (xb @ w_eu[e])) @ w_ed[e]

    yd = lax.map(expert_block, (xd.reshape(n_blk, MOE_BLK, dm), blk_e)).reshape(n_rows, dm)
    y_slots = yd[dest] * flat_w[order][:, None].astype(yd.dtype)
    out = jax.ops.segment_sum(y_slots, tok_sorted, num_segments=n_tok)
    return out.reshape(bsz, seq, dm).astype(h.dtype)


def setup_inputs(seed: int = 0) -> dict:
    key = jax.random.key(seed)
    ks = iter(jax.random.split(key, 48))
    L = DEPTH

    def nrm(shape, scale):
        return jax.random.normal(next(ks), shape, jnp.float32) * scale

    n_idx = jnp.arange(SSM_STATE, dtype=jnp.float32)
    return {
        'x': nrm((BATCH, SEQ, D_MODEL), 1.0),
        'p': nrm((DEPTH, BATCH, SEQ, PLE_DIM), 1.0),
        'rel_bias': nrm((NUM_BUCKETS, N_HEADS), 0.5),
        'g_mix': 1.0 + nrm((L, D_MODEL), 0.02),
        'w_in': nrm((L, D_MODEL, D_IN), D_MODEL ** -0.5),
        'w_gate': nrm((L, D_MODEL, 2 * D_MODEL), D_MODEL ** -0.5),
        'b_gate': nrm((L, 2 * D_MODEL), 0.02),
        'ssm_a_re': -0.5 + nrm((L, SSM_GROUPS, SSM_STATE), 0.01),
        'ssm_a_im': math.pi * n_idx + nrm((L, SSM_GROUPS, SSM_STATE), 0.01),
        'ssm_log_dt': jax.random.uniform(next(ks), (L, SSM_GROUPS), jnp.float32,
                                         minval=math.log(1e-3), maxval=math.log(1e-1)),
        'ssm_b_re': nrm((L, SSM_GROUPS, SSM_STATE, SSM_CH), (2 * SSM_CH) ** -0.5),
        'ssm_b_im': nrm((L, SSM_GROUPS, SSM_STATE, SSM_CH), (2 * SSM_CH) ** -0.5),
        'ssm_c_re': nrm((L, SSM_GROUPS, SSM_CH, SSM_STATE), (2 * SSM_STATE) ** -0.5),
        'ssm_c_im': nrm((L, SSM_GROUPS, SSM_CH, SSM_STATE), (2 * SSM_STATE) ** -0.5),
        'ssm_d': nrm((L, D_SSM), 1.0),
        'w_glu': nrm((L, D_SSM, D_SSM), D_SSM ** -0.5),
        'b_glu': nrm((L, D_SSM), 0.02),
        'sinks': nrm((L, N_HEADS), 0.5),
        'w_br_ssm': nrm((L, D_SSM, D_MODEL), D_SSM ** -0.5),
        'w_br_attn': nrm((L, ATT_W, D_MODEL), ATT_W ** -0.5),
        'w_out': nrm((L, D_MODEL, D_MODEL), D_MODEL ** -0.5),
        'g_ffn': 1.0 + nrm((L, D_MODEL), 0.02),
        'w_router_group': nrm((L, D_MODEL, N_GROUPS), D_MODEL ** -0.5),
        'b_router_group': nrm((L, N_GROUPS), 0.01),
        'w_router_expert': nrm((L, D_MODEL, N_EXPERTS), D_MODEL ** -0.5),
        'b_router_expert': nrm((L, N_EXPERTS), 0.01),
        'w_e_gate': nrm((L, N_EXPERTS, D_MODEL, D_FF_EXPERT), D_MODEL ** -0.5),
        'w_e_up': nrm((L, N_EXPERTS, D_MODEL, D_FF_EXPERT), D_MODEL ** -0.5),
        'w_e_down': nrm((L, N_EXPERTS, D_FF_EXPERT, D_MODEL), D_FF_EXPERT ** -0.5),
        'g_ple': 1.0 + nrm((L, D_MODEL), 0.02),
        'w_ple_gate': nrm((L, D_MODEL, D_MODEL), D_MODEL ** -0.5),
        'b_ple_gate': nrm((L, D_MODEL), 0.02),
        'w_ple_proj': nrm((L, PLE_DIM, D_MODEL), PLE_DIM ** -0.5),
        'g_final': 1.0 + nrm((D_MODEL,), 0.02),
    }


def reference(x, p, rel_bias, g_mix, w_in, w_gate, b_gate, ssm_a_re, ssm_a_im, ssm_log_dt,
              ssm_b_re, ssm_b_im, ssm_c_re, ssm_c_im, ssm_d, w_glu, b_glu, sinks,
              w_br_ssm, w_br_attn, w_out, g_ffn, w_router_group, b_router_group,
              w_router_expert, b_router_expert, w_e_gate, w_e_up, w_e_down,
              g_ple, w_ple_gate, b_ple_gate, w_ple_proj, g_final):
    bsz, seq, _ = x.shape
    for i in range(DEPTH):
        h = rmsnorm(x, g_mix[i])
        proj = h @ w_in[i]
        u = proj[..., :D_SSM]
        q = proj[..., D_SSM:D_SSM + ATT_W].reshape(bsz, seq, N_HEADS, HEAD_DIM)
        k = proj[..., D_SSM + ATT_W:D_SSM + ATT_W + N_KV * HEAD_DIM].reshape(bsz, seq, N_KV, HEAD_DIM)
        v = proj[..., D_SSM + ATT_W + N_KV * HEAD_DIM:].reshape(bsz, seq, N_KV, HEAD_DIM)
        y_ssm = s5_mixer(u, ssm_a_re[i], ssm_a_im[i], ssm_log_dt[i], ssm_b_re[i], ssm_b_im[i],
                         ssm_c_re[i], ssm_c_im[i], ssm_d[i], w_glu[i], b_glu[i])
        y_att = window_attention(q, k, v, rel_bias, sinks[i])
        gates = jax.nn.sigmoid(h @ w_gate[i] + b_gate[i]).reshape(bsz, seq, 2, D_MODEL)
        merged = gates[:, :, 0] * (y_ssm @ w_br_ssm[i]) + gates[:, :, 1] * (y_att @ w_br_attn[i])
        x = x + merged @ w_out[i]
        h2 = rmsnorm(x, g_ffn[i])
        x = x + hier_moe(h2, w_router_group[i], b_router_group[i], w_router_expert[i],
                         b_router_expert[i], w_e_gate[i], w_e_up[i], w_e_down[i])
        h3 = rmsnorm(x, g_ple[i])
        x = x + (p[i] @ w_ple_proj[i]) * jax.nn.sigmoid(h3 @ w_ple_gate[i] + b_ple_gate[i])
    return rmsnorm(x, g_final)
```

```python
import functools
import math

import numpy as np
import jax
import jax.numpy as jnp
from jax import lax
from jax.experimental import pallas as pl
from jax.experimental.pallas import tpu as pltpu

D_MODEL = 1024
D_SSM = 512
SSM_CH = 16
SSM_GROUPS = D_SSM // SSM_CH
SSM_STATE = 64
N_STATE = SSM_GROUPS * SSM_STATE
N_HEADS = 8
N_KV = 2
HEAD_DIM = 64
GROUP = N_HEADS // N_KV
ATT_W = N_HEADS * HEAD_DIM
KV_W = 2 * N_KV * HEAD_DIM
WINDOW = 128
BLOCK = 128
NUM_BUCKETS = 32
MAX_DIST = 128
D_IN = D_SSM + ATT_W + KV_W
N_GROUPS = 4
EXPERTS_PER_GROUP = 8
N_EXPERTS = N_GROUPS * EXPERTS_PER_GROUP
TOP_K = 2
D_FF_EXPERT = 512
MOE_BLK = 256
PLE_DIM = 256
EPS = 1e-6

LANES = 128
NEG = -1e30
PACK_W = D_MODEL // 2

BF16 = jnp.bfloat16
F32 = jnp.float32
VMEM_LIMIT = 56 * 1024 * 1024


def _cparams(sem):
    return pltpu.CompilerParams(dimension_semantics=sem, vmem_limit_bytes=VMEM_LIMIT)


def _rms(x, g):
    ms = jnp.mean(x * x, axis=-1, keepdims=True)
    return x * lax.rsqrt(ms + EPS) * g


def _sigmoid(x):
    return 1.0 / (1.0 + jnp.exp(-x))


def _dot(a, b):
    return jnp.dot(a, b, preferred_element_type=F32)


def _pack_rows(y):
    lo = pltpu.bitcast(y[:, :PACK_W].astype(BF16).astype(F32), jnp.uint32)
    hi = pltpu.bitcast(y[:, PACK_W:].astype(BF16).astype(F32), jnp.uint32)
    return hi | lax.shift_right_logical(lo, jnp.uint32(16))


def _unpack_rows(w):
    lo = pltpu.bitcast(lax.shift_left(w, jnp.uint32(16)), F32)
    hi = pltpu.bitcast(w & jnp.uint32(0xFFFF0000), F32)
    return jnp.concatenate([lo, hi], axis=1)


def _proj_kernel(x_ref, g_ref, w_ref, u_ref, q_ref, kv_ref):
    h = _rms(x_ref[...], g_ref[...]).astype(BF16)
    proj = _dot(h, w_ref[...])
    u_ref[...] = proj[:, :D_SSM].astype(BF16)
    q_ref[...] = (proj[:, D_SSM:D_SSM + ATT_W] * (HEAD_DIM ** -0.5)).astype(BF16)
    kv_ref[...] = proj[:, D_SSM + ATT_W:].astype(BF16)


def _proj(x2d, g_mix, w_in_b, tm=512):
    t = x2d.shape[0]
    row = lambda w: pl.BlockSpec((tm, w), lambda i: (i, 0))
    full = lambda a: pl.BlockSpec(a.shape, lambda i: (0,) * a.ndim)
    return pl.pallas_call(
        _proj_kernel,
        grid=(t // tm,),
        in_specs=[row(D_MODEL), full(g_mix), full(w_in_b)],
        out_specs=[row(D_SSM), row(ATT_W), row(KV_W)],
        out_shape=[jax.ShapeDtypeStruct((t, D_SSM), BF16),
                   jax.ShapeDtypeStruct((t, ATT_W), BF16),
                   jax.ShapeDtypeStruct((t, KV_W), BF16)],
        compiler_params=_cparams(("parallel",)),
        name="proj",
    )(x2d, g_mix, w_in_b)


def _ssmprep_kernel(are_ref, aim_ref, ldt_ref, bre_ref, bim_ref, abr_ref, abi_ref, bbr_ref, bbi_ref):
    a_re, a_im = are_ref[...], aim_ref[...]
    dt = jnp.exp(ldt_ref[...])
    mag = jnp.exp(a_re * dt)
    ab_re = mag * jnp.cos(a_im * dt)
    ab_im = mag * jnp.sin(a_im * dt)
    abr_ref[...] = ab_re
    abi_ref[...] = ab_im
    den = a_re * a_re + a_im * a_im
    c_re = ((ab_re - 1.0) * a_re + ab_im * a_im) / den
    c_im = (ab_im * a_re - (ab_re - 1.0) * a_im) / den
    for c in range(SSM_CH):
        b_re, b_im = bre_ref[c], bim_ref[c]
        bbr_ref[c] = c_re * b_re - c_im * b_im
        bbi_ref[c] = c_re * b_im + c_im * b_re


def _ssmprep(a_re, a_im, log_dt, b_re, b_im):
    g, p = a_re.shape
    bt_re = jnp.transpose(b_re, (2, 0, 1))
    bt_im = jnp.transpose(b_im, (2, 0, 1))
    gp = jax.ShapeDtypeStruct((g, p), F32)
    cgp = jax.ShapeDtypeStruct((SSM_CH, g, p), F32)
    return pl.pallas_call(
        _ssmprep_kernel, out_shape=[gp, gp, cgp, cgp], name="ssmprep",
    )(a_re, a_im, log_dt.reshape(g, 1), bt_re, bt_im)


def _ssm_weights(ab_re, ab_im, bb_re, bb_im, c_re, c_im):
    g, p = ab_re.shape
    kb = D_SSM // 256
    gk = g // kb
    eye = jnp.eye(gk, dtype=F32)

    def in_block(bb):
        b4 = jnp.transpose(bb, (1, 0, 2)).reshape(kb, gk, SSM_CH, p)
        return jnp.einsum("kgcp,gh->kgchp", b4, eye).reshape(kb, gk * SSM_CH, gk * p)

    w_in = jnp.concatenate([in_block(bb_re), in_block(bb_im)], axis=2).astype(BF16)
    nj = D_SSM // LANES
    gj = g // nj
    eyej = jnp.eye(gj, dtype=F32)

    def out_block(c):
        c4 = jnp.transpose(c, (0, 2, 1)).reshape(nj, gj, p, SSM_CH)
        return jnp.einsum("jgpc,gh->jgphc", c4, eyej).reshape(nj, gj * p, gj * SSM_CH)

    w_out = jnp.stack([out_block(c_re), out_block(-c_im)], axis=1).astype(BF16)
    return w_in, w_out, ab_re.reshape(1, g * p), ab_im.reshape(1, g * p)


def _gelu_tanh(x):
    return 0.5 * x * (1.0 + jnp.tanh(math.sqrt(2.0 / math.pi) * (x + 0.044715 * (x * x * x))))


def _ssm_kernel(u_ref, win_ref, wout_ref, ar_ref, ai_ref, d_ref, wglu_ref, bglu_ref, y_ref,
                perm_ref, permt_ref, bu_ref, xb_ref, st_ref, *, nb, lc):
    rows = nb * lc
    ns = N_STATE
    i = pl.program_id(0)

    @pl.when(i == 0)
    def _():
        r = lax.broadcasted_iota(jnp.int32, (rows, rows), 0)
        c = lax.broadcasted_iota(jnp.int32, (rows, rows), 1)
        sh = nb.bit_length() - 1
        perm_ref[...] = jnp.where(c == (r & (nb - 1)) * lc + (r >> sh), 1.0, 0.0).astype(BF16)
        permt_ref[...] = jnp.where(r == (c & (nb - 1)) * lc + (c >> sh), 1.0, 0.0).astype(BF16)
        st_ref[...] = jnp.zeros_like(st_ref)

    u = u_ref[...].reshape(rows, D_SSM)
    up = _dot(perm_ref[...], u).astype(BF16)
    kb = win_ref.shape[0]
    half = ns // kb
    for k in range(kb):
        res = _dot(up[:, 256 * k:256 * (k + 1)], win_ref[k])
        bu_ref[:, half * k:half * (k + 1)] = res[:, :half]
        bu_ref[:, ns + half * k:ns + half * (k + 1)] = res[:, half:]

    ar, ai = ar_ref[...], ai_ref[...]

    def body(j, carry):
        xr, xi = carry
        keep_r, keep_i = [], []
        for s in range(2):
            row = pl.multiple_of((2 * j + s) * nb, nb)
            br = bu_ref[pl.ds(row, nb), 0:ns]
            bi = bu_ref[pl.ds(row, nb), ns:2 * ns]
            xr, xi = ar * xr - ai * xi + br, ar * xi + ai * xr + bi
            keep_r.append(xr)
            keep_i.append(xi)
        row2 = pl.multiple_of(j * 2 * nb, 2 * nb)
        xb_ref[pl.ds(row2, 2 * nb), 0:ns] = jnp.concatenate(keep_r, axis=0).astype(BF16)
        xb_ref[pl.ds(row2, 2 * nb), ns:2 * ns] = jnp.concatenate(keep_i, axis=0).astype(BF16)
        return xr, xi

    xr, xi = lax.fori_loop(0, lc // 2, body, (st_ref[:, 0:ns], st_ref[:, ns:2 * ns]))
    st_ref[:, 0:ns] = xr
    st_ref[:, ns:2 * ns] = xi

    nj = wout_ref.shape[0]
    kw = ns // nj
    ys = []
    for j in range(nj):
        ys.append(_dot(xb_ref[:, kw * j:kw * (j + 1)], wout_ref[j, 0])
                  + _dot(xb_ref[:, ns + kw * j:ns + kw * (j + 1)], wout_ref[j, 1]))
    y_tb = jnp.concatenate(ys, axis=1).astype(BF16)
    y = _dot(permt_ref[...], y_tb) + d_ref[...] * u.astype(F32)
    y = _gelu_tanh(y)
    z = _dot(y.astype(BF16), wglu_ref[...]) + bglu_ref[...]
    y_ref[...] = (y * _sigmoid(z)).astype(BF16).reshape(nb, lc, D_SSM)


def _ssm(u3, w_in, w_out, ar, ai, d_skip, w_glu_b, b_glu, lc=64):
    nb, s, _ = u3.shape
    assert nb & (nb - 1) == 0 and s % lc == 0 and lc % 16 == 0
    rows = nb * lc
    full = lambda a: pl.BlockSpec(a.shape, lambda i: (0,) * a.ndim)
    blk = pl.BlockSpec((nb, lc, D_SSM), lambda i: (0, i, 0))
    return pl.pallas_call(
        functools.partial(_ssm_kernel, nb=nb, lc=lc),
        grid=(s // lc,),
        in_specs=[blk, full(w_in), full(w_out), full(ar), full(ai), full(d_skip), full(w_glu_b), full(b_glu)],
        out_specs=blk,
        out_shape=jax.ShapeDtypeStruct((nb, s, D_SSM), BF16),
        scratch_shapes=[pltpu.VMEM((rows, rows), BF16), pltpu.VMEM((rows, rows), BF16),
                        pltpu.VMEM((rows, 2 * N_STATE), F32), pltpu.VMEM((rows, 2 * N_STATE), BF16),
                        pltpu.VMEM((nb, 2 * N_STATE), F32)],
        compiler_params=_cparams(("arbitrary",)),
        name="ssm",
    )(u3, w_in, w_out, ar, ai, d_skip, w_glu_b, b_glu)


def _t5_bucket_np(rel):
    max_exact = NUM_BUCKETS // 2
    relf = np.maximum(rel, 1).astype(np.float32)
    large = max_exact + (np.log(relf / np.float32(max_exact)) / np.float32(math.log(MAX_DIST / max_exact))
                         * np.float32(NUM_BUCKETS - max_exact)).astype(np.int32)
    large = np.minimum(large, NUM_BUCKETS - 1)
    return np.where(rel < max_exact, rel, large)


def _bias_table(rel_bias):
    q_loc = np.arange(BLOCK)[:, None]
    c_loc = np.arange(2 * BLOCK)[None, :]
    rel = q_loc + BLOCK - c_loc
    valid = (rel >= 0) & (rel < WINDOW)
    bucket = _t5_bucket_np(np.maximum(rel, 0))
    bias = rel_bias.astype(F32)[bucket]
    bias = jnp.where(valid[:, :, None], bias, NEG)
    return jnp.transpose(bias, (2, 0, 1))


def _attn_kernel(sink_ref, q_ref, kvp_ref, kvc_ref, bias_ref, o_ref, *, nq):
    n = pl.program_id(1)
    kv_all = jnp.concatenate([kvp_ref[0], kvc_ref[0]], axis=0)
    col = lax.broadcasted_iota(jnp.int32, (BLOCK, 2 * BLOCK), 1)
    no_prev = jnp.where(col < BLOCK, jnp.where(n == 0, NEG, 0.0), 0.0)
    for j in range(nq):
        q = q_ref[0, BLOCK * j:BLOCK * (j + 1), :]
        kv = kv_all[BLOCK * j:BLOCK * (j + 2), :]
        outs = []
        for h in range(N_HEADS):
            g = h // GROUP
            qh = q[:, HEAD_DIM * h:HEAD_DIM * (h + 1)]
            kh = kv[:, HEAD_DIM * g:HEAD_DIM * (g + 1)]
            vh = kv[:, N_KV * HEAD_DIM + HEAD_DIM * g:N_KV * HEAD_DIM + HEAD_DIM * (g + 1)]
            s = lax.dot_general(qh, kh, (((1,), (1,)), ((), ())), preferred_element_type=F32)
            s = s + bias_ref[h]
            if j == 0:
                s = s + no_prev
            sink = sink_ref[h]
            m = jnp.maximum(jnp.max(s, axis=-1, keepdims=True), sink)
            e = jnp.exp(s - m)
            den = jnp.sum(e, axis=-1, keepdims=True) + jnp.exp(sink - m)
            outs.append(_dot(e.astype(BF16), vh) / den)
        o_ref[0, BLOCK * j:BLOCK * (j + 1), :] = jnp.concatenate(outs, axis=1).astype(BF16)


def _attn(q3, kv3, bias, sinks, nq=4):
    b, s, _ = q3.shape
    qb = nq * BLOCK
    grid_spec = pltpu.PrefetchScalarGridSpec(
        num_scalar_prefetch=0,
        grid=(b, s // qb),
        in_specs=[pl.BlockSpec(memory_space=pltpu.SMEM),
                  pl.BlockSpec((1, qb, ATT_W), lambda i, n: (i, n, 0)),
                  pl.BlockSpec((1, BLOCK, KV_W), lambda i, n: (i, jnp.maximum(n * nq - 1, 0), 0)),
                  pl.BlockSpec((1, qb, KV_W), lambda i, n: (i, n, 0)),
                  pl.BlockSpec(bias.shape, lambda i, n: (0, 0, 0))],
        out_specs=pl.BlockSpec((1, qb, ATT_W), lambda i, n: (i, n, 0)),
    )
    return pl.pallas_call(
        functools.partial(_attn_kernel, nq=nq),
        grid_spec=grid_spec,
        out_shape=jax.ShapeDtypeStruct((b, s, ATT_W), BF16),
        compiler_params=_cparams(("parallel", "parallel")),
        name="attn",
    )(sinks, q3, kv3, kv3, bias)


def _route(logits):
    lane = lax.broadcasted_iota(jnp.int32, logits.shape, 1).astype(F32)
    big = float(LANES)
    gl = jnp.where(lane < N_GROUPS, logits, NEG)
    mg = jnp.max(gl, axis=-1, keepdims=True)
    gidx = jnp.min(jnp.where(gl == mg, lane, big), axis=-1, keepdims=True)
    gsum = jnp.sum(jnp.where(lane < N_GROUPS, jnp.exp(gl - mg), 0.0), axis=-1, keepdims=True)
    gp = 1.0 / gsum
    lo = N_GROUPS + EXPERTS_PER_GROUP * gidx
    el = jnp.where(lane >= lo, jnp.where(lane < lo + EXPERTS_PER_GROUP, logits, NEG), NEG)
    m1 = jnp.max(el, axis=-1, keepdims=True)
    i1 = jnp.min(jnp.where(el == m1, lane, big), axis=-1, keepdims=True)
    el2 = jnp.where(lane == i1, NEG, el)
    m2 = jnp.max(el2, axis=-1, keepdims=True)
    i2 = jnp.min(jnp.where(el2 == m2, lane, big), axis=-1, keepdims=True)
    t = jnp.exp(m2 - m1)
    w1 = gp / (1.0 + t)
    w2 = gp * t / (1.0 + t)
    return jnp.where(lane == 0, i1 - N_GROUPS,
                     jnp.where(lane == 1, i2 - N_GROUPS,
                               jnp.where(lane == 2, w1, jnp.where(lane == 3, w2, 0.0))))


def _merge_kernel(x_ref, ya_ref, yb_ref, gmix_ref, wg_ref, bg_ref, wa_ref, wb_ref, wo_ref, gffn_ref,
                  wr_ref, br_ref, x1_ref, h2_ref, route_ref):
    x = x_ref[...]
    h = _rms(x, gmix_ref[...]).astype(BF16)
    gates = _sigmoid(_dot(h, wg_ref[...]) + bg_ref[...])
    merged = (gates[:, :D_MODEL] * _dot(ya_ref[...], wa_ref[...])
              + gates[:, D_MODEL:] * _dot(yb_ref[...], wb_ref[...]))
    x1 = x + _dot(merged.astype(BF16), wo_ref[...])
    x1_ref[...] = x1
    h2 = _rms(x1, gffn_ref[...])
    h2_ref[...] = _pack_rows(h2)
    logits = jnp.dot(h2, wr_ref[...], preferred_element_type=F32,
                     precision=lax.Precision.HIGHEST) + br_ref[...]
    route_ref[...] = _route(logits)


def _merge(x2d, ya, yb, g_mix, wg, bg, wa, wb, wo, g_ffn, wr, br, tm=256):
    t = x2d.shape[0]
    row = lambda w: pl.BlockSpec((tm, w), lambda i: (i, 0))
    full = lambda a: pl.BlockSpec(a.shape, lambda i: (0,) * a.ndim)
    return pl.pallas_call(
        _merge_kernel,
        grid=(t // tm,),
        in_specs=[row(D_MODEL), row(D_SSM), row(ATT_W), full(g_mix), full(wg), full(bg), full(wa), full(wb),
                  full(wo), full(g_ffn), full(wr), full(br)],
        out_specs=[row(D_MODEL), row(PACK_W), row(LANES)],
        out_shape=[jax.ShapeDtypeStruct((t, D_MODEL), F32),
                   jax.ShapeDtypeStruct((t, PACK_W), jnp.uint32),
                   jax.ShapeDtypeStruct((t, LANES), F32)],
        compiler_params=_cparams(("parallel",)),
        name="merge",
    )(x2d, ya, yb, g_mix, wg, bg, wa, wb, wo, g_ffn, wr, br)


def _rank_kernel(route_ref, rank_ref, cnt_ref, tri_ref, acc_ref):
    i = pl.program_id(0)
    m = route_ref.shape[0]

    @pl.when(i == 0)
    def _():
        r = lax.broadcasted_iota(jnp.int32, (m, m), 0)
        c = lax.broadcasted_iota(jnp.int32, (m, m), 1)
        tri_ref[...] = jnp.where(r > c, 1.0, 0.0).astype(BF16)
        acc_ref[...] = jnp.zeros_like(acc_ref)

    rt = route_ref[...]
    lane = lax.broadcasted_iota(jnp.int32, rt.shape, 1).astype(F32)
    e1, e2 = rt[:, 0:1], rt[:, 1:2]
    is1, is2 = lane == e1, lane == e2
    member = jnp.where(is1, 1.0, jnp.where(is2, 1.0, 0.0))
    before = _dot(tri_ref[...], member.astype(BF16)) + acc_ref[0:1, :]
    r1 = jnp.sum(jnp.where(is1, before, 0.0), axis=-1, keepdims=True)
    r2 = jnp.sum(jnp.where(is2, before, 0.0), axis=-1, keepdims=True)
    rank_ref[...] = jnp.where(lane == 0, r1, jnp.where(lane == 1, r2, 0.0))
    acc = acc_ref[...] + jnp.sum(member, axis=0, keepdims=True)
    acc_ref[...] = acc
    cnt_ref[...] = acc


def _rank(route, tr=512):
    t = route.shape[0]
    return pl.pallas_call(
        _rank_kernel,
        grid=(t // tr,),
        in_specs=[pl.BlockSpec((tr, LANES), lambda i: (i, 0))],
        out_specs=[pl.BlockSpec((tr, LANES), lambda i: (i, 0)), pl.BlockSpec((8, LANES), lambda i: (0, 0))],
        out_shape=[jax.ShapeDtypeStruct((t, LANES), F32), jax.ShapeDtypeStruct((8, LANES), F32)],
        scratch_shapes=[pltpu.VMEM((tr, tr), BF16), pltpu.VMEM((8, LANES), F32)],
        compiler_params=_cparams(("arbitrary",)),
        name="rank",
    )(route)


def _row_copy(src_hbm, dst_vmem, sem, src_row, dst_row):
    return pltpu.make_async_copy(src_hbm.at[pl.ds(src_row, 1)], dst_vmem.at[pl.ds(dst_row, 1)], sem)


def _moe_kernel(be_ref, bn_ref, tok_ref, h2_hbm, wg_ref, wu_ref, wd_ref, yd_ref,
                xw_ref, sem, wgb_ref, wub_ref, wdb_ref):
    i = pl.program_id(0)

    @pl.when(bn_ref[i] == 0)
    def _():
        yd_ref[...] = jnp.zeros_like(yd_ref)

    @pl.when(bn_ref[i] > 0)
    def _():
        def issue(r, c):
            _row_copy(h2_hbm, xw_ref, sem, tok_ref[0, 0, r], r).start()
            return c

        lax.fori_loop(0, MOE_BLK, issue, 0, unroll=8)

        prev = be_ref[jnp.maximum(i - 1, 0)]

        @pl.when(jnp.logical_or(i == 0, be_ref[i] != prev))
        def _():
            wgb_ref[...] = wg_ref[0].astype(BF16)
            wub_ref[...] = wu_ref[0].astype(BF16)
            wdb_ref[...] = wd_ref[0].astype(BF16)

        def drain(r, c):
            _row_copy(h2_hbm, xw_ref, sem, 0, r).wait()
            return c

        lax.fori_loop(0, MOE_BLK, drain, 0, unroll=8)

        x = _unpack_rows(xw_ref[...]).astype(BF16)
        g = _dot(x, wgb_ref[...])
        u = _dot(x, wub_ref[...])
        a = (g * _sigmoid(g) * u).astype(BF16)
        yd_ref[...] = _pack_rows(_dot(a, wdb_ref[...]))


def _moe(blk_e, blk_n, tok3, h2w, w_eg, w_eu, w_ed):
    n_blk = blk_e.shape[0]
    grid_spec = pltpu.PrefetchScalarGridSpec(
        num_scalar_prefetch=2,
        grid=(n_blk,),
        in_specs=[pl.BlockSpec((1, 1, MOE_BLK), lambda i, be, bn: (i, 0, 0), memory_space=pltpu.SMEM),
                  pl.BlockSpec(memory_space=pl.ANY),
                  pl.BlockSpec((1, D_MODEL, D_FF_EXPERT), lambda i, be, bn: (be[i], 0, 0)),
                  pl.BlockSpec((1, D_MODEL, D_FF_EXPERT), lambda i, be, bn: (be[i], 0, 0)),
                  pl.BlockSpec((1, D_FF_EXPERT, D_MODEL), lambda i, be, bn: (be[i], 0, 0))],
        out_specs=pl.BlockSpec((MOE_BLK, PACK_W), lambda i, be, bn: (i, 0)),
        scratch_shapes=[pltpu.VMEM((MOE_BLK, PACK_W), jnp.uint32), pltpu.SemaphoreType.DMA,
                        pltpu.VMEM((D_MODEL, D_FF_EXPERT), BF16), pltpu.VMEM((D_MODEL, D_FF_EXPERT), BF16),
                        pltpu.VMEM((D_FF_EXPERT, D_MODEL), BF16)],
    )
    return pl.pallas_call(
        _moe_kernel,
        grid_spec=grid_spec,
        out_shape=jax.ShapeDtypeStruct((n_blk * MOE_BLK, PACK_W), jnp.uint32),
        compiler_params=_cparams(("arbitrary",)),
        name="moe",
    )(blk_e, blk_n, tok3, h2w, w_eg, w_eu, w_ed)


def _final_kernel(dest_ref, x1_ref, p_ref, route_ref, yd_hbm, wpp_ref, wpg_ref, bpg_ref, gple_ref, gfin_ref,
                  o_ref, yw_ref, sem, *, tm):
    def issue(r, c):
        _row_copy(yd_hbm, yw_ref, sem, dest_ref[0, 0, r], r).start()
        return c

    lax.fori_loop(0, 2 * tm, issue, 0, unroll=8)
    pp = _dot(p_ref[...].astype(BF16), wpp_ref[...])

    def drain(r, c):
        _row_copy(yd_hbm, yw_ref, sem, 0, r).wait()
        return c

    lax.fori_loop(0, 2 * tm, drain, 0, unroll=8)

    rt = route_ref[...]
    y0 = _unpack_rows(yw_ref[0:tm, :])
    y1 = _unpack_rows(yw_ref[tm:2 * tm, :])
    x2 = x1_ref[...] + rt[:, 2:3] * y0 + rt[:, 3:4] * y1
    h3 = _rms(x2, gple_ref[...]).astype(BF16)
    x3 = x2 + pp * _sigmoid(_dot(h3, wpg_ref[...]) + bpg_ref[...])
    o_ref[...] = _rms(x3, gfin_ref[...])


def _final(dest3, x1, p2d, route, ydw, wpp, wpg, bpg, g_ple, g_final, tm=256):
    t = x1.shape[0]
    row = lambda w: pl.BlockSpec((tm, w), lambda i: (i, 0))
    full = lambda a: pl.BlockSpec(a.shape, lambda i: (0,) * a.ndim)
    return pl.pallas_call(
        functools.partial(_final_kernel, tm=tm),
        grid=(t // tm,),
        in_specs=[pl.BlockSpec((1, 1, 2 * tm), lambda i: (i, 0, 0), memory_space=pltpu.SMEM),
                  row(D_MODEL), row(PLE_DIM), row(LANES), pl.BlockSpec(memory_space=pl.ANY),
                  full(wpp), full(wpg), full(bpg), full(g_ple), full(g_final)],
        out_specs=row(D_MODEL),
        out_shape=jax.ShapeDtypeStruct((t, D_MODEL), F32),
        scratch_shapes=[pltpu.VMEM((2 * tm, PACK_W), jnp.uint32), pltpu.SemaphoreType.DMA],
        compiler_params=_cparams(("arbitrary",)),
        name="final",
    )(dest3, x1, p2d, route, ydw, wpp, wpg, bpg, g_ple, g_final)


def _dispatch_plan(route, ranks, counts_f, tm):
    t = route.shape[0]
    n_slots = t * TOP_K
    expert = route[:, :TOP_K].astype(jnp.int32)
    rank = ranks[:, :TOP_K].astype(jnp.int32)
    counts = counts_f[0, :N_EXPERTS].astype(jnp.int32)
    padded = ((counts + MOE_BLK - 1) // MOE_BLK) * MOE_BLK
    pad_end = jnp.cumsum(padded)
    pad_start = pad_end - padded
    dest = pad_start[expert] + rank
    n_rows = n_slots + N_EXPERTS * MOE_BLK
    n_blk = n_rows // MOE_BLK
    blk_start = jnp.arange(n_blk, dtype=jnp.int32) * MOE_BLK
    blk_e = jnp.minimum(jnp.searchsorted(pad_end, blk_start, side="right"), N_EXPERTS - 1).astype(jnp.int32)
    blk_n = jnp.clip(pad_start[blk_e] + counts[blk_e] - blk_start, 0, MOE_BLK).astype(jnp.int32)
    tok = jnp.zeros((n_rows,), jnp.int32).at[dest.reshape(-1)].set(
        jnp.arange(n_slots, dtype=jnp.int32) // TOP_K, unique_indices=True)
    dest3 = jnp.transpose(dest.reshape(t // tm, tm, TOP_K), (0, 2, 1)).reshape(t // tm, 1, TOP_K * tm)
    return blk_e, blk_n, tok.reshape(n_blk, 1, MOE_BLK), dest3


def _row(v):
    return v.reshape(1, -1).astype(F32)


def kernel(x, p, rel_bias, g_mix, w_in, w_gate, b_gate, ssm_a_re, ssm_a_im, ssm_log_dt, ssm_b_re, ssm_b_im,
           ssm_c_re, ssm_c_im, ssm_d, w_glu, b_glu, sinks, w_br_ssm, w_br_attn, w_out, g_ffn, w_router_group,
           b_router_group, w_router_expert, b_router_expert, w_e_gate, w_e_up, w_e_down, g_ple, w_ple_gate,
           b_ple_gate, w_ple_proj, g_final):
    bsz, seq, dm = x.shape
    assert g_mix.shape[0] == 1, "one layer followed by the final norm"
    i = 0
    t = bsz * seq
    x2d = x.reshape(t, dm)
    bias = _bias_table(rel_bias)
    tm_final = 256
    u, q, kv = _proj(x2d, _row(g_mix[i]), w_in[i].astype(BF16))
    ab_re, ab_im, bb_re, bb_im = _ssmprep(ssm_a_re[i], ssm_a_im[i], ssm_log_dt[i], ssm_b_re[i], ssm_b_im[i])
    ws_in, ws_out, ar, ai = _ssm_weights(ab_re, ab_im, bb_re, bb_im, ssm_c_re[i], ssm_c_im[i])
    y_ssm = _ssm(u.reshape(bsz, seq, D_SSM), ws_in, ws_out, ar, ai, _row(ssm_d[i]),
                 w_glu[i].astype(BF16), _row(b_glu[i]))
    y_att = _attn(q.reshape(bsz, seq, ATT_W), kv.reshape(bsz, seq, KV_W), bias, sinks[i].astype(F32))
    w_router = jnp.zeros((dm, LANES), F32)
    w_router = w_router.at[:, :N_GROUPS].set(w_router_group[i])
    w_router = w_router.at[:, N_GROUPS:N_GROUPS + N_EXPERTS].set(w_router_expert[i])
    b_router = jnp.zeros((1, LANES), F32)
    b_router = b_router.at[0, :N_GROUPS].set(b_router_group[i])
    b_router = b_router.at[0, N_GROUPS:N_GROUPS + N_EXPERTS].set(b_router_expert[i])
    x1, h2w, route = _merge(x2d, y_ssm.reshape(t, D_SSM), y_att.reshape(t, ATT_W), _row(g_mix[i]),
                            w_gate[i].astype(BF16), _row(b_gate[i]), w_br_ssm[i].astype(BF16),
                            w_br_attn[i].astype(BF16), w_out[i].astype(BF16), _row(g_ffn[i]),
                            w_router, b_router)
    ranks, counts = _rank(route)
    blk_e, blk_n, tok3, dest3 = _dispatch_plan(route, ranks, counts, tm_final)
    ydw = _moe(blk_e, blk_n, tok3, h2w, w_e_gate[i], w_e_up[i], w_e_down[i])
    out = _final(dest3, x1, p[i].reshape(t, PLE_DIM), route, ydw, w_ple_proj[i].astype(BF16),
                 w_ple_gate[i].astype(BF16), _row(b_ple_gate[i]), _row(g_ple[i]), _row(g_final), tm=tm_final)
    return out.reshape(bsz, seq, dm)
```

```python
import functools
import math

import numpy as np
import jax
import jax.numpy as jnp
from jax import lax
from jax.experimental import pallas as pl
from jax.experimental.pallas import tpu as pltpu

D_MODEL = 1024
D_SSM = 512
SSM_CH = 16
SSM_GROUPS = D_SSM // SSM_CH
SSM_STATE = 64
N_STATE = SSM_GROUPS * SSM_STATE
N_HEADS = 8
N_KV = 2
HEAD_DIM = 64
GROUP = N_HEADS // N_KV
ATT_W = N_HEADS * HEAD_DIM
KV_W = 2 * N_KV * HEAD_DIM
WINDOW = 128
BLOCK = 128
NUM_BUCKETS = 32
MAX_DIST = 128
D_IN = D_SSM + ATT_W + KV_W
N_GROUPS = 4
EXPERTS_PER_GROUP = 8
N_EXPERTS = N_GROUPS * EXPERTS_PER_GROUP
TOP_K = 2
D_FF_EXPERT = 512
MOE_BLK = 256
PLE_DIM = 256
EPS = 1e-6

LANES = 128
NEG = -1e30
PACK_W = D_MODEL // 2

BF16 = jnp.bfloat16
F32 = jnp.float32
VMEM_LIMIT = 56 * 1024 * 1024


def _cparams(sem):
    return pltpu.CompilerParams(dimension_semantics=sem, vmem_limit_bytes=VMEM_LIMIT)


def _rms(x, g):
    ms = jnp.mean(x * x, axis=-1, keepdims=True)
    return x * lax.rsqrt(ms + EPS) * g


def _sigmoid(x):
    return 1.0 / (1.0 + jnp.exp(-x))


def _dot(a, b):
    return jnp.dot(a, b, preferred_element_type=F32)


def _pack_rows(y):
    lo = pltpu.bitcast(y[:, :PACK_W].astype(BF16).astype(F32), jnp.uint32)
    hi = pltpu.bitcast(y[:, PACK_W:].astype(BF16).astype(F32), jnp.uint32)
    return hi | lax.shift_right_logical(lo, jnp.uint32(16))


def _unpack_rows(w):
    lo = pltpu.bitcast(lax.shift_left(w, jnp.uint32(16)), F32)
    hi = pltpu.bitcast(w & jnp.uint32(0xFFFF0000), F32)
    return jnp.concatenate([lo, hi], axis=1)


def _proj_kernel(x_ref, g_ref, w_ref, u_ref, q_ref, kv_ref):
    h = _rms(x_ref[...], g_ref[...]).astype(BF16)
    proj = _dot(h, w_ref[...])
    u_ref[...] = proj[:, :D_SSM].astype(BF16)
    q_ref[...] = (proj[:, D_SSM:D_SSM + ATT_W] * (HEAD_DIM ** -0.5)).astype(BF16)
    kv_ref[...] = proj[:, D_SSM + ATT_W:].astype(BF16)


def _proj(x2d, g_mix, w_in_b, tm=512):
    t = x2d.shape[0]
    row = lambda w: pl.BlockSpec((tm, w), lambda i: (i, 0))
    full = lambda a: pl.BlockSpec(a.shape, lambda i: (0,) * a.ndim)
    return pl.pallas_call(
        _proj_kernel,
        grid=(t // tm,),
        in_specs=[row(D_MODEL), full(g_mix), full(w_in_b)],
        out_specs=[row(D_SSM), row(ATT_W), row(KV_W)],
        out_shape=[jax.ShapeDtypeStruct((t, D_SSM), BF16),
                   jax.ShapeDtypeStruct((t, ATT_W), BF16),
                   jax.ShapeDtypeStruct((t, KV_W), BF16)],
        compiler_params=_cparams(("parallel",)),
        name="proj",
    )(x2d, g_mix, w_in_b)


def _ssmprep_kernel(are_ref, aim_ref, ldt_ref, bre_ref, bim_ref, abr_ref, abi_ref, bbr_ref, bbi_ref):
    a_re, a_im = are_ref[...], aim_ref[...]
    dt = jnp.exp(ldt_ref[...])
    mag = jnp.exp(a_re * dt)
    ab_re = mag * jnp.cos(a_im * dt)
    ab_im = mag * jnp.sin(a_im * dt)
    abr_ref[...] = ab_re
    abi_ref[...] = ab_im
    den = a_re * a_re + a_im * a_im
    c_re = ((ab_re - 1.0) * a_re + ab_im * a_im) / den
    c_im = (ab_im * a_re - (ab_re - 1.0) * a_im) / den
    for c in range(SSM_CH):
        b_re, b_im = bre_ref[c], bim_ref[c]
        bbr_ref[c] = c_re * b_re - c_im * b_im
        bbi_ref[c] = c_re * b_im + c_im * b_re


def _ssmprep(a_re, a_im, log_dt, b_re, b_im):
    g, p = a_re.shape
    bt_re = jnp.transpose(b_re, (2, 0, 1))
    bt_im = jnp.transpose(b_im, (2, 0, 1))
    gp = jax.ShapeDtypeStruct((g, p), F32)
    cgp = jax.ShapeDtypeStruct((SSM_CH, g, p), F32)
    return pl.pallas_call(
        _ssmprep_kernel, out_shape=[gp, gp, cgp, cgp], name="ssmprep",
    )(a_re, a_im, log_dt.reshape(g, 1), bt_re, bt_im)


def _ssm_weights(ab_re, ab_im, bb_re, bb_im, c_re, c_im):
    g, p = ab_re.shape
    kb = D_SSM // 256
    gk = g // kb
    eye = jnp.eye(gk, dtype=F32)

    def in_block(bb):
        b4 = jnp.transpose(bb, (1, 0, 2)).reshape(kb, gk, SSM_CH, p)
        return jnp.einsum("kgcp,gh->kgchp", b4, eye).reshape(kb, gk * SSM_CH, gk * p)

    w_in = jnp.concatenate([in_block(bb_re), in_block(bb_im)], axis=2).astype(BF16)
    nj = D_SSM // LANES
    gj = g // nj
    eyej = jnp.eye(gj, dtype=F32)

    def out_block(c):
        c4 = jnp.transpose(c, (0, 2, 1)).reshape(nj, gj, p, SSM_CH)
        return jnp.einsum("jgpc,gh->jgphc", c4, eyej).reshape(nj, gj * p, gj * SSM_CH)

    w_out = jnp.stack([out_block(c_re), out_block(-c_im)], axis=1).astype(BF16)
    return w_in, w_out, ab_re.reshape(1, g * p), ab_im.reshape(1, g * p)


def _gelu_tanh(x):
    return 0.5 * x * (1.0 + jnp.tanh(math.sqrt(2.0 / math.pi) * (x + 0.044715 * (x * x * x))))


def _ssm_kernel(u_ref, win_ref, wout_ref, ar_ref, ai_ref, d_ref, wglu_ref, bglu_ref, y_ref,
                perm_ref, permt_ref, bu_ref, xb_ref, st_ref, *, nb, lc):
    rows = nb * lc
    ns = N_STATE
    i = pl.program_id(0)

    @pl.when(i == 0)
    def _():
        r = lax.broadcasted_iota(jnp.int32, (rows, rows), 0)
        c = lax.broadcasted_iota(jnp.int32, (rows, rows), 1)
        sh = nb.bit_length() - 1
        perm_ref[...] = jnp.where(c == (r & (nb - 1)) * lc + (r >> sh), 1.0, 0.0).astype(BF16)
        permt_ref[...] = jnp.where(r == (c & (nb - 1)) * lc + (c >> sh), 1.0, 0.0).astype(BF16)
        st_ref[...] = jnp.zeros_like(st_ref)

    u = u_ref[...].reshape(rows, D_SSM)
    up = _dot(perm_ref[...], u).astype(BF16)
    kb = win_ref.shape[0]
    half = ns // kb
    for k in range(kb):
        res = _dot(up[:, 256 * k:256 * (k + 1)], win_ref[k])
        bu_ref[:, half * k:half * (k + 1)] = res[:, :half]
        bu_ref[:, ns + half * k:ns + half * (k + 1)] = res[:, half:]

    ar, ai = ar_ref[...], ai_ref[...]

    def body(j, carry):
        xr, xi = carry
        keep_r, keep_i = [], []
        for s in range(2):
            row = pl.multiple_of((2 * j + s) * nb, nb)
            br = bu_ref[pl.ds(row, nb), 0:ns]
            bi = bu_ref[pl.ds(row, nb), ns:2 * ns]
            xr, xi = ar * xr - ai * xi + br, ar * xi + ai * xr + bi
            keep_r.append(xr)
            keep_i.append(xi)
        row2 = pl.multiple_of(j * 2 * nb, 2 * nb)
        xb_ref[pl.ds(row2, 2 * nb), 0:ns] = jnp.concatenate(keep_r, axis=0).astype(BF16)
        xb_ref[pl.ds(row2, 2 * nb), ns:2 * ns] = jnp.concatenate(keep_i, axis=0).astype(BF16)
        return xr, xi

    xr, xi = lax.fori_loop(0, lc // 2, body, (st_ref[:, 0:ns], st_ref[:, ns:2 * ns]))
    st_ref[:, 0:ns] = xr
    st_ref[:, ns:2 * ns] = xi

    nj = wout_ref.shape[0]
    kw = ns // nj
    ys = []
    for j in range(nj):
        ys.append(_dot(xb_ref[:, kw * j:kw * (j + 1)], wout_ref[j, 0])
                  + _dot(xb_ref[:, ns + kw * j:ns + kw * (j + 1)], wout_ref[j, 1]))
    y_tb = jnp.concatenate(ys, axis=1).astype(BF16)
    y = _dot(permt_ref[...], y_tb) + d_ref[...] * u.astype(F32)
    y = _gelu_tanh(y)
    z = _dot(y.astype(BF16), wglu_ref[...]) + bglu_ref[...]
    y_ref[...] = (y * _sigmoid(z)).astype(BF16).reshape(nb, lc, D_SSM)


def _ssm(u3, w_in, w_out, ar, ai, d_skip, w_glu_b, b_glu, lc=64):
    nb, s, _ = u3.shape
    assert nb & (nb - 1) == 0 and s % lc == 0 and lc % 16 == 0
    rows = nb * lc
    full = lambda a: pl.BlockSpec(a.shape, lambda i: (0,) * a.ndim)
    blk = pl.BlockSpec((nb, lc, D_SSM), lambda i: (0, i, 0))
    return pl.pallas_call(
        functools.partial(_ssm_kernel, nb=nb, lc=lc),
        grid=(s // lc,),
        in_specs=[blk, full(w_in), full(w_out), full(ar), full(ai), full(d_skip), full(w_glu_b), full(b_glu)],
        out_specs=blk,
        out_shape=jax.ShapeDtypeStruct((nb, s, D_SSM), BF16),
        scratch_shapes=[pltpu.VMEM((rows, rows), BF16), pltpu.VMEM((rows, rows), BF16),
                        pltpu.VMEM((rows, 2 * N_STATE), F32), pltpu.VMEM((rows, 2 * N_STATE), BF16),
                        pltpu.VMEM((nb, 2 * N_STATE), F32)],
        compiler_params=_cparams(("arbitrary",)),
        name="ssm",
    )(u3, w_in, w_out, ar, ai, d_skip, w_glu_b, b_glu)


def _t5_bucket_np(rel):
    max_exact = NUM_BUCKETS // 2
    relf = np.maximum(rel, 1).astype(np.float32)
    large = max_exact + (np.log(relf / np.float32(max_exact)) / np.float32(math.log(MAX_DIST / max_exact))
                         * np.float32(NUM_BUCKETS - max_exact)).astype(np.int32)
    large = np.minimum(large, NUM_BUCKETS - 1)
    return np.where(rel < max_exact, rel, large)


def _bias_table(rel_bias):
    q_loc = np.arange(BLOCK)[:, None]
    c_loc = np.arange(2 * BLOCK)[None, :]
    rel = q_loc + BLOCK - c_loc
    valid = (rel >= 0) & (rel < WINDOW)
    bucket = _t5_bucket_np(np.maximum(rel, 0))
    bias = rel_bias.astype(F32)[bucket]
    bias = jnp.where(valid[:, :, None], bias, NEG)
    return jnp.transpose(bias, (2, 0, 1))


def _attn_kernel(sink_ref, q_ref, kvp_ref, kvc_ref, bias_ref, o_ref, *, nq):
    n = pl.program_id(1)
    kv_all = jnp.concatenate([kvp_ref[0], kvc_ref[0]], axis=0)
    col = lax.broadcasted_iota(jnp.int32, (BLOCK, 2 * BLOCK), 1)
    no_prev = jnp.where(col < BLOCK, jnp.where(n == 0, NEG, 0.0), 0.0)
    for j in range(nq):
        q = q_ref[0, BLOCK * j:BLOCK * (j + 1), :]
        kv = kv_all[BLOCK * j:BLOCK * (j + 2), :]
        outs = []
        for h in range(N_HEADS):
            g = h // GROUP
            qh = q[:, HEAD_DIM * h:HEAD_DIM * (h + 1)]
            kh = kv[:, HEAD_DIM * g:HEAD_DIM * (g + 1)]
            vh = kv[:, N_KV * HEAD_DIM + HEAD_DIM * g:N_KV * HEAD_DIM + HEAD_DIM * (g + 1)]
            s = lax.dot_general(qh, kh, (((1,), (1,)), ((), ())), preferred_element_type=F32)
            s = s + bias_ref[h]
            if j == 0:
                s = s + no_prev
            sink = sink_ref[h]
            m = jnp.maximum(jnp.max(s, axis=-1, keepdims=True), sink)
            e = jnp.exp(s - m)
            den = jnp.sum(e, axis=-1, keepdims=True) + jnp.exp(sink - m)
            outs.append(_dot(e.astype(BF16), vh) / den)
        o_ref[0, BLOCK * j:BLOCK * (j + 1), :] = jnp.concatenate(outs, axis=1).astype(BF16)


def _attn(q3, kv3, bias, sinks, nq=4):
    b, s, _ = q3.shape
    qb = nq * BLOCK
    grid_spec = pltpu.PrefetchScalarGridSpec(
        num_scalar_prefetch=0,
        grid=(b, s // qb),
        in_specs=[pl.BlockSpec(memory_space=pltpu.SMEM),
                  pl.BlockSpec((1, qb, ATT_W), lambda i, n: (i, n, 0)),
                  pl.BlockSpec((1, BLOCK, KV_W), lambda i, n: (i, jnp.maximum(n * nq - 1, 0), 0)),
                  pl.BlockSpec((1, qb, KV_W), lambda i, n: (i, n, 0)),
                  pl.BlockSpec(bias.shape, lambda i, n: (0, 0, 0))],
        out_specs=pl.BlockSpec((1, qb, ATT_W), lambda i, n: (i, n, 0)),
    )
    return pl.pallas_call(
        functools.partial(_attn_kernel, nq=nq),
        grid_spec=grid_spec,
        out_shape=jax.ShapeDtypeStruct((b, s, ATT_W), BF16),
        compiler_params=_cparams(("parallel", "parallel")),
        name="attn",
    )(sinks, q3, kv3, kv3, bias)


def _route(logits):
    lane = lax.broadcasted_iota(jnp.int32, logits.shape, 1).astype(F32)
    big = float(LANES)
    gl = jnp.where(lane < N_GROUPS, logits, NEG)
    mg = jnp.max(gl, axis=-1, keepdims=True)
    gidx = jnp.min(jnp.where(gl == mg, lane, big), axis=-1, keepdims=True)
    gsum = jnp.sum(jnp.where(lane < N_GROUPS, jnp.exp(gl - mg), 0.0), axis=-1, keepdims=True)
    gp = 1.0 / gsum
    lo = N_GROUPS + EXPERTS_PER_GROUP * gidx
    el = jnp.where(lane >= lo, jnp.where(lane < lo + EXPERTS_PER_GROUP, logits, NEG), NEG)
    m1 = jnp.max(el, axis=-1, keepdims=True)
    i1 = jnp.min(jnp.where(el == m1, lane, big), axis=-1, keepdims=True)
    el2 = jnp.where(lane == i1, NEG, el)
    m2 = jnp.max(el2, axis=-1, keepdims=True)
    i2 = jnp.min(jnp.where(el2 == m2, lane, big), axis=-1, keepdims=True)
    t = jnp.exp(m2 - m1)
    w1 = gp / (1.0 + t)
    w2 = gp * t / (1.0 + t)
    return jnp.where(lane == 0, i1 - N_GROUPS,
                     jnp.where(lane == 1, i2 - N_GROUPS,
                               jnp.where(lane == 2, w1, jnp.where(lane == 3, w2, 0.0))))


def _merge_kernel(x_ref, ya_ref, yb_ref, gmix_ref, wg_ref, bg_ref, wa_ref, wb_ref, wo_ref, gffn_ref,
                  wr_ref, br_ref, x1_ref, h2_ref, route_ref, cnt_ref, acc_ref):
    @pl.when(pl.program_id(0) == 0)
    def _():
        acc_ref[...] = jnp.zeros_like(acc_ref)

    x = x_ref[...]
    h = _rms(x, gmix_ref[...]).astype(BF16)
    gates = _sigmoid(_dot(h, wg_ref[...]) + bg_ref[...])
    merged = (gates[:, :D_MODEL] * _dot(ya_ref[...], wa_ref[...])
              + gates[:, D_MODEL:] * _dot(yb_ref[...], wb_ref[...]))
    x1 = x + _dot(merged.astype(BF16), wo_ref[...])
    x1_ref[...] = x1
    h2 = _rms(x1, gffn_ref[...])
    h2_ref[...] = _pack_rows(h2)
    route = _route(_dot(h2.astype(BF16), wr_ref[...]) + br_ref[...])
    route_ref[...] = route
    lane = lax.broadcasted_iota(jnp.int32, route.shape, 1).astype(F32)
    member = jnp.where(lane == route[:, 0:1], 1.0, jnp.where(lane == route[:, 1:2], 1.0, 0.0))
    acc = acc_ref[...] + jnp.sum(member, axis=0, keepdims=True)
    acc_ref[...] = acc
    cnt_ref[...] = acc


def _merge(x2d, ya, yb, g_mix, wg, bg, wa, wb, wo, g_ffn, wr, br, tm=512):
    t = x2d.shape[0]
    row = lambda w: pl.BlockSpec((tm, w), lambda i: (i, 0))
    full = lambda a: pl.BlockSpec(a.shape, lambda i: (0,) * a.ndim)
    once = lambda a: pl.BlockSpec(a.shape, lambda i: (0,) * a.ndim, pipeline_mode=pl.Buffered(1))
    return pl.pallas_call(
        _merge_kernel,
        grid=(t // tm,),
        in_specs=[row(D_MODEL), row(D_SSM), row(ATT_W), full(g_mix), once(wg), full(bg), once(wa), once(wb),
                  once(wo), full(g_ffn), full(wr), full(br)],
        out_specs=[row(D_MODEL), row(PACK_W), row(LANES), pl.BlockSpec((8, LANES), lambda i: (0, 0))],
        out_shape=[jax.ShapeDtypeStruct((t, D_MODEL), F32),
                   jax.ShapeDtypeStruct((t, PACK_W), jnp.uint32),
                   jax.ShapeDtypeStruct((t, LANES), F32),
                   jax.ShapeDtypeStruct((8, LANES), F32)],
        scratch_shapes=[pltpu.VMEM((8, LANES), F32)],
        compiler_params=_cparams(("arbitrary",)),
        name="merge",
    )(x2d, ya, yb, g_mix, wg, bg, wa, wb, wo, g_ffn, wr, br)


def _rank_kernel(route_ref, start_ref, dest_ref, tri_ref, acc_ref):
    i = pl.program_id(0)
    m = route_ref.shape[0]

    @pl.when(i == 0)
    def _():
        r = lax.broadcasted_iota(jnp.int32, (m, m), 0)
        c = lax.broadcasted_iota(jnp.int32, (m, m), 1)
        tri_ref[...] = jnp.where(r > c, 1.0, 0.0).astype(BF16)
        acc_ref[...] = jnp.zeros_like(acc_ref) + start_ref[...]

    rt = route_ref[...]
    lane = lax.broadcasted_iota(jnp.int32, rt.shape, 1).astype(F32)
    e1, e2 = rt[:, 0:1], rt[:, 1:2]
    is1, is2 = lane == e1, lane == e2
    member = jnp.where(is1, 1.0, jnp.where(is2, 1.0, 0.0))
    before = _dot(tri_ref[...], member.astype(BF16)) + acc_ref[0:1, :]
    d1 = jnp.sum(jnp.where(is1, before, 0.0), axis=-1, keepdims=True)
    d2 = jnp.sum(jnp.where(is2, before, 0.0), axis=-1, keepdims=True)
    dest_ref[...] = jnp.where(lane == 0, d1, jnp.where(lane == 1, d2, 0.0)).astype(jnp.int32)
    acc_ref[...] = acc_ref[...] + jnp.sum(member, axis=0, keepdims=True)


def _rank(route, pad_start_row, tr=512):
    t = route.shape[0]
    return pl.pallas_call(
        _rank_kernel,
        grid=(t // tr,),
        in_specs=[pl.BlockSpec((tr, LANES), lambda i: (i, 0)), pl.BlockSpec((1, LANES), lambda i: (0, 0))],
        out_specs=pl.BlockSpec((tr, LANES), lambda i: (i, 0)),
        out_shape=jax.ShapeDtypeStruct((t, LANES), jnp.int32),
        scratch_shapes=[pltpu.VMEM((tr, tr), BF16), pltpu.VMEM((8, LANES), F32)],
        compiler_params=_cparams(("arbitrary",)),
        name="rank",
    )(route, pad_start_row)


def _row_copy(src_ref, dst_ref, sem, src_row, dst_row):
    return pltpu.make_async_copy(src_ref.at[pl.ds(src_row, 1)], dst_ref.at[pl.ds(dst_row, 1)], sem)


def _dispatch_kernel(dest_ref, h2_ref, xd_in, xd_out, sem, *, tr):
    del xd_in
    copies = [_row_copy(h2_ref, xd_out, sem, r % tr, dest_ref[0, 0, r]) for r in range(TOP_K * tr)]
    for c in copies:
        c.start()
    for c in copies:
        c.wait()


def _dispatch(dest3, h2w, n_rows):
    nt, _, n = dest3.shape
    tr = n // TOP_K
    xd0 = jnp.zeros((n_rows, PACK_W), jnp.uint32)
    return pl.pallas_call(
        functools.partial(_dispatch_kernel, tr=tr),
        grid=(nt,),
        in_specs=[pl.BlockSpec((1, 1, n), lambda i: (i, 0, 0), memory_space=pltpu.SMEM),
                  pl.BlockSpec((tr, PACK_W), lambda i: (i, 0)),
                  pl.BlockSpec(memory_space=pl.ANY)],
        out_specs=pl.BlockSpec(memory_space=pl.ANY),
        out_shape=jax.ShapeDtypeStruct((n_rows, PACK_W), jnp.uint32),
        scratch_shapes=[pltpu.SemaphoreType.DMA],
        input_output_aliases={2: 0},
        compiler_params=_cparams(("arbitrary",)),
        name="dispatch",
    )(dest3, h2w, xd0)


def _moe_kernel(be_ref, bn_ref, xd_ref, wg_ref, wu_ref, wd_ref, yd_ref, wgb_ref, wub_ref, wdb_ref):
    i = pl.program_id(0)

    @pl.when(bn_ref[i] == 0)
    def _():
        yd_ref[...] = jnp.zeros_like(yd_ref)

    @pl.when(bn_ref[i] > 0)
    def _():
        prev = be_ref[jnp.maximum(i - 1, 0)]

        @pl.when(jnp.logical_or(i == 0, be_ref[i] != prev))
        def _():
            wgb_ref[...] = wg_ref[0].astype(BF16)
            wub_ref[...] = wu_ref[0].astype(BF16)
            wdb_ref[...] = wd_ref[0].astype(BF16)

        x = _unpack_rows(xd_ref[...]).astype(BF16)
        g = _dot(x, wgb_ref[...])
        u = _dot(x, wub_ref[...])
        a = (g * _sigmoid(g) * u).astype(BF16)
        yd_ref[...] = _pack_rows(_dot(a, wdb_ref[...]))


def _moe(blk_e, blk_n, xdw, w_eg, w_eu, w_ed):
    n_blk = blk_e.shape[0]
    grid_spec = pltpu.PrefetchScalarGridSpec(
        num_scalar_prefetch=2,
        grid=(n_blk,),
        in_specs=[pl.BlockSpec((MOE_BLK, PACK_W), lambda i, be, bn: (jnp.where(bn[i] > 0, i, 0), 0)),
                  pl.BlockSpec((1, D_MODEL, D_FF_EXPERT), lambda i, be, bn: (be[i], 0, 0)),
                  pl.BlockSpec((1, D_MODEL, D_FF_EXPERT), lambda i, be, bn: (be[i], 0, 0)),
                  pl.BlockSpec((1, D_FF_EXPERT, D_MODEL), lambda i, be, bn: (be[i], 0, 0))],
        out_specs=pl.BlockSpec((MOE_BLK, PACK_W), lambda i, be, bn: (i, 0)),
        scratch_shapes=[pltpu.VMEM((D_MODEL, D_FF_EXPERT), BF16), pltpu.VMEM((D_MODEL, D_FF_EXPERT), BF16),
                        pltpu.VMEM((D_FF_EXPERT, D_MODEL), BF16)],
    )
    return pl.pallas_call(
        _moe_kernel,
        grid_spec=grid_spec,
        out_shape=jax.ShapeDtypeStruct((n_blk * MOE_BLK, PACK_W), jnp.uint32),
        compiler_params=_cparams(("arbitrary",)),
        name="moe",
    )(blk_e, blk_n, xdw, w_eg, w_eu, w_ed)


def _final_kernel(dcur_ref, dnxt_ref, x1_ref, p_ref, route_ref, yd_hbm, wpp_ref, wpg_ref, bpg_ref, gple_ref,
                  gfin_ref, o_ref, yw_ref, sem, *, tm):
    j = pl.program_id(0)
    last = pl.num_programs(0) - 1
    n = TOP_K * tm

    def gather(dref, half, slot):
        return [_row_copy(yd_hbm, yw_ref.at[slot], sem.at[slot], dref[0, half, r], r) for r in range(n)]

    def tile(half, slot):
        rows = slice(half * tm, (half + 1) * tm)
        rt = route_ref[rows, :]
        y0 = _unpack_rows(yw_ref[slot, 0:tm, :])
        y1 = _unpack_rows(yw_ref[slot, tm:n, :])
        x2 = x1_ref[rows, :] + rt[:, 2:3] * y0 + rt[:, 3:4] * y1
        h3 = _rms(x2, gple_ref[...]).astype(BF16)
        pp = _dot(p_ref[rows, :].astype(BF16), wpp_ref[...])
        x3 = x2 + pp * _sigmoid(_dot(h3, wpg_ref[...]) + bpg_ref[...])
        o_ref[rows, :] = _rms(x3, gfin_ref[...])

    @pl.when(j == 0)
    def _():
        for c in gather(dcur_ref, 0, 0):
            c.start()

    for c in gather(dcur_ref, 0, 0):
        c.wait()
    for c in gather(dcur_ref, 1, 1):
        c.start()
    tile(0, 0)
    for c in gather(dcur_ref, 1, 1):
        c.wait()
    for c in gather(dnxt_ref, 0, 0):
        c.start()
    tile(1, 1)

    @pl.when(j == last)
    def _():
        for c in gather(dnxt_ref, 0, 0):
            c.wait()


def _final(dest4, x1, p2d, route, ydw, wpp, wpg, bpg, g_ple, g_final):
    t = x1.shape[0]
    ns, _, n = dest4.shape
    tm = n // TOP_K
    row = lambda w: pl.BlockSpec((2 * tm, w), lambda i: (i, 0))
    full = lambda a: pl.BlockSpec(a.shape, lambda i: (0,) * a.ndim)
    return pl.pallas_call(
        functools.partial(_final_kernel, tm=tm),
        grid=(ns,),
        in_specs=[pl.BlockSpec((1, 2, n), lambda i: (i, 0, 0), memory_space=pltpu.SMEM),
                  pl.BlockSpec((1, 2, n), lambda i: (jnp.minimum(i + 1, ns - 1), 0, 0), memory_space=pltpu.SMEM),
                  row(D_MODEL), row(PLE_DIM), row(LANES), pl.BlockSpec(memory_space=pl.ANY),
                  full(wpp), full(wpg), full(bpg), full(g_ple), full(g_final)],
        out_specs=row(D_MODEL),
        out_shape=jax.ShapeDtypeStruct((t, D_MODEL), F32),
        scratch_shapes=[pltpu.VMEM((2, n, PACK_W), jnp.uint32), pltpu.SemaphoreType.DMA((2,))],
        compiler_params=_cparams(("arbitrary",)),
        name="final",
    )(dest4, dest4, x1, p2d, route, ydw, wpp, wpg, bpg, g_ple, g_final)


def _block_plan(counts_f, n_blk):
    counts = counts_f[0, :N_EXPERTS].astype(jnp.int32)
    padded = ((counts + MOE_BLK - 1) // MOE_BLK) * MOE_BLK
    pad_end = jnp.cumsum(padded)
    pad_start = pad_end - padded
    blk_start = jnp.arange(n_blk, dtype=jnp.int32) * MOE_BLK
    blk_e = jnp.minimum(jnp.sum(pad_end[None, :] <= blk_start[:, None], axis=1), N_EXPERTS - 1).astype(jnp.int32)
    onehot = blk_e[:, None] == jnp.arange(N_EXPERTS, dtype=jnp.int32)[None, :]
    seg_end = jnp.sum(jnp.where(onehot, (pad_start + counts)[None, :], 0), axis=1)
    blk_n = jnp.clip(seg_end - blk_start, 0, MOE_BLK).astype(jnp.int32)
    start_row = jnp.zeros((1, LANES), F32).at[0, :N_EXPERTS].set(pad_start.astype(F32))
    return start_row, blk_e, blk_n


def _row(v):
    return v.reshape(1, -1).astype(F32)


def kernel(x, p, rel_bias, g_mix, w_in, w_gate, b_gate, ssm_a_re, ssm_a_im, ssm_log_dt, ssm_b_re, ssm_b_im,
           ssm_c_re, ssm_c_im, ssm_d, w_glu, b_glu, sinks, w_br_ssm, w_br_attn, w_out, g_ffn, w_router_group,
           b_router_group, w_router_expert, b_router_expert, w_e_gate, w_e_up, w_e_down, g_ple, w_ple_gate,
           b_ple_gate, w_ple_proj, g_final):
    bsz, seq, dm = x.shape
    assert g_mix.shape[0] == 1, "one layer followed by the final norm"
    i = 0
    t = bsz * seq
    x2d = x.reshape(t, dm)
    bias = _bias_table(rel_bias)
    tm_rows = 256
    u, q, kv = _proj(x2d, _row(g_mix[i]), w_in[i].astype(BF16))
    ab_re, ab_im, bb_re, bb_im = _ssmprep(ssm_a_re[i], ssm_a_im[i], ssm_log_dt[i], ssm_b_re[i], ssm_b_im[i])
    ws_in, ws_out, ar, ai = _ssm_weights(ab_re, ab_im, bb_re, bb_im, ssm_c_re[i], ssm_c_im[i])
    y_ssm = _ssm(u.reshape(bsz, seq, D_SSM), ws_in, ws_out, ar, ai, _row(ssm_d[i]),
                 w_glu[i].astype(BF16), _row(b_glu[i]))
    y_att = _attn(q.reshape(bsz, seq, ATT_W), kv.reshape(bsz, seq, KV_W), bias, sinks[i].astype(F32))
    w_router = jnp.zeros((dm, LANES), F32)
    w_router = w_router.at[:, :N_GROUPS].set(w_router_group[i])
    w_router = w_router.at[:, N_GROUPS:N_GROUPS + N_EXPERTS].set(w_router_expert[i])
    b_router = jnp.zeros((1, LANES), F32)
    b_router = b_router.at[0, :N_GROUPS].set(b_router_group[i])
    b_router = b_router.at[0, N_GROUPS:N_GROUPS + N_EXPERTS].set(b_router_expert[i])
    x1, h2w, route, counts = _merge(x2d, y_ssm.reshape(t, D_SSM), y_att.reshape(t, ATT_W), _row(g_mix[i]),
                                    w_gate[i].astype(BF16), _row(b_gate[i]), w_br_ssm[i].astype(BF16),
                                    w_br_attn[i].astype(BF16), w_out[i].astype(BF16), _row(g_ffn[i]),
                                    w_router.astype(BF16), b_router)
    n_rows = t * TOP_K + N_EXPERTS * MOE_BLK
    start_row, blk_e, blk_n = _block_plan(counts, n_rows // MOE_BLK)
    dest = _rank(route, start_row)[:, :TOP_K]
    dest3 = jnp.transpose(dest.reshape(t // tm_rows, tm_rows, TOP_K), (0, 2, 1)).reshape(t // tm_rows, 1, -1)
    xdw = _dispatch(dest3, h2w, n_rows)
    ydw = _moe(blk_e, blk_n, xdw, w_e_gate[i], w_e_up[i], w_e_down[i])
    out = _final(dest3.reshape(t // (2 * tm_rows), 2, -1), x1, p[i].reshape(t, PLE_DIM), route, ydw,
                 w_ple_proj[i].astype(BF16), w_ple_gate[i].astype(BF16), _row(b_ple_gate[i]), _row(g_ple[i]),
                 _row(g_final))
    return out.reshape(bsz, seq, dm)
```

```python
import functools
import math

import numpy as np
import jax
import jax.numpy as jnp
from jax import lax
from jax.experimental import pallas as pl
from jax.experimental.pallas import tpu as pltpu

D_MODEL = 1024
D_SSM = 512
SSM_CH = 16
SSM_GROUPS = D_SSM // SSM_CH
SSM_STATE = 64
N_STATE = SSM_GROUPS * SSM_STATE
N_HEADS = 8
N_KV = 2
HEAD_DIM = 64
GROUP = N_HEADS // N_KV
ATT_W = N_HEADS * HEAD_DIM
KV_W = 2 * N_KV * HEAD_DIM
WINDOW = 128
BLOCK = 128
NUM_BUCKETS = 32
MAX_DIST = 128
D_IN = D_SSM + ATT_W + KV_W
N_GROUPS = 4
EXPERTS_PER_GROUP = 8
N_EXPERTS = N_GROUPS * EXPERTS_PER_GROUP
TOP_K = 2
D_FF_EXPERT = 512
MOE_BLK = 256
PLE_DIM = 256
EPS = 1e-6

LANES = 128
NEG = -1e30
PACK_W = D_MODEL // 2

BF16 = jnp.bfloat16
F32 = jnp.float32
VMEM_LIMIT = 56 * 1024 * 1024


def _cparams(sem):
    return pltpu.CompilerParams(dimension_semantics=sem, vmem_limit_bytes=VMEM_LIMIT)


def _rms(x, g):
    ms = jnp.mean(x * x, axis=-1, keepdims=True)
    return x * lax.rsqrt(ms + EPS) * g


def _sigmoid(x):
    return 1.0 / (1.0 + jnp.exp(-x))


def _dot(a, b):
    return jnp.dot(a, b, preferred_element_type=F32)


def _pack_rows(y):
    lo = pltpu.bitcast(y[:, :PACK_W].astype(BF16).astype(F32), jnp.uint32)
    hi = pltpu.bitcast(y[:, PACK_W:].astype(BF16).astype(F32), jnp.uint32)
    return hi | lax.shift_right_logical(lo, jnp.uint32(16))


def _unpack_rows(w):
    lo = pltpu.bitcast(lax.shift_left(w, jnp.uint32(16)), F32)
    hi = pltpu.bitcast(w & jnp.uint32(0xFFFF0000), F32)
    return jnp.concatenate([lo, hi], axis=1)


def _proj_kernel(x_ref, g_ref, w_ref, u_ref, q_ref, kv_ref):
    h = _rms(x_ref[...], g_ref[...]).astype(BF16)
    proj = _dot(h, w_ref[...])
    u_ref[...] = proj[:, :D_SSM].astype(BF16)
    q_ref[...] = (proj[:, D_SSM:D_SSM + ATT_W] * (HEAD_DIM ** -0.5)).astype(BF16)
    kv_ref[...] = proj[:, D_SSM + ATT_W:].astype(BF16)


def _proj(x2d, g_mix, w_in_b, tm=512):
    t = x2d.shape[0]
    row = lambda w: pl.BlockSpec((tm, w), lambda i: (i, 0))
    full = lambda a: pl.BlockSpec(a.shape, lambda i: (0,) * a.ndim)
    return pl.pallas_call(
        _proj_kernel,
        grid=(t // tm,),
        in_specs=[row(D_MODEL), full(g_mix), full(w_in_b)],
        out_specs=[row(D_SSM), row(ATT_W), row(KV_W)],
        out_shape=[jax.ShapeDtypeStruct((t, D_SSM), BF16),
                   jax.ShapeDtypeStruct((t, ATT_W), BF16),
                   jax.ShapeDtypeStruct((t, KV_W), BF16)],
        compiler_params=_cparams(("parallel",)),
        name="proj",
    )(x2d, g_mix, w_in_b)


def _ssmprep_kernel(are_ref, aim_ref, ldt_ref, bre_ref, bim_ref, abr_ref, abi_ref, bbr_ref, bbi_ref):
    a_re, a_im = are_ref[...], aim_ref[...]
    dt = jnp.exp(ldt_ref[...])
    mag = jnp.exp(a_re * dt)
    ab_re = mag * jnp.cos(a_im * dt)
    ab_im = mag * jnp.sin(a_im * dt)
    abr_ref[...] = ab_re
    abi_ref[...] = ab_im
    den = a_re * a_re + a_im * a_im
    c_re = ((ab_re - 1.0) * a_re + ab_im * a_im) / den
    c_im = (ab_im * a_re - (ab_re - 1.0) * a_im) / den
    for c in range(SSM_CH):
        b_re, b_im = bre_ref[c], bim_ref[c]
        bbr_ref[c] = c_re * b_re - c_im * b_im
        bbi_ref[c] = c_re * b_im + c_im * b_re


def _ssmprep(a_re, a_im, log_dt, b_re, b_im):
    g, p = a_re.shape
    bt_re = jnp.transpose(b_re, (2, 0, 1))
    bt_im = jnp.transpose(b_im, (2, 0, 1))
    gp = jax.ShapeDtypeStruct((g, p), F32)
    cgp = jax.ShapeDtypeStruct((SSM_CH, g, p), F32)
    return pl.pallas_call(
        _ssmprep_kernel, out_shape=[gp, gp, cgp, cgp], name="ssmprep",
    )(a_re, a_im, log_dt.reshape(g, 1), bt_re, bt_im)


def _ssm_weights(ab_re, ab_im, bb_re, bb_im, c_re, c_im):
    g, p = ab_re.shape
    kb = D_SSM // 256
    gk = g // kb
    eye = jnp.eye(gk, dtype=F32)

    def in_block(bb):
        b4 = jnp.transpose(bb, (1, 0, 2)).reshape(kb, gk, SSM_CH, p)
        return jnp.einsum("kgcp,gh->kgchp", b4, eye).reshape(kb, gk * SSM_CH, gk * p)

    w_in = jnp.concatenate([in_block(bb_re), in_block(bb_im)], axis=2).astype(BF16)
    nj = D_SSM // LANES
    gj = g // nj
    eyej = jnp.eye(gj, dtype=F32)

    def out_block(c):
        c4 = jnp.transpose(c, (0, 2, 1)).reshape(nj, gj, p, SSM_CH)
        return jnp.einsum("jgpc,gh->jgphc", c4, eyej).reshape(nj, gj * p, gj * SSM_CH)

    w_out = jnp.stack([out_block(c_re), out_block(-c_im)], axis=1).astype(BF16)
    return w_in, w_out, ab_re.reshape(1, g * p), ab_im.reshape(1, g * p)


def _gelu_tanh(x):
    return 0.5 * x * (1.0 + jnp.tanh(math.sqrt(2.0 / math.pi) * (x + 0.044715 * (x * x * x))))


def _ssm_kernel(u_ref, win_ref, wout_ref, ar_ref, ai_ref, d_ref, wglu_ref, bglu_ref, y_ref,
                perm_ref, permt_ref, bu_ref, xb_ref, st_ref, *, nb, lc):
    rows = nb * lc
    ns = N_STATE
    i = pl.program_id(0)

    @pl.when(i == 0)
    def _():
        r = lax.broadcasted_iota(jnp.int32, (rows, rows), 0)
        c = lax.broadcasted_iota(jnp.int32, (rows, rows), 1)
        sh = nb.bit_length() - 1
        perm_ref[...] = jnp.where(c == (r & (nb - 1)) * lc + (r >> sh), 1.0, 0.0).astype(BF16)
        permt_ref[...] = jnp.where(r == (c & (nb - 1)) * lc + (c >> sh), 1.0, 0.0).astype(BF16)
        st_ref[...] = jnp.zeros_like(st_ref)

    u = u_ref[...].reshape(rows, D_SSM)
    up = _dot(perm_ref[...], u).astype(BF16)
    kb = win_ref.shape[0]
    half = ns // kb
    for k in range(kb):
        res = _dot(up[:, 256 * k:256 * (k + 1)], win_ref[k])
        bu_ref[:, half * k:half * (k + 1)] = res[:, :half]
        bu_ref[:, ns + half * k:ns + half * (k + 1)] = res[:, half:]

    ar, ai = ar_ref[...], ai_ref[...]

    def body(j, carry):
        xr, xi = carry
        keep_r, keep_i = [], []
        for s in range(2):
            row = pl.multiple_of((2 * j + s) * nb, nb)
            br = bu_ref[pl.ds(row, nb), 0:ns]
            bi = bu_ref[pl.ds(row, nb), ns:2 * ns]
            xr, xi = ar * xr - ai * xi + br, ar * xi + ai * xr + bi
            keep_r.append(xr)
            keep_i.append(xi)
        row2 = pl.multiple_of(j * 2 * nb, 2 * nb)
        xb_ref[pl.ds(row2, 2 * nb), 0:ns] = jnp.concatenate(keep_r, axis=0).astype(BF16)
        xb_ref[pl.ds(row2, 2 * nb), ns:2 * ns] = jnp.concatenate(keep_i, axis=0).astype(BF16)
        return xr, xi

    xr, xi = lax.fori_loop(0, lc // 2, body, (st_ref[:, 0:ns], st_ref[:, ns:2 * ns]))
    st_ref[:, 0:ns] = xr
    st_ref[:, ns:2 * ns] = xi

    nj = wout_ref.shape[0]
    kw = ns // nj
    ys = []
    for j in range(nj):
        ys.append(_dot(xb_ref[:, kw * j:kw * (j + 1)], wout_ref[j, 0])
                  + _dot(xb_ref[:, ns + kw * j:ns + kw * (j + 1)], wout_ref[j, 1]))
    y_tb = jnp.concatenate(ys, axis=1).astype(BF16)
    y = _dot(permt_ref[...], y_tb) + d_ref[...] * u.astype(F32)
    y = _gelu_tanh(y)
    z = _dot(y.astype(BF16), wglu_ref[...]) + bglu_ref[...]
    y_ref[...] = (y * _sigmoid(z)).astype(BF16).reshape(nb, lc, D_SSM)


def _ssm(u3, w_in, w_out, ar, ai, d_skip, w_glu_b, b_glu, lc=64):
    nb, s, _ = u3.shape
    assert nb & (nb - 1) == 0 and s % lc == 0 and lc % 16 == 0
    rows = nb * lc
    full = lambda a: pl.BlockSpec(a.shape, lambda i: (0,) * a.ndim)
    blk = pl.BlockSpec((nb, lc, D_SSM), lambda i: (0, i, 0))
    return pl.pallas_call(
        functools.partial(_ssm_kernel, nb=nb, lc=lc),
        grid=(s // lc,),
        in_specs=[blk, full(w_in), full(w_out), full(ar), full(ai), full(d_skip), full(w_glu_b), full(b_glu)],
        out_specs=blk,
        out_shape=jax.ShapeDtypeStruct((nb, s, D_SSM), BF16),
        scratch_shapes=[pltpu.VMEM((rows, rows), BF16), pltpu.VMEM((rows, rows), BF16),
                        pltpu.VMEM((rows, 2 * N_STATE), F32), pltpu.VMEM((rows, 2 * N_STATE), BF16),
                        pltpu.VMEM((nb, 2 * N_STATE), F32)],
        compiler_params=_cparams(("arbitrary",)),
        name="ssm",
    )(u3, w_in, w_out, ar, ai, d_skip, w_glu_b, b_glu)


def _t5_bucket_np(rel):
    max_exact = NUM_BUCKETS // 2
    relf = np.maximum(rel, 1).astype(np.float32)
    large = max_exact + (np.log(relf / np.float32(max_exact)) / np.float32(math.log(MAX_DIST / max_exact))
                         * np.float32(NUM_BUCKETS - max_exact)).astype(np.int32)
    large = np.minimum(large, NUM_BUCKETS - 1)
    return np.where(rel < max_exact, rel, large)


def _bias_table(rel_bias):
    q_loc = np.arange(BLOCK)[:, None]
    c_loc = np.arange(2 * BLOCK)[None, :]
    rel = q_loc + BLOCK - c_loc
    valid = (rel >= 0) & (rel < WINDOW)
    bucket = _t5_bucket_np(np.maximum(rel, 0)).reshape(-1, 1)
    onehot = (jnp.asarray(bucket) == jnp.arange(NUM_BUCKETS)[None, :]).astype(F32)
    bias = jnp.dot(onehot, rel_bias.astype(F32), precision=lax.Precision.HIGHEST)
    bias = jnp.where(valid.reshape(-1, 1), bias, NEG)
    return jnp.transpose(bias, (1, 0)).reshape(N_HEADS, BLOCK, 2 * BLOCK)


def _attn_kernel(sink_ref, q_ref, kvp_ref, kvc_ref, bias_ref, o_ref, *, nq):
    n = pl.program_id(1)
    kv_all = jnp.concatenate([kvp_ref[0], kvc_ref[0]], axis=0)
    col = lax.broadcasted_iota(jnp.int32, (BLOCK, 2 * BLOCK), 1)
    no_prev = jnp.where(col < BLOCK, jnp.where(n == 0, NEG, 0.0), 0.0)
    for j in range(nq):
        q = q_ref[0, BLOCK * j:BLOCK * (j + 1), :]
        kv = kv_all[BLOCK * j:BLOCK * (j + 2), :]
        outs = []
        for h in range(N_HEADS):
            g = h // GROUP
            qh = q[:, HEAD_DIM * h:HEAD_DIM * (h + 1)]
            kh = kv[:, HEAD_DIM * g:HEAD_DIM * (g + 1)]
            vh = kv[:, N_KV * HEAD_DIM + HEAD_DIM * g:N_KV * HEAD_DIM + HEAD_DIM * (g + 1)]
            s = lax.dot_general(qh, kh, (((1,), (1,)), ((), ())), preferred_element_type=F32)
            s = s + bias_ref[h]
            if j == 0:
                s = s + no_prev
            sink = sink_ref[h]
            m = jnp.maximum(jnp.max(s, axis=-1, keepdims=True), sink)
            e = jnp.exp(s - m)
            den = jnp.sum(e, axis=-1, keepdims=True) + jnp.exp(sink - m)
            outs.append(_dot(e.astype(BF16), vh) / den)
        o_ref[0, BLOCK * j:BLOCK * (j + 1), :] = jnp.concatenate(outs, axis=1).astype(BF16)


def _attn(q3, kv3, bias, sinks, nq=4):
    b, s, _ = q3.shape
    qb = nq * BLOCK
    grid_spec = pltpu.PrefetchScalarGridSpec(
        num_scalar_prefetch=0,
        grid=(b, s // qb),
        in_specs=[pl.BlockSpec(memory_space=pltpu.SMEM),
                  pl.BlockSpec((1, qb, ATT_W), lambda i, n: (i, n, 0)),
                  pl.BlockSpec((1, BLOCK, KV_W), lambda i, n: (i, jnp.maximum(n * nq - 1, 0), 0)),
                  pl.BlockSpec((1, qb, KV_W), lambda i, n: (i, n, 0)),
                  pl.BlockSpec(bias.shape, lambda i, n: (0, 0, 0))],
        out_specs=pl.BlockSpec((1, qb, ATT_W), lambda i, n: (i, n, 0)),
    )
    return pl.pallas_call(
        functools.partial(_attn_kernel, nq=nq),
        grid_spec=grid_spec,
        out_shape=jax.ShapeDtypeStruct((b, s, ATT_W), BF16),
        compiler_params=_cparams(("parallel", "parallel")),
        name="attn",
    )(sinks, q3, kv3, kv3, bias)


def _route(logits):
    lane = lax.broadcasted_iota(jnp.int32, logits.shape, 1).astype(F32)
    big = float(LANES)
    gl = jnp.where(lane < N_GROUPS, logits, NEG)
    mg = jnp.max(gl, axis=-1, keepdims=True)
    gidx = jnp.min(jnp.where(gl == mg, lane, big), axis=-1, keepdims=True)
    gsum = jnp.sum(jnp.where(lane < N_GROUPS, jnp.exp(gl - mg), 0.0), axis=-1, keepdims=True)
    gp = 1.0 / gsum
    lo = N_GROUPS + EXPERTS_PER_GROUP * gidx
    el = jnp.where(lane >= lo, jnp.where(lane < lo + EXPERTS_PER_GROUP, logits, NEG), NEG)
    m1 = jnp.max(el, axis=-1, keepdims=True)
    i1 = jnp.min(jnp.where(el == m1, lane, big), axis=-1, keepdims=True)
    el2 = jnp.where(lane == i1, NEG, el)
    m2 = jnp.max(el2, axis=-1, keepdims=True)
    i2 = jnp.min(jnp.where(el2 == m2, lane, big), axis=-1, keepdims=True)
    t = jnp.exp(m2 - m1)
    w1 = gp / (1.0 + t)
    w2 = gp * t / (1.0 + t)
    return jnp.where(lane == 0, i1 - N_GROUPS,
                     jnp.where(lane == 1, i2 - N_GROUPS,
                               jnp.where(lane == 2, w1, jnp.where(lane == 3, w2, 0.0))))


def _merge_kernel(x_ref, ya_ref, yb_ref, gmix_ref, wg_ref, bg_ref, wa_ref, wb_ref, wo_ref, gffn_ref,
                  wr_ref, br_ref, x1_ref, h2_ref, route_ref, cnt_ref, acc_ref):
    @pl.when(pl.program_id(0) == 0)
    def _():
        acc_ref[...] = jnp.zeros_like(acc_ref)

    x = x_ref[...]
    h = _rms(x, gmix_ref[...]).astype(BF16)
    gates = _sigmoid(_dot(h, wg_ref[...]) + bg_ref[...])
    merged = (gates[:, :D_MODEL] * _dot(ya_ref[...], wa_ref[...])
              + gates[:, D_MODEL:] * _dot(yb_ref[...], wb_ref[...]))
    x1 = x + _dot(merged.astype(BF16), wo_ref[...])
    x1_ref[...] = x1
    h2 = _rms(x1, gffn_ref[...])
    h2_ref[...] = _pack_rows(h2)
    route = _route(_dot(h2.astype(BF16), wr_ref[...]) + br_ref[...])
    route_ref[...] = route
    lane = lax.broadcasted_iota(jnp.int32, route.shape, 1).astype(F32)
    member = jnp.where(lane == route[:, 0:1], 1.0, jnp.where(lane == route[:, 1:2], 1.0, 0.0))
    acc = acc_ref[...] + jnp.sum(member, axis=0, keepdims=True)
    acc_ref[...] = acc
    cnt_ref[...] = acc


def _merge(x2d, ya, yb, g_mix, wg, bg, wa, wb, wo, g_ffn, wr, br, tm=512):
    t = x2d.shape[0]
    row = lambda w: pl.BlockSpec((tm, w), lambda i: (i, 0))
    full = lambda a: pl.BlockSpec(a.shape, lambda i: (0,) * a.ndim)
    once = lambda a: pl.BlockSpec(a.shape, lambda i: (0,) * a.ndim, pipeline_mode=pl.Buffered(1))
    return pl.pallas_call(
        _merge_kernel,
        grid=(t // tm,),
        in_specs=[row(D_MODEL), row(D_SSM), row(ATT_W), full(g_mix), once(wg), full(bg), once(wa), once(wb),
                  once(wo), full(g_ffn), full(wr), full(br)],
        out_specs=[row(D_MODEL), row(PACK_W), row(LANES), pl.BlockSpec((8, LANES), lambda i: (0, 0))],
        out_shape=[jax.ShapeDtypeStruct((t, D_MODEL), F32),
                   jax.ShapeDtypeStruct((t, PACK_W), jnp.uint32),
                   jax.ShapeDtypeStruct((t, LANES), F32),
                   jax.ShapeDtypeStruct((8, LANES), F32)],
        scratch_shapes=[pltpu.VMEM((8, LANES), F32)],
        compiler_params=_cparams(("arbitrary",)),
        name="merge",
    )(x2d, ya, yb, g_mix, wg, bg, wa, wb, wo, g_ffn, wr, br)


def _rank_kernel(route_ref, start_ref, dest_ref, tri_ref, acc_ref):
    i = pl.program_id(0)
    m = route_ref.shape[0]

    @pl.when(i == 0)
    def _():
        r = lax.broadcasted_iota(jnp.int32, (m, m), 0)
        c = lax.broadcasted_iota(jnp.int32, (m, m), 1)
        tri_ref[...] = jnp.where(r > c, 1.0, 0.0).astype(BF16)
        acc_ref[...] = jnp.zeros_like(acc_ref) + start_ref[...]

    rt = route_ref[...]
    lane = lax.broadcasted_iota(jnp.int32, rt.shape, 1).astype(F32)
    e1, e2 = rt[:, 0:1], rt[:, 1:2]
    is1, is2 = lane == e1, lane == e2
    member = jnp.where(is1, 1.0, jnp.where(is2, 1.0, 0.0))
    before = _dot(tri_ref[...], member.astype(BF16)) + acc_ref[0:1, :]
    d1 = jnp.sum(jnp.where(is1, before, 0.0), axis=-1, keepdims=True)
    d2 = jnp.sum(jnp.where(is2, before, 0.0), axis=-1, keepdims=True)
    dest_ref[...] = jnp.where(lane == 0, d1, jnp.where(lane == 1, d2, 0.0)).astype(jnp.int32)
    acc_ref[...] = acc_ref[...] + jnp.sum(member, axis=0, keepdims=True)


def _rank(route, pad_start_row, tr=512):
    t = route.shape[0]
    return pl.pallas_call(
        _rank_kernel,
        grid=(t // tr,),
        in_specs=[pl.BlockSpec((tr, LANES), lambda i: (i, 0)), pl.BlockSpec((1, LANES), lambda i: (0, 0))],
        out_specs=pl.BlockSpec((tr, LANES), lambda i: (i, 0)),
        out_shape=jax.ShapeDtypeStruct((t, LANES), jnp.int32),
        scratch_shapes=[pltpu.VMEM((tr, tr), BF16), pltpu.VMEM((8, LANES), F32)],
        compiler_params=_cparams(("arbitrary",)),
        name="rank",
    )(route, pad_start_row)


def _row_copy(src_ref, dst_ref, sem, src_row, dst_row):
    return pltpu.make_async_copy(src_ref.at[pl.ds(src_row, 1)], dst_ref.at[pl.ds(dst_row, 1)], sem)


def _dispatch_kernel(dest_ref, h2_ref, xd_in, xd_out, sem, *, tr):
    del xd_in
    copies = [_row_copy(h2_ref, xd_out, sem, r % tr, dest_ref[0, 0, r]) for r in range(TOP_K * tr)]
    for c in copies:
        c.start()
    for c in copies:
        c.wait()


def _dispatch(dest3, h2w, n_rows):
    nt, _, n = dest3.shape
    tr = n // TOP_K
    xd0 = jnp.zeros((n_rows, PACK_W), jnp.uint32)
    return pl.pallas_call(
        functools.partial(_dispatch_kernel, tr=tr),
        grid=(nt,),
        in_specs=[pl.BlockSpec((1, 1, n), lambda i: (i, 0, 0), memory_space=pltpu.SMEM),
                  pl.BlockSpec((tr, PACK_W), lambda i: (i, 0)),
                  pl.BlockSpec(memory_space=pl.ANY)],
        out_specs=pl.BlockSpec(memory_space=pl.ANY),
        out_shape=jax.ShapeDtypeStruct((n_rows, PACK_W), jnp.uint32),
        scratch_shapes=[pltpu.SemaphoreType.DMA],
        input_output_aliases={2: 0},
        compiler_params=_cparams(("arbitrary",)),
        name="dispatch",
    )(dest3, h2w, xd0)


def _moe_kernel(be_ref, bn_ref, xd_ref, wg_ref, wu_ref, wd_ref, yd_ref, wgb_ref, wub_ref, wdb_ref):
    i = pl.program_id(0)

    @pl.when(bn_ref[i] == 0)
    def _():
        yd_ref[...] = jnp.zeros_like(yd_ref)

    @pl.when(bn_ref[i] > 0)
    def _():
        prev = be_ref[jnp.maximum(i - 1, 0)]

        @pl.when(jnp.logical_or(i == 0, be_ref[i] != prev))
        def _():
            wgb_ref[...] = wg_ref[0].astype(BF16)
            wub_ref[...] = wu_ref[0].astype(BF16)
            wdb_ref[...] = wd_ref[0].astype(BF16)

        x = _unpack_rows(xd_ref[...]).astype(BF16)
        g = _dot(x, wgb_ref[...])
        u = _dot(x, wub_ref[...])
        a = (g * _sigmoid(g) * u).astype(BF16)
        yd_ref[...] = _pack_rows(_dot(a, wdb_ref[...]))


def _moe(blk_e, blk_n, xdw, w_eg, w_eu, w_ed):
    n_blk = blk_e.shape[0]
    grid_spec = pltpu.PrefetchScalarGridSpec(
        num_scalar_prefetch=2,
        grid=(n_blk,),
        in_specs=[pl.BlockSpec((MOE_BLK, PACK_W), lambda i, be, bn: (jnp.where(bn[i] > 0, i, 0), 0)),
                  pl.BlockSpec((1, D_MODEL, D_FF_EXPERT), lambda i, be, bn: (be[i], 0, 0)),
                  pl.BlockSpec((1, D_MODEL, D_FF_EXPERT), lambda i, be, bn: (be[i], 0, 0)),
                  pl.BlockSpec((1, D_FF_EXPERT, D_MODEL), lambda i, be, bn: (be[i], 0, 0))],
        out_specs=pl.BlockSpec((MOE_BLK, PACK_W), lambda i, be, bn: (i, 0)),
        scratch_shapes=[pltpu.VMEM((D_MODEL, D_FF_EXPERT), BF16), pltpu.VMEM((D_MODEL, D_FF_EXPERT), BF16),
                        pltpu.VMEM((D_FF_EXPERT, D_MODEL), BF16)],
    )
    return pl.pallas_call(
        _moe_kernel,
        grid_spec=grid_spec,
        out_shape=jax.ShapeDtypeStruct((n_blk * MOE_BLK, PACK_W), jnp.uint32),
        compiler_params=_cparams(("arbitrary",)),
        name="moe",
    )(blk_e, blk_n, xdw, w_eg, w_eu, w_ed)


def _final_kernel(dcur_ref, dnxt_ref, x1_ref, p_ref, route_ref, yd_hbm, wpp_ref, wpg_ref, bpg_ref, gple_ref,
                  gfin_ref, o_ref, yw0_ref, yw1_ref, sem, *, tm):
    j = pl.program_id(0)
    last = pl.num_programs(0) - 1
    n = TOP_K * tm
    yw = (yw0_ref, yw1_ref)

    def gather(dref, half, slot):
        return [_row_copy(yd_hbm, yw[slot], sem.at[slot], dref[0, half, r], r) for r in range(n)]

    def tile(half, slot):
        rows = slice(half * tm, (half + 1) * tm)
        rt = route_ref[rows, :]
        y0 = _unpack_rows(yw[slot][0:tm, :])
        y1 = _unpack_rows(yw[slot][tm:n, :])
        x2 = x1_ref[rows, :] + rt[:, 2:3] * y0 + rt[:, 3:4] * y1
        h3 = _rms(x2, gple_ref[...]).astype(BF16)
        pp = _dot(p_ref[rows, :].astype(BF16), wpp_ref[...])
        x3 = x2 + pp * _sigmoid(_dot(h3, wpg_ref[...]) + bpg_ref[...])
        o_ref[rows, :] = _rms(x3, gfin_ref[...])

    @pl.when(j == 0)
    def _():
        for c in gather(dcur_ref, 0, 0):
            c.start()

    for c in gather(dcur_ref, 0, 0):
        c.wait()
    for c in gather(dcur_ref, 1, 1):
        c.start()
    tile(0, 0)
    for c in gather(dcur_ref, 1, 1):
        c.wait()
    for c in gather(dnxt_ref, 0, 0):
        c.start()
    tile(1, 1)

    @pl.when(j == last)
    def _():
        for c in gather(dnxt_ref, 0, 0):
            c.wait()


def _final(dest4, x1, p2d, route, ydw, wpp, wpg, bpg, g_ple, g_final):
    t = x1.shape[0]
    ns, _, n = dest4.shape
    tm = n // TOP_K
    row = lambda w: pl.BlockSpec((2 * tm, w), lambda i: (i, 0))
    full = lambda a: pl.BlockSpec(a.shape, lambda i: (0,) * a.ndim)
    return pl.pallas_call(
        functools.partial(_final_kernel, tm=tm),
        grid=(ns,),
        in_specs=[pl.BlockSpec((1, 2, n), lambda i: (i, 0, 0), memory_space=pltpu.SMEM),
                  pl.BlockSpec((1, 2, n), lambda i: (jnp.minimum(i + 1, ns - 1), 0, 0), memory_space=pltpu.SMEM),
                  row(D_MODEL), row(PLE_DIM), row(LANES), pl.BlockSpec(memory_space=pl.ANY),
                  full(wpp), full(wpg), full(bpg), full(g_ple), full(g_final)],
        out_specs=row(D_MODEL),
        out_shape=jax.ShapeDtypeStruct((t, D_MODEL), F32),
        scratch_shapes=[pltpu.VMEM((n, PACK_W), jnp.uint32), pltpu.VMEM((n, PACK_W), jnp.uint32),
                        pltpu.SemaphoreType.DMA((2,))],
        compiler_params=_cparams(("arbitrary",)),
        name="final",
    )(dest4, dest4, x1, p2d, route, ydw, wpp, wpg, bpg, g_ple, g_final)


def _block_plan(counts_f, n_blk):
    counts = counts_f[0, :N_EXPERTS].astype(jnp.int32)
    padded = ((counts + MOE_BLK - 1) // MOE_BLK) * MOE_BLK
    pad_end = jnp.cumsum(padded)
    pad_start = pad_end - padded
    blk_start = jnp.arange(n_blk, dtype=jnp.int32) * MOE_BLK
    blk_e = jnp.minimum(jnp.sum(pad_end[None, :] <= blk_start[:, None], axis=1), N_EXPERTS - 1).astype(jnp.int32)
    onehot = blk_e[:, None] == jnp.arange(N_EXPERTS, dtype=jnp.int32)[None, :]
    seg_end = jnp.sum(jnp.where(onehot, (pad_start + counts)[None, :], 0), axis=1)
    blk_n = jnp.clip(seg_end - blk_start, 0, MOE_BLK).astype(jnp.int32)
    start_row = jnp.zeros((1, LANES), F32).at[0, :N_EXPERTS].set(pad_start.astype(F32))
    return start_row, blk_e, blk_n


def _row(v):
    return v.reshape(1, -1).astype(F32)


def kernel(x, p, rel_bias, g_mix, w_in, w_gate, b_gate, ssm_a_re, ssm_a_im, ssm_log_dt, ssm_b_re, ssm_b_im,
           ssm_c_re, ssm_c_im, ssm_d, w_glu, b_glu, sinks, w_br_ssm, w_br_attn, w_out, g_ffn, w_router_group,
           b_router_group, w_router_expert, b_router_expert, w_e_gate, w_e_up, w_e_down, g_ple, w_ple_gate,
           b_ple_gate, w_ple_proj, g_final):
    bsz, seq, dm = x.shape
    assert g_mix.shape[0] == 1, "one layer followed by the final norm"
    i = 0
    t = bsz * seq
    x2d = x.reshape(t, dm)
    bias = _bias_table(rel_bias)
    tm_rows = 256
    u, q, kv = _proj(x2d, _row(g_mix[i]), w_in[i].astype(BF16))
    ab_re, ab_im, bb_re, bb_im = _ssmprep(ssm_a_re[i], ssm_a_im[i], ssm_log_dt[i], ssm_b_re[i], ssm_b_im[i])
    ws_in, ws_out, ar, ai = _ssm_weights(ab_re, ab_im, bb_re, bb_im, ssm_c_re[i], ssm_c_im[i])
    y_ssm = _ssm(u.reshape(bsz, seq, D_SSM), ws_in, ws_out, ar, ai, _row(ssm_d[i]),
                 w_glu[i].astype(BF16), _row(b_glu[i]))
    y_att = _attn(q.reshape(bsz, seq, ATT_W), kv.reshape(bsz, seq, KV_W), bias, sinks[i].astype(F32))
    w_router = jnp.zeros((dm, LANES), F32)
    w_router = w_router.at[:, :N_GROUPS].set(w_router_group[i])
    w_router = w_router.at[:, N_GROUPS:N_GROUPS + N_EXPERTS].set(w_router_expert[i])
    b_router = jnp.zeros((1, LANES), F32)
    b_router = b_router.at[0, :N_GROUPS].set(b_router_group[i])
    b_router = b_router.at[0, N_GROUPS:N_GROUPS + N_EXPERTS].set(b_router_expert[i])
    x1, h2w, route, counts = _merge(x2d, y_ssm.reshape(t, D_SSM), y_att.reshape(t, ATT_W), _row(g_mix[i]),
                                    w_gate[i].astype(BF16), _row(b_gate[i]), w_br_ssm[i].astype(BF16),
                                    w_br_attn[i].astype(BF16), w_out[i].astype(BF16), _row(g_ffn[i]),
                                    w_router.astype(BF16), b_router)
    n_rows = t * TOP_K + N_EXPERTS * MOE_BLK
    start_row, blk_e, blk_n = _block_plan(counts, n_rows // MOE_BLK)
    dest = _rank(route, start_row)[:, :TOP_K]
    dest3 = jnp.transpose(dest.reshape(t // tm_rows, tm_rows, TOP_K), (0, 2, 1)).reshape(t // tm_rows, 1, -1)
    xdw = _dispatch(dest3, h2w, n_rows)
    ydw = _moe(blk_e, blk_n, xdw, w_e_gate[i], w_e_up[i], w_e_down[i])
    out = _final(dest3.reshape(t // (2 * tm_rows), 2, -1), x1, p[i].reshape(t, PLE_DIM), route, ydw,
                 w_ple_proj[i].astype(BF16), w_ple_gate[i].astype(BF16), _row(b_ple_gate[i]), _row(g_ple[i]),
                 _row(g_final))
    return out.reshape(bsz, seq, dm)
```

```python
import functools
import math

import numpy as np
import jax
import jax.numpy as jnp
from jax import lax
from jax.experimental import pallas as pl
from jax.experimental.pallas import tpu as pltpu

D_MODEL = 1024
D_SSM = 512
SSM_CH = 16
SSM_GROUPS = D_SSM // SSM_CH
SSM_STATE = 64
N_STATE = SSM_GROUPS * SSM_STATE
N_HEADS = 8
N_KV = 2
HEAD_DIM = 64
GROUP = N_HEADS // N_KV
ATT_W = N_HEADS * HEAD_DIM
KV_W = 2 * N_KV * HEAD_DIM
WINDOW = 128
BLOCK = 128
NUM_BUCKETS = 32
MAX_DIST = 128
D_IN = D_SSM + ATT_W + KV_W
N_GROUPS = 4
EXPERTS_PER_GROUP = 8
N_EXPERTS = N_GROUPS * EXPERTS_PER_GROUP
TOP_K = 2
D_FF_EXPERT = 512
MOE_BLK = 256
PLE_DIM = 256
EPS = 1e-6

LANES = 128
NEG = -1e30
PACK_W = D_MODEL // 2

BF16 = jnp.bfloat16
F32 = jnp.float32
VMEM_LIMIT = 56 * 1024 * 1024


def _cparams(sem):
    return pltpu.CompilerParams(dimension_semantics=sem, vmem_limit_bytes=VMEM_LIMIT)


def _rms(x, g):
    ms = jnp.mean(x * x, axis=-1, keepdims=True)
    return x * lax.rsqrt(ms + EPS) * g


def _sigmoid(x):
    return 1.0 / (1.0 + jnp.exp(-x))


def _dot(a, b):
    return jnp.dot(a, b, preferred_element_type=F32)


def _pack_rows(y):
    lo = pltpu.bitcast(y[:, :PACK_W].astype(BF16).astype(F32), jnp.uint32)
    hi = pltpu.bitcast(y[:, PACK_W:].astype(BF16).astype(F32), jnp.uint32)
    return hi | lax.shift_right_logical(lo, jnp.uint32(16))


def _unpack_rows(w):
    lo = pltpu.bitcast(lax.shift_left(w, jnp.uint32(16)), F32)
    hi = pltpu.bitcast(w & jnp.uint32(0xFFFF0000), F32)
    return jnp.concatenate([lo, hi], axis=1)


def _proj_kernel(x_ref, g_ref, w_ref, u_ref, q_ref, kv_ref):
    h = _rms(x_ref[...], g_ref[...]).astype(BF16)
    proj = _dot(h, w_ref[...])
    u_ref[...] = proj[:, :D_SSM].astype(BF16)
    q_ref[...] = (proj[:, D_SSM:D_SSM + ATT_W] * (HEAD_DIM ** -0.5)).astype(BF16)
    kv_ref[...] = proj[:, D_SSM + ATT_W:].astype(BF16)


def _proj(x2d, g_mix, w_in_b, tm=512):
    t = x2d.shape[0]
    row = lambda w: pl.BlockSpec((tm, w), lambda i: (i, 0))
    full = lambda a: pl.BlockSpec(a.shape, lambda i: (0,) * a.ndim)
    return pl.pallas_call(
        _proj_kernel,
        grid=(t // tm,),
        in_specs=[row(D_MODEL), full(g_mix), full(w_in_b)],
        out_specs=[row(D_SSM), row(ATT_W), row(KV_W)],
        out_shape=[jax.ShapeDtypeStruct((t, D_SSM), BF16),
                   jax.ShapeDtypeStruct((t, ATT_W), BF16),
                   jax.ShapeDtypeStruct((t, KV_W), BF16)],
        compiler_params=_cparams(("parallel",)),
        name="proj",
    )(x2d, g_mix, w_in_b)


def _ssmprep_kernel(are_ref, aim_ref, ldt_ref, bre_ref, bim_ref, abr_ref, abi_ref, bbr_ref, bbi_ref):
    a_re, a_im = are_ref[...], aim_ref[...]
    dt = jnp.exp(ldt_ref[...])
    mag = jnp.exp(a_re * dt)
    ab_re = mag * jnp.cos(a_im * dt)
    ab_im = mag * jnp.sin(a_im * dt)
    abr_ref[...] = ab_re
    abi_ref[...] = ab_im
    den = a_re * a_re + a_im * a_im
    c_re = ((ab_re - 1.0) * a_re + ab_im * a_im) / den
    c_im = (ab_im * a_re - (ab_re - 1.0) * a_im) / den
    for c in range(SSM_CH):
        b_re, b_im = bre_ref[c], bim_ref[c]
        bbr_ref[c] = c_re * b_re - c_im * b_im
        bbi_ref[c] = c_re * b_im + c_im * b_re


def _ssmprep(a_re, a_im, log_dt, b_re, b_im):
    g, p = a_re.shape
    bt_re = jnp.transpose(b_re, (2, 0, 1))
    bt_im = jnp.transpose(b_im, (2, 0, 1))
    gp = jax.ShapeDtypeStruct((g, p), F32)
    cgp = jax.ShapeDtypeStruct((SSM_CH, g, p), F32)
    return pl.pallas_call(
        _ssmprep_kernel, out_shape=[gp, gp, cgp, cgp], name="ssmprep",
    )(a_re, a_im, log_dt.reshape(g, 1), bt_re, bt_im)


def _ssm_weights(ab_re, ab_im, bb_re, bb_im, c_re, c_im):
    g, p = ab_re.shape
    kb = D_SSM // 256
    gk = g // kb
    eye = jnp.eye(gk, dtype=F32)

    def in_block(bb):
        b4 = jnp.transpose(bb, (1, 0, 2)).reshape(kb, gk, SSM_CH, p)
        return jnp.einsum("kgcp,gh->kgchp", b4, eye).reshape(kb, gk * SSM_CH, gk * p)

    w_in = jnp.concatenate([in_block(bb_re), in_block(bb_im)], axis=2).astype(BF16)
    nj = D_SSM // LANES
    gj = g // nj
    eyej = jnp.eye(gj, dtype=F32)

    def out_block(c):
        c4 = jnp.transpose(c, (0, 2, 1)).reshape(nj, gj, p, SSM_CH)
        return jnp.einsum("jgpc,gh->jgphc", c4, eyej).reshape(nj, gj * p, gj * SSM_CH)

    w_out = jnp.stack([out_block(c_re), out_block(-c_im)], axis=1).astype(BF16)
    return w_in, w_out, ab_re.reshape(1, g * p), ab_im.reshape(1, g * p)


def _gelu_tanh(x):
    return 0.5 * x * (1.0 + jnp.tanh(math.sqrt(2.0 / math.pi) * (x + 0.044715 * (x * x * x))))


def _ssm_kernel(u_ref, win_ref, wout_ref, ar_ref, ai_ref, d_ref, wglu_ref, bglu_ref, y_ref,
                perm_ref, permt_ref, bu_ref, xb_ref, st_ref, *, nb, lc):
    rows = nb * lc
    ns = N_STATE
    i = pl.program_id(0)

    @pl.when(i == 0)
    def _():
        r = lax.broadcasted_iota(jnp.int32, (rows, rows), 0)
        c = lax.broadcasted_iota(jnp.int32, (rows, rows), 1)
        sh = nb.bit_length() - 1
        perm_ref[...] = jnp.where(c == (r & (nb - 1)) * lc + (r >> sh), 1.0, 0.0).astype(BF16)
        permt_ref[...] = jnp.where(r == (c & (nb - 1)) * lc + (c >> sh), 1.0, 0.0).astype(BF16)
        st_ref[...] = jnp.zeros_like(st_ref)

    u = u_ref[...].reshape(rows, D_SSM)
    up = _dot(perm_ref[...], u).astype(BF16)
    kb = win_ref.shape[0]
    half = ns // kb
    for k in range(kb):
        res = _dot(up[:, 256 * k:256 * (k + 1)], win_ref[k])
        bu_ref[:, half * k:half * (k + 1)] = res[:, :half]
        bu_ref[:, ns + half * k:ns + half * (k + 1)] = res[:, half:]

    ar, ai = ar_ref[...], ai_ref[...]

    def body(j, carry):
        xr, xi = carry
        keep_r, keep_i = [], []
        for s in range(2):
            row = pl.multiple_of((2 * j + s) * nb, nb)
            br = bu_ref[pl.ds(row, nb), 0:ns]
            bi = bu_ref[pl.ds(row, nb), ns:2 * ns]
            xr, xi = ar * xr - ai * xi + br, ar * xi + ai * xr + bi
            keep_r.append(xr)
            keep_i.append(xi)
        row2 = pl.multiple_of(j * 2 * nb, 2 * nb)
        xb_ref[pl.ds(row2, 2 * nb), 0:ns] = jnp.concatenate(keep_r, axis=0).astype(BF16)
        xb_ref[pl.ds(row2, 2 * nb), ns:2 * ns] = jnp.concatenate(keep_i, axis=0).astype(BF16)
        return xr, xi

    xr, xi = lax.fori_loop(0, lc // 2, body, (st_ref[:, 0:ns], st_ref[:, ns:2 * ns]))
    st_ref[:, 0:ns] = xr
    st_ref[:, ns:2 * ns] = xi

    nj = wout_ref.shape[0]
    kw = ns // nj
    ys = []
    for j in range(nj):
        ys.append(_dot(xb_ref[:, kw * j:kw * (j + 1)], wout_ref[j, 0])
                  + _dot(xb_ref[:, ns + kw * j:ns + kw * (j + 1)], wout_ref[j, 1]))
    y_tb = jnp.concatenate(ys, axis=1).astype(BF16)
    y = _dot(permt_ref[...], y_tb) + d_ref[...] * u.astype(F32)
    y = _gelu_tanh(y)
    z = _dot(y.astype(BF16), wglu_ref[...]) + bglu_ref[...]
    y_ref[...] = (y * _sigmoid(z)).astype(BF16).reshape(nb, lc, D_SSM)


def _ssm(u3, w_in, w_out, ar, ai, d_skip, w_glu_b, b_glu, lc=64):
    nb, s, _ = u3.shape
    assert nb & (nb - 1) == 0 and s % lc == 0 and lc % 16 == 0
    rows = nb * lc
    full = lambda a: pl.BlockSpec(a.shape, lambda i: (0,) * a.ndim)
    blk = pl.BlockSpec((nb, lc, D_SSM), lambda i: (0, i, 0))
    return pl.pallas_call(
        functools.partial(_ssm_kernel, nb=nb, lc=lc),
        grid=(s // lc,),
        in_specs=[blk, full(w_in), full(w_out), full(ar), full(ai), full(d_skip), full(w_glu_b), full(b_glu)],
        out_specs=blk,
        out_shape=jax.ShapeDtypeStruct((nb, s, D_SSM), BF16),
        scratch_shapes=[pltpu.VMEM((rows, rows), BF16), pltpu.VMEM((rows, rows), BF16),
                        pltpu.VMEM((rows, 2 * N_STATE), F32), pltpu.VMEM((rows, 2 * N_STATE), BF16),
                        pltpu.VMEM((nb, 2 * N_STATE), F32)],
        compiler_params=_cparams(("arbitrary",)),
        name="ssm",
    )(u3, w_in, w_out, ar, ai, d_skip, w_glu_b, b_glu)


def _t5_bucket_np(rel):
    max_exact = NUM_BUCKETS // 2
    relf = np.maximum(rel, 1).astype(np.float32)
    large = max_exact + (np.log(relf / np.float32(max_exact)) / np.float32(math.log(MAX_DIST / max_exact))
                         * np.float32(NUM_BUCKETS - max_exact)).astype(np.int32)
    large = np.minimum(large, NUM_BUCKETS - 1)
    return np.where(rel < max_exact, rel, large)


def _bias_table(rel_bias):
    q_loc = np.arange(BLOCK)[:, None]
    c_loc = np.arange(2 * BLOCK)[None, :]
    rel = q_loc + BLOCK - c_loc
    valid = (rel >= 0) & (rel < WINDOW)
    bucket = _t5_bucket_np(np.maximum(rel, 0)).reshape(-1, 1)
    onehot = (jnp.asarray(bucket) == jnp.arange(NUM_BUCKETS)[None, :]).astype(F32)
    bias = jnp.dot(onehot, rel_bias.astype(F32), precision=lax.Precision.HIGHEST)
    bias = jnp.where(valid.reshape(-1, 1), bias, NEG)
    return jnp.transpose(bias, (1, 0)).reshape(N_HEADS, BLOCK, 2 * BLOCK)


def _attn_kernel(sink_ref, q_ref, kvp_ref, kvc_ref, bias_ref, o_ref, *, nq):
    n = pl.program_id(1)
    kv_all = jnp.concatenate([kvp_ref[0], kvc_ref[0]], axis=0)
    col = lax.broadcasted_iota(jnp.int32, (BLOCK, 2 * BLOCK), 1)
    no_prev = jnp.where(col < BLOCK, jnp.where(n == 0, NEG, 0.0), 0.0)
    for j in range(nq):
        q = q_ref[0, BLOCK * j:BLOCK * (j + 1), :]
        kv = kv_all[BLOCK * j:BLOCK * (j + 2), :]
        outs = []
        for h in range(N_HEADS):
            g = h // GROUP
            qh = q[:, HEAD_DIM * h:HEAD_DIM * (h + 1)]
            kh = kv[:, HEAD_DIM * g:HEAD_DIM * (g + 1)]
            vh = kv[:, N_KV * HEAD_DIM + HEAD_DIM * g:N_KV * HEAD_DIM + HEAD_DIM * (g + 1)]
            s = lax.dot_general(qh, kh, (((1,), (1,)), ((), ())), preferred_element_type=F32)
            s = s + bias_ref[h]
            if j == 0:
                s = s + no_prev
            sink = sink_ref[h]
            m = jnp.maximum(jnp.max(s, axis=-1, keepdims=True), sink)
            e = jnp.exp(s - m)
            den = jnp.sum(e, axis=-1, keepdims=True) + jnp.exp(sink - m)
            outs.append(_dot(e.astype(BF16), vh) / den)
        o_ref[0, BLOCK * j:BLOCK * (j + 1), :] = jnp.concatenate(outs, axis=1).astype(BF16)


def _attn(q3, kv3, bias, sinks, nq=4):
    b, s, _ = q3.shape
    qb = nq * BLOCK
    grid_spec = pltpu.PrefetchScalarGridSpec(
        num_scalar_prefetch=0,
        grid=(b, s // qb),
        in_specs=[pl.BlockSpec(memory_space=pltpu.SMEM),
                  pl.BlockSpec((1, qb, ATT_W), lambda i, n: (i, n, 0)),
                  pl.BlockSpec((1, BLOCK, KV_W), lambda i, n: (i, jnp.maximum(n * nq - 1, 0), 0)),
                  pl.BlockSpec((1, qb, KV_W), lambda i, n: (i, n, 0)),
                  pl.BlockSpec(bias.shape, lambda i, n: (0, 0, 0))],
        out_specs=pl.BlockSpec((1, qb, ATT_W), lambda i, n: (i, n, 0)),
    )
    return pl.pallas_call(
        functools.partial(_attn_kernel, nq=nq),
        grid_spec=grid_spec,
        out_shape=jax.ShapeDtypeStruct((b, s, ATT_W), BF16),
        compiler_params=_cparams(("parallel", "parallel")),
        name="attn",
    )(sinks, q3, kv3, kv3, bias)


def _route(logits):
    lane = lax.broadcasted_iota(jnp.int32, logits.shape, 1).astype(F32)
    big = float(LANES)
    gl = jnp.where(lane < N_GROUPS, logits, NEG)
    mg = jnp.max(gl, axis=-1, keepdims=True)
    gidx = jnp.min(jnp.where(gl == mg, lane, big), axis=-1, keepdims=True)
    gsum = jnp.sum(jnp.where(lane < N_GROUPS, jnp.exp(gl - mg), 0.0), axis=-1, keepdims=True)
    gp = 1.0 / gsum
    lo = N_GROUPS + EXPERTS_PER_GROUP * gidx
    el = jnp.where(lane >= lo, jnp.where(lane < lo + EXPERTS_PER_GROUP, logits, NEG), NEG)
    m1 = jnp.max(el, axis=-1, keepdims=True)
    i1 = jnp.min(jnp.where(el == m1, lane, big), axis=-1, keepdims=True)
    el2 = jnp.where(lane == i1, NEG, el)
    m2 = jnp.max(el2, axis=-1, keepdims=True)
    i2 = jnp.min(jnp.where(el2 == m2, lane, big), axis=-1, keepdims=True)
    t = jnp.exp(m2 - m1)
    w1 = gp / (1.0 + t)
    w2 = gp * t / (1.0 + t)
    return jnp.where(lane == 0, i1 - N_GROUPS,
                     jnp.where(lane == 1, i2 - N_GROUPS,
                               jnp.where(lane == 2, w1, jnp.where(lane == 3, w2, 0.0))))


def _merge_kernel(x_ref, ya_ref, yb_ref, gmix_ref, wg_ref, bg_ref, wa_ref, wb_ref, wo_ref, gffn_ref,
                  wr_ref, br_ref, x1_ref, h2_ref, route_ref, cnt_ref, acc_ref):
    @pl.when(pl.program_id(0) == 0)
    def _():
        acc_ref[...] = jnp.zeros_like(acc_ref)

    x = x_ref[...]
    h = _rms(x, gmix_ref[...]).astype(BF16)
    gates = _sigmoid(_dot(h, wg_ref[...]) + bg_ref[...])
    merged = (gates[:, :D_MODEL] * _dot(ya_ref[...], wa_ref[...])
              + gates[:, D_MODEL:] * _dot(yb_ref[...], wb_ref[...]))
    x1 = x + _dot(merged.astype(BF16), wo_ref[...])
    x1_ref[...] = x1
    h2 = _rms(x1, gffn_ref[...])
    h2_ref[...] = _pack_rows(h2)
    route = _route(_dot(h2.astype(BF16), wr_ref[...]) + br_ref[...])
    route_ref[...] = route
    lane = lax.broadcasted_iota(jnp.int32, route.shape, 1).astype(F32)
    member = jnp.where(lane == route[:, 0:1], 1.0, jnp.where(lane == route[:, 1:2], 1.0, 0.0))
    acc = acc_ref[...] + jnp.sum(member, axis=0, keepdims=True)
    acc_ref[...] = acc
    cnt_ref[...] = acc


def _merge(x2d, ya, yb, g_mix, wg, bg, wa, wb, wo, g_ffn, wr, br, tm=512):
    t = x2d.shape[0]
    row = lambda w: pl.BlockSpec((tm, w), lambda i: (i, 0))
    full = lambda a: pl.BlockSpec(a.shape, lambda i: (0,) * a.ndim)
    once = lambda a: pl.BlockSpec(a.shape, lambda i: (0,) * a.ndim, pipeline_mode=pl.Buffered(1))
    return pl.pallas_call(
        _merge_kernel,
        grid=(t // tm,),
        in_specs=[row(D_MODEL), row(D_SSM), row(ATT_W), full(g_mix), once(wg), full(bg), once(wa), once(wb),
                  once(wo), full(g_ffn), full(wr), full(br)],
        out_specs=[row(D_MODEL), row(PACK_W), row(LANES), pl.BlockSpec((8, LANES), lambda i: (0, 0))],
        out_shape=[jax.ShapeDtypeStruct((t, D_MODEL), F32),
                   jax.ShapeDtypeStruct((t, PACK_W), jnp.uint32),
                   jax.ShapeDtypeStruct((t, LANES), F32),
                   jax.ShapeDtypeStruct((8, LANES), F32)],
        scratch_shapes=[pltpu.VMEM((8, LANES), F32)],
        compiler_params=_cparams(("arbitrary",)),
        name="merge",
    )(x2d, ya, yb, g_mix, wg, bg, wa, wb, wo, g_ffn, wr, br)


def _rank_kernel(route_ref, start_ref, dest_ref, tri_ref, acc_ref):
    i = pl.program_id(0)
    m = route_ref.shape[0]

    @pl.when(i == 0)
    def _():
        r = lax.broadcasted_iota(jnp.int32, (m, m), 0)
        c = lax.broadcasted_iota(jnp.int32, (m, m), 1)
        tri_ref[...] = jnp.where(r > c, 1.0, 0.0).astype(BF16)
        acc_ref[...] = jnp.zeros_like(acc_ref) + start_ref[...]

    rt = route_ref[...]
    lane = lax.broadcasted_iota(jnp.int32, rt.shape, 1).astype(F32)
    e1, e2 = rt[:, 0:1], rt[:, 1:2]
    is1, is2 = lane == e1, lane == e2
    member = jnp.where(is1, 1.0, jnp.where(is2, 1.0, 0.0))
    before = _dot(tri_ref[...], member.astype(BF16)) + acc_ref[0:1, :]
    d1 = jnp.sum(jnp.where(is1, before, 0.0), axis=-1, keepdims=True)
    d2 = jnp.sum(jnp.where(is2, before, 0.0), axis=-1, keepdims=True)
    dest_ref[...] = jnp.where(lane == 0, d1, jnp.where(lane == 1, d2, 0.0)).astype(jnp.int32)
    acc_ref[...] = acc_ref[...] + jnp.sum(member, axis=0, keepdims=True)


def _rank(route, pad_start_row, tr=512):
    t = route.shape[0]
    return pl.pallas_call(
        _rank_kernel,
        grid=(t // tr,),
        in_specs=[pl.BlockSpec((tr, LANES), lambda i: (i, 0)), pl.BlockSpec((1, LANES), lambda i: (0, 0))],
        out_specs=pl.BlockSpec((tr, LANES), lambda i: (i, 0)),
        out_shape=jax.ShapeDtypeStruct((t, LANES), jnp.int32),
        scratch_shapes=[pltpu.VMEM((tr, tr), BF16), pltpu.VMEM((8, LANES), F32)],
        compiler_params=_cparams(("arbitrary",)),
        name="rank",
    )(route, pad_start_row)


def _row_copy(src_ref, dst_ref, sem, src_row, dst_row):
    return pltpu.make_async_copy(src_ref.at[pl.ds(src_row, 1)], dst_ref.at[pl.ds(dst_row, 1)], sem)


def _start_all(copies):
    for k, c in enumerate(copies):
        c.start(priority=k % 2)


def _dispatch_kernel(dest_ref, h2_ref, xd_in, xd_out, sem, *, tr):
    del xd_in
    copies = [_row_copy(h2_ref, xd_out, sem, r % tr, dest_ref[0, 0, r]) for r in range(TOP_K * tr)]
    _start_all(copies)
    for c in copies:
        c.wait()


def _dispatch(dest3, h2w, n_rows):
    nt, _, n = dest3.shape
    tr = n // TOP_K
    xd0 = jnp.zeros((n_rows, PACK_W), jnp.uint32)
    return pl.pallas_call(
        functools.partial(_dispatch_kernel, tr=tr),
        grid=(nt,),
        in_specs=[pl.BlockSpec((1, 1, n), lambda i: (i, 0, 0), memory_space=pltpu.SMEM),
                  pl.BlockSpec((tr, PACK_W), lambda i: (i, 0)),
                  pl.BlockSpec(memory_space=pl.ANY)],
        out_specs=pl.BlockSpec(memory_space=pl.ANY),
        out_shape=jax.ShapeDtypeStruct((n_rows, PACK_W), jnp.uint32),
        scratch_shapes=[pltpu.SemaphoreType.DMA],
        input_output_aliases={2: 0},
        compiler_params=_cparams(("arbitrary",)),
        name="dispatch",
    )(dest3, h2w, xd0)


def _moe_kernel(be_ref, bn_ref, xd_ref, wg_ref, wu_ref, wd_ref, yd_ref, wgb_ref, wub_ref, wdb_ref):
    i = pl.program_id(0)

    @pl.when(bn_ref[i] == 0)
    def _():
        yd_ref[...] = jnp.zeros_like(yd_ref)

    @pl.when(bn_ref[i] > 0)
    def _():
        prev = be_ref[jnp.maximum(i - 1, 0)]

        @pl.when(jnp.logical_or(i == 0, be_ref[i] != prev))
        def _():
            wgb_ref[...] = wg_ref[0].astype(BF16)
            wub_ref[...] = wu_ref[0].astype(BF16)
            wdb_ref[...] = wd_ref[0].astype(BF16)

        x = _unpack_rows(xd_ref[...]).astype(BF16)
        g = _dot(x, wgb_ref[...])
        u = _dot(x, wub_ref[...])
        a = (g * _sigmoid(g) * u).astype(BF16)
        yd_ref[...] = _pack_rows(_dot(a, wdb_ref[...]))


def _moe(blk_e, blk_n, xdw, w_eg, w_eu, w_ed):
    n_blk = blk_e.shape[0]
    grid_spec = pltpu.PrefetchScalarGridSpec(
        num_scalar_prefetch=2,
        grid=(n_blk,),
        in_specs=[pl.BlockSpec((MOE_BLK, PACK_W), lambda i, be, bn: (jnp.where(bn[i] > 0, i, 0), 0)),
                  pl.BlockSpec((1, D_MODEL, D_FF_EXPERT), lambda i, be, bn: (be[i], 0, 0)),
                  pl.BlockSpec((1, D_MODEL, D_FF_EXPERT), lambda i, be, bn: (be[i], 0, 0)),
                  pl.BlockSpec((1, D_FF_EXPERT, D_MODEL), lambda i, be, bn: (be[i], 0, 0))],
        out_specs=pl.BlockSpec((MOE_BLK, PACK_W), lambda i, be, bn: (i, 0)),
        scratch_shapes=[pltpu.VMEM((D_MODEL, D_FF_EXPERT), BF16), pltpu.VMEM((D_MODEL, D_FF_EXPERT), BF16),
                        pltpu.VMEM((D_FF_EXPERT, D_MODEL), BF16)],
    )
    return pl.pallas_call(
        _moe_kernel,
        grid_spec=grid_spec,
        out_shape=jax.ShapeDtypeStruct((n_blk * MOE_BLK, PACK_W), jnp.uint32),
        compiler_params=_cparams(("arbitrary",)),
        name="moe",
    )(blk_e, blk_n, xdw, w_eg, w_eu, w_ed)


def _final_kernel(dcur_ref, dnxt_ref, x1_ref, p_ref, route_ref, yd_hbm, wpp_ref, wpg_ref, bpg_ref, gple_ref,
                  gfin_ref, o_ref, yw0_ref, yw1_ref, sem, *, tm):
    j = pl.program_id(0)
    last = pl.num_programs(0) - 1
    n = TOP_K * tm
    yw = (yw0_ref, yw1_ref)

    def gather(dref, half, slot):
        return [_row_copy(yd_hbm, yw[slot], sem.at[slot], dref[0, half, r], r) for r in range(n)]

    def tile(half, slot):
        rows = slice(half * tm, (half + 1) * tm)
        rt = route_ref[rows, :]
        y0 = _unpack_rows(yw[slot][0:tm, :])
        y1 = _unpack_rows(yw[slot][tm:n, :])
        x2 = x1_ref[rows, :] + rt[:, 2:3] * y0 + rt[:, 3:4] * y1
        h3 = _rms(x2, gple_ref[...]).astype(BF16)
        pp = _dot(p_ref[rows, :].astype(BF16), wpp_ref[...])
        x3 = x2 + pp * _sigmoid(_dot(h3, wpg_ref[...]) + bpg_ref[...])
        o_ref[rows, :] = _rms(x3, gfin_ref[...])

    @pl.when(j == 0)
    def _():
        _start_all(gather(dcur_ref, 0, 0))

    for c in gather(dcur_ref, 0, 0):
        c.wait()
    _start_all(gather(dcur_ref, 1, 1))
    tile(0, 0)
    for c in gather(dcur_ref, 1, 1):
        c.wait()
    _start_all(gather(dnxt_ref, 0, 0))
    tile(1, 1)

    @pl.when(j == last)
    def _():
        for c in gather(dnxt_ref, 0, 0):
            c.wait()


def _final(dest4, x1, p2d, route, ydw, wpp, wpg, bpg, g_ple, g_final):
    t = x1.shape[0]
    ns, _, n = dest4.shape
    tm = n // TOP_K
    row = lambda w: pl.BlockSpec((2 * tm, w), lambda i: (i, 0))
    full = lambda a: pl.BlockSpec(a.shape, lambda i: (0,) * a.ndim)
    return pl.pallas_call(
        functools.partial(_final_kernel, tm=tm),
        grid=(ns,),
        in_specs=[pl.BlockSpec((1, 2, n), lambda i: (i, 0, 0), memory_space=pltpu.SMEM),
                  pl.BlockSpec((1, 2, n), lambda i: (jnp.minimum(i + 1, ns - 1), 0, 0), memory_space=pltpu.SMEM),
                  row(D_MODEL), row(PLE_DIM), row(LANES), pl.BlockSpec(memory_space=pl.ANY),
                  full(wpp), full(wpg), full(bpg), full(g_ple), full(g_final)],
        out_specs=row(D_MODEL),
        out_shape=jax.ShapeDtypeStruct((t, D_MODEL), F32),
        scratch_shapes=[pltpu.VMEM((n, PACK_W), jnp.uint32), pltpu.VMEM((n, PACK_W), jnp.uint32),
                        pltpu.SemaphoreType.DMA((2,))],
        compiler_params=_cparams(("arbitrary",)),
        name="final",
    )(dest4, dest4, x1, p2d, route, ydw, wpp, wpg, bpg, g_ple, g_final)


def _block_plan(counts_f, n_blk):
    counts = counts_f[0, :N_EXPERTS].astype(jnp.int32)
    padded = ((counts + MOE_BLK - 1) // MOE_BLK) * MOE_BLK
    pad_end = jnp.cumsum(padded)
    pad_start = pad_end - padded
    blk_start = jnp.arange(n_blk, dtype=jnp.int32) * MOE_BLK
    blk_e = jnp.minimum(jnp.sum(pad_end[None, :] <= blk_start[:, None], axis=1), N_EXPERTS - 1).astype(jnp.int32)
    onehot = blk_e[:, None] == jnp.arange(N_EXPERTS, dtype=jnp.int32)[None, :]
    seg_end = jnp.sum(jnp.where(onehot, (pad_start + counts)[None, :], 0), axis=1)
    blk_n = jnp.clip(seg_end - blk_start, 0, MOE_BLK).astype(jnp.int32)
    start_row = jnp.zeros((1, LANES), F32).at[0, :N_EXPERTS].set(pad_start.astype(F32))
    return start_row, blk_e, blk_n


def _row(v):
    return v.reshape(1, -1).astype(F32)


def kernel(x, p, rel_bias, g_mix, w_in, w_gate, b_gate, ssm_a_re, ssm_a_im, ssm_log_dt, ssm_b_re, ssm_b_im,
           ssm_c_re, ssm_c_im, ssm_d, w_glu, b_glu, sinks, w_br_ssm, w_br_attn, w_out, g_ffn, w_router_group,
           b_router_group, w_router_expert, b_router_expert, w_e_gate, w_e_up, w_e_down, g_ple, w_ple_gate,
           b_ple_gate, w_ple_proj, g_final):
    bsz, seq, dm = x.shape
    assert g_mix.shape[0] == 1, "one layer followed by the final norm"
    i = 0
    t = bsz * seq
    x2d = x.reshape(t, dm)
    bias = _bias_table(rel_bias)
    tm_rows = 256
    u, q, kv = _proj(x2d, _row(g_mix[i]), w_in[i].astype(BF16))
    ab_re, ab_im, bb_re, bb_im = _ssmprep(ssm_a_re[i], ssm_a_im[i], ssm_log_dt[i], ssm_b_re[i], ssm_b_im[i])
    ws_in, ws_out, ar, ai = _ssm_weights(ab_re, ab_im, bb_re, bb_im, ssm_c_re[i], ssm_c_im[i])
    y_ssm = _ssm(u.reshape(bsz, seq, D_SSM), ws_in, ws_out, ar, ai, _row(ssm_d[i]),
                 w_glu[i].astype(BF16), _row(b_glu[i]))
    y_att = _attn(q.reshape(bsz, seq, ATT_W), kv.reshape(bsz, seq, KV_W), bias, sinks[i].astype(F32))
    w_router = jnp.zeros((dm, LANES), F32)
    w_router = w_router.at[:, :N_GROUPS].set(w_router_group[i])
    w_router = w_router.at[:, N_GROUPS:N_GROUPS + N_EXPERTS].set(w_router_expert[i])
    b_router = jnp.zeros((1, LANES), F32)
    b_router = b_router.at[0, :N_GROUPS].set(b_router_group[i])
    b_router = b_router.at[0, N_GROUPS:N_GROUPS + N_EXPERTS].set(b_router_expert[i])
    x1, h2w, route, counts = _merge(x2d, y_ssm.reshape(t, D_SSM), y_att.reshape(t, ATT_W), _row(g_mix[i]),
                                    w_gate[i].astype(BF16), _row(b_gate[i]), w_br_ssm[i].astype(BF16),
                                    w_br_attn[i].astype(BF16), w_out[i].astype(BF16), _row(g_ffn[i]),
                                    w_router.astype(BF16), b_router)
    n_rows = t * TOP_K + N_EXPERTS * MOE_BLK
    start_row, blk_e, blk_n = _block_plan(counts, n_rows // MOE_BLK)
    dest = _rank(route, start_row)[:, :TOP_K]
    dest3 = jnp.transpose(dest.reshape(t // tm_rows, tm_rows, TOP_K), (0, 2, 1)).reshape(t // tm_rows, 1, -1)
    xdw = _dispatch(dest3, h2w, n_rows)
    ydw = _moe(blk_e, blk_n, xdw, w_e_gate[i], w_e_up[i], w_e_down[i])
    out = _final(dest3.reshape(t // (2 * tm_rows), 2, -1), x1, p[i].reshape(t, PLE_DIM), route, ydw,
                 w_ple_proj[i].astype(BF16), w_ple_gate[i].astype(BF16), _row(b_ple_gate[i]), _row(g_ple[i]),
                 _row(g_final))
    return out.reshape(bsz, seq, dm)
```

```python
import functools
import math

import numpy as np
import jax
import jax.numpy as jnp
from jax import lax
from jax.experimental import pallas as pl
from jax.experimental.pallas import tpu as pltpu

D_MODEL = 1024
D_SSM = 512
SSM_CH = 16
SSM_GROUPS = D_SSM // SSM_CH
SSM_STATE = 64
N_STATE = SSM_GROUPS * SSM_STATE
N_HEADS = 8
N_KV = 2
HEAD_DIM = 64
GROUP = N_HEADS // N_KV
ATT_W = N_HEADS * HEAD_DIM
KV_W = 2 * N_KV * HEAD_DIM
WINDOW = 128
BLOCK = 128
NUM_BUCKETS = 32
MAX_DIST = 128
D_IN = D_SSM + ATT_W + KV_W
N_GROUPS = 4
EXPERTS_PER_GROUP = 8
N_EXPERTS = N_GROUPS * EXPERTS_PER_GROUP
TOP_K = 2
D_FF_EXPERT = 512
MOE_BLK = 256
PLE_DIM = 256
EPS = 1e-6

LANES = 128
NEG = -1e30
PACK_W = D_MODEL // 2

BF16 = jnp.bfloat16
F32 = jnp.float32
VMEM_LIMIT = 56 * 1024 * 1024


def _cparams(sem):
    return pltpu.CompilerParams(dimension_semantics=sem, vmem_limit_bytes=VMEM_LIMIT)


def _rms(x, g):
    ms = jnp.mean(x * x, axis=-1, keepdims=True)
    return x * lax.rsqrt(ms + EPS) * g


def _sigmoid(x):
    return 1.0 / (1.0 + jnp.exp(-x))


def _dot(a, b):
    return jnp.dot(a, b, preferred_element_type=F32)


def _pack_rows(y):
    lo = pltpu.bitcast(y[:, :PACK_W].astype(BF16).astype(F32), jnp.uint32)
    hi = pltpu.bitcast(y[:, PACK_W:].astype(BF16).astype(F32), jnp.uint32)
    return hi | lax.shift_right_logical(lo, jnp.uint32(16))


def _unpack_rows(w):
    lo = pltpu.bitcast(lax.shift_left(w, jnp.uint32(16)), F32)
    hi = pltpu.bitcast(w & jnp.uint32(0xFFFF0000), F32)
    return jnp.concatenate([lo, hi], axis=1)


def _proj_kernel(x_ref, g_ref, w_ref, u_ref, q_ref, kv_ref):
    h = _rms(x_ref[...], g_ref[...]).astype(BF16)
    proj = _dot(h, w_ref[...])
    u_ref[...] = proj[:, :D_SSM].astype(BF16)
    q_ref[...] = (proj[:, D_SSM:D_SSM + ATT_W] * (HEAD_DIM ** -0.5)).astype(BF16)
    kv_ref[...] = proj[:, D_SSM + ATT_W:].astype(BF16)


def _proj(x2d, g_mix, w_in_b, tm=512):
    t = x2d.shape[0]
    row = lambda w: pl.BlockSpec((tm, w), lambda i: (i, 0))
    full = lambda a: pl.BlockSpec(a.shape, lambda i: (0,) * a.ndim)
    return pl.pallas_call(
        _proj_kernel,
        grid=(t // tm,),
        in_specs=[row(D_MODEL), full(g_mix), full(w_in_b)],
        out_specs=[row(D_SSM), row(ATT_W), row(KV_W)],
        out_shape=[jax.ShapeDtypeStruct((t, D_SSM), BF16),
                   jax.ShapeDtypeStruct((t, ATT_W), BF16),
                   jax.ShapeDtypeStruct((t, KV_W), BF16)],
        compiler_params=_cparams(("parallel",)),
        name="proj",
    )(x2d, g_mix, w_in_b)


def _ssmprep_kernel(are_ref, aim_ref, ldt_ref, bre_ref, bim_ref, abr_ref, abi_ref, bbr_ref, bbi_ref):
    a_re, a_im = are_ref[...], aim_ref[...]
    dt = jnp.exp(ldt_ref[...])
    mag = jnp.exp(a_re * dt)
    ab_re = mag * jnp.cos(a_im * dt)
    ab_im = mag * jnp.sin(a_im * dt)
    abr_ref[...] = ab_re
    abi_ref[...] = ab_im
    den = a_re * a_re + a_im * a_im
    c_re = ((ab_re - 1.0) * a_re + ab_im * a_im) / den
    c_im = (ab_im * a_re - (ab_re - 1.0) * a_im) / den
    for c in range(SSM_CH):
        b_re, b_im = bre_ref[c], bim_ref[c]
        bbr_ref[c] = c_re * b_re - c_im * b_im
        bbi_ref[c] = c_re * b_im + c_im * b_re


def _ssmprep(a_re, a_im, log_dt, b_re, b_im):
    g, p = a_re.shape
    bt_re = jnp.transpose(b_re, (2, 0, 1))
    bt_im = jnp.transpose(b_im, (2, 0, 1))
    gp = jax.ShapeDtypeStruct((g, p), F32)
    cgp = jax.ShapeDtypeStruct((SSM_CH, g, p), F32)
    return pl.pallas_call(
        _ssmprep_kernel, out_shape=[gp, gp, cgp, cgp], name="ssmprep",
    )(a_re, a_im, log_dt.reshape(g, 1), bt_re, bt_im)


def _ssm_weights(ab_re, ab_im, bb_re, bb_im, c_re, c_im):
    g, p = ab_re.shape
    kb = D_SSM // 256
    gk = g // kb
    eye = jnp.eye(gk, dtype=F32)

    def in_block(bb):
        b4 = jnp.transpose(bb, (1, 0, 2)).reshape(kb, gk, SSM_CH, p)
        return jnp.einsum("kgcp,gh->kgchp", b4, eye).reshape(kb, gk * SSM_CH, gk * p)

    w_in = jnp.concatenate([in_block(bb_re), in_block(bb_im)], axis=2).astype(BF16)
    nj = D_SSM // LANES
    gj = g // nj
    eyej = jnp.eye(gj, dtype=F32)

    def out_block(c):
        c4 = jnp.transpose(c, (0, 2, 1)).reshape(nj, gj, p, SSM_CH)
        return jnp.einsum("jgpc,gh->jgphc", c4, eyej).reshape(nj, gj * p, gj * SSM_CH)

    w_out = jnp.stack([out_block(c_re), out_block(-c_im)], axis=1).astype(BF16)
    return w_in, w_out, ab_re.reshape(1, g * p), ab_im.reshape(1, g * p)


def _gelu_tanh(x):
    return 0.5 * x * (1.0 + jnp.tanh(math.sqrt(2.0 / math.pi) * (x + 0.044715 * (x * x * x))))


def _ssm_kernel(u_ref, win_ref, wout_ref, ar_ref, ai_ref, d_ref, wglu_ref, bglu_ref, y_ref,
                perm_ref, permt_ref, bu_ref, xb_ref, st_ref, *, nb, lc):
    rows = nb * lc
    ns = N_STATE
    i = pl.program_id(0)

    @pl.when(i == 0)
    def _():
        r = lax.broadcasted_iota(jnp.int32, (rows, rows), 0)
        c = lax.broadcasted_iota(jnp.int32, (rows, rows), 1)
        sh = nb.bit_length() - 1
        perm_ref[...] = jnp.where(c == (r & (nb - 1)) * lc + (r >> sh), 1.0, 0.0).astype(BF16)
        permt_ref[...] = jnp.where(r == (c & (nb - 1)) * lc + (c >> sh), 1.0, 0.0).astype(BF16)
        st_ref[...] = jnp.zeros_like(st_ref)

    u = u_ref[...].reshape(rows, D_SSM)
    up = _dot(perm_ref[...], u).astype(BF16)
    kb = win_ref.shape[0]
    half = ns // kb
    for k in range(kb):
        res = _dot(up[:, 256 * k:256 * (k + 1)], win_ref[k])
        bu_ref[:, half * k:half * (k + 1)] = res[:, :half]
        bu_ref[:, ns + half * k:ns + half * (k + 1)] = res[:, half:]

    ar, ai = ar_ref[...], ai_ref[...]

    def body(j, carry):
        xr, xi = carry
        keep_r, keep_i = [], []
        for s in range(2):
            row = pl.multiple_of((2 * j + s) * nb, nb)
            br = bu_ref[pl.ds(row, nb), 0:ns]
            bi = bu_ref[pl.ds(row, nb), ns:2 * ns]
            xr, xi = ar * xr - ai * xi + br, ar * xi + ai * xr + bi
            keep_r.append(xr)
            keep_i.append(xi)
        row2 = pl.multiple_of(j * 2 * nb, 2 * nb)
        xb_ref[pl.ds(row2, 2 * nb), 0:ns] = jnp.concatenate(keep_r, axis=0).astype(BF16)
        xb_ref[pl.ds(row2, 2 * nb), ns:2 * ns] = jnp.concatenate(keep_i, axis=0).astype(BF16)
        return xr, xi

    xr, xi = lax.fori_loop(0, lc // 2, body, (st_ref[:, 0:ns], st_ref[:, ns:2 * ns]))
    st_ref[:, 0:ns] = xr
    st_ref[:, ns:2 * ns] = xi

    nj = wout_ref.shape[0]
    kw = ns // nj
    ys = []
    for j in range(nj):
        ys.append(_dot(xb_ref[:, kw * j:kw * (j + 1)], wout_ref[j, 0])
                  + _dot(xb_ref[:, ns + kw * j:ns + kw * (j + 1)], wout_ref[j, 1]))
    y_tb = jnp.concatenate(ys, axis=1).astype(BF16)
    y = _dot(permt_ref[...], y_tb) + d_ref[...] * u.astype(F32)
    y = _gelu_tanh(y)
    z = _dot(y.astype(BF16), wglu_ref[...]) + bglu_ref[...]
    y_ref[...] = (y * _sigmoid(z)).astype(BF16).reshape(nb, lc, D_SSM)


def _ssm(u3, w_in, w_out, ar, ai, d_skip, w_glu_b, b_glu, lc=64):
    nb, s, _ = u3.shape
    assert nb & (nb - 1) == 0 and s % lc == 0 and lc % 16 == 0
    rows = nb * lc
    full = lambda a: pl.BlockSpec(a.shape, lambda i: (0,) * a.ndim)
    blk = pl.BlockSpec((nb, lc, D_SSM), lambda i: (0, i, 0))
    return pl.pallas_call(
        functools.partial(_ssm_kernel, nb=nb, lc=lc),
        grid=(s // lc,),
        in_specs=[blk, full(w_in), full(w_out), full(ar), full(ai), full(d_skip), full(w_glu_b), full(b_glu)],
        out_specs=blk,
        out_shape=jax.ShapeDtypeStruct((nb, s, D_SSM), BF16),
        scratch_shapes=[pltpu.VMEM((rows, rows), BF16), pltpu.VMEM((rows, rows), BF16),
                        pltpu.VMEM((rows, 2 * N_STATE), F32), pltpu.VMEM((rows, 2 * N_STATE), BF16),
                        pltpu.VMEM((nb, 2 * N_STATE), F32)],
        compiler_params=_cparams(("arbitrary",)),
        name="ssm",
    )(u3, w_in, w_out, ar, ai, d_skip, w_glu_b, b_glu)


def _t5_bucket_np(rel):
    max_exact = NUM_BUCKETS // 2
    relf = np.maximum(rel, 1).astype(np.float32)
    large = max_exact + (np.log(relf / np.float32(max_exact)) / np.float32(math.log(MAX_DIST / max_exact))
                         * np.float32(NUM_BUCKETS - max_exact)).astype(np.int32)
    large = np.minimum(large, NUM_BUCKETS - 1)
    return np.where(rel < max_exact, rel, large)


def _bias_table(rel_bias):
    q_loc = np.arange(BLOCK)[:, None]
    c_loc = np.arange(2 * BLOCK)[None, :]
    rel = q_loc + BLOCK - c_loc
    valid = (rel >= 0) & (rel < WINDOW)
    bucket = _t5_bucket_np(np.maximum(rel, 0)).reshape(-1, 1)
    onehot = (jnp.asarray(bucket) == jnp.arange(NUM_BUCKETS)[None, :]).astype(F32)
    bias = jnp.dot(onehot, rel_bias.astype(F32), precision=lax.Precision.HIGHEST)
    bias = jnp.where(valid.reshape(-1, 1), bias, NEG)
    return jnp.transpose(bias, (1, 0)).reshape(N_HEADS, BLOCK, 2 * BLOCK)


def _attn_kernel(sink_ref, q_ref, kvp_ref, kvc_ref, bias_ref, o_ref, *, nq):
    n = pl.program_id(1)
    kv_all = jnp.concatenate([kvp_ref[0], kvc_ref[0]], axis=0)
    col = lax.broadcasted_iota(jnp.int32, (BLOCK, 2 * BLOCK), 1)
    no_prev = jnp.where(col < BLOCK, jnp.where(n == 0, NEG, 0.0), 0.0)
    for j in range(nq):
        q = q_ref[0, BLOCK * j:BLOCK * (j + 1), :]
        kv = kv_all[BLOCK * j:BLOCK * (j + 2), :]
        outs = []
        for h in range(N_HEADS):
            g = h // GROUP
            qh = q[:, HEAD_DIM * h:HEAD_DIM * (h + 1)]
            kh = kv[:, HEAD_DIM * g:HEAD_DIM * (g + 1)]
            vh = kv[:, N_KV * HEAD_DIM + HEAD_DIM * g:N_KV * HEAD_DIM + HEAD_DIM * (g + 1)]
            s = lax.dot_general(qh, kh, (((1,), (1,)), ((), ())), preferred_element_type=F32)
            s = s + bias_ref[h]
            if j == 0:
                s = s + no_prev
            sink = sink_ref[h]
            m = jnp.maximum(jnp.max(s, axis=-1, keepdims=True), sink)
            e = jnp.exp(s - m)
            den = jnp.sum(e, axis=-1, keepdims=True) + jnp.exp(sink - m)
            outs.append(_dot(e.astype(BF16), vh) / den)
        o_ref[0, BLOCK * j:BLOCK * (j + 1), :] = jnp.concatenate(outs, axis=1).astype(BF16)


def _attn(q3, kv3, bias, sinks, nq=4):
    b, s, _ = q3.shape
    qb = nq * BLOCK
    grid_spec = pltpu.PrefetchScalarGridSpec(
        num_scalar_prefetch=0,
        grid=(b, s // qb),
        in_specs=[pl.BlockSpec(memory_space=pltpu.SMEM),
                  pl.BlockSpec((1, qb, ATT_W), lambda i, n: (i, n, 0)),
                  pl.BlockSpec((1, BLOCK, KV_W), lambda i, n: (i, jnp.maximum(n * nq - 1, 0), 0)),
                  pl.BlockSpec((1, qb, KV_W), lambda i, n: (i, n, 0)),
                  pl.BlockSpec(bias.shape, lambda i, n: (0, 0, 0))],
        out_specs=pl.BlockSpec((1, qb, ATT_W), lambda i, n: (i, n, 0)),
    )
    return pl.pallas_call(
        functools.partial(_attn_kernel, nq=nq),
        grid_spec=grid_spec,
        out_shape=jax.ShapeDtypeStruct((b, s, ATT_W), BF16),
        compiler_params=_cparams(("parallel", "parallel")),
        name="attn",
    )(sinks, q3, kv3, kv3, bias)


def _route(logits):
    lane = lax.broadcasted_iota(jnp.int32, logits.shape, 1).astype(F32)
    big = float(LANES)
    gl = jnp.where(lane < N_GROUPS, logits, NEG)
    mg = jnp.max(gl, axis=-1, keepdims=True)
    gidx = jnp.min(jnp.where(gl == mg, lane, big), axis=-1, keepdims=True)
    gsum = jnp.sum(jnp.where(lane < N_GROUPS, jnp.exp(gl - mg), 0.0), axis=-1, keepdims=True)
    gp = 1.0 / gsum
    lo = N_GROUPS + EXPERTS_PER_GROUP * gidx
    el = jnp.where(lane >= lo, jnp.where(lane < lo + EXPERTS_PER_GROUP, logits, NEG), NEG)
    m1 = jnp.max(el, axis=-1, keepdims=True)
    i1 = jnp.min(jnp.where(el == m1, lane, big), axis=-1, keepdims=True)
    el2 = jnp.where(lane == i1, NEG, el)
    m2 = jnp.max(el2, axis=-1, keepdims=True)
    i2 = jnp.min(jnp.where(el2 == m2, lane, big), axis=-1, keepdims=True)
    t = jnp.exp(m2 - m1)
    w1 = gp / (1.0 + t)
    w2 = gp * t / (1.0 + t)
    return jnp.where(lane == 0, i1 - N_GROUPS,
                     jnp.where(lane == 1, i2 - N_GROUPS,
                               jnp.where(lane == 2, w1, jnp.where(lane == 3, w2, 0.0))))


def _row_copy(src_ref, dst_ref, sem, src_row, dst_row):
    return pltpu.make_async_copy(src_ref.at[pl.ds(src_row, 1)], dst_ref.at[pl.ds(dst_row, 1)], sem)


def _start_all(copies):
    for k, c in enumerate(copies):
        c.start(priority=k % 2)


def _merge_kernel(x_ref, ya_ref, yb_ref, gmix_ref, wg_ref, bg_ref, wa_ref, wb_ref, wo_ref, gffn_ref, wr_ref, br_ref,
                  x1_ref, route_ref, dest_ref, alloc_ref, state_ref, xd_hbm,
                  tri_ref, pre_ref, st_ref, hbuf_ref, destv_ref, dests_ref, stv_ref, sts_ref, zblk_ref,
                  sem_rows, sem_idx, sem_pad, *, tm, n_blk):
    i = pl.program_id(0)
    last = pl.num_programs(0) - 1
    slot = lax.rem(i, 2)
    prev = 1 - slot
    n = TOP_K * tm
    dump0 = n_blk * MOE_BLK

    def row_copies(s):
        return [_row_copy(hbuf_ref.at[s], xd_hbm, sem_rows.at[s], r % tm, dests_ref[s, r // tm, r % tm])
                for r in range(n)]

    def wait_rows(s):
        for _ in range(n):
            _row_copy(hbuf_ref.at[s], xd_hbm, sem_rows.at[s], 0, dump0).wait()

    def index_copy(s):
        return pltpu.make_async_copy(destv_ref.at[s], dests_ref.at[s], sem_idx.at[s])

    @pl.when(i == 0)
    def _():
        r = lax.broadcasted_iota(jnp.int32, (tm, tm), 0)
        c = lax.broadcasted_iota(jnp.int32, (tm, tm), 1)
        tri_ref[...] = jnp.where(r > c, 1.0, 0.0).astype(BF16)
        r = lax.broadcasted_iota(jnp.int32, (LANES, LANES), 0)
        c = lax.broadcasted_iota(jnp.int32, (LANES, LANES), 1)
        pre_ref[...] = jnp.where(r < c, 1.0, 0.0).astype(BF16)
        r = lax.broadcasted_iota(jnp.int32, (8, LANES), 0)
        st_ref[...] = jnp.where(r == 0, float(MOE_BLK), 0.0)
        hbuf_ref[1] = jnp.zeros((tm, PACK_W), jnp.uint32)

        def spare(t, carry):
            dests_ref[1, 0, t] = dump0 + t
            dests_ref[1, 1, t] = dump0 + tm + t
            return carry

        lax.fori_loop(0, tm, spare, 0)

    @pl.when(i >= 1)
    def _():
        wait_rows(slot)
        index_copy(prev).wait()

    _start_all(row_copies(prev))

    x = x_ref[...]
    h = _rms(x, gmix_ref[...]).astype(BF16)
    gates = _sigmoid(_dot(h, wg_ref[...]) + bg_ref[...])
    merged = (gates[:, :D_MODEL] * _dot(ya_ref[...], wa_ref[...])
              + gates[:, D_MODEL:] * _dot(yb_ref[...], wb_ref[...]))
    x1 = x + _dot(merged.astype(BF16), wo_ref[...])
    x1_ref[...] = x1
    h2 = _rms(x1, gffn_ref[...])
    hbuf_ref[slot] = _pack_rows(h2)
    route = _route(_dot(h2.astype(BF16), wr_ref[...]) + br_ref[...])
    route_ref[...] = route

    lane = lax.broadcasted_iota(jnp.int32, route.shape, 1).astype(F32)
    e1, e2 = route[:, 0:1], route[:, 1:2]
    is1, is2 = lane == e1, lane == e2
    member = jnp.where(is1, 1.0, jnp.where(is2, 1.0, 0.0))
    before = _dot(tri_ref[...], member.astype(BF16))
    cnt = jnp.sum(member, axis=0, keepdims=True)
    fill, blk, free = st_ref[0:1, :], st_ref[1:2, :], st_ref[2:3, :]
    need = fill + cnt
    new = jnp.floor((need + float(MOE_BLK - 1)) * (1.0 / MOE_BLK)) - 1.0
    base = free + _dot(jnp.broadcast_to(new, (8, LANES)).astype(BF16), pre_ref[...])[0:1, :]
    q = fill + before
    jb = jnp.floor(q * (1.0 / MOE_BLK))
    rowid = jnp.where(jb == 0.0, blk, base + jb - 1.0) * float(MOE_BLK) + (q - jb * float(MOE_BLK))
    d1 = jnp.sum(jnp.where(is1, rowid, 0.0), axis=-1, keepdims=True)
    d2 = jnp.sum(jnp.where(is2, rowid, 0.0), axis=-1, keepdims=True)
    slab = jnp.where(lane == 0, d1, jnp.where(lane == 1, d2, 0.0))
    d8 = jnp.transpose(slab, (1, 0))[0:8, :].astype(jnp.int32)
    destv_ref[slot] = d8
    dest_ref[0] = d8
    index_copy(slot).start()

    r8 = lax.broadcasted_iota(jnp.int32, (8, LANES), 0)
    alloc_ref[0] = jnp.where(r8 == 0, new, jnp.where(r8 == 1, base, 0.0))
    state = jnp.where(r8 == 0, need - new * float(MOE_BLK),
                      jnp.where(r8 == 1, jnp.where(new > 0.0, base + new - 1.0, blk),
                                jnp.where(r8 == 2, free + jnp.sum(new, axis=-1, keepdims=True), 0.0)))
    st_ref[...] = state
    state_ref[...] = state

    @pl.when(i == last)
    def _():
        index_copy(slot).wait()
        _start_all(row_copies(slot))
        wait_rows(prev)
        wait_rows(slot)
        stv_ref[...] = state.astype(jnp.int32)
        cp = pltpu.make_async_copy(stv_ref, sts_ref, sem_pad)
        cp.start()
        cp.wait()
        zblk_ref[...] = jnp.zeros_like(zblk_ref)

        def zero_rows(e, carry):
            first = sts_ref[1, e] * MOE_BLK

            def start(r, c):
                _row_copy(zblk_ref, xd_hbm, sem_pad, 0, first + r).start()
                return c

            def wait(r, c):
                _row_copy(zblk_ref, xd_hbm, sem_pad, 0, first + r).wait()
                return c

            lax.fori_loop(sts_ref[0, e], MOE_BLK, start, 0)
            lax.fori_loop(sts_ref[0, e], MOE_BLK, wait, 0)
            return carry

        lax.fori_loop(0, N_EXPERTS, zero_rows, 0)

        def block_copy(b):
            return pltpu.make_async_copy(zblk_ref, xd_hbm.at[pl.ds(pl.multiple_of(b * MOE_BLK, MOE_BLK), MOE_BLK)],
                                         sem_pad)

        def zero_block(b, c):
            block_copy(b).start()
            block_copy(b).wait()
            return c

        lax.fori_loop(sts_ref[2, 0], n_blk, zero_block, 0)


def _merge(x2d, ya, yb, g_mix, wg, bg, wa, wb, wo, g_ffn, wr, br, n_blk, tm=512):
    t = x2d.shape[0]
    nt = t // tm
    n_rows = n_blk * MOE_BLK + TOP_K * tm
    row = lambda w: pl.BlockSpec((tm, w), lambda i: (i, 0))
    full = lambda a: pl.BlockSpec(a.shape, lambda i: (0,) * a.ndim)
    once = lambda a: pl.BlockSpec(a.shape, lambda i: (0,) * a.ndim, pipeline_mode=pl.Buffered(1))
    return pl.pallas_call(
        functools.partial(_merge_kernel, tm=tm, n_blk=n_blk),
        grid=(nt,),
        in_specs=[row(D_MODEL), row(D_SSM), row(ATT_W), full(g_mix), once(wg), full(bg), once(wa), once(wb),
                  once(wo), full(g_ffn), full(wr), full(br)],
        out_specs=[row(D_MODEL), row(LANES), pl.BlockSpec((1, 8, tm), lambda i: (i, 0, 0)),
                   pl.BlockSpec((1, 8, LANES), lambda i: (i, 0, 0)), pl.BlockSpec((8, LANES), lambda i: (0, 0)),
                   pl.BlockSpec(memory_space=pl.ANY)],
        out_shape=[jax.ShapeDtypeStruct((t, D_MODEL), F32),
                   jax.ShapeDtypeStruct((t, LANES), F32),
                   jax.ShapeDtypeStruct((nt, 8, tm), jnp.int32),
                   jax.ShapeDtypeStruct((nt, 8, LANES), F32),
                   jax.ShapeDtypeStruct((8, LANES), F32),
                   jax.ShapeDtypeStruct((n_rows, PACK_W), jnp.uint32)],
        scratch_shapes=[pltpu.VMEM((tm, tm), BF16), pltpu.VMEM((LANES, LANES), BF16), pltpu.VMEM((8, LANES), F32),
                        pltpu.VMEM((2, tm, PACK_W), jnp.uint32), pltpu.VMEM((2, 8, tm), jnp.int32),
                        pltpu.SMEM((2, 8, tm), jnp.int32), pltpu.VMEM((8, LANES), jnp.int32),
                        pltpu.SMEM((8, LANES), jnp.int32), pltpu.VMEM((MOE_BLK, PACK_W), jnp.uint32),
                        pltpu.SemaphoreType.DMA((2,)), pltpu.SemaphoreType.DMA((2,)), pltpu.SemaphoreType.DMA],
        compiler_params=_cparams(("arbitrary",)),
        name="merge",
    )(x2d, ya, yb, g_mix, wg, bg, wa, wb, wo, g_ffn, wr, br)


def _moe_kernel(be_ref, bn_ref, bi_ref, xd_ref, wg_ref, wu_ref, wd_ref, yd_ref, wgb_ref, wub_ref, wdb_ref):
    del bi_ref
    i = pl.program_id(0)

    @pl.when(bn_ref[i] == 0)
    def _():
        yd_ref[...] = jnp.zeros_like(yd_ref)

    @pl.when(bn_ref[i] > 0)
    def _():
        prev = be_ref[jnp.maximum(i - 1, 0)]

        @pl.when(jnp.logical_or(i == 0, be_ref[i] != prev))
        def _():
            wgb_ref[...] = wg_ref[0].astype(BF16)
            wub_ref[...] = wu_ref[0].astype(BF16)
            wdb_ref[...] = wd_ref[0].astype(BF16)

        x = _unpack_rows(xd_ref[...]).astype(BF16)
        g = _dot(x, wgb_ref[...])
        u = _dot(x, wub_ref[...])
        a = (g * _sigmoid(g) * u).astype(BF16)
        yd_ref[...] = _pack_rows(_dot(a, wdb_ref[...]))


def _moe(blk_e, blk_n, blk_i, xdw, w_eg, w_eu, w_ed):
    n_blk = blk_e.shape[0]
    grid_spec = pltpu.PrefetchScalarGridSpec(
        num_scalar_prefetch=3,
        grid=(n_blk,),
        in_specs=[pl.BlockSpec((MOE_BLK, PACK_W), lambda i, be, bn, bi: (bi[i], 0)),
                  pl.BlockSpec((1, D_MODEL, D_FF_EXPERT), lambda i, be, bn, bi: (be[i], 0, 0)),
                  pl.BlockSpec((1, D_MODEL, D_FF_EXPERT), lambda i, be, bn, bi: (be[i], 0, 0)),
                  pl.BlockSpec((1, D_FF_EXPERT, D_MODEL), lambda i, be, bn, bi: (be[i], 0, 0))],
        out_specs=pl.BlockSpec((MOE_BLK, PACK_W), lambda i, be, bn, bi: (bi[i], 0)),
        scratch_shapes=[pltpu.VMEM((D_MODEL, D_FF_EXPERT), BF16), pltpu.VMEM((D_MODEL, D_FF_EXPERT), BF16),
                        pltpu.VMEM((D_FF_EXPERT, D_MODEL), BF16)],
    )
    return pl.pallas_call(
        _moe_kernel,
        grid_spec=grid_spec,
        out_shape=jax.ShapeDtypeStruct((n_blk * MOE_BLK, PACK_W), jnp.uint32),
        compiler_params=_cparams(("arbitrary",)),
        name="moe",
    )(blk_e, blk_n, blk_i, xdw, w_eg, w_eu, w_ed)


def _final_kernel(dcur_ref, dnxt_ref, x1_ref, p_ref, route_ref, yd_hbm, wpp_ref, wpg_ref, bpg_ref, gple_ref,
                  gfin_ref, o_ref, yw0_ref, yw1_ref, sem, *, tm):
    j = pl.program_id(0)
    last = pl.num_programs(0) - 1
    n = TOP_K * tm
    yw = (yw0_ref, yw1_ref)

    def gather(dref, half, slot):
        return [_row_copy(yd_hbm, yw[slot], sem.at[slot], dref[0, r // tm, half * tm + r % tm], r)
                for r in range(n)]

    def tile(half, slot):
        rows = slice(half * tm, (half + 1) * tm)
        rt = route_ref[rows, :]
        y0 = _unpack_rows(yw[slot][0:tm, :])
        y1 = _unpack_rows(yw[slot][tm:n, :])
        x2 = x1_ref[rows, :] + rt[:, 2:3] * y0 + rt[:, 3:4] * y1
        h3 = _rms(x2, gple_ref[...]).astype(BF16)
        pp = _dot(p_ref[rows, :].astype(BF16), wpp_ref[...])
        x3 = x2 + pp * _sigmoid(_dot(h3, wpg_ref[...]) + bpg_ref[...])
        o_ref[rows, :] = _rms(x3, gfin_ref[...])

    @pl.when(j == 0)
    def _():
        _start_all(gather(dcur_ref, 0, 0))

    for c in gather(dcur_ref, 0, 0):
        c.wait()
    _start_all(gather(dcur_ref, 1, 1))
    tile(0, 0)
    for c in gather(dcur_ref, 1, 1):
        c.wait()
    _start_all(gather(dnxt_ref, 0, 0))
    tile(1, 1)

    @pl.when(j == last)
    def _():
        for c in gather(dnxt_ref, 0, 0):
            c.wait()


def _final(dest, x1, p2d, route, ydw, wpp, wpg, bpg, g_ple, g_final):
    t = x1.shape[0]
    ns, _, two_tm = dest.shape
    tm = two_tm // 2
    n = TOP_K * tm
    row = lambda w: pl.BlockSpec((2 * tm, w), lambda i: (i, 0))
    full = lambda a: pl.BlockSpec(a.shape, lambda i: (0,) * a.ndim)
    return pl.pallas_call(
        functools.partial(_final_kernel, tm=tm),
        grid=(ns,),
        in_specs=[pl.BlockSpec((1, 8, two_tm), lambda i: (i, 0, 0), memory_space=pltpu.SMEM),
                  pl.BlockSpec((1, 8, two_tm), lambda i: (jnp.minimum(i + 1, ns - 1), 0, 0),
                               memory_space=pltpu.SMEM),
                  row(D_MODEL), row(PLE_DIM), row(LANES), pl.BlockSpec(memory_space=pl.ANY),
                  full(wpp), full(wpg), full(bpg), full(g_ple), full(g_final)],
        out_specs=row(D_MODEL),
        out_shape=jax.ShapeDtypeStruct((t, D_MODEL), F32),
        scratch_shapes=[pltpu.VMEM((n, PACK_W), jnp.uint32), pltpu.VMEM((n, PACK_W), jnp.uint32),
                        pltpu.SemaphoreType.DMA((2,))],
        compiler_params=_cparams(("arbitrary",)),
        name="final",
    )(dest, dest, x1, p2d, route, ydw, wpp, wpg, bpg, g_ple, g_final)


def _block_plan(alloc, state, n_blk):
    new = alloc[:, 0, :N_EXPERTS].astype(jnp.int32).reshape(-1)
    base = alloc[:, 1, :N_EXPERTS].astype(jnp.int32).reshape(-1)
    expert = jnp.tile(jnp.arange(N_EXPERTS, dtype=jnp.int32), alloc.shape[0])
    fill = state[0, :N_EXPERTS].astype(jnp.int32)
    last_blk = state[1, :N_EXPERTS].astype(jnp.int32)
    taken = state[2, 0].astype(jnp.int32)
    b = jnp.arange(n_blk, dtype=jnp.int32)
    opened = (base[None, :] <= b[:, None]) & (b[:, None] < (base + new)[None, :])
    e_of = jnp.sum(jnp.where(opened, expert[None, :], 0), axis=1)
    onehot = e_of[:, None] == jnp.arange(N_EXPERTS, dtype=jnp.int32)[None, :]
    pick = lambda v: jnp.sum(jnp.where(onehot, v[None, :], 0), axis=1)
    rows = jnp.where(b < taken, jnp.where(b == pick(last_blk), pick(fill), MOE_BLK), 0)
    key = jnp.where(b < taken, e_of, N_EXPERTS) * n_blk + b
    pos = jnp.sum(key[None, :] < key[:, None], axis=1)
    at = pos[None, :] == b[:, None]
    order = lambda v: jnp.sum(jnp.where(at, v[None, :], 0), axis=1).astype(jnp.int32)
    return order(jnp.where(b < taken, e_of, N_EXPERTS - 1)), order(rows), order(b)


def _row(v):
    return v.reshape(1, -1).astype(F32)


def kernel(x, p, rel_bias, g_mix, w_in, w_gate, b_gate, ssm_a_re, ssm_a_im, ssm_log_dt, ssm_b_re, ssm_b_im,
           ssm_c_re, ssm_c_im, ssm_d, w_glu, b_glu, sinks, w_br_ssm, w_br_attn, w_out, g_ffn, w_router_group,
           b_router_group, w_router_expert, b_router_expert, w_e_gate, w_e_up, w_e_down, g_ple, w_ple_gate,
           b_ple_gate, w_ple_proj, g_final):
    bsz, seq, dm = x.shape
    assert g_mix.shape[0] == 1, "one layer followed by the final norm"
    i = 0
    t = bsz * seq
    x2d = x.reshape(t, dm)
    bias = _bias_table(rel_bias)
    u, q, kv = _proj(x2d, _row(g_mix[i]), w_in[i].astype(BF16))
    ab_re, ab_im, bb_re, bb_im = _ssmprep(ssm_a_re[i], ssm_a_im[i], ssm_log_dt[i], ssm_b_re[i], ssm_b_im[i])
    ws_in, ws_out, ar, ai = _ssm_weights(ab_re, ab_im, bb_re, bb_im, ssm_c_re[i], ssm_c_im[i])
    y_ssm = _ssm(u.reshape(bsz, seq, D_SSM), ws_in, ws_out, ar, ai, _row(ssm_d[i]),
                 w_glu[i].astype(BF16), _row(b_glu[i]))
    y_att = _attn(q.reshape(bsz, seq, ATT_W), kv.reshape(bsz, seq, KV_W), bias, sinks[i].astype(F32))
    w_router = jnp.zeros((dm, LANES), F32)
    w_router = w_router.at[:, :N_GROUPS].set(w_router_group[i])
    w_router = w_router.at[:, N_GROUPS:N_GROUPS + N_EXPERTS].set(w_router_expert[i])
    b_router = jnp.zeros((1, LANES), F32)
    b_router = b_router.at[0, :N_GROUPS].set(b_router_group[i])
    b_router = b_router.at[0, N_GROUPS:N_GROUPS + N_EXPERTS].set(b_router_expert[i])
    n_blk = t * TOP_K // MOE_BLK + N_EXPERTS
    x1, route, dest, alloc, state, xdw = _merge(
        x2d, y_ssm.reshape(t, D_SSM), y_att.reshape(t, ATT_W), _row(g_mix[i]), w_gate[i].astype(BF16),
        _row(b_gate[i]), w_br_ssm[i].astype(BF16), w_br_attn[i].astype(BF16), w_out[i].astype(BF16),
        _row(g_ffn[i]), w_router.astype(BF16), b_router, n_blk)
    blk_e, blk_n, blk_i = _block_plan(alloc, state, n_blk)
    ydw = _moe(blk_e, blk_n, blk_i, xdw, w_e_gate[i], w_e_up[i], w_e_down[i])
    out = _final(dest, x1, p[i].reshape(t, PLE_DIM), route, ydw, w_ple_proj[i].astype(BF16),
                 w_ple_gate[i].astype(BF16), _row(b_ple_gate[i]), _row(g_ple[i]), _row(g_final))
    return out.reshape(bsz, seq, dm)
```

```python
import functools
import math

import numpy as np
import jax
import jax.numpy as jnp
from jax import lax
from jax.experimental import pallas as pl
from jax.experimental.pallas import tpu as pltpu

D_MODEL = 1024
D_SSM = 512
SSM_CH = 16
SSM_GROUPS = D_SSM // SSM_CH
SSM_STATE = 64
N_STATE = SSM_GROUPS * SSM_STATE
N_HEADS = 8
N_KV = 2
HEAD_DIM = 64
GROUP = N_HEADS // N_KV
ATT_W = N_HEADS * HEAD_DIM
KV_W = 2 * N_KV * HEAD_DIM
WINDOW = 128
BLOCK = 128
NUM_BUCKETS = 32
MAX_DIST = 128
D_IN = D_SSM + ATT_W + KV_W
N_GROUPS = 4
EXPERTS_PER_GROUP = 8
N_EXPERTS = N_GROUPS * EXPERTS_PER_GROUP
TOP_K = 2
D_FF_EXPERT = 512
MOE_BLK = 256
PLE_DIM = 256
EPS = 1e-6

LANES = 128
NEG = -1e30
PACK_W = D_MODEL // 2

BF16 = jnp.bfloat16
F32 = jnp.float32
VMEM_LIMIT = 56 * 1024 * 1024
NBUF = 3


def _cparams(sem):
    return pltpu.CompilerParams(dimension_semantics=sem, vmem_limit_bytes=VMEM_LIMIT)


def _rms(x, g):
    ms = jnp.mean(x * x, axis=-1, keepdims=True)
    return x * lax.rsqrt(ms + EPS) * g


def _sigmoid(x):
    return 1.0 / (1.0 + jnp.exp(-x))


def _dot(a, b):
    return jnp.dot(a, b, preferred_element_type=F32)


def _pack_rows(y):
    lo = pltpu.bitcast(y[:, :PACK_W].astype(BF16).astype(F32), jnp.uint32)
    hi = pltpu.bitcast(y[:, PACK_W:].astype(BF16).astype(F32), jnp.uint32)
    return hi | lax.shift_right_logical(lo, jnp.uint32(16))


def _unpack_rows(w):
    lo = pltpu.bitcast(lax.shift_left(w, jnp.uint32(16)), F32)
    hi = pltpu.bitcast(w & jnp.uint32(0xFFFF0000), F32)
    return jnp.concatenate([lo, hi], axis=1)


def _proj_kernel(x_ref, g_ref, w_ref, u_ref, q_ref, kv_ref):
    h = _rms(x_ref[...], g_ref[...]).astype(BF16)
    proj = _dot(h, w_ref[...])
    u_ref[...] = proj[:, :D_SSM].astype(BF16)
    q_ref[...] = (proj[:, D_SSM:D_SSM + ATT_W] * (HEAD_DIM ** -0.5)).astype(BF16)
    kv_ref[...] = proj[:, D_SSM + ATT_W:].astype(BF16)


def _proj(x2d, g_mix, w_in_b, tm=512):
    t = x2d.shape[0]
    row = lambda w: pl.BlockSpec((tm, w), lambda i: (i, 0))
    full = lambda a: pl.BlockSpec(a.shape, lambda i: (0,) * a.ndim)
    return pl.pallas_call(
        _proj_kernel,
        grid=(t // tm,),
        in_specs=[row(D_MODEL), full(g_mix), full(w_in_b)],
        out_specs=[row(D_SSM), row(ATT_W), row(KV_W)],
        out_shape=[jax.ShapeDtypeStruct((t, D_SSM), BF16),
                   jax.ShapeDtypeStruct((t, ATT_W), BF16),
                   jax.ShapeDtypeStruct((t, KV_W), BF16)],
        compiler_params=_cparams(("parallel",)),
        name="proj",
    )(x2d, g_mix, w_in_b)


def _ssmprep_kernel(are_ref, aim_ref, ldt_ref, bre_ref, bim_ref, abr_ref, abi_ref, bbr_ref, bbi_ref):
    a_re, a_im = are_ref[...], aim_ref[...]
    dt = jnp.exp(ldt_ref[...])
    mag = jnp.exp(a_re * dt)
    ab_re = mag * jnp.cos(a_im * dt)
    ab_im = mag * jnp.sin(a_im * dt)
    abr_ref[...] = ab_re
    abi_ref[...] = ab_im
    den = a_re * a_re + a_im * a_im
    c_re = ((ab_re - 1.0) * a_re + ab_im * a_im) / den
    c_im = (ab_im * a_re - (ab_re - 1.0) * a_im) / den
    for c in range(SSM_CH):
        b_re, b_im = bre_ref[c], bim_ref[c]
        bbr_ref[c] = c_re * b_re - c_im * b_im
        bbi_ref[c] = c_re * b_im + c_im * b_re


def _ssmprep(a_re, a_im, log_dt, b_re, b_im):
    g, p = a_re.shape
    bt_re = jnp.transpose(b_re, (2, 0, 1))
    bt_im = jnp.transpose(b_im, (2, 0, 1))
    gp = jax.ShapeDtypeStruct((g, p), F32)
    cgp = jax.ShapeDtypeStruct((SSM_CH, g, p), F32)
    return pl.pallas_call(
        _ssmprep_kernel, out_shape=[gp, gp, cgp, cgp], name="ssmprep",
    )(a_re, a_im, log_dt.reshape(g, 1), bt_re, bt_im)


def _ssm_weights(ab_re, ab_im, bb_re, bb_im, c_re, c_im):
    g, p = ab_re.shape
    kb = D_SSM // 256
    gk = g // kb
    eye = jnp.eye(gk, dtype=F32)

    def in_block(bb):
        b4 = jnp.transpose(bb, (1, 0, 2)).reshape(kb, gk, SSM_CH, p)
        return jnp.einsum("kgcp,gh->kgchp", b4, eye).reshape(kb, gk * SSM_CH, gk * p)

    w_in = jnp.concatenate([in_block(bb_re), in_block(bb_im)], axis=2).astype(BF16)
    nj = D_SSM // LANES
    gj = g // nj
    eyej = jnp.eye(gj, dtype=F32)

    def out_block(c):
        c4 = jnp.transpose(c, (0, 2, 1)).reshape(nj, gj, p, SSM_CH)
        return jnp.einsum("jgpc,gh->jgphc", c4, eyej).reshape(nj, gj * p, gj * SSM_CH)

    w_out = jnp.stack([out_block(c_re), out_block(-c_im)], axis=1).astype(BF16)
    return w_in, w_out, ab_re.reshape(1, g * p), ab_im.reshape(1, g * p)


def _gelu_tanh(x):
    return 0.5 * x * (1.0 + jnp.tanh(math.sqrt(2.0 / math.pi) * (x + 0.044715 * (x * x * x))))


def _ssm_kernel(u_ref, win_ref, wout_ref, ar_ref, ai_ref, d_ref, wglu_ref, bglu_ref, y_ref,
                perm_ref, permt_ref, bu_ref, xb_ref, st_ref, *, nb, lc):
    rows = nb * lc
    ns = N_STATE
    i = pl.program_id(0)

    @pl.when(i == 0)
    def _():
        r = lax.broadcasted_iota(jnp.int32, (rows, rows), 0)
        c = lax.broadcasted_iota(jnp.int32, (rows, rows), 1)
        sh = nb.bit_length() - 1
        perm_ref[...] = jnp.where(c == (r & (nb - 1)) * lc + (r >> sh), 1.0, 0.0).astype(BF16)
        permt_ref[...] = jnp.where(r == (c & (nb - 1)) * lc + (c >> sh), 1.0, 0.0).astype(BF16)
        st_ref[...] = jnp.zeros_like(st_ref)

    u = u_ref[...].reshape(rows, D_SSM)
    up = _dot(perm_ref[...], u).astype(BF16)
    kb = win_ref.shape[0]
    half = ns // kb
    for k in range(kb):
        res = _dot(up[:, 256 * k:256 * (k + 1)], win_ref[k])
        bu_ref[:, half * k:half * (k + 1)] = res[:, :half]
        bu_ref[:, ns + half * k:ns + half * (k + 1)] = res[:, half:]

    ar, ai = ar_ref[...], ai_ref[...]

    def body(j, carry):
        xr, xi = carry
        keep_r, keep_i = [], []
        for s in range(2):
            row = pl.multiple_of((2 * j + s) * nb, nb)
            br = bu_ref[pl.ds(row, nb), 0:ns]
            bi = bu_ref[pl.ds(row, nb), ns:2 * ns]
            xr, xi = ar * xr - ai * xi + br, ar * xi + ai * xr + bi
            keep_r.append(xr)
            keep_i.append(xi)
        row2 = pl.multiple_of(j * 2 * nb, 2 * nb)
        xb_ref[pl.ds(row2, 2 * nb), 0:ns] = jnp.concatenate(keep_r, axis=0).astype(BF16)
        xb_ref[pl.ds(row2, 2 * nb), ns:2 * ns] = jnp.concatenate(keep_i, axis=0).astype(BF16)
        return xr, xi

    xr, xi = lax.fori_loop(0, lc // 2, body, (st_ref[:, 0:ns], st_ref[:, ns:2 * ns]))
    st_ref[:, 0:ns] = xr
    st_ref[:, ns:2 * ns] = xi

    nj = wout_ref.shape[0]
    kw = ns // nj
    ys = []
    for j in range(nj):
        ys.append(_dot(xb_ref[:, kw * j:kw * (j + 1)], wout_ref[j, 0])
                  + _dot(xb_ref[:, ns + kw * j:ns + kw * (j + 1)], wout_ref[j, 1]))
    y_tb = jnp.concatenate(ys, axis=1).astype(BF16)
    y = _dot(permt_ref[...], y_tb) + d_ref[...] * u.astype(F32)
    y = _gelu_tanh(y)
    z = _dot(y.astype(BF16), wglu_ref[...]) + bglu_ref[...]
    y_ref[...] = (y * _sigmoid(z)).astype(BF16).reshape(nb, lc, D_SSM)


def _ssm(u3, w_in, w_out, ar, ai, d_skip, w_glu_b, b_glu, lc=64):
    nb, s, _ = u3.shape
    assert nb & (nb - 1) == 0 and s % lc == 0 and lc % 16 == 0
    rows = nb * lc
    full = lambda a: pl.BlockSpec(a.shape, lambda i: (0,) * a.ndim)
    blk = pl.BlockSpec((nb, lc, D_SSM), lambda i: (0, i, 0))
    return pl.pallas_call(
        functools.partial(_ssm_kernel, nb=nb, lc=lc),
        grid=(s // lc,),
        in_specs=[blk, full(w_in), full(w_out), full(ar), full(ai), full(d_skip), full(w_glu_b), full(b_glu)],
        out_specs=blk,
        out_shape=jax.ShapeDtypeStruct((nb, s, D_SSM), BF16),
        scratch_shapes=[pltpu.VMEM((rows, rows), BF16), pltpu.VMEM((rows, rows), BF16),
                        pltpu.VMEM((rows, 2 * N_STATE), F32), pltpu.VMEM((rows, 2 * N_STATE), BF16),
                        pltpu.VMEM((nb, 2 * N_STATE), F32)],
        compiler_params=_cparams(("arbitrary",)),
        name="ssm",
    )(u3, w_in, w_out, ar, ai, d_skip, w_glu_b, b_glu)


def _t5_bucket_np(rel):
    max_exact = NUM_BUCKETS // 2
    relf = np.maximum(rel, 1).astype(np.float32)
    large = max_exact + (np.log(relf / np.float32(max_exact)) / np.float32(math.log(MAX_DIST / max_exact))
                         * np.float32(NUM_BUCKETS - max_exact)).astype(np.int32)
    large = np.minimum(large, NUM_BUCKETS - 1)
    return np.where(rel < max_exact, rel, large)


def _bias_table(rel_bias):
    q_loc = np.arange(BLOCK)[:, None]
    c_loc = np.arange(2 * BLOCK)[None, :]
    rel = q_loc + BLOCK - c_loc
    valid = (rel >= 0) & (rel < WINDOW)
    bucket = _t5_bucket_np(np.maximum(rel, 0)).reshape(-1, 1)
    onehot = (jnp.asarray(bucket) == jnp.arange(NUM_BUCKETS)[None, :]).astype(F32)
    bias = jnp.dot(onehot, rel_bias.astype(F32), precision=lax.Precision.HIGHEST)
    bias = jnp.where(valid.reshape(-1, 1), bias, NEG)
    return jnp.transpose(bias, (1, 0)).reshape(N_HEADS, BLOCK, 2 * BLOCK)


def _attn_kernel(sink_ref, q_ref, kvp_ref, kvc_ref, bias_ref, o_ref, *, nq):
    n = pl.program_id(1)
    kv_all = jnp.concatenate([kvp_ref[0], kvc_ref[0]], axis=0)
    col = lax.broadcasted_iota(jnp.int32, (BLOCK, 2 * BLOCK), 1)
    no_prev = jnp.where(col < BLOCK, jnp.where(n == 0, NEG, 0.0), 0.0)
    for j in range(nq):
        q = q_ref[0, BLOCK * j:BLOCK * (j + 1), :]
        kv = kv_all[BLOCK * j:BLOCK * (j + 2), :]
        outs = []
        for h in range(N_HEADS):
            g = h // GROUP
            qh = q[:, HEAD_DIM * h:HEAD_DIM * (h + 1)]
            kh = kv[:, HEAD_DIM * g:HEAD_DIM * (g + 1)]
            vh = kv[:, N_KV * HEAD_DIM + HEAD_DIM * g:N_KV * HEAD_DIM + HEAD_DIM * (g + 1)]
            s = lax.dot_general(qh, kh, (((1,), (1,)), ((), ())), preferred_element_type=F32)
            s = s + bias_ref[h]
            if j == 0:
                s = s + no_prev
            sink = sink_ref[h]
            m = jnp.maximum(jnp.max(s, axis=-1, keepdims=True), sink)
            e = jnp.exp(s - m)
            den = jnp.sum(e, axis=-1, keepdims=True) + jnp.exp(sink - m)
            outs.append(_dot(e.astype(BF16), vh) / den)
        o_ref[0, BLOCK * j:BLOCK * (j + 1), :] = jnp.concatenate(outs, axis=1).astype(BF16)


def _attn(q3, kv3, bias, sinks, nq=4):
    b, s, _ = q3.shape
    qb = nq * BLOCK
    grid_spec = pltpu.PrefetchScalarGridSpec(
        num_scalar_prefetch=0,
        grid=(b, s // qb),
        in_specs=[pl.BlockSpec(memory_space=pltpu.SMEM),
                  pl.BlockSpec((1, qb, ATT_W), lambda i, n: (i, n, 0)),
                  pl.BlockSpec((1, BLOCK, KV_W), lambda i, n: (i, jnp.maximum(n * nq - 1, 0), 0)),
                  pl.BlockSpec((1, qb, KV_W), lambda i, n: (i, n, 0)),
                  pl.BlockSpec(bias.shape, lambda i, n: (0, 0, 0))],
        out_specs=pl.BlockSpec((1, qb, ATT_W), lambda i, n: (i, n, 0)),
    )
    return pl.pallas_call(
        functools.partial(_attn_kernel, nq=nq),
        grid_spec=grid_spec,
        out_shape=jax.ShapeDtypeStruct((b, s, ATT_W), BF16),
        compiler_params=_cparams(("parallel", "parallel")),
        name="attn",
    )(sinks, q3, kv3, kv3, bias)


def _route(logits):
    lane = lax.broadcasted_iota(jnp.int32, logits.shape, 1).astype(F32)
    big = float(LANES)
    gl = jnp.where(lane < N_GROUPS, logits, NEG)
    mg = jnp.max(gl, axis=-1, keepdims=True)
    gidx = jnp.min(jnp.where(gl == mg, lane, big), axis=-1, keepdims=True)
    gsum = jnp.sum(jnp.where(lane < N_GROUPS, jnp.exp(gl - mg), 0.0), axis=-1, keepdims=True)
    gp = 1.0 / gsum
    lo = N_GROUPS + EXPERTS_PER_GROUP * gidx
    el = jnp.where(lane >= lo, jnp.where(lane < lo + EXPERTS_PER_GROUP, logits, NEG), NEG)
    m1 = jnp.max(el, axis=-1, keepdims=True)
    i1 = jnp.min(jnp.where(el == m1, lane, big), axis=-1, keepdims=True)
    el2 = jnp.where(lane == i1, NEG, el)
    m2 = jnp.max(el2, axis=-1, keepdims=True)
    i2 = jnp.min(jnp.where(el2 == m2, lane, big), axis=-1, keepdims=True)
    t = jnp.exp(m2 - m1)
    w1 = gp / (1.0 + t)
    w2 = gp * t / (1.0 + t)
    return jnp.where(lane == 0, i1 - N_GROUPS,
                     jnp.where(lane == 1, i2 - N_GROUPS,
                               jnp.where(lane == 2, w1, jnp.where(lane == 3, w2, 0.0))))


def _row_copy(src_ref, dst_ref, sem, src_row, dst_row):
    return pltpu.make_async_copy(src_ref.at[pl.ds(src_row, 1)], dst_ref.at[pl.ds(dst_row, 1)], sem)


def _start_all(copies):
    for k, c in enumerate(copies):
        c.start(priority=k % 2)


def _merge_kernel(x_ref, ya_ref, yb_ref, gmix_ref, wg_ref, bg_ref, wa_ref, wb_ref, wo_ref, gffn_ref, wr_ref, br_ref,
                  x1_ref, route_ref, dest_ref, alloc_ref, state_ref, xd_hbm,
                  tri_ref, pre_ref, st_ref, hbuf_ref, destv_ref, dests_ref, stv_ref, sts_ref, zblk_ref,
                  sem_rows, sem_idx, sem_pad, *, tm, n_blk):
    i = pl.program_id(0)
    last = pl.num_programs(0) - 1
    slot = lax.rem(i, NBUF)
    prev = lax.rem(i + NBUF - 1, NBUF)
    n = TOP_K * tm
    dump0 = n_blk * MOE_BLK

    def row_copies(s):
        return [_row_copy(hbuf_ref.at[s], xd_hbm, sem_rows.at[s], r % tm, dests_ref[s, r // tm, r % tm])
                for r in range(n)]

    def wait_rows(s):
        for _ in range(n):
            _row_copy(hbuf_ref.at[s], xd_hbm, sem_rows.at[s], 0, dump0).wait()

    def index_copy(s):
        return pltpu.make_async_copy(destv_ref.at[s], dests_ref.at[s], sem_idx.at[s])

    @pl.when(i == 0)
    def _():
        r = lax.broadcasted_iota(jnp.int32, (tm, tm), 0)
        c = lax.broadcasted_iota(jnp.int32, (tm, tm), 1)
        tri_ref[...] = jnp.where(r > c, 1.0, 0.0).astype(BF16)
        r = lax.broadcasted_iota(jnp.int32, (LANES, LANES), 0)
        c = lax.broadcasted_iota(jnp.int32, (LANES, LANES), 1)
        pre_ref[...] = jnp.where(r < c, 1.0, 0.0).astype(BF16)
        r = lax.broadcasted_iota(jnp.int32, (8, LANES), 0)
        st_ref[...] = jnp.where(r == 0, float(MOE_BLK), 0.0)
        hbuf_ref[NBUF - 1] = jnp.zeros((tm, PACK_W), jnp.uint32)

        def spare(t, carry):
            dests_ref[NBUF - 1, 0, t] = dump0 + t
            dests_ref[NBUF - 1, 1, t] = dump0 + tm + t
            return carry

        lax.fori_loop(0, tm, spare, 0)

    @pl.when(i >= NBUF - 1)
    def _():
        wait_rows(slot)

    @pl.when(i >= 1)
    def _():
        index_copy(prev).wait()

    _start_all(row_copies(prev))

    x = x_ref[...]
    h = _rms(x, gmix_ref[...]).astype(BF16)
    gates = _sigmoid(_dot(h, wg_ref[...]) + bg_ref[...])
    merged = (gates[:, :D_MODEL] * _dot(ya_ref[...], wa_ref[...])
              + gates[:, D_MODEL:] * _dot(yb_ref[...], wb_ref[...]))
    x1 = x + _dot(merged.astype(BF16), wo_ref[...])
    x1_ref[...] = x1
    h2 = _rms(x1, gffn_ref[...])
    hbuf_ref[slot] = _pack_rows(h2)
    route = _route(_dot(h2.astype(BF16), wr_ref[...]) + br_ref[...])
    route_ref[...] = route

    lane = lax.broadcasted_iota(jnp.int32, route.shape, 1).astype(F32)
    e1, e2 = route[:, 0:1], route[:, 1:2]
    is1, is2 = lane == e1, lane == e2
    member = jnp.where(is1, 1.0, jnp.where(is2, 1.0, 0.0))
    before = _dot(tri_ref[...], member.astype(BF16))
    cnt = jnp.sum(member, axis=0, keepdims=True)
    fill, blk, free = st_ref[0:1, :], st_ref[1:2, :], st_ref[2:3, :]
    need = fill + cnt
    new = jnp.floor((need + float(MOE_BLK - 1)) * (1.0 / MOE_BLK)) - 1.0
    base = free + _dot(jnp.broadcast_to(new, (8, LANES)).astype(BF16), pre_ref[...])[0:1, :]
    q = fill + before
    jb = jnp.floor(q * (1.0 / MOE_BLK))
    rowid = jnp.where(jb == 0.0, blk, base + jb - 1.0) * float(MOE_BLK) + (q - jb * float(MOE_BLK))
    d1 = jnp.sum(jnp.where(is1, rowid, 0.0), axis=-1, keepdims=True)
    d2 = jnp.sum(jnp.where(is2, rowid, 0.0), axis=-1, keepdims=True)
    slab = jnp.where(lane == 0, d1, jnp.where(lane == 1, d2, 0.0))
    d8 = jnp.transpose(slab, (1, 0))[0:8, :].astype(jnp.int32)
    destv_ref[slot] = d8
    dest_ref[0] = d8
    index_copy(slot).start()

    r8 = lax.broadcasted_iota(jnp.int32, (8, LANES), 0)
    alloc_ref[0] = jnp.where(r8 == 0, new, jnp.where(r8 == 1, base, 0.0))
    state = jnp.where(r8 == 0, need - new * float(MOE_BLK),
                      jnp.where(r8 == 1, jnp.where(new > 0.0, base + new - 1.0, blk),
                                jnp.where(r8 == 2, free + jnp.sum(new, axis=-1, keepdims=True), 0.0)))
    st_ref[...] = state
    state_ref[...] = state

    @pl.when(i == last)
    def _():
        index_copy(slot).wait()
        _start_all(row_copies(slot))
        for s in range(NBUF):
            wait_rows(s)
        stv_ref[...] = state.astype(jnp.int32)
        cp = pltpu.make_async_copy(stv_ref, sts_ref, sem_pad)
        cp.start()
        cp.wait()
        zblk_ref[...] = jnp.zeros_like(zblk_ref)

        def zero_rows(e, carry):
            first = sts_ref[1, e] * MOE_BLK

            def start(r, c):
                _row_copy(zblk_ref, xd_hbm, sem_pad, 0, first + r).start()
                return c

            def wait(r, c):
                _row_copy(zblk_ref, xd_hbm, sem_pad, 0, first + r).wait()
                return c

            lax.fori_loop(sts_ref[0, e], MOE_BLK, start, 0)
            lax.fori_loop(sts_ref[0, e], MOE_BLK, wait, 0)
            return carry

        lax.fori_loop(0, N_EXPERTS, zero_rows, 0)

        def block_copy(b):
            return pltpu.make_async_copy(zblk_ref, xd_hbm.at[pl.ds(pl.multiple_of(b * MOE_BLK, MOE_BLK), MOE_BLK)],
                                         sem_pad)

        def zero_block(b, c):
            block_copy(b).start()
            block_copy(b).wait()
            return c

        lax.fori_loop(sts_ref[2, 0], n_blk, zero_block, 0)


def _merge(x2d, ya, yb, g_mix, wg, bg, wa, wb, wo, g_ffn, wr, br, n_blk, tm=512):
    t = x2d.shape[0]
    nt = t // tm
    n_rows = n_blk * MOE_BLK + TOP_K * tm
    row = lambda w: pl.BlockSpec((tm, w), lambda i: (i, 0))
    full = lambda a: pl.BlockSpec(a.shape, lambda i: (0,) * a.ndim)
    once = lambda a: pl.BlockSpec(a.shape, lambda i: (0,) * a.ndim, pipeline_mode=pl.Buffered(1))
    return pl.pallas_call(
        functools.partial(_merge_kernel, tm=tm, n_blk=n_blk),
        grid=(nt,),
        in_specs=[row(D_MODEL), row(D_SSM), row(ATT_W), full(g_mix), once(wg), full(bg), once(wa), once(wb),
                  once(wo), full(g_ffn), full(wr), full(br)],
        out_specs=[row(D_MODEL), row(LANES), pl.BlockSpec((1, 8, tm), lambda i: (i, 0, 0)),
                   pl.BlockSpec((1, 8, LANES), lambda i: (i, 0, 0)), pl.BlockSpec((8, LANES), lambda i: (0, 0)),
                   pl.BlockSpec(memory_space=pl.ANY)],
        out_shape=[jax.ShapeDtypeStruct((t, D_MODEL), F32),
                   jax.ShapeDtypeStruct((t, LANES), F32),
                   jax.ShapeDtypeStruct((nt, 8, tm), jnp.int32),
                   jax.ShapeDtypeStruct((nt, 8, LANES), F32),
                   jax.ShapeDtypeStruct((8, LANES), F32),
                   jax.ShapeDtypeStruct((n_rows, PACK_W), jnp.uint32)],
        scratch_shapes=[pltpu.VMEM((tm, tm), BF16), pltpu.VMEM((LANES, LANES), BF16), pltpu.VMEM((8, LANES), F32),
                        pltpu.VMEM((NBUF, tm, PACK_W), jnp.uint32), pltpu.VMEM((NBUF, 8, tm), jnp.int32),
                        pltpu.SMEM((NBUF, 8, tm), jnp.int32), pltpu.VMEM((8, LANES), jnp.int32),
                        pltpu.SMEM((8, LANES), jnp.int32), pltpu.VMEM((MOE_BLK, PACK_W), jnp.uint32),
                        pltpu.SemaphoreType.DMA((NBUF,)), pltpu.SemaphoreType.DMA((NBUF,)),
                        pltpu.SemaphoreType.DMA],
        compiler_params=_cparams(("arbitrary",)),
        name="merge",
    )(x2d, ya, yb, g_mix, wg, bg, wa, wb, wo, g_ffn, wr, br)


def _moe_kernel(be_ref, bn_ref, bi_ref, xd_ref, wg_ref, wu_ref, wd_ref, yd_ref, wgb_ref, wub_ref, wdb_ref):
    del bi_ref
    i = pl.program_id(0)

    @pl.when(bn_ref[i] == 0)
    def _():
        yd_ref[...] = jnp.zeros_like(yd_ref)

    @pl.when(bn_ref[i] > 0)
    def _():
        prev = be_ref[jnp.maximum(i - 1, 0)]

        @pl.when(jnp.logical_or(i == 0, be_ref[i] != prev))
        def _():
            wgb_ref[...] = wg_ref[0].astype(BF16)
            wub_ref[...] = wu_ref[0].astype(BF16)
            wdb_ref[...] = wd_ref[0].astype(BF16)

        x = _unpack_rows(xd_ref[...]).astype(BF16)
        g = _dot(x, wgb_ref[...])
        u = _dot(x, wub_ref[...])
        a = (g * _sigmoid(g) * u).astype(BF16)
        yd_ref[...] = _pack_rows(_dot(a, wdb_ref[...]))


def _moe(blk_e, blk_n, blk_i, xdw, w_eg, w_eu, w_ed):
    n_blk = blk_e.shape[0]
    grid_spec = pltpu.PrefetchScalarGridSpec(
        num_scalar_prefetch=3,
        grid=(n_blk,),
        in_specs=[pl.BlockSpec((MOE_BLK, PACK_W), lambda i, be, bn, bi: (bi[i], 0)),
                  pl.BlockSpec((1, D_MODEL, D_FF_EXPERT), lambda i, be, bn, bi: (be[i], 0, 0)),
                  pl.BlockSpec((1, D_MODEL, D_FF_EXPERT), lambda i, be, bn, bi: (be[i], 0, 0)),
                  pl.BlockSpec((1, D_FF_EXPERT, D_MODEL), lambda i, be, bn, bi: (be[i], 0, 0))],
        out_specs=pl.BlockSpec((MOE_BLK, PACK_W), lambda i, be, bn, bi: (bi[i], 0)),
        scratch_shapes=[pltpu.VMEM((D_MODEL, D_FF_EXPERT), BF16), pltpu.VMEM((D_MODEL, D_FF_EXPERT), BF16),
                        pltpu.VMEM((D_FF_EXPERT, D_MODEL), BF16)],
    )
    return pl.pallas_call(
        _moe_kernel,
        grid_spec=grid_spec,
        out_shape=jax.ShapeDtypeStruct((n_blk * MOE_BLK, PACK_W), jnp.uint32),
        compiler_params=_cparams(("arbitrary",)),
        name="moe",
    )(blk_e, blk_n, blk_i, xdw, w_eg, w_eu, w_ed)


def _final_kernel(dcur_ref, dnxt_ref, x1_ref, p_ref, route_ref, yd_hbm, wpp_ref, wpg_ref, bpg_ref, gple_ref,
                  gfin_ref, o_ref, yw0_ref, yw1_ref, sem, *, tm):
    j = pl.program_id(0)
    last = pl.num_programs(0) - 1
    n = TOP_K * tm
    yw = (yw0_ref, yw1_ref)

    def gather(dref, half, slot):
        return [_row_copy(yd_hbm, yw[slot], sem.at[slot], dref[0, r // tm, half * tm + r % tm], r)
                for r in range(n)]

    def tile(half, slot):
        rows = slice(half * tm, (half + 1) * tm)
        rt = route_ref[rows, :]
        y0 = _unpack_rows(yw[slot][0:tm, :])
        y1 = _unpack_rows(yw[slot][tm:n, :])
        x2 = x1_ref[rows, :] + rt[:, 2:3] * y0 + rt[:, 3:4] * y1
        h3 = _rms(x2, gple_ref[...]).astype(BF16)
        pp = _dot(p_ref[rows, :].astype(BF16), wpp_ref[...])
        x3 = x2 + pp * _sigmoid(_dot(h3, wpg_ref[...]) + bpg_ref[...])
        o_ref[rows, :] = _rms(x3, gfin_ref[...])

    @pl.when(j == 0)
    def _():
        _start_all(gather(dcur_ref, 0, 0))

    for c in gather(dcur_ref, 0, 0):
        c.wait()
    _start_all(gather(dcur_ref, 1, 1))
    tile(0, 0)
    for c in gather(dcur_ref, 1, 1):
        c.wait()
    _start_all(gather(dnxt_ref, 0, 0))
    tile(1, 1)

    @pl.when(j == last)
    def _():
        for c in gather(dnxt_ref, 0, 0):
            c.wait()


def _final(dest, x1, p2d, route, ydw, wpp, wpg, bpg, g_ple, g_final):
    t = x1.shape[0]
    ns, _, two_tm = dest.shape
    tm = two_tm // 2
    n = TOP_K * tm
    row = lambda w: pl.BlockSpec((2 * tm, w), lambda i: (i, 0))
    full = lambda a: pl.BlockSpec(a.shape, lambda i: (0,) * a.ndim)
    return pl.pallas_call(
        functools.partial(_final_kernel, tm=tm),
        grid=(ns,),
        in_specs=[pl.BlockSpec((1, 8, two_tm), lambda i: (i, 0, 0), memory_space=pltpu.SMEM),
                  pl.BlockSpec((1, 8, two_tm), lambda i: (jnp.minimum(i + 1, ns - 1), 0, 0),
                               memory_space=pltpu.SMEM),
                  row(D_MODEL), row(PLE_DIM), row(LANES), pl.BlockSpec(memory_space=pl.ANY),
                  full(wpp), full(wpg), full(bpg), full(g_ple), full(g_final)],
        out_specs=row(D_MODEL),
        out_shape=jax.ShapeDtypeStruct((t, D_MODEL), F32),
        scratch_shapes=[pltpu.VMEM((n, PACK_W), jnp.uint32), pltpu.VMEM((n, PACK_W), jnp.uint32),
                        pltpu.SemaphoreType.DMA((2,))],
        compiler_params=_cparams(("arbitrary",)),
        name="final",
    )(dest, dest, x1, p2d, route, ydw, wpp, wpg, bpg, g_ple, g_final)


def _block_plan(alloc, state, n_blk):
    new = alloc[:, 0, :N_EXPERTS].astype(jnp.int32).reshape(-1)
    base = alloc[:, 1, :N_EXPERTS].astype(jnp.int32).reshape(-1)
    expert = jnp.tile(jnp.arange(N_EXPERTS, dtype=jnp.int32), alloc.shape[0])
    fill = state[0, :N_EXPERTS].astype(jnp.int32)
    last_blk = state[1, :N_EXPERTS].astype(jnp.int32)
    taken = state[2, 0].astype(jnp.int32)
    b = jnp.arange(n_blk, dtype=jnp.int32)
    opened = (base[None, :] <= b[:, None]) & (b[:, None] < (base + new)[None, :])
    e_of = jnp.sum(jnp.where(opened, expert[None, :], 0), axis=1)
    onehot = e_of[:, None] == jnp.arange(N_EXPERTS, dtype=jnp.int32)[None, :]
    pick = lambda v: jnp.sum(jnp.where(onehot, v[None, :], 0), axis=1)
    rows = jnp.where(b < taken, jnp.where(b == pick(last_blk), pick(fill), MOE_BLK), 0)
    key = jnp.where(b < taken, e_of, N_EXPERTS) * n_blk + b
    pos = jnp.sum(key[None, :] < key[:, None], axis=1)
    at = pos[None, :] == b[:, None]
    order = lambda v: jnp.sum(jnp.where(at, v[None, :], 0), axis=1).astype(jnp.int32)
    return order(jnp.where(b < taken, e_of, N_EXPERTS - 1)), order(rows), order(b)


def _row(v):
    return v.reshape(1, -1).astype(F32)


def kernel(x, p, rel_bias, g_mix, w_in, w_gate, b_gate, ssm_a_re, ssm_a_im, ssm_log_dt, ssm_b_re, ssm_b_im,
           ssm_c_re, ssm_c_im, ssm_d, w_glu, b_glu, sinks, w_br_ssm, w_br_attn, w_out, g_ffn, w_router_group,
           b_router_group, w_router_expert, b_router_expert, w_e_gate, w_e_up, w_e_down, g_ple, w_ple_gate,
           b_ple_gate, w_ple_proj, g_final):
    bsz, seq, dm = x.shape
    assert g_mix.shape[0] == 1, "one layer followed by the final norm"
    i = 0
    t = bsz * seq
    x2d = x.reshape(t, dm)
    bias = _bias_table(rel_bias)
    u, q, kv = _proj(x2d, _row(g_mix[i]), w_in[i].astype(BF16))
    ab_re, ab_im, bb_re, bb_im = _ssmprep(ssm_a_re[i], ssm_a_im[i], ssm_log_dt[i], ssm_b_re[i], ssm_b_im[i])
    ws_in, ws_out, ar, ai = _ssm_weights(ab_re, ab_im, bb_re, bb_im, ssm_c_re[i], ssm_c_im[i])
    y_ssm = _ssm(u.reshape(bsz, seq, D_SSM), ws_in, ws_out, ar, ai, _row(ssm_d[i]),
                 w_glu[i].astype(BF16), _row(b_glu[i]))
    y_att = _attn(q.reshape(bsz, seq, ATT_W), kv.reshape(bsz, seq, KV_W), bias, sinks[i].astype(F32))
    w_router = jnp.zeros((dm, LANES), F32)
    w_router = w_router.at[:, :N_GROUPS].set(w_router_group[i])
    w_router = w_router.at[:, N_GROUPS:N_GROUPS + N_EXPERTS].set(w_router_expert[i])
    b_router = jnp.zeros((1, LANES), F32)
    b_router = b_router.at[0, :N_GROUPS].set(b_router_group[i])
    b_router = b_router.at[0, N_GROUPS:N_GROUPS + N_EXPERTS].set(b_router_expert[i])
    n_blk = t * TOP_K // MOE_BLK + N_EXPERTS
    x1, route, dest, alloc, state, xdw = _merge(
        x2d, y_ssm.reshape(t, D_SSM), y_att.reshape(t, ATT_W), _row(g_mix[i]), w_gate[i].astype(BF16),
        _row(b_gate[i]), w_br_ssm[i].astype(BF16), w_br_attn[i].astype(BF16), w_out[i].astype(BF16),
        _row(g_ffn[i]), w_router.astype(BF16), b_router, n_blk)
    blk_e, blk_n, blk_i = _block_plan(alloc, state, n_blk)
    ydw = _moe(blk_e, blk_n, blk_i, xdw, w_e_gate[i], w_e_up[i], w_e_down[i])
    out = _final(dest, x1, p[i].reshape(t, PLE_DIM), route, ydw, w_ple_proj[i].astype(BF16),
                 w_ple_gate[i].astype(BF16), _row(b_ple_gate[i]), _row(g_ple[i]), _row(g_final))
    return out.reshape(bsz, seq, dm)
```

```python
import functools
import math

import numpy as np
import jax
import jax.numpy as jnp
from jax import lax
from jax.experimental import pallas as pl
from jax.experimental.pallas import tpu as pltpu

D_MODEL = 1024
D_SSM = 512
SSM_CH = 16
SSM_GROUPS = D_SSM // SSM_CH
SSM_STATE = 64
N_STATE = SSM_GROUPS * SSM_STATE
N_HEADS = 8
N_KV = 2
HEAD_DIM = 64
GROUP = N_HEADS // N_KV
ATT_W = N_HEADS * HEAD_DIM
KV_W = 2 * N_KV * HEAD_DIM
WINDOW = 128
BLOCK = 128
NUM_BUCKETS = 32
MAX_DIST = 128
D_IN = D_SSM + ATT_W + KV_W
N_GROUPS = 4
EXPERTS_PER_GROUP = 8
N_EXPERTS = N_GROUPS * EXPERTS_PER_GROUP
TOP_K = 2
D_FF_EXPERT = 512
MOE_BLK = 256
PLE_DIM = 256
EPS = 1e-6

LANES = 128
NEG = -1e30
PACK_W = D_MODEL // 2

BF16 = jnp.bfloat16
F32 = jnp.float32
VMEM_LIMIT = 56 * 1024 * 1024
NBUF = 3


def _cparams(sem):
    return pltpu.CompilerParams(dimension_semantics=sem, vmem_limit_bytes=VMEM_LIMIT)


def _rms(x, g):
    ms = jnp.mean(x * x, axis=-1, keepdims=True)
    return x * lax.rsqrt(ms + EPS) * g


def _sigmoid(x):
    return 1.0 / (1.0 + jnp.exp(-x))


def _dot(a, b):
    return jnp.dot(a, b, preferred_element_type=F32)


def _pack_rows(y):
    lo = pltpu.bitcast(y[:, :PACK_W].astype(BF16).astype(F32), jnp.uint32)
    hi = pltpu.bitcast(y[:, PACK_W:].astype(BF16).astype(F32), jnp.uint32)
    return hi | lax.shift_right_logical(lo, jnp.uint32(16))


def _unpack_rows(w):
    lo = pltpu.bitcast(lax.shift_left(w, jnp.uint32(16)), F32)
    hi = pltpu.bitcast(w & jnp.uint32(0xFFFF0000), F32)
    return jnp.concatenate([lo, hi], axis=1)


def _proj_kernel(x_ref, g_ref, w_ref, u_ref, q_ref, kv_ref):
    h = _rms(x_ref[...], g_ref[...]).astype(BF16)
    proj = _dot(h, w_ref[...])
    u_ref[...] = proj[:, :D_SSM].astype(BF16)
    q_ref[...] = (proj[:, D_SSM:D_SSM + ATT_W] * (HEAD_DIM ** -0.5)).astype(BF16)
    kv_ref[...] = proj[:, D_SSM + ATT_W:].astype(BF16)


def _proj(x2d, g_mix, w_in_b, tm=512):
    t = x2d.shape[0]
    row = lambda w: pl.BlockSpec((tm, w), lambda i: (i, 0))
    full = lambda a: pl.BlockSpec(a.shape, lambda i: (0,) * a.ndim)
    return pl.pallas_call(
        _proj_kernel,
        grid=(t // tm,),
        in_specs=[row(D_MODEL), full(g_mix), full(w_in_b)],
        out_specs=[row(D_SSM), row(ATT_W), row(KV_W)],
        out_shape=[jax.ShapeDtypeStruct((t, D_SSM), BF16),
                   jax.ShapeDtypeStruct((t, ATT_W), BF16),
                   jax.ShapeDtypeStruct((t, KV_W), BF16)],
        compiler_params=_cparams(("parallel",)),
        name="proj",
    )(x2d, g_mix, w_in_b)


def _ssmprep_kernel(are_ref, aim_ref, ldt_ref, bre_ref, bim_ref, abr_ref, abi_ref, bbr_ref, bbi_ref):
    a_re, a_im = are_ref[...], aim_ref[...]
    dt = jnp.exp(ldt_ref[...])
    mag = jnp.exp(a_re * dt)
    ab_re = mag * jnp.cos(a_im * dt)
    ab_im = mag * jnp.sin(a_im * dt)
    abr_ref[...] = ab_re
    abi_ref[...] = ab_im
    den = a_re * a_re + a_im * a_im
    c_re = ((ab_re - 1.0) * a_re + ab_im * a_im) / den
    c_im = (ab_im * a_re - (ab_re - 1.0) * a_im) / den
    for c in range(SSM_CH):
        b_re, b_im = bre_ref[c], bim_ref[c]
        bbr_ref[c] = c_re * b_re - c_im * b_im
        bbi_ref[c] = c_re * b_im + c_im * b_re


def _ssmprep(a_re, a_im, log_dt, b_re, b_im):
    g, p = a_re.shape
    bt_re = jnp.transpose(b_re, (2, 0, 1))
    bt_im = jnp.transpose(b_im, (2, 0, 1))
    gp = jax.ShapeDtypeStruct((g, p), F32)
    cgp = jax.ShapeDtypeStruct((SSM_CH, g, p), F32)
    return pl.pallas_call(
        _ssmprep_kernel, out_shape=[gp, gp, cgp, cgp], name="ssmprep",
    )(a_re, a_im, log_dt.reshape(g, 1), bt_re, bt_im)


def _ssm_weights(ab_re, ab_im, bb_re, bb_im, c_re, c_im):
    g, p = ab_re.shape
    kb = D_SSM // 256
    gk = g // kb
    eye = jnp.eye(gk, dtype=F32)

    def in_block(bb):
        b4 = jnp.transpose(bb, (1, 0, 2)).reshape(kb, gk, SSM_CH, p)
        return jnp.einsum("kgcp,gh->kgchp", b4, eye).reshape(kb, gk * SSM_CH, gk * p)

    w_in = jnp.concatenate([in_block(bb_re), in_block(bb_im)], axis=2).astype(BF16)
    nj = D_SSM // LANES
    gj = g // nj
    eyej = jnp.eye(gj, dtype=F32)

    def out_block(c):
        c4 = jnp.transpose(c, (0, 2, 1)).reshape(nj, gj, p, SSM_CH)
        return jnp.einsum("jgpc,gh->jgphc", c4, eyej).reshape(nj, gj * p, gj * SSM_CH)

    w_out = jnp.stack([out_block(c_re), out_block(-c_im)], axis=1).astype(BF16)
    return w_in, w_out, ab_re.reshape(1, g * p), ab_im.reshape(1, g * p)


def _gelu_tanh(x):
    return 0.5 * x * (1.0 + jnp.tanh(math.sqrt(2.0 / math.pi) * (x + 0.044715 * (x * x * x))))


def _ssm_kernel(u_ref, win_ref, wout_ref, ar_ref, ai_ref, d_ref, wglu_ref, bglu_ref, y_ref,
                perm_ref, permt_ref, bu_ref, xb_ref, st_ref, *, nb, lc):
    rows = nb * lc
    ns = N_STATE
    i = pl.program_id(0)

    @pl.when(i == 0)
    def _():
        r = lax.broadcasted_iota(jnp.int32, (rows, rows), 0)
        c = lax.broadcasted_iota(jnp.int32, (rows, rows), 1)
        sh = nb.bit_length() - 1
        perm_ref[...] = jnp.where(c == (r & (nb - 1)) * lc + (r >> sh), 1.0, 0.0).astype(BF16)
        permt_ref[...] = jnp.where(r == (c & (nb - 1)) * lc + (c >> sh), 1.0, 0.0).astype(BF16)
        st_ref[...] = jnp.zeros_like(st_ref)

    u = u_ref[...].reshape(rows, D_SSM)
    up = _dot(perm_ref[...], u).astype(BF16)
    kb = win_ref.shape[0]
    half = ns // kb
    for k in range(kb):
        res = _dot(up[:, 256 * k:256 * (k + 1)], win_ref[k])
        bu_ref[:, half * k:half * (k + 1)] = res[:, :half]
        bu_ref[:, ns + half * k:ns + half * (k + 1)] = res[:, half:]

    ar, ai = ar_ref[...], ai_ref[...]

    def body(j, carry):
        xr, xi = carry
        keep_r, keep_i = [], []
        for s in range(2):
            row = pl.multiple_of((2 * j + s) * nb, nb)
            br = bu_ref[pl.ds(row, nb), 0:ns]
            bi = bu_ref[pl.ds(row, nb), ns:2 * ns]
            xr, xi = ar * xr - ai * xi + br, ar * xi + ai * xr + bi
            keep_r.append(xr)
            keep_i.append(xi)
        row2 = pl.multiple_of(j * 2 * nb, 2 * nb)
        xb_ref[pl.ds(row2, 2 * nb), 0:ns] = jnp.concatenate(keep_r, axis=0).astype(BF16)
        xb_ref[pl.ds(row2, 2 * nb), ns:2 * ns] = jnp.concatenate(keep_i, axis=0).astype(BF16)
        return xr, xi

    xr, xi = lax.fori_loop(0, lc // 2, body, (st_ref[:, 0:ns], st_ref[:, ns:2 * ns]))
    st_ref[:, 0:ns] = xr
    st_ref[:, ns:2 * ns] = xi

    nj = wout_ref.shape[0]
    kw = ns // nj
    ys = []
    for j in range(nj):
        ys.append(_dot(xb_ref[:, kw * j:kw * (j + 1)], wout_ref[j, 0])
                  + _dot(xb_ref[:, ns + kw * j:ns + kw * (j + 1)], wout_ref[j, 1]))
    y_tb = jnp.concatenate(ys, axis=1).astype(BF16)
    y = _dot(permt_ref[...], y_tb) + d_ref[...] * u.astype(F32)
    y = _gelu_tanh(y)
    z = _dot(y.astype(BF16), wglu_ref[...]) + bglu_ref[...]
    y_ref[...] = (y * _sigmoid(z)).astype(BF16).reshape(nb, lc, D_SSM)


def _ssm(u3, w_in, w_out, ar, ai, d_skip, w_glu_b, b_glu, lc=64):
    nb, s, _ = u3.shape
    assert nb & (nb - 1) == 0 and s % lc == 0 and lc % 16 == 0
    rows = nb * lc
    full = lambda a: pl.BlockSpec(a.shape, lambda i: (0,) * a.ndim)
    blk = pl.BlockSpec((nb, lc, D_SSM), lambda i: (0, i, 0))
    return pl.pallas_call(
        functools.partial(_ssm_kernel, nb=nb, lc=lc),
        grid=(s // lc,),
        in_specs=[blk, full(w_in), full(w_out), full(ar), full(ai), full(d_skip), full(w_glu_b), full(b_glu)],
        out_specs=blk,
        out_shape=jax.ShapeDtypeStruct((nb, s, D_SSM), BF16),
        scratch_shapes=[pltpu.VMEM((rows, rows), BF16), pltpu.VMEM((rows, rows), BF16),
                        pltpu.VMEM((rows, 2 * N_STATE), F32), pltpu.VMEM((rows, 2 * N_STATE), BF16),
                        pltpu.VMEM((nb, 2 * N_STATE), F32)],
        compiler_params=_cparams(("arbitrary",)),
        name="ssm",
    )(u3, w_in, w_out, ar, ai, d_skip, w_glu_b, b_glu)


def _t5_bucket_np(rel):
    max_exact = NUM_BUCKETS // 2
    relf = np.maximum(rel, 1).astype(np.float32)
    large = max_exact + (np.log(relf / np.float32(max_exact)) / np.float32(math.log(MAX_DIST / max_exact))
                         * np.float32(NUM_BUCKETS - max_exact)).astype(np.int32)
    large = np.minimum(large, NUM_BUCKETS - 1)
    return np.where(rel < max_exact, rel, large)


def _bias_table(rel_bias):
    q_loc = np.arange(BLOCK)[:, None]
    c_loc = np.arange(2 * BLOCK)[None, :]
    rel = q_loc + BLOCK - c_loc
    valid = (rel >= 0) & (rel < WINDOW)
    bucket = _t5_bucket_np(np.maximum(rel, 0)).reshape(-1, 1)
    onehot = (jnp.asarray(bucket) == jnp.arange(NUM_BUCKETS)[None, :]).astype(F32)
    bias = jnp.dot(onehot, rel_bias.astype(F32), precision=lax.Precision.HIGHEST)
    bias = jnp.where(valid.reshape(-1, 1), bias, NEG)
    return jnp.transpose(bias, (1, 0)).reshape(N_HEADS, BLOCK, 2 * BLOCK)


def _attn_kernel(sink_ref, q_ref, kvp_ref, kvc_ref, bias_ref, o_ref, *, nq):
    n = pl.program_id(1)
    kv_all = jnp.concatenate([kvp_ref[0], kvc_ref[0]], axis=0)
    col = lax.broadcasted_iota(jnp.int32, (BLOCK, 2 * BLOCK), 1)
    no_prev = jnp.where(col < BLOCK, jnp.where(n == 0, NEG, 0.0), 0.0)
    for j in range(nq):
        q = q_ref[0, BLOCK * j:BLOCK * (j + 1), :]
        kv = kv_all[BLOCK * j:BLOCK * (j + 2), :]
        outs = []
        for h in range(N_HEADS):
            g = h // GROUP
            qh = q[:, HEAD_DIM * h:HEAD_DIM * (h + 1)]
            kh = kv[:, HEAD_DIM * g:HEAD_DIM * (g + 1)]
            vh = kv[:, N_KV * HEAD_DIM + HEAD_DIM * g:N_KV * HEAD_DIM + HEAD_DIM * (g + 1)]
            s = lax.dot_general(qh, kh, (((1,), (1,)), ((), ())), preferred_element_type=F32)
            s = s + bias_ref[h]
            if j == 0:
                s = s + no_prev
            sink = sink_ref[h]
            m = jnp.maximum(jnp.max(s, axis=-1, keepdims=True), sink)
            e = jnp.exp(s - m)
            den = jnp.sum(e, axis=-1, keepdims=True) + jnp.exp(sink - m)
            outs.append(_dot(e.astype(BF16), vh) / den)
        o_ref[0, BLOCK * j:BLOCK * (j + 1), :] = jnp.concatenate(outs, axis=1).astype(BF16)


def _attn(q3, kv3, bias, sinks, nq=4):
    b, s, _ = q3.shape
    qb = nq * BLOCK
    grid_spec = pltpu.PrefetchScalarGridSpec(
        num_scalar_prefetch=0,
        grid=(b, s // qb),
        in_specs=[pl.BlockSpec(memory_space=pltpu.SMEM),
                  pl.BlockSpec((1, qb, ATT_W), lambda i, n: (i, n, 0)),
                  pl.BlockSpec((1, BLOCK, KV_W), lambda i, n: (i, jnp.maximum(n * nq - 1, 0), 0)),
                  pl.BlockSpec((1, qb, KV_W), lambda i, n: (i, n, 0)),
                  pl.BlockSpec(bias.shape, lambda i, n: (0, 0, 0))],
        out_specs=pl.BlockSpec((1, qb, ATT_W), lambda i, n: (i, n, 0)),
    )
    return pl.pallas_call(
        functools.partial(_attn_kernel, nq=nq),
        grid_spec=grid_spec,
        out_shape=jax.ShapeDtypeStruct((b, s, ATT_W), BF16),
        compiler_params=_cparams(("parallel", "parallel")),
        name="attn",
    )(sinks, q3, kv3, kv3, bias)


def _route(logits):
    lane = lax.broadcasted_iota(jnp.int32, logits.shape, 1).astype(F32)
    big = float(LANES)
    gl = jnp.where(lane < N_GROUPS, logits, NEG)
    mg = jnp.max(gl, axis=-1, keepdims=True)
    gidx = jnp.min(jnp.where(gl == mg, lane, big), axis=-1, keepdims=True)
    gsum = jnp.sum(jnp.where(lane < N_GROUPS, jnp.exp(gl - mg), 0.0), axis=-1, keepdims=True)
    gp = 1.0 / gsum
    lo = N_GROUPS + EXPERTS_PER_GROUP * gidx
    el = jnp.where(lane >= lo, jnp.where(lane < lo + EXPERTS_PER_GROUP, logits, NEG), NEG)
    m1 = jnp.max(el, axis=-1, keepdims=True)
    i1 = jnp.min(jnp.where(el == m1, lane, big), axis=-1, keepdims=True)
    el2 = jnp.where(lane == i1, NEG, el)
    m2 = jnp.max(el2, axis=-1, keepdims=True)
    i2 = jnp.min(jnp.where(el2 == m2, lane, big), axis=-1, keepdims=True)
    t = jnp.exp(m2 - m1)
    w1 = gp / (1.0 + t)
    w2 = gp * t / (1.0 + t)
    return jnp.where(lane == 0, i1 - N_GROUPS,
                     jnp.where(lane == 1, i2 - N_GROUPS,
                               jnp.where(lane == 2, w1, jnp.where(lane == 3, w2, 0.0))))


def _row_copy(src_ref, dst_ref, sem, src_row, dst_row):
    return pltpu.make_async_copy(src_ref.at[pl.ds(src_row, 1)], dst_ref.at[pl.ds(dst_row, 1)], sem)


def _start_all(copies):
    for k, c in enumerate(copies):
        c.start(priority=k % 2)


def _merge_kernel(x_ref, ya_ref, yb_ref, gmix_ref, wg_ref, bg_ref, wa_ref, wb_ref, wo_ref, gffn_ref, wr_ref, br_ref,
                  x1_ref, route_ref, dest_ref, alloc_ref, state_ref, xd_hbm,
                  tri_ref, pre_ref, st_ref, hbuf_ref, destv_ref, dests_ref, stv_ref, sts_ref, zblk_ref,
                  hs_ref, gate_ref, sem_rows, sem_idx, sem_pad, *, tm, n_blk):
    i = pl.program_id(0)
    last = pl.num_programs(0) - 1
    slot = lax.rem(i, NBUF)
    prev = lax.rem(i + NBUF - 1, NBUF)
    n = TOP_K * tm
    dump0 = n_blk * MOE_BLK

    def row_copies(s, lo=0, hi=n):
        return [_row_copy(hbuf_ref.at[s], xd_hbm, sem_rows.at[s], r % tm, dests_ref[s, r // tm, r % tm])
                for r in range(lo, hi)]

    def wait_rows(s):
        for _ in range(n):
            _row_copy(hbuf_ref.at[s], xd_hbm, sem_rows.at[s], 0, dump0).wait()

    def index_copy(s):
        return pltpu.make_async_copy(destv_ref.at[s], dests_ref.at[s], sem_idx.at[s])

    @pl.when(i == 0)
    def _():
        r = lax.broadcasted_iota(jnp.int32, (tm, tm), 0)
        c = lax.broadcasted_iota(jnp.int32, (tm, tm), 1)
        tri_ref[...] = jnp.where(r > c, 1.0, 0.0).astype(BF16)
        r = lax.broadcasted_iota(jnp.int32, (LANES, LANES), 0)
        c = lax.broadcasted_iota(jnp.int32, (LANES, LANES), 1)
        pre_ref[...] = jnp.where(r < c, 1.0, 0.0).astype(BF16)
        r = lax.broadcasted_iota(jnp.int32, (8, LANES), 0)
        st_ref[...] = jnp.where(r == 0, float(MOE_BLK), 0.0)
        hbuf_ref[NBUF - 1] = jnp.zeros((tm, PACK_W), jnp.uint32)

        def spare(t, carry):
            dests_ref[NBUF - 1, 0, t] = dump0 + t
            dests_ref[NBUF - 1, 1, t] = dump0 + tm + t
            return carry

        lax.fori_loop(0, tm, spare, 0)

    @pl.when(i >= NBUF - 1)
    def _():
        wait_rows(slot)

    @pl.when(i >= 1)
    def _():
        index_copy(prev).wait()

    group = n // 4
    once = jnp.minimum(dests_ref[prev, 0, 0], 0) + 1

    def region(body):
        lax.fori_loop(0, once, lambda _, c: (body(), c)[1], 0)

    def gates_lo():
        _start_all(row_copies(prev, 0, group))
        h = _rms(x_ref[...], gmix_ref[...]).astype(BF16)
        hs_ref[...] = h
        gate_ref[0] = _sigmoid(_dot(h, wg_ref[:, 0:D_MODEL]) + bg_ref[:, 0:D_MODEL])

    def gates_hi():
        _start_all(row_copies(prev, group, 2 * group))
        gate_ref[1] = _sigmoid(_dot(hs_ref[...], wg_ref[:, D_MODEL:2 * D_MODEL]) + bg_ref[:, D_MODEL:2 * D_MODEL])

    def branches():
        _start_all(row_copies(prev, 2 * group, 3 * group))
        merged = gate_ref[0] * _dot(ya_ref[...], wa_ref[...]) + gate_ref[1] * _dot(yb_ref[...], wb_ref[...])
        hs_ref[...] = merged.astype(BF16)

    region(gates_lo)
    region(gates_hi)
    region(branches)
    _start_all(row_copies(prev, 3 * group, n))
    x1 = x_ref[...] + _dot(hs_ref[...], wo_ref[...])
    x1_ref[...] = x1
    h2 = _rms(x1, gffn_ref[...])
    hbuf_ref[slot] = _pack_rows(h2)
    route = _route(_dot(h2.astype(BF16), wr_ref[...]) + br_ref[...])
    route_ref[...] = route

    lane = lax.broadcasted_iota(jnp.int32, route.shape, 1).astype(F32)
    e1, e2 = route[:, 0:1], route[:, 1:2]
    is1, is2 = lane == e1, lane == e2
    member = jnp.where(is1, 1.0, jnp.where(is2, 1.0, 0.0))
    before = _dot(tri_ref[...], member.astype(BF16))
    cnt = jnp.sum(member, axis=0, keepdims=True)
    fill, blk, free = st_ref[0:1, :], st_ref[1:2, :], st_ref[2:3, :]
    need = fill + cnt
    new = jnp.floor((need + float(MOE_BLK - 1)) * (1.0 / MOE_BLK)) - 1.0
    base = free + _dot(jnp.broadcast_to(new, (8, LANES)).astype(BF16), pre_ref[...])[0:1, :]
    q = fill + before
    jb = jnp.floor(q * (1.0 / MOE_BLK))
    rowid = jnp.where(jb == 0.0, blk, base + jb - 1.0) * float(MOE_BLK) + (q - jb * float(MOE_BLK))
    d1 = jnp.sum(jnp.where(is1, rowid, 0.0), axis=-1, keepdims=True)
    d2 = jnp.sum(jnp.where(is2, rowid, 0.0), axis=-1, keepdims=True)
    slab = jnp.where(lane == 0, d1, jnp.where(lane == 1, d2, 0.0))
    d8 = jnp.transpose(slab, (1, 0))[0:8, :].astype(jnp.int32)
    destv_ref[slot] = d8
    dest_ref[0] = d8
    index_copy(slot).start()

    r8 = lax.broadcasted_iota(jnp.int32, (8, LANES), 0)
    alloc_ref[0] = jnp.where(r8 == 0, new, jnp.where(r8 == 1, base, 0.0))
    state = jnp.where(r8 == 0, need - new * float(MOE_BLK),
                      jnp.where(r8 == 1, jnp.where(new > 0.0, base + new - 1.0, blk),
                                jnp.where(r8 == 2, free + jnp.sum(new, axis=-1, keepdims=True), 0.0)))
    st_ref[...] = state
    state_ref[...] = state

    @pl.when(i == last)
    def _():
        index_copy(slot).wait()
        _start_all(row_copies(slot))
        for s in range(NBUF):
            wait_rows(s)
        stv_ref[...] = state.astype(jnp.int32)
        cp = pltpu.make_async_copy(stv_ref, sts_ref, sem_pad)
        cp.start()
        cp.wait()
        zblk_ref[...] = jnp.zeros_like(zblk_ref)

        def zero_rows(e, carry):
            first = sts_ref[1, e] * MOE_BLK

            def start(r, c):
                _row_copy(zblk_ref, xd_hbm, sem_pad, 0, first + r).start()
                return c

            def wait(r, c):
                _row_copy(zblk_ref, xd_hbm, sem_pad, 0, first + r).wait()
                return c

            lax.fori_loop(sts_ref[0, e], MOE_BLK, start, 0)
            lax.fori_loop(sts_ref[0, e], MOE_BLK, wait, 0)
            return carry

        lax.fori_loop(0, N_EXPERTS, zero_rows, 0)

        def block_copy(b):
            return pltpu.make_async_copy(zblk_ref, xd_hbm.at[pl.ds(pl.multiple_of(b * MOE_BLK, MOE_BLK), MOE_BLK)],
                                         sem_pad)

        def zero_block(b, c):
            block_copy(b).start()
            block_copy(b).wait()
            return c

        lax.fori_loop(sts_ref[2, 0], n_blk, zero_block, 0)


def _merge(x2d, ya, yb, g_mix, wg, bg, wa, wb, wo, g_ffn, wr, br, n_blk, tm=512):
    t = x2d.shape[0]
    nt = t // tm
    n_rows = n_blk * MOE_BLK + TOP_K * tm
    row = lambda w: pl.BlockSpec((tm, w), lambda i: (i, 0))
    full = lambda a: pl.BlockSpec(a.shape, lambda i: (0,) * a.ndim)
    once = lambda a: pl.BlockSpec(a.shape, lambda i: (0,) * a.ndim, pipeline_mode=pl.Buffered(1))
    return pl.pallas_call(
        functools.partial(_merge_kernel, tm=tm, n_blk=n_blk),
        grid=(nt,),
        in_specs=[row(D_MODEL), row(D_SSM), row(ATT_W), full(g_mix), once(wg), full(bg), once(wa), once(wb),
                  once(wo), full(g_ffn), full(wr), full(br)],
        out_specs=[row(D_MODEL), row(LANES), pl.BlockSpec((1, 8, tm), lambda i: (i, 0, 0)),
                   pl.BlockSpec((1, 8, LANES), lambda i: (i, 0, 0)), pl.BlockSpec((8, LANES), lambda i: (0, 0)),
                   pl.BlockSpec(memory_space=pl.ANY)],
        out_shape=[jax.ShapeDtypeStruct((t, D_MODEL), F32),
                   jax.ShapeDtypeStruct((t, LANES), F32),
                   jax.ShapeDtypeStruct((nt, 8, tm), jnp.int32),
                   jax.ShapeDtypeStruct((nt, 8, LANES), F32),
                   jax.ShapeDtypeStruct((8, LANES), F32),
                   jax.ShapeDtypeStruct((n_rows, PACK_W), jnp.uint32)],
        scratch_shapes=[pltpu.VMEM((tm, tm), BF16), pltpu.VMEM((LANES, LANES), BF16), pltpu.VMEM((8, LANES), F32),
                        pltpu.VMEM((NBUF, tm, PACK_W), jnp.uint32), pltpu.VMEM((NBUF, 8, tm), jnp.int32),
                        pltpu.SMEM((NBUF, 8, tm), jnp.int32), pltpu.VMEM((8, LANES), jnp.int32),
                        pltpu.SMEM((8, LANES), jnp.int32), pltpu.VMEM((MOE_BLK, PACK_W), jnp.uint32),
                        pltpu.VMEM((tm, D_MODEL), BF16), pltpu.VMEM((2, tm, D_MODEL), F32),
                        pltpu.SemaphoreType.DMA((NBUF,)), pltpu.SemaphoreType.DMA((NBUF,)),
                        pltpu.SemaphoreType.DMA],
        compiler_params=_cparams(("arbitrary",)),
        name="merge",
    )(x2d, ya, yb, g_mix, wg, bg, wa, wb, wo, g_ffn, wr, br)


def _moe_kernel(be_ref, bn_ref, bi_ref, xd_ref, wg_ref, wu_ref, wd_ref, yd_ref, wgb_ref, wub_ref, wdb_ref):
    del bi_ref
    i = pl.program_id(0)

    @pl.when(bn_ref[i] == 0)
    def _():
        yd_ref[...] = jnp.zeros_like(yd_ref)

    @pl.when(bn_ref[i] > 0)
    def _():
        prev = be_ref[jnp.maximum(i - 1, 0)]

        @pl.when(jnp.logical_or(i == 0, be_ref[i] != prev))
        def _():
            wgb_ref[...] = wg_ref[0].astype(BF16)
            wub_ref[...] = wu_ref[0].astype(BF16)
            wdb_ref[...] = wd_ref[0].astype(BF16)

        x = _unpack_rows(xd_ref[...]).astype(BF16)
        g = _dot(x, wgb_ref[...])
        u = _dot(x, wub_ref[...])
        a = (g * _sigmoid(g) * u).astype(BF16)
        yd_ref[...] = _pack_rows(_dot(a, wdb_ref[...]))


def _moe(blk_e, blk_n, blk_i, xdw, w_eg, w_eu, w_ed):
    n_blk = blk_e.shape[0]
    grid_spec = pltpu.PrefetchScalarGridSpec(
        num_scalar_prefetch=3,
        grid=(n_blk,),
        in_specs=[pl.BlockSpec((MOE_BLK, PACK_W), lambda i, be, bn, bi: (bi[i], 0)),
                  pl.BlockSpec((1, D_MODEL, D_FF_EXPERT), lambda i, be, bn, bi: (be[i], 0, 0)),
                  pl.BlockSpec((1, D_MODEL, D_FF_EXPERT), lambda i, be, bn, bi: (be[i], 0, 0)),
                  pl.BlockSpec((1, D_FF_EXPERT, D_MODEL), lambda i, be, bn, bi: (be[i], 0, 0))],
        out_specs=pl.BlockSpec((MOE_BLK, PACK_W), lambda i, be, bn, bi: (bi[i], 0)),
        scratch_shapes=[pltpu.VMEM((D_MODEL, D_FF_EXPERT), BF16), pltpu.VMEM((D_MODEL, D_FF_EXPERT), BF16),
                        pltpu.VMEM((D_FF_EXPERT, D_MODEL), BF16)],
    )
    return pl.pallas_call(
        _moe_kernel,
        grid_spec=grid_spec,
        out_shape=jax.ShapeDtypeStruct((n_blk * MOE_BLK, PACK_W), jnp.uint32),
        compiler_params=_cparams(("arbitrary",)),
        name="moe",
    )(blk_e, blk_n, blk_i, xdw, w_eg, w_eu, w_ed)


def _final_kernel(dcur_ref, dnxt_ref, x1_ref, p_ref, route_ref, yd_hbm, wpp_ref, wpg_ref, bpg_ref, gple_ref,
                  gfin_ref, o_ref, yw0_ref, yw1_ref, sem, *, tm):
    j = pl.program_id(0)
    last = pl.num_programs(0) - 1
    n = TOP_K * tm
    yw = (yw0_ref, yw1_ref)

    def gather(dref, half, slot):
        return [_row_copy(yd_hbm, yw[slot], sem.at[slot], dref[0, r // tm, half * tm + r % tm], r)
                for r in range(n)]

    def tile(half, slot):
        rows = slice(half * tm, (half + 1) * tm)
        rt = route_ref[rows, :]
        y0 = _unpack_rows(yw[slot][0:tm, :])
        y1 = _unpack_rows(yw[slot][tm:n, :])
        x2 = x1_ref[rows, :] + rt[:, 2:3] * y0 + rt[:, 3:4] * y1
        h3 = _rms(x2, gple_ref[...]).astype(BF16)
        pp = _dot(p_ref[rows, :].astype(BF16), wpp_ref[...])
        x3 = x2 + pp * _sigmoid(_dot(h3, wpg_ref[...]) + bpg_ref[...])
        o_ref[rows, :] = _rms(x3, gfin_ref[...])

    @pl.when(j == 0)
    def _():
        _start_all(gather(dcur_ref, 0, 0))

    for c in gather(dcur_ref, 0, 0):
        c.wait()
    _start_all(gather(dcur_ref, 1, 1))
    tile(0, 0)
    for c in gather(dcur_ref, 1, 1):
        c.wait()
    _start_all(gather(dnxt_ref, 0, 0))
    tile(1, 1)

    @pl.when(j == last)
    def _():
        for c in gather(dnxt_ref, 0, 0):
            c.wait()


def _final(dest, x1, p2d, route, ydw, wpp, wpg, bpg, g_ple, g_final):
    t = x1.shape[0]
    ns, _, two_tm = dest.shape
    tm = two_tm // 2
    n = TOP_K * tm
    row = lambda w: pl.BlockSpec((2 * tm, w), lambda i: (i, 0))
    full = lambda a: pl.BlockSpec(a.shape, lambda i: (0,) * a.ndim)
    return pl.pallas_call(
        functools.partial(_final_kernel, tm=tm),
        grid=(ns,),
        in_specs=[pl.BlockSpec((1, 8, two_tm), lambda i: (i, 0, 0), memory_space=pltpu.SMEM),
                  pl.BlockSpec((1, 8, two_tm), lambda i: (jnp.minimum(i + 1, ns - 1), 0, 0),
                               memory_space=pltpu.SMEM),
                  row(D_MODEL), row(PLE_DIM), row(LANES), pl.BlockSpec(memory_space=pl.ANY),
                  full(wpp), full(wpg), full(bpg), full(g_ple), full(g_final)],
        out_specs=row(D_MODEL),
        out_shape=jax.ShapeDtypeStruct((t, D_MODEL), F32),
        scratch_shapes=[pltpu.VMEM((n, PACK_W), jnp.uint32), pltpu.VMEM((n, PACK_W), jnp.uint32),
                        pltpu.SemaphoreType.DMA((2,))],
        compiler_params=_cparams(("arbitrary",)),
        name="final",
    )(dest, dest, x1, p2d, route, ydw, wpp, wpg, bpg, g_ple, g_final)


def _block_plan(alloc, state, n_blk):
    new = alloc[:, 0, :N_EXPERTS].astype(jnp.int32).reshape(-1)
    base = alloc[:, 1, :N_EXPERTS].astype(jnp.int32).reshape(-1)
    expert = jnp.tile(jnp.arange(N_EXPERTS, dtype=jnp.int32), alloc.shape[0])
    fill = state[0, :N_EXPERTS].astype(jnp.int32)
    last_blk = state[1, :N_EXPERTS].astype(jnp.int32)
    taken = state[2, 0].astype(jnp.int32)
    b = jnp.arange(n_blk, dtype=jnp.int32)
    opened = (base[None, :] <= b[:, None]) & (b[:, None] < (base + new)[None, :])
    e_of = jnp.sum(jnp.where(opened, expert[None, :], 0), axis=1)
    onehot = e_of[:, None] == jnp.arange(N_EXPERTS, dtype=jnp.int32)[None, :]
    pick = lambda v: jnp.sum(jnp.where(onehot, v[None, :], 0), axis=1)
    rows = jnp.where(b < taken, jnp.where(b == pick(last_blk), pick(fill), MOE_BLK), 0)
    key = jnp.where(b < taken, e_of, N_EXPERTS) * n_blk + b
    pos = jnp.sum(key[None, :] < key[:, None], axis=1)
    at = pos[None, :] == b[:, None]
    order = lambda v: jnp.sum(jnp.where(at, v[None, :], 0), axis=1).astype(jnp.int32)
    return order(jnp.where(b < taken, e_of, N_EXPERTS - 1)), order(rows), order(b)


def _row(v):
    return v.reshape(1, -1).astype(F32)


def kernel(x, p, rel_bias, g_mix, w_in, w_gate, b_gate, ssm_a_re, ssm_a_im, ssm_log_dt, ssm_b_re, ssm_b_im,
           ssm_c_re, ssm_c_im, ssm_d, w_glu, b_glu, sinks, w_br_ssm, w_br_attn, w_out, g_ffn, w_router_group,
           b_router_group, w_router_expert, b_router_expert, w_e_gate, w_e_up, w_e_down, g_ple, w_ple_gate,
           b_ple_gate, w_ple_proj, g_final):
    bsz, seq, dm = x.shape
    assert g_mix.shape[0] == 1, "one layer followed by the final norm"
    i = 0
    t = bsz * seq
    x2d = x.reshape(t, dm)
    bias = _bias_table(rel_bias)
    u, q, kv = _proj(x2d, _row(g_mix[i]), w_in[i].astype(BF16))
    ab_re, ab_im, bb_re, bb_im = _ssmprep(ssm_a_re[i], ssm_a_im[i], ssm_log_dt[i], ssm_b_re[i], ssm_b_im[i])
    ws_in, ws_out, ar, ai = _ssm_weights(ab_re, ab_im, bb_re, bb_im, ssm_c_re[i], ssm_c_im[i])
    y_ssm = _ssm(u.reshape(bsz, seq, D_SSM), ws_in, ws_out, ar, ai, _row(ssm_d[i]),
                 w_glu[i].astype(BF16), _row(b_glu[i]))
    y_att = _attn(q.reshape(bsz, seq, ATT_W), kv.reshape(bsz, seq, KV_W), bias, sinks[i].astype(F32))
    w_router = jnp.zeros((dm, LANES), F32)
    w_router = w_router.at[:, :N_GROUPS].set(w_router_group[i])
    w_router = w_router.at[:, N_GROUPS:N_GROUPS + N_EXPERTS].set(w_router_expert[i])
    b_router = jnp.zeros((1, LANES), F32)
    b_router = b_router.at[0, :N_GROUPS].set(b_router_group[i])
    b_router = b_router.at[0, N_GROUPS:N_GROUPS + N_EXPERTS].set(b_router_expert[i])
    n_blk = t * TOP_K // MOE_BLK + N_EXPERTS
    x1, route, dest, alloc, state, xdw = _merge(
        x2d, y_ssm.reshape(t, D_SSM), y_att.reshape(t, ATT_W), _row(g_mix[i]), w_gate[i].astype(BF16),
        _row(b_gate[i]), w_br_ssm[i].astype(BF16), w_br_attn[i].astype(BF16), w_out[i].astype(BF16),
        _row(g_ffn[i]), w_router.astype(BF16), b_router, n_blk)
    blk_e, blk_n, blk_i = _block_plan(alloc, state, n_blk)
    ydw = _moe(blk_e, blk_n, blk_i, xdw, w_e_gate[i], w_e_up[i], w_e_down[i])
    out = _final(dest, x1, p[i].reshape(t, PLE_DIM), route, ydw, w_ple_proj[i].astype(BF16),
                 w_ple_gate[i].astype(BF16), _row(b_ple_gate[i]), _row(g_ple[i]), _row(g_final))
    return out.reshape(bsz, seq, dm)
```

```python
import functools
import math

import numpy as np
import jax
import jax.numpy as jnp
from jax import lax
from jax.experimental import pallas as pl
from jax.experimental.pallas import tpu as pltpu

D_MODEL = 1024
D_SSM = 512
SSM_CH = 16
SSM_GROUPS = D_SSM // SSM_CH
SSM_STATE = 64
N_STATE = SSM_GROUPS * SSM_STATE
N_HEADS = 8
N_KV = 2
HEAD_DIM = 64
GROUP = N_HEADS // N_KV
ATT_W = N_HEADS * HEAD_DIM
KV_W = 2 * N_KV * HEAD_DIM
WINDOW = 128
BLOCK = 128
NUM_BUCKETS = 32
MAX_DIST = 128
D_IN = D_SSM + ATT_W + KV_W
N_GROUPS = 4
EXPERTS_PER_GROUP = 8
N_EXPERTS = N_GROUPS * EXPERTS_PER_GROUP
TOP_K = 2
D_FF_EXPERT = 512
MOE_BLK = 512
PLE_DIM = 256
EPS = 1e-6

LANES = 128
NEG = -1e30
PACK_W = D_MODEL // 2

BF16 = jnp.bfloat16
F32 = jnp.float32
VMEM_LIMIT = 56 * 1024 * 1024
NBUF = 3


def _cparams(sem):
    return pltpu.CompilerParams(dimension_semantics=sem, vmem_limit_bytes=VMEM_LIMIT)


def _rms(x, g):
    ms = jnp.mean(x * x, axis=-1, keepdims=True)
    return x * lax.rsqrt(ms + EPS) * g


def _sigmoid(x):
    return 1.0 / (1.0 + jnp.exp(-x))


def _dot(a, b):
    return jnp.dot(a, b, preferred_element_type=F32)


def _pack_rows(y):
    lo = pltpu.bitcast(y[:, :PACK_W].astype(BF16).astype(F32), jnp.uint32)
    hi = pltpu.bitcast(y[:, PACK_W:].astype(BF16).astype(F32), jnp.uint32)
    return hi | lax.shift_right_logical(lo, jnp.uint32(16))


def _unpack_rows(w):
    lo = pltpu.bitcast(lax.shift_left(w, jnp.uint32(16)), F32)
    hi = pltpu.bitcast(w & jnp.uint32(0xFFFF0000), F32)
    return jnp.concatenate([lo, hi], axis=1)


def _proj_kernel(x_ref, g_ref, w_ref, u_ref, q_ref, kv_ref):
    h = _rms(x_ref[...], g_ref[...]).astype(BF16)
    proj = _dot(h, w_ref[...])
    u_ref[...] = proj[:, :D_SSM].astype(BF16)
    q_ref[...] = (proj[:, D_SSM:D_SSM + ATT_W] * (HEAD_DIM ** -0.5)).astype(BF16)
    kv_ref[...] = proj[:, D_SSM + ATT_W:].astype(BF16)


def _time_major_spec(tm, seq):
    per_batch = seq // tm
    return pl.BlockSpec((tm, D_SSM), lambda i: (i % per_batch, i // per_batch))


def _proj(x2d, g_mix, w_in_b, seq, tm=512):
    t = x2d.shape[0]
    row = lambda w: pl.BlockSpec((tm, w), lambda i: (i, 0))
    full = lambda a: pl.BlockSpec(a.shape, lambda i: (0,) * a.ndim)
    return pl.pallas_call(
        _proj_kernel,
        grid=(t // tm,),
        in_specs=[row(D_MODEL), full(g_mix), full(w_in_b)],
        out_specs=[_time_major_spec(tm, seq), row(ATT_W), row(KV_W)],
        out_shape=[jax.ShapeDtypeStruct((seq, t // seq * D_SSM), BF16),
                   jax.ShapeDtypeStruct((t, ATT_W), BF16),
                   jax.ShapeDtypeStruct((t, KV_W), BF16)],
        compiler_params=_cparams(("parallel",)),
        name="proj",
    )(x2d, g_mix, w_in_b)


def _ssmprep_kernel(are_ref, aim_ref, ldt_ref, bre_ref, bim_ref, abr_ref, abi_ref, bbr_ref, bbi_ref):
    a_re, a_im = are_ref[...], aim_ref[...]
    dt = jnp.exp(ldt_ref[...])
    mag = jnp.exp(a_re * dt)
    ab_re = mag * jnp.cos(a_im * dt)
    ab_im = mag * jnp.sin(a_im * dt)
    abr_ref[...] = ab_re
    abi_ref[...] = ab_im
    den = a_re * a_re + a_im * a_im
    c_re = ((ab_re - 1.0) * a_re + ab_im * a_im) / den
    c_im = (ab_im * a_re - (ab_re - 1.0) * a_im) / den
    for c in range(SSM_CH):
        b_re, b_im = bre_ref[c], bim_ref[c]
        bbr_ref[c] = c_re * b_re - c_im * b_im
        bbi_ref[c] = c_re * b_im + c_im * b_re


def _ssmprep(a_re, a_im, log_dt, b_re, b_im):
    g, p = a_re.shape
    bt_re = jnp.transpose(b_re, (2, 0, 1))
    bt_im = jnp.transpose(b_im, (2, 0, 1))
    gp = jax.ShapeDtypeStruct((g, p), F32)
    cgp = jax.ShapeDtypeStruct((SSM_CH, g, p), F32)
    return pl.pallas_call(
        _ssmprep_kernel, out_shape=[gp, gp, cgp, cgp], name="ssmprep",
    )(a_re, a_im, log_dt.reshape(g, 1), bt_re, bt_im)


def _ssm_weights(ab_re, ab_im, bb_re, bb_im, c_re, c_im):
    g, p = ab_re.shape
    kb = D_SSM // 256
    gk = g // kb
    eye = jnp.eye(gk, dtype=F32)

    def in_block(bb):
        b4 = jnp.transpose(bb, (1, 0, 2)).reshape(kb, gk, SSM_CH, p)
        return jnp.einsum("kgcp,gh->kgchp", b4, eye).reshape(kb, gk * SSM_CH, gk * p)

    w_in = jnp.concatenate([in_block(bb_re), in_block(bb_im)], axis=2).astype(BF16)
    nj = D_SSM // LANES
    gj = g // nj
    eyej = jnp.eye(gj, dtype=F32)

    def out_block(c):
        c4 = jnp.transpose(c, (0, 2, 1)).reshape(nj, gj, p, SSM_CH)
        return jnp.einsum("jgpc,gh->jgphc", c4, eyej).reshape(nj, gj * p, gj * SSM_CH)

    w_out = jnp.stack([out_block(c_re), out_block(-c_im)], axis=1).astype(BF16)
    return w_in, w_out, ab_re.reshape(1, g * p), ab_im.reshape(1, g * p)


def _gelu_tanh(x):
    return 0.5 * x * (1.0 + jnp.tanh(math.sqrt(2.0 / math.pi) * (x + 0.044715 * (x * x * x))))


def _ssm_kernel(u_ref, ul_ref, win_ref, wout_ref, ar_ref, ai_ref, d_ref, wglu_ref, bglu_ref, y_ref,
                bu0_ref, bu1_ref, xb0_ref, xb1_ref, st_ref, *, nb, lc):
    rows = nb * lc
    ns = N_STATE
    k = pl.program_id(0)
    bu = (bu0_ref, bu1_ref)
    xb = (xb0_ref, xb1_ref)

    @pl.when(k == 0)
    def _():
        bu1_ref[...] = jnp.zeros_like(bu1_ref)
        xb0_ref[...] = jnp.zeros_like(xb0_ref)
        xb1_ref[...] = jnp.zeros_like(xb1_ref)
        st_ref[...] = jnp.zeros_like(st_ref)

    kb = win_ref.shape[0]
    half = ns // kb
    nj = wout_ref.shape[0]
    kw = ns // nj
    ar, ai = ar_ref[...], ai_ref[...]
    xr, xi = st_ref[:, 0:ns], st_ref[:, ns:2 * ns]

    for h in range(2):
        part = slice(h * rows, (h + 1) * rows)
        u_new = u_ref[part, :]
        for j in range(kb):
            res = _dot(u_new[:, 256 * j:256 * (j + 1)], win_ref[j])
            bu[h][:, half * j:half * (j + 1)] = res[:, :half]
            bu[h][:, ns + half * j:ns + half * (j + 1)] = res[:, half:]
        src, dst = bu[1 - h], xb[1 - h]
        for t in range(0, lc, 2):
            keep_r, keep_i = [], []
            for s in range(2):
                row = (t + s) * nb
                br = src[row:row + nb, 0:ns]
                bi = src[row:row + nb, ns:2 * ns]
                xr, xi = ar * xr - ai * xi + br, ar * xi + ai * xr + bi
                keep_r.append(xr)
                keep_i.append(xi)
            dst[t * nb:(t + 2) * nb, 0:ns] = jnp.concatenate(keep_r, axis=0).astype(BF16)
            dst[t * nb:(t + 2) * nb, ns:2 * ns] = jnp.concatenate(keep_i, axis=0).astype(BF16)
        ys = []
        for j in range(nj):
            ys.append(_dot(xb[h][:, kw * j:kw * (j + 1)], wout_ref[j, 0])
                      + _dot(xb[h][:, ns + kw * j:ns + kw * (j + 1)], wout_ref[j, 1]))
        y = jnp.concatenate(ys, axis=1) + d_ref[...] * ul_ref[part, :].astype(F32)
        y = _gelu_tanh(y)
        z = _dot(y.astype(BF16), wglu_ref[...]) + bglu_ref[...]
        y_ref[part, :] = (y * _sigmoid(z)).astype(BF16)

    st_ref[:, 0:ns] = xr
    st_ref[:, ns:2 * ns] = xi


def _ssm(u_tm, nb, w_in, w_out, ar, ai, d_skip, w_glu_b, b_glu, lc=64):
    t = u_tm.shape[0]
    rows = nb * lc
    nk = t // (2 * rows)
    assert t % (2 * rows) == 0 and lc % 16 == 0 and nb % 8 == 0
    full = lambda a: pl.BlockSpec(a.shape, lambda i: (0,) * a.ndim)
    lead = pl.BlockSpec((2 * rows, D_SSM), lambda i: (jnp.minimum(i, nk - 1), 0))
    lag = pl.BlockSpec((2 * rows, D_SSM), lambda i: (jnp.maximum(i - 1, 0), 0))
    return pl.pallas_call(
        functools.partial(_ssm_kernel, nb=nb, lc=lc),
        grid=(nk + 1,),
        in_specs=[lead, lag, full(w_in), full(w_out), full(ar), full(ai), full(d_skip), full(w_glu_b), full(b_glu)],
        out_specs=lag,
        out_shape=jax.ShapeDtypeStruct((t, D_SSM), BF16),
        scratch_shapes=[pltpu.VMEM((rows, 2 * N_STATE), F32), pltpu.VMEM((rows, 2 * N_STATE), F32),
                        pltpu.VMEM((rows, 2 * N_STATE), BF16), pltpu.VMEM((rows, 2 * N_STATE), BF16),
                        pltpu.VMEM((nb, 2 * N_STATE), F32)],
        compiler_params=_cparams(("arbitrary",)),
        name="ssm",
    )(u_tm, u_tm, w_in, w_out, ar, ai, d_skip, w_glu_b, b_glu)


def _t5_bucket_np(rel):
    max_exact = NUM_BUCKETS // 2
    relf = np.maximum(rel, 1).astype(np.float32)
    large = max_exact + (np.log(relf / np.float32(max_exact)) / np.float32(math.log(MAX_DIST / max_exact))
                         * np.float32(NUM_BUCKETS - max_exact)).astype(np.int32)
    large = np.minimum(large, NUM_BUCKETS - 1)
    return np.where(rel < max_exact, rel, large)


def _bias_table(rel_bias):
    q_loc = np.arange(BLOCK)[:, None]
    c_loc = np.arange(2 * BLOCK)[None, :]
    rel = q_loc + BLOCK - c_loc
    valid = (rel >= 0) & (rel < WINDOW)
    bucket = _t5_bucket_np(np.maximum(rel, 0)).reshape(-1, 1)
    onehot = (jnp.asarray(bucket) == jnp.arange(NUM_BUCKETS)[None, :]).astype(F32)
    bias = jnp.dot(onehot, rel_bias.astype(F32), precision=lax.Precision.HIGHEST)
    bias = jnp.where(valid.reshape(-1, 1), bias, NEG)
    return jnp.transpose(bias, (1, 0)).reshape(N_HEADS, BLOCK, 2 * BLOCK)


def _attn_kernel(sink_ref, q_ref, kvp_ref, kvc_ref, bias_ref, o_ref, *, nq):
    n = pl.program_id(1)
    kv_all = jnp.concatenate([kvp_ref[0], kvc_ref[0]], axis=0)
    col = lax.broadcasted_iota(jnp.int32, (BLOCK, 2 * BLOCK), 1)
    no_prev = jnp.where(col < BLOCK, jnp.where(n == 0, NEG, 0.0), 0.0)
    for j in range(nq):
        q = q_ref[0, BLOCK * j:BLOCK * (j + 1), :]
        kv = kv_all[BLOCK * j:BLOCK * (j + 2), :]
        outs = []
        for h in range(N_HEADS):
            g = h // GROUP
            qh = q[:, HEAD_DIM * h:HEAD_DIM * (h + 1)]
            kh = kv[:, HEAD_DIM * g:HEAD_DIM * (g + 1)]
            vh = kv[:, N_KV * HEAD_DIM + HEAD_DIM * g:N_KV * HEAD_DIM + HEAD_DIM * (g + 1)]
            s = lax.dot_general(qh, kh, (((1,), (1,)), ((), ())), preferred_element_type=F32)
            s = s + bias_ref[h]
            if j == 0:
                s = s + no_prev
            sink = sink_ref[h]
            m = jnp.maximum(jnp.max(s, axis=-1, keepdims=True), sink)
            e = jnp.exp(s - m)
            den = jnp.sum(e, axis=-1, keepdims=True) + jnp.exp(sink - m)
            outs.append(_dot(e.astype(BF16), vh) / den)
        o_ref[0, BLOCK * j:BLOCK * (j + 1), :] = jnp.concatenate(outs, axis=1).astype(BF16)


def _attn(q3, kv3, bias, sinks, nq=4):
    b, s, _ = q3.shape
    qb = nq * BLOCK
    grid_spec = pltpu.PrefetchScalarGridSpec(
        num_scalar_prefetch=0,
        grid=(b, s // qb),
        in_specs=[pl.BlockSpec(memory_space=pltpu.SMEM),
                  pl.BlockSpec((1, qb, ATT_W), lambda i, n: (i, n, 0)),
                  pl.BlockSpec((1, BLOCK, KV_W), lambda i, n: (i, jnp.maximum(n * nq - 1, 0), 0)),
                  pl.BlockSpec((1, qb, KV_W), lambda i, n: (i, n, 0)),
                  pl.BlockSpec(bias.shape, lambda i, n: (0, 0, 0))],
        out_specs=pl.BlockSpec((1, qb, ATT_W), lambda i, n: (i, n, 0)),
    )
    return pl.pallas_call(
        functools.partial(_attn_kernel, nq=nq),
        grid_spec=grid_spec,
        out_shape=jax.ShapeDtypeStruct((b, s, ATT_W), BF16),
        compiler_params=_cparams(("parallel", "parallel")),
        name="attn",
    )(sinks, q3, kv3, kv3, bias)


def _route(logits):
    lane = lax.broadcasted_iota(jnp.int32, logits.shape, 1).astype(F32)
    big = float(LANES)
    gl = jnp.where(lane < N_GROUPS, logits, NEG)
    mg = jnp.max(gl, axis=-1, keepdims=True)
    gidx = jnp.min(jnp.where(gl == mg, lane, big), axis=-1, keepdims=True)
    gsum = jnp.sum(jnp.where(lane < N_GROUPS, jnp.exp(gl - mg), 0.0), axis=-1, keepdims=True)
    gp = 1.0 / gsum
    lo = N_GROUPS + EXPERTS_PER_GROUP * gidx
    el = jnp.where(lane >= lo, jnp.where(lane < lo + EXPERTS_PER_GROUP, logits, NEG), NEG)
    m1 = jnp.max(el, axis=-1, keepdims=True)
    i1 = jnp.min(jnp.where(el == m1, lane, big), axis=-1, keepdims=True)
    el2 = jnp.where(lane == i1, NEG, el)
    m2 = jnp.max(el2, axis=-1, keepdims=True)
    i2 = jnp.min(jnp.where(el2 == m2, lane, big), axis=-1, keepdims=True)
    t = jnp.exp(m2 - m1)
    w1 = gp / (1.0 + t)
    w2 = gp * t / (1.0 + t)
    return jnp.where(lane == 0, i1 - N_GROUPS,
                     jnp.where(lane == 1, i2 - N_GROUPS,
                               jnp.where(lane == 2, w1, jnp.where(lane == 3, w2, 0.0))))


def _row_copy(src_ref, dst_ref, sem, src_row, dst_row):
    return pltpu.make_async_copy(src_ref.at[pl.ds(src_row, 1)], dst_ref.at[pl.ds(dst_row, 1)], sem)


def _start_all(copies):
    for k, c in enumerate(copies):
        c.start(priority=k % 2)


def _merge_kernel(x_ref, ya_ref, yb_ref, gmix_ref, wg_ref, bg_ref, wa_ref, wb_ref, wo_ref, gffn_ref, wr_ref, br_ref,
                  x1_ref, route_ref, dest_ref, alloc_ref, state_ref, xd_hbm,
                  tri_ref, pre_ref, st_ref, hbuf_ref, destv_ref, dests_ref, stv_ref, sts_ref, zblk_ref,
                  hs_ref, gate_ref, sem_rows, sem_idx, sem_pad, *, tm, n_blk):
    i = pl.program_id(0)
    last = pl.num_programs(0) - 1
    slot = lax.rem(i, NBUF)
    prev = lax.rem(i + NBUF - 1, NBUF)
    n = TOP_K * tm
    dump0 = n_blk * MOE_BLK

    def row_copies(s, lo=0, hi=n):
        return [_row_copy(hbuf_ref.at[s], xd_hbm, sem_rows.at[s], r % tm, dests_ref[s, r // tm, r % tm])
                for r in range(lo, hi)]

    def wait_rows(s):
        for _ in range(n):
            _row_copy(hbuf_ref.at[s], xd_hbm, sem_rows.at[s], 0, dump0).wait()

    def index_copy(s):
        return pltpu.make_async_copy(destv_ref.at[s], dests_ref.at[s], sem_idx.at[s])

    @pl.when(i == 0)
    def _():
        r = lax.broadcasted_iota(jnp.int32, (tm, tm), 0)
        c = lax.broadcasted_iota(jnp.int32, (tm, tm), 1)
        tri_ref[...] = jnp.where(r > c, 1.0, 0.0).astype(BF16)
        r = lax.broadcasted_iota(jnp.int32, (LANES, LANES), 0)
        c = lax.broadcasted_iota(jnp.int32, (LANES, LANES), 1)
        pre_ref[...] = jnp.where(r < c, 1.0, 0.0).astype(BF16)
        r = lax.broadcasted_iota(jnp.int32, (8, LANES), 0)
        st_ref[...] = jnp.where(r == 0, float(MOE_BLK), 0.0)
        hbuf_ref[NBUF - 1] = jnp.zeros((tm, PACK_W), jnp.uint32)

        def spare(t, carry):
            dests_ref[NBUF - 1, 0, t] = dump0 + t
            dests_ref[NBUF - 1, 1, t] = dump0 + tm + t
            return carry

        lax.fori_loop(0, tm, spare, 0)

    @pl.when(i >= NBUF - 1)
    def _():
        wait_rows(slot)

    @pl.when(i >= 1)
    def _():
        index_copy(prev).wait()

    group = n // 4
    once = jnp.minimum(dests_ref[prev, 0, 0], 0) + 1

    def region(body):
        lax.fori_loop(0, once, lambda _, c: (body(), c)[1], 0)

    def gates_lo():
        _start_all(row_copies(prev, 0, group))
        h = _rms(x_ref[...], gmix_ref[...]).astype(BF16)
        hs_ref[...] = h
        gate_ref[0] = _sigmoid(_dot(h, wg_ref[:, 0:D_MODEL]) + bg_ref[:, 0:D_MODEL])

    def gates_hi():
        _start_all(row_copies(prev, group, 2 * group))
        gate_ref[1] = _sigmoid(_dot(hs_ref[...], wg_ref[:, D_MODEL:2 * D_MODEL]) + bg_ref[:, D_MODEL:2 * D_MODEL])

    def branches():
        _start_all(row_copies(prev, 2 * group, 3 * group))
        merged = gate_ref[0] * _dot(ya_ref[...], wa_ref[...]) + gate_ref[1] * _dot(yb_ref[...], wb_ref[...])
        hs_ref[...] = merged.astype(BF16)

    region(gates_lo)
    region(gates_hi)
    region(branches)
    _start_all(row_copies(prev, 3 * group, n))
    x1 = x_ref[...] + _dot(hs_ref[...], wo_ref[...])
    x1_ref[...] = x1
    h2 = _rms(x1, gffn_ref[...])
    hbuf_ref[slot] = _pack_rows(h2)
    route = _route(_dot(h2.astype(BF16), wr_ref[...]) + br_ref[...])
    route_ref[...] = route

    lane = lax.broadcasted_iota(jnp.int32, route.shape, 1).astype(F32)
    e1, e2 = route[:, 0:1], route[:, 1:2]
    is1, is2 = lane == e1, lane == e2
    member = jnp.where(is1, 1.0, jnp.where(is2, 1.0, 0.0))
    before = _dot(tri_ref[...], member.astype(BF16))
    cnt = jnp.sum(member, axis=0, keepdims=True)
    fill, blk, free = st_ref[0:1, :], st_ref[1:2, :], st_ref[2:3, :]
    need = fill + cnt
    new = jnp.floor((need + float(MOE_BLK - 1)) * (1.0 / MOE_BLK)) - 1.0
    base = free + _dot(jnp.broadcast_to(new, (8, LANES)).astype(BF16), pre_ref[...])[0:1, :]
    q = fill + before
    jb = jnp.floor(q * (1.0 / MOE_BLK))
    rowid = jnp.where(jb == 0.0, blk, base + jb - 1.0) * float(MOE_BLK) + (q - jb * float(MOE_BLK))
    d1 = jnp.sum(jnp.where(is1, rowid, 0.0), axis=-1, keepdims=True)
    d2 = jnp.sum(jnp.where(is2, rowid, 0.0), axis=-1, keepdims=True)
    slab = jnp.where(lane == 0, d1, jnp.where(lane == 1, d2, 0.0))
    d8 = jnp.transpose(slab, (1, 0))[0:8, :].astype(jnp.int32)
    destv_ref[slot] = d8
    dest_ref[0] = d8
    index_copy(slot).start()

    r8 = lax.broadcasted_iota(jnp.int32, (8, LANES), 0)
    alloc_ref[0] = jnp.where(r8 == 0, new, jnp.where(r8 == 1, base, 0.0))
    state = jnp.where(r8 == 0, need - new * float(MOE_BLK),
                      jnp.where(r8 == 1, jnp.where(new > 0.0, base + new - 1.0, blk),
                                jnp.where(r8 == 2, free + jnp.sum(new, axis=-1, keepdims=True), 0.0)))
    st_ref[...] = state
    state_ref[...] = state

    @pl.when(i == last)
    def _():
        index_copy(slot).wait()
        _start_all(row_copies(slot))
        for s in range(NBUF):
            wait_rows(s)
        stv_ref[...] = state.astype(jnp.int32)
        cp = pltpu.make_async_copy(stv_ref, sts_ref, sem_pad)
        cp.start()
        cp.wait()
        zblk_ref[...] = jnp.zeros_like(zblk_ref)

        def zero_rows(e, carry):
            first = sts_ref[1, e] * MOE_BLK

            def start(r, c):
                _row_copy(zblk_ref, xd_hbm, sem_pad, 0, first + r).start()
                return c

            def wait(r, c):
                _row_copy(zblk_ref, xd_hbm, sem_pad, 0, first + r).wait()
                return c

            lax.fori_loop(sts_ref[0, e], MOE_BLK, start, 0)
            lax.fori_loop(sts_ref[0, e], MOE_BLK, wait, 0)
            return carry

        lax.fori_loop(0, N_EXPERTS, zero_rows, 0)

        def block_copy(b):
            return pltpu.make_async_copy(zblk_ref, xd_hbm.at[pl.ds(pl.multiple_of(b * MOE_BLK, MOE_BLK), MOE_BLK)],
                                         sem_pad)

        def zero_block(b, c):
            block_copy(b).start()
            block_copy(b).wait()
            return c

        lax.fori_loop(sts_ref[2, 0], n_blk, zero_block, 0)


def _merge(x2d, ya_tm, yb, g_mix, wg, bg, wa, wb, wo, g_ffn, wr, br, n_blk, tm=512):
    t = x2d.shape[0]
    nt = t // tm
    n_rows = n_blk * MOE_BLK + TOP_K * tm
    row = lambda w: pl.BlockSpec((tm, w), lambda i: (i, 0))
    full = lambda a: pl.BlockSpec(a.shape, lambda i: (0,) * a.ndim)
    once = lambda a: pl.BlockSpec(a.shape, lambda i: (0,) * a.ndim, pipeline_mode=pl.Buffered(1))
    return pl.pallas_call(
        functools.partial(_merge_kernel, tm=tm, n_blk=n_blk),
        grid=(nt,),
        in_specs=[row(D_MODEL), _time_major_spec(tm, ya_tm.shape[0]), row(ATT_W), full(g_mix), once(wg), full(bg),
                  once(wa), once(wb), once(wo), full(g_ffn), full(wr), full(br)],
        out_specs=[row(D_MODEL), row(LANES), pl.BlockSpec((1, 8, tm), lambda i: (i, 0, 0)),
                   pl.BlockSpec((1, 8, LANES), lambda i: (i, 0, 0)), pl.BlockSpec((8, LANES), lambda i: (0, 0)),
                   pl.BlockSpec(memory_space=pl.ANY)],
        out_shape=[jax.ShapeDtypeStruct((t, D_MODEL), F32),
                   jax.ShapeDtypeStruct((t, LANES), F32),
                   jax.ShapeDtypeStruct((nt, 8, tm), jnp.int32),
                   jax.ShapeDtypeStruct((nt, 8, LANES), F32),
                   jax.ShapeDtypeStruct((8, LANES), F32),
                   jax.ShapeDtypeStruct((n_rows, PACK_W), jnp.uint32)],
        scratch_shapes=[pltpu.VMEM((tm, tm), BF16), pltpu.VMEM((LANES, LANES), BF16), pltpu.VMEM((8, LANES), F32),
                        pltpu.VMEM((NBUF, tm, PACK_W), jnp.uint32), pltpu.VMEM((NBUF, 8, tm), jnp.int32),
                        pltpu.SMEM((NBUF, 8, tm), jnp.int32), pltpu.VMEM((8, LANES), jnp.int32),
                        pltpu.SMEM((8, LANES), jnp.int32), pltpu.VMEM((MOE_BLK, PACK_W), jnp.uint32),
                        pltpu.VMEM((tm, D_MODEL), BF16), pltpu.VMEM((2, tm, D_MODEL), F32),
                        pltpu.SemaphoreType.DMA((NBUF,)), pltpu.SemaphoreType.DMA((NBUF,)),
                        pltpu.SemaphoreType.DMA],
        compiler_params=_cparams(("arbitrary",)),
        name="merge",
    )(x2d, ya_tm, yb, g_mix, wg, bg, wa, wb, wo, g_ffn, wr, br)


def _moe_kernel(be_ref, bn_ref, bi_ref, xd_ref, wg_ref, wu_ref, wd_ref, yd_ref, wgb_ref, wub_ref, wdb_ref):
    del bi_ref
    i = pl.program_id(0)

    @pl.when(bn_ref[i] == 0)
    def _():
        yd_ref[...] = jnp.zeros_like(yd_ref)

    @pl.when(bn_ref[i] > 0)
    def _():
        prev = be_ref[jnp.maximum(i - 1, 0)]

        @pl.when(jnp.logical_or(i == 0, be_ref[i] != prev))
        def _():
            wgb_ref[...] = wg_ref[0].astype(BF16)
            wub_ref[...] = wu_ref[0].astype(BF16)
            wdb_ref[...] = wd_ref[0].astype(BF16)

        x = _unpack_rows(xd_ref[...]).astype(BF16)
        g = _dot(x, wgb_ref[...])
        u = _dot(x, wub_ref[...])
        a = (g * _sigmoid(g) * u).astype(BF16)
        yd_ref[...] = _pack_rows(_dot(a, wdb_ref[...]))


def _moe(blk_e, blk_n, blk_i, xdw, w_eg, w_eu, w_ed):
    n_blk = blk_e.shape[0]
    grid_spec = pltpu.PrefetchScalarGridSpec(
        num_scalar_prefetch=3,
        grid=(n_blk,),
        in_specs=[pl.BlockSpec((MOE_BLK, PACK_W), lambda i, be, bn, bi: (bi[i], 0)),
                  pl.BlockSpec((1, D_MODEL, D_FF_EXPERT), lambda i, be, bn, bi: (be[i], 0, 0)),
                  pl.BlockSpec((1, D_MODEL, D_FF_EXPERT), lambda i, be, bn, bi: (be[i], 0, 0)),
                  pl.BlockSpec((1, D_FF_EXPERT, D_MODEL), lambda i, be, bn, bi: (be[i], 0, 0))],
        out_specs=pl.BlockSpec((MOE_BLK, PACK_W), lambda i, be, bn, bi: (bi[i], 0)),
        scratch_shapes=[pltpu.VMEM((D_MODEL, D_FF_EXPERT), BF16), pltpu.VMEM((D_MODEL, D_FF_EXPERT), BF16),
                        pltpu.VMEM((D_FF_EXPERT, D_MODEL), BF16)],
    )
    return pl.pallas_call(
        _moe_kernel,
        grid_spec=grid_spec,
        out_shape=jax.ShapeDtypeStruct((n_blk * MOE_BLK, PACK_W), jnp.uint32),
        compiler_params=_cparams(("arbitrary",)),
        name="moe",
    )(blk_e, blk_n, blk_i, xdw, w_eg, w_eu, w_ed)


def _final_kernel(dcur_ref, dnxt_ref, x1_ref, p_ref, route_ref, yd_hbm, wpp_ref, wpg_ref, bpg_ref, gple_ref,
                  gfin_ref, o_ref, yw0_ref, yw1_ref, sem, *, tm):
    j = pl.program_id(0)
    last = pl.num_programs(0) - 1
    n = TOP_K * tm
    yw = (yw0_ref, yw1_ref)

    def gather(dref, half, slot):
        return [_row_copy(yd_hbm, yw[slot], sem.at[slot], dref[0, r // tm, half * tm + r % tm], r)
                for r in range(n)]

    def tile(half, slot):
        rows = slice(half * tm, (half + 1) * tm)
        rt = route_ref[rows, :]
        y0 = _unpack_rows(yw[slot][0:tm, :])
        y1 = _unpack_rows(yw[slot][tm:n, :])
        x2 = x1_ref[rows, :] + rt[:, 2:3] * y0 + rt[:, 3:4] * y1
        h3 = _rms(x2, gple_ref[...]).astype(BF16)
        pp = _dot(p_ref[rows, :].astype(BF16), wpp_ref[...])
        x3 = x2 + pp * _sigmoid(_dot(h3, wpg_ref[...]) + bpg_ref[...])
        o_ref[rows, :] = _rms(x3, gfin_ref[...])

    @pl.when(j == 0)
    def _():
        _start_all(gather(dcur_ref, 0, 0))

    for c in gather(dcur_ref, 0, 0):
        c.wait()
    _start_all(gather(dcur_ref, 1, 1))
    tile(0, 0)
    for c in gather(dcur_ref, 1, 1):
        c.wait()
    _start_all(gather(dnxt_ref, 0, 0))
    tile(1, 1)

    @pl.when(j == last)
    def _():
        for c in gather(dnxt_ref, 0, 0):
            c.wait()


def _final(dest, x1, p2d, route, ydw, wpp, wpg, bpg, g_ple, g_final):
    t = x1.shape[0]
    ns, _, two_tm = dest.shape
    tm = two_tm // 2
    n = TOP_K * tm
    row = lambda w: pl.BlockSpec((2 * tm, w), lambda i: (i, 0))
    full = lambda a: pl.BlockSpec(a.shape, lambda i: (0,) * a.ndim)
    return pl.pallas_call(
        functools.partial(_final_kernel, tm=tm),
        grid=(ns,),
        in_specs=[pl.BlockSpec((1, 8, two_tm), lambda i: (i, 0, 0), memory_space=pltpu.SMEM),
                  pl.BlockSpec((1, 8, two_tm), lambda i: (jnp.minimum(i + 1, ns - 1), 0, 0),
                               memory_space=pltpu.SMEM),
                  row(D_MODEL), row(PLE_DIM), row(LANES), pl.BlockSpec(memory_space=pl.ANY),
                  full(wpp), full(wpg), full(bpg), full(g_ple), full(g_final)],
        out_specs=row(D_MODEL),
        out_shape=jax.ShapeDtypeStruct((t, D_MODEL), F32),
        scratch_shapes=[pltpu.VMEM((n, PACK_W), jnp.uint32), pltpu.VMEM((n, PACK_W), jnp.uint32),
                        pltpu.SemaphoreType.DMA((2,))],
        compiler_params=_cparams(("arbitrary",)),
        name="final",
    )(dest, dest, x1, p2d, route, ydw, wpp, wpg, bpg, g_ple, g_final)


def _block_plan(alloc, state, n_blk):
    new = alloc[:, 0, :N_EXPERTS].astype(jnp.int32).reshape(-1)
    base = alloc[:, 1, :N_EXPERTS].astype(jnp.int32).reshape(-1)
    expert = jnp.tile(jnp.arange(N_EXPERTS, dtype=jnp.int32), alloc.shape[0])
    fill = state[0, :N_EXPERTS].astype(jnp.int32)
    last_blk = state[1, :N_EXPERTS].astype(jnp.int32)
    taken = state[2, 0].astype(jnp.int32)
    b = jnp.arange(n_blk, dtype=jnp.int32)
    opened = (base[None, :] <= b[:, None]) & (b[:, None] < (base + new)[None, :])
    e_of = jnp.sum(jnp.where(opened, expert[None, :], 0), axis=1)
    onehot = e_of[:, None] == jnp.arange(N_EXPERTS, dtype=jnp.int32)[None, :]
    pick = lambda v: jnp.sum(jnp.where(onehot, v[None, :], 0), axis=1)
    rows = jnp.where(b < taken, jnp.where(b == pick(last_blk), pick(fill), MOE_BLK), 0)
    key = jnp.where(b < taken, e_of, N_EXPERTS) * n_blk + b
    pos = jnp.sum(key[None, :] < key[:, None], axis=1)
    at = pos[None, :] == b[:, None]
    order = lambda v: jnp.sum(jnp.where(at, v[None, :], 0), axis=1).astype(jnp.int32)
    return order(jnp.where(b < taken, e_of, N_EXPERTS - 1)), order(rows), order(b)


def _row(v):
    return v.reshape(1, -1).astype(F32)


def kernel(x, p, rel_bias, g_mix, w_in, w_gate, b_gate, ssm_a_re, ssm_a_im, ssm_log_dt, ssm_b_re, ssm_b_im,
           ssm_c_re, ssm_c_im, ssm_d, w_glu, b_glu, sinks, w_br_ssm, w_br_attn, w_out, g_ffn, w_router_group,
           b_router_group, w_router_expert, b_router_expert, w_e_gate, w_e_up, w_e_down, g_ple, w_ple_gate,
           b_ple_gate, w_ple_proj, g_final):
    bsz, seq, dm = x.shape
    assert g_mix.shape[0] == 1, "one layer followed by the final norm"
    i = 0
    t = bsz * seq
    x2d = x.reshape(t, dm)
    bias = _bias_table(rel_bias)
    u, q, kv = _proj(x2d, _row(g_mix[i]), w_in[i].astype(BF16), seq)
    ab_re, ab_im, bb_re, bb_im = _ssmprep(ssm_a_re[i], ssm_a_im[i], ssm_log_dt[i], ssm_b_re[i], ssm_b_im[i])
    ws_in, ws_out, ar, ai = _ssm_weights(ab_re, ab_im, bb_re, bb_im, ssm_c_re[i], ssm_c_im[i])
    y_ssm = _ssm(u.reshape(t, D_SSM), bsz, ws_in, ws_out, ar, ai, _row(ssm_d[i]),
                 w_glu[i].astype(BF16), _row(b_glu[i])).reshape(seq, bsz * D_SSM)
    y_att = _attn(q.reshape(bsz, seq, ATT_W), kv.reshape(bsz, seq, KV_W), bias, sinks[i].astype(F32))
    w_router = jnp.zeros((dm, LANES), F32)
    w_router = w_router.at[:, :N_GROUPS].set(w_router_group[i])
    w_router = w_router.at[:, N_GROUPS:N_GROUPS + N_EXPERTS].set(w_router_expert[i])
    b_router = jnp.zeros((1, LANES), F32)
    b_router = b_router.at[0, :N_GROUPS].set(b_router_group[i])
    b_router = b_router.at[0, N_GROUPS:N_GROUPS + N_EXPERTS].set(b_router_expert[i])
    n_blk = t * TOP_K // MOE_BLK + N_EXPERTS
    x1, route, dest, alloc, state, xdw = _merge(
        x2d, y_ssm, y_att.reshape(t, ATT_W), _row(g_mix[i]), w_gate[i].astype(BF16),
        _row(b_gate[i]), w_br_ssm[i].astype(BF16), w_br_attn[i].astype(BF16), w_out[i].astype(BF16),
        _row(g_ffn[i]), w_router.astype(BF16), b_router, n_blk)
    blk_e, blk_n, blk_i = _block_plan(alloc, state, n_blk)
    ydw = _moe(blk_e, blk_n, blk_i, xdw, w_e_gate[i], w_e_up[i], w_e_down[i])
    out = _final(dest, x1, p[i].reshape(t, PLE_DIM), route, ydw, w_ple_proj[i].astype(BF16),
                 w_ple_gate[i].astype(BF16), _row(b_ple_gate[i]), _row(g_ple[i]), _row(g_final))
    return out.reshape(bsz, seq, dm)
```

```python
import functools
import math

import numpy as np
import jax
import jax.numpy as jnp
from jax import lax
from jax.experimental import pallas as pl
from jax.experimental.pallas import tpu as pltpu

D_MODEL = 1024
D_SSM = 512
SSM_CH = 16
SSM_GROUPS = D_SSM // SSM_CH
SSM_STATE = 64
N_STATE = SSM_GROUPS * SSM_STATE
N_HEADS = 8
N_KV = 2
HEAD_DIM = 64
GROUP = N_HEADS // N_KV
ATT_W = N_HEADS * HEAD_DIM
KV_W = 2 * N_KV * HEAD_DIM
WINDOW = 128
BLOCK = 128
NUM_BUCKETS = 32
MAX_DIST = 128
D_IN = D_SSM + ATT_W + KV_W
N_GROUPS = 4
EXPERTS_PER_GROUP = 8
N_EXPERTS = N_GROUPS * EXPERTS_PER_GROUP
TOP_K = 2
D_FF_EXPERT = 512
MOE_BLK = 512
PLE_DIM = 256
EPS = 1e-6

LANES = 128
NEG = -1e30
PACK_W = D_MODEL // 2

BF16 = jnp.bfloat16
F32 = jnp.float32
VMEM_LIMIT = 56 * 1024 * 1024
NBUF = 3


def _cparams(sem):
    return pltpu.CompilerParams(dimension_semantics=sem, vmem_limit_bytes=VMEM_LIMIT)


def _rms(x, g):
    ms = jnp.mean(x * x, axis=-1, keepdims=True)
    return x * lax.rsqrt(ms + EPS) * g


def _sigmoid(x):
    return 1.0 / (1.0 + jnp.exp(-x))


def _dot(a, b):
    return jnp.dot(a, b, preferred_element_type=F32)


def _pack_rows(y):
    lo = pltpu.bitcast(y[:, :PACK_W].astype(BF16).astype(F32), jnp.uint32)
    hi = pltpu.bitcast(y[:, PACK_W:].astype(BF16).astype(F32), jnp.uint32)
    return hi | lax.shift_right_logical(lo, jnp.uint32(16))


def _unpack_rows(w):
    lo = pltpu.bitcast(lax.shift_left(w, jnp.uint32(16)), F32)
    hi = pltpu.bitcast(w & jnp.uint32(0xFFFF0000), F32)
    return jnp.concatenate([lo, hi], axis=1)


def _proj_kernel(x_ref, g_ref, w_ref, u_ref, q_ref, kv_ref):
    h = _rms(x_ref[...], g_ref[...]).astype(BF16)
    proj = _dot(h, w_ref[...])
    u_ref[...] = proj[:, :D_SSM].astype(BF16)
    q_ref[...] = (proj[:, D_SSM:D_SSM + ATT_W] * (HEAD_DIM ** -0.5)).astype(BF16)
    kv_ref[...] = proj[:, D_SSM + ATT_W:].astype(BF16)


def _proj(x2d, g_mix, w_in_b, tm=512):
    t = x2d.shape[0]
    row = lambda w: pl.BlockSpec((tm, w), lambda i: (i, 0))
    full = lambda a: pl.BlockSpec(a.shape, lambda i: (0,) * a.ndim)
    return pl.pallas_call(
        _proj_kernel,
        grid=(t // tm,),
        in_specs=[row(D_MODEL), full(g_mix), full(w_in_b)],
        out_specs=[row(D_SSM), row(ATT_W), row(KV_W)],
        out_shape=[jax.ShapeDtypeStruct((t, D_SSM), BF16),
                   jax.ShapeDtypeStruct((t, ATT_W), BF16),
                   jax.ShapeDtypeStruct((t, KV_W), BF16)],
        compiler_params=_cparams(("parallel",)),
        name="proj",
    )(x2d, g_mix, w_in_b)


def _ssmprep_kernel(are_ref, aim_ref, ldt_ref, bre_ref, bim_ref, abr_ref, abi_ref, bbr_ref, bbi_ref):
    a_re, a_im = are_ref[...], aim_ref[...]
    dt = jnp.exp(ldt_ref[...])
    mag = jnp.exp(a_re * dt)
    ab_re = mag * jnp.cos(a_im * dt)
    ab_im = mag * jnp.sin(a_im * dt)
    abr_ref[...] = ab_re
    abi_ref[...] = ab_im
    den = a_re * a_re + a_im * a_im
    c_re = ((ab_re - 1.0) * a_re + ab_im * a_im) / den
    c_im = (ab_im * a_re - (ab_re - 1.0) * a_im) / den
    for c in range(SSM_CH):
        b_re, b_im = bre_ref[c], bim_ref[c]
        bbr_ref[c] = c_re * b_re - c_im * b_im
        bbi_ref[c] = c_re * b_im + c_im * b_re


def _ssmprep(a_re, a_im, log_dt, b_re, b_im):
    g, p = a_re.shape
    bt_re = jnp.transpose(b_re, (2, 0, 1))
    bt_im = jnp.transpose(b_im, (2, 0, 1))
    gp = jax.ShapeDtypeStruct((g, p), F32)
    cgp = jax.ShapeDtypeStruct((SSM_CH, g, p), F32)
    return pl.pallas_call(
        _ssmprep_kernel, out_shape=[gp, gp, cgp, cgp], name="ssmprep",
    )(a_re, a_im, log_dt.reshape(g, 1), bt_re, bt_im)


def _ssm_weights(ab_re, ab_im, bb_re, bb_im, c_re, c_im):
    g, p = ab_re.shape
    kb = D_SSM // 256
    gk = g // kb
    eye = jnp.eye(gk, dtype=F32)

    def in_block(bb):
        b4 = jnp.transpose(bb, (1, 0, 2)).reshape(kb, gk, SSM_CH, p)
        return jnp.einsum("kgcp,gh->kgchp", b4, eye).reshape(kb, gk * SSM_CH, gk * p)

    w_in = jnp.concatenate([in_block(bb_re), in_block(bb_im)], axis=2).astype(BF16)
    nj = D_SSM // LANES
    gj = g // nj
    eyej = jnp.eye(gj, dtype=F32)

    def out_block(c):
        c4 = jnp.transpose(c, (0, 2, 1)).reshape(nj, gj, p, SSM_CH)
        return jnp.einsum("jgpc,gh->jgphc", c4, eyej).reshape(nj, gj * p, gj * SSM_CH)

    w_out = jnp.stack([out_block(c_re), out_block(-c_im)], axis=1).astype(BF16)
    return w_in, w_out, ab_re.reshape(1, g * p), ab_im.reshape(1, g * p)


def _gelu_tanh(x):
    return 0.5 * x * (1.0 + jnp.tanh(math.sqrt(2.0 / math.pi) * (x + 0.044715 * (x * x * x))))


def _ssm_kernel(u_ref, ul_ref, win_ref, wout_ref, ar_ref, ai_ref, d_ref, wglu_ref, bglu_ref, y_ref,
                perm_ref, permt_ref, bu0_ref, bu1_ref, xb0_ref, xb1_ref, st_ref, *, nb, lc):
    rows = nb * lc
    ns = N_STATE
    k = pl.program_id(0)
    bu = (bu0_ref, bu1_ref)
    xb = (xb0_ref, xb1_ref)

    @pl.when(k == 0)
    def _():
        r = lax.broadcasted_iota(jnp.int32, (rows, rows), 0)
        c = lax.broadcasted_iota(jnp.int32, (rows, rows), 1)
        sh = nb.bit_length() - 1
        perm_ref[...] = jnp.where(c == (r & (nb - 1)) * lc + (r >> sh), 1.0, 0.0).astype(BF16)
        permt_ref[...] = jnp.where(r == (c & (nb - 1)) * lc + (c >> sh), 1.0, 0.0).astype(BF16)
        bu1_ref[...] = jnp.zeros_like(bu1_ref)
        xb0_ref[...] = jnp.zeros_like(xb0_ref)
        xb1_ref[...] = jnp.zeros_like(xb1_ref)
        st_ref[...] = jnp.zeros_like(st_ref)

    kb = win_ref.shape[0]
    half = ns // kb
    nj = wout_ref.shape[0]
    kw = ns // nj
    ar, ai = ar_ref[...], ai_ref[...]
    xr, xi = st_ref[:, 0:ns], st_ref[:, ns:2 * ns]

    for h in range(2):
        part = slice(h * lc, (h + 1) * lc)
        u_new = _dot(perm_ref[...], u_ref[:, part, :].reshape(rows, D_SSM)).astype(BF16)
        for j in range(kb):
            res = _dot(u_new[:, 256 * j:256 * (j + 1)], win_ref[j])
            bu[h][:, half * j:half * (j + 1)] = res[:, :half]
            bu[h][:, ns + half * j:ns + half * (j + 1)] = res[:, half:]
        src, dst = bu[1 - h], xb[1 - h]
        for t in range(0, lc, 2):
            keep_r, keep_i = [], []
            for s in range(2):
                row = (t + s) * nb
                br = src[row:row + nb, 0:ns]
                bi = src[row:row + nb, ns:2 * ns]
                xr, xi = ar * xr - ai * xi + br, ar * xi + ai * xr + bi
                keep_r.append(xr)
                keep_i.append(xi)
            dst[t * nb:(t + 2) * nb, 0:ns] = jnp.concatenate(keep_r, axis=0).astype(BF16)
            dst[t * nb:(t + 2) * nb, ns:2 * ns] = jnp.concatenate(keep_i, axis=0).astype(BF16)
        ys = []
        for j in range(nj):
            ys.append(_dot(xb[h][:, kw * j:kw * (j + 1)], wout_ref[j, 0])
                      + _dot(xb[h][:, ns + kw * j:ns + kw * (j + 1)], wout_ref[j, 1]))
        y_tb = jnp.concatenate(ys, axis=1).astype(BF16)
        y = _dot(permt_ref[...], y_tb) + d_ref[...] * ul_ref[:, part, :].reshape(rows, D_SSM).astype(F32)
        y = _gelu_tanh(y)
        z = _dot(y.astype(BF16), wglu_ref[...]) + bglu_ref[...]
        y_ref[:, part, :] = (y * _sigmoid(z)).astype(BF16).reshape(nb, lc, D_SSM)

    st_ref[:, 0:ns] = xr
    st_ref[:, ns:2 * ns] = xi


def _ssm(u3, w_in, w_out, ar, ai, d_skip, w_glu_b, b_glu, lc=64):
    nb, s, _ = u3.shape
    rows = nb * lc
    nk = s // (2 * lc)
    assert nb & (nb - 1) == 0 and nb % 8 == 0 and s % (2 * lc) == 0 and lc % 16 == 0
    full = lambda a: pl.BlockSpec(a.shape, lambda i: (0,) * a.ndim)
    lead = pl.BlockSpec((nb, 2 * lc, D_SSM), lambda i: (0, jnp.minimum(i, nk - 1), 0))
    lag = pl.BlockSpec((nb, 2 * lc, D_SSM), lambda i: (0, jnp.maximum(i - 1, 0), 0))
    return pl.pallas_call(
        functools.partial(_ssm_kernel, nb=nb, lc=lc),
        grid=(nk + 1,),
        in_specs=[lead, lag, full(w_in), full(w_out), full(ar), full(ai), full(d_skip), full(w_glu_b), full(b_glu)],
        out_specs=lag,
        out_shape=jax.ShapeDtypeStruct((nb, s, D_SSM), BF16),
        scratch_shapes=[pltpu.VMEM((rows, rows), BF16), pltpu.VMEM((rows, rows), BF16),
                        pltpu.VMEM((rows, 2 * N_STATE), F32), pltpu.VMEM((rows, 2 * N_STATE), F32),
                        pltpu.VMEM((rows, 2 * N_STATE), BF16), pltpu.VMEM((rows, 2 * N_STATE), BF16),
                        pltpu.VMEM((nb, 2 * N_STATE), F32)],
        compiler_params=_cparams(("arbitrary",)),
        name="ssm",
    )(u3, u3, w_in, w_out, ar, ai, d_skip, w_glu_b, b_glu)


def _t5_bucket_np(rel):
    max_exact = NUM_BUCKETS // 2
    relf = np.maximum(rel, 1).astype(np.float32)
    large = max_exact + (np.log(relf / np.float32(max_exact)) / np.float32(math.log(MAX_DIST / max_exact))
                         * np.float32(NUM_BUCKETS - max_exact)).astype(np.int32)
    large = np.minimum(large, NUM_BUCKETS - 1)
    return np.where(rel < max_exact, rel, large)


def _bias_table(rel_bias):
    q_loc = np.arange(BLOCK)[:, None]
    c_loc = np.arange(2 * BLOCK)[None, :]
    rel = q_loc + BLOCK - c_loc
    valid = (rel >= 0) & (rel < WINDOW)
    bucket = _t5_bucket_np(np.maximum(rel, 0)).reshape(-1, 1)
    onehot = (jnp.asarray(bucket) == jnp.arange(NUM_BUCKETS)[None, :]).astype(F32)
    bias = jnp.dot(onehot, rel_bias.astype(F32), precision=lax.Precision.HIGHEST)
    bias = jnp.where(valid.reshape(-1, 1), bias, NEG)
    return jnp.transpose(bias, (1, 0)).reshape(N_HEADS, BLOCK, 2 * BLOCK)


def _attn_kernel(sink_ref, q_ref, kvp_ref, kvc_ref, bias_ref, o_ref, *, nq):
    n = pl.program_id(1)
    kv_all = jnp.concatenate([kvp_ref[0], kvc_ref[0]], axis=0)
    col = lax.broadcasted_iota(jnp.int32, (BLOCK, 2 * BLOCK), 1)
    no_prev = jnp.where(col < BLOCK, jnp.where(n == 0, NEG, 0.0), 0.0)
    for j in range(nq):
        q = q_ref[0, BLOCK * j:BLOCK * (j + 1), :]
        kv = kv_all[BLOCK * j:BLOCK * (j + 2), :]
        outs = []
        for h in range(N_HEADS):
            g = h // GROUP
            qh = q[:, HEAD_DIM * h:HEAD_DIM * (h + 1)]
            kh = kv[:, HEAD_DIM * g:HEAD_DIM * (g + 1)]
            vh = kv[:, N_KV * HEAD_DIM + HEAD_DIM * g:N_KV * HEAD_DIM + HEAD_DIM * (g + 1)]
            s = lax.dot_general(qh, kh, (((1,), (1,)), ((), ())), preferred_element_type=F32)
            s = s + bias_ref[h]
            if j == 0:
                s = s + no_prev
            sink = sink_ref[h]
            m = jnp.maximum(jnp.max(s, axis=-1, keepdims=True), sink)
            e = jnp.exp(s - m)
            den = jnp.sum(e, axis=-1, keepdims=True) + jnp.exp(sink - m)
            outs.append(_dot(e.astype(BF16), vh) / den)
        o_ref[0, BLOCK * j:BLOCK * (j + 1), :] = jnp.concatenate(outs, axis=1).astype(BF16)


def _attn(q3, kv3, bias, sinks, nq=4):
    b, s, _ = q3.shape
    qb = nq * BLOCK
    grid_spec = pltpu.PrefetchScalarGridSpec(
        num_scalar_prefetch=0,
        grid=(b, s // qb),
        in_specs=[pl.BlockSpec(memory_space=pltpu.SMEM),
                  pl.BlockSpec((1, qb, ATT_W), lambda i, n: (i, n, 0)),
                  pl.BlockSpec((1, BLOCK, KV_W), lambda i, n: (i, jnp.maximum(n * nq - 1, 0), 0)),
                  pl.BlockSpec((1, qb, KV_W), lambda i, n: (i, n, 0)),
                  pl.BlockSpec(bias.shape, lambda i, n: (0, 0, 0))],
        out_specs=pl.BlockSpec((1, qb, ATT_W), lambda i, n: (i, n, 0)),
    )
    return pl.pallas_call(
        functools.partial(_attn_kernel, nq=nq),
        grid_spec=grid_spec,
        out_shape=jax.ShapeDtypeStruct((b, s, ATT_W), BF16),
        compiler_params=_cparams(("parallel", "parallel")),
        name="attn",
    )(sinks, q3, kv3, kv3, bias)


def _route(logits):
    lane = lax.broadcasted_iota(jnp.int32, logits.shape, 1).astype(F32)
    big = float(LANES)
    gl = jnp.where(lane < N_GROUPS, logits, NEG)
    mg = jnp.max(gl, axis=-1, keepdims=True)
    gidx = jnp.min(jnp.where(gl == mg, lane, big), axis=-1, keepdims=True)
    gsum = jnp.sum(jnp.where(lane < N_GROUPS, jnp.exp(gl - mg), 0.0), axis=-1, keepdims=True)
    gp = 1.0 / gsum
    lo = N_GROUPS + EXPERTS_PER_GROUP * gidx
    el = jnp.where(lane >= lo, jnp.where(lane < lo + EXPERTS_PER_GROUP, logits, NEG), NEG)
    m1 = jnp.max(el, axis=-1, keepdims=True)
    i1 = jnp.min(jnp.where(el == m1, lane, big), axis=-1, keepdims=True)
    el2 = jnp.where(lane == i1, NEG, el)
    m2 = jnp.max(el2, axis=-1, keepdims=True)
    i2 = jnp.min(jnp.where(el2 == m2, lane, big), axis=-1, keepdims=True)
    t = jnp.exp(m2 - m1)
    w1 = gp / (1.0 + t)
    w2 = gp * t / (1.0 + t)
    return jnp.where(lane == 0, i1 - N_GROUPS,
                     jnp.where(lane == 1, i2 - N_GROUPS,
                               jnp.where(lane == 2, w1, jnp.where(lane == 3, w2, 0.0))))


def _row_copy(src_ref, dst_ref, sem, src_row, dst_row):
    return pltpu.make_async_copy(src_ref.at[pl.ds(src_row, 1)], dst_ref.at[pl.ds(dst_row, 1)], sem)


def _start_all(copies):
    for k, c in enumerate(copies):
        c.start(priority=k % 2)


def _merge_kernel(x_ref, ya_ref, yb_ref, gmix_ref, wg_ref, bg_ref, wa_ref, wb_ref, wo_ref, gffn_ref, wr_ref, br_ref,
                  x1_ref, route_ref, dest_ref, alloc_ref, state_ref, xd_hbm,
                  tri_ref, pre_ref, st_ref, hbuf_ref, destv_ref, dests_ref, stv_ref, sts_ref, zblk_ref,
                  hs_ref, gate_ref, sem_rows, sem_idx, sem_pad, *, tm, n_blk):
    i = pl.program_id(0)
    last = pl.num_programs(0) - 1
    slot = lax.rem(i, NBUF)
    prev = lax.rem(i + NBUF - 1, NBUF)
    n = TOP_K * tm
    dump0 = n_blk * MOE_BLK

    def row_copies(s, lo=0, hi=n):
        return [_row_copy(hbuf_ref.at[s], xd_hbm, sem_rows.at[s], r % tm, dests_ref[s, r // tm, r % tm])
                for r in range(lo, hi)]

    def wait_rows(s):
        for _ in range(n):
            _row_copy(hbuf_ref.at[s], xd_hbm, sem_rows.at[s], 0, dump0).wait()

    def index_copy(s):
        return pltpu.make_async_copy(destv_ref.at[s], dests_ref.at[s], sem_idx.at[s])

    @pl.when(i == 0)
    def _():
        r = lax.broadcasted_iota(jnp.int32, (tm, tm), 0)
        c = lax.broadcasted_iota(jnp.int32, (tm, tm), 1)
        tri_ref[...] = jnp.where(r > c, 1.0, 0.0).astype(BF16)
        r = lax.broadcasted_iota(jnp.int32, (LANES, LANES), 0)
        c = lax.broadcasted_iota(jnp.int32, (LANES, LANES), 1)
        pre_ref[...] = jnp.where(r < c, 1.0, 0.0).astype(BF16)
        r = lax.broadcasted_iota(jnp.int32, (8, LANES), 0)
        st_ref[...] = jnp.where(r == 0, float(MOE_BLK), 0.0)
        hbuf_ref[NBUF - 1] = jnp.zeros((tm, PACK_W), jnp.uint32)

        def spare(t, carry):
            dests_ref[NBUF - 1, 0, t] = dump0 + t
            dests_ref[NBUF - 1, 1, t] = dump0 + tm + t
            return carry

        lax.fori_loop(0, tm, spare, 0)

    @pl.when(i >= NBUF - 1)
    def _():
        wait_rows(slot)

    @pl.when(i >= 1)
    def _():
        index_copy(prev).wait()

    group = n // 4
    once = jnp.minimum(dests_ref[prev, 0, 0], 0) + 1

    def region(body):
        lax.fori_loop(0, once, lambda _, c: (body(), c)[1], 0)

    def gates_lo():
        _start_all(row_copies(prev, 0, group))
        h = _rms(x_ref[...], gmix_ref[...]).astype(BF16)
        hs_ref[...] = h
        gate_ref[0] = _sigmoid(_dot(h, wg_ref[:, 0:D_MODEL]) + bg_ref[:, 0:D_MODEL])

    def gates_hi():
        _start_all(row_copies(prev, group, 2 * group))
        gate_ref[1] = _sigmoid(_dot(hs_ref[...], wg_ref[:, D_MODEL:2 * D_MODEL]) + bg_ref[:, D_MODEL:2 * D_MODEL])

    def branches():
        _start_all(row_copies(prev, 2 * group, 3 * group))
        merged = gate_ref[0] * _dot(ya_ref[...], wa_ref[...]) + gate_ref[1] * _dot(yb_ref[...], wb_ref[...])
        hs_ref[...] = merged.astype(BF16)

    region(gates_lo)
    region(gates_hi)
    region(branches)
    _start_all(row_copies(prev, 3 * group, n))
    x1 = x_ref[...] + _dot(hs_ref[...], wo_ref[...])
    x1_ref[...] = x1
    h2 = _rms(x1, gffn_ref[...])
    hbuf_ref[slot] = _pack_rows(h2)
    route = _route(_dot(h2.astype(BF16), wr_ref[...]) + br_ref[...])
    route_ref[...] = route

    lane = lax.broadcasted_iota(jnp.int32, route.shape, 1).astype(F32)
    e1, e2 = route[:, 0:1], route[:, 1:2]
    is1, is2 = lane == e1, lane == e2
    member = jnp.where(is1, 1.0, jnp.where(is2, 1.0, 0.0))
    before = _dot(tri_ref[...], member.astype(BF16))
    cnt = jnp.sum(member, axis=0, keepdims=True)
    fill, blk, free = st_ref[0:1, :], st_ref[1:2, :], st_ref[2:3, :]
    need = fill + cnt
    new = jnp.floor((need + float(MOE_BLK - 1)) * (1.0 / MOE_BLK)) - 1.0
    base = free + _dot(jnp.broadcast_to(new, (8, LANES)).astype(BF16), pre_ref[...])[0:1, :]
    q = fill + before
    jb = jnp.floor(q * (1.0 / MOE_BLK))
    rowid = jnp.where(jb == 0.0, blk, base + jb - 1.0) * float(MOE_BLK) + (q - jb * float(MOE_BLK))
    d1 = jnp.sum(jnp.where(is1, rowid, 0.0), axis=-1, keepdims=True)
    d2 = jnp.sum(jnp.where(is2, rowid, 0.0), axis=-1, keepdims=True)
    slab = jnp.where(lane == 0, d1, jnp.where(lane == 1, d2, 0.0))
    d8 = jnp.transpose(slab, (1, 0))[0:8, :].astype(jnp.int32)
    destv_ref[slot] = d8
    dest_ref[0] = d8
    index_copy(slot).start()

    r8 = lax.broadcasted_iota(jnp.int32, (8, LANES), 0)
    alloc_ref[0] = jnp.where(r8 == 0, new, jnp.where(r8 == 1, base, 0.0))
    state = jnp.where(r8 == 0, need - new * float(MOE_BLK),
                      jnp.where(r8 == 1, jnp.where(new > 0.0, base + new - 1.0, blk),
                                jnp.where(r8 == 2, free + jnp.sum(new, axis=-1, keepdims=True), 0.0)))
    st_ref[...] = state
    state_ref[...] = state

    @pl.when(i == last)
    def _():
        index_copy(slot).wait()
        _start_all(row_copies(slot))
        for s in range(NBUF):
            wait_rows(s)
        stv_ref[...] = state.astype(jnp.int32)
        cp = pltpu.make_async_copy(stv_ref, sts_ref, sem_pad)
        cp.start()
        cp.wait()
        zblk_ref[...] = jnp.zeros_like(zblk_ref)

        def zero_rows(e, carry):
            first = sts_ref[1, e] * MOE_BLK

            def start(r, c):
                _row_copy(zblk_ref, xd_hbm, sem_pad, 0, first + r).start()
                return c

            def wait(r, c):
                _row_copy(zblk_ref, xd_hbm, sem_pad, 0, first + r).wait()
                return c

            lax.fori_loop(sts_ref[0, e], MOE_BLK, start, 0)
            lax.fori_loop(sts_ref[0, e], MOE_BLK, wait, 0)
            return carry

        lax.fori_loop(0, N_EXPERTS, zero_rows, 0)

        def block_copy(b):
            return pltpu.make_async_copy(zblk_ref, xd_hbm.at[pl.ds(pl.multiple_of(b * MOE_BLK, MOE_BLK), MOE_BLK)],
                                         sem_pad)

        def zero_block(b, c):
            block_copy(b).start()
            block_copy(b).wait()
            return c

        lax.fori_loop(sts_ref[2, 0], n_blk, zero_block, 0)


def _merge(x2d, ya, yb, g_mix, wg, bg, wa, wb, wo, g_ffn, wr, br, n_blk, tm=512):
    t = x2d.shape[0]
    nt = t // tm
    n_rows = n_blk * MOE_BLK + TOP_K * tm
    row = lambda w: pl.BlockSpec((tm, w), lambda i: (i, 0))
    full = lambda a: pl.BlockSpec(a.shape, lambda i: (0,) * a.ndim)
    once = lambda a: pl.BlockSpec(a.shape, lambda i: (0,) * a.ndim, pipeline_mode=pl.Buffered(1))
    return pl.pallas_call(
        functools.partial(_merge_kernel, tm=tm, n_blk=n_blk),
        grid=(nt,),
        in_specs=[row(D_MODEL), row(D_SSM), row(ATT_W), full(g_mix), once(wg), full(bg),
                  once(wa), once(wb), once(wo), full(g_ffn), full(wr), full(br)],
        out_specs=[row(D_MODEL), row(LANES), pl.BlockSpec((1, 8, tm), lambda i: (i, 0, 0)),
                   pl.BlockSpec((1, 8, LANES), lambda i: (i, 0, 0)), pl.BlockSpec((8, LANES), lambda i: (0, 0)),
                   pl.BlockSpec(memory_space=pl.ANY)],
        out_shape=[jax.ShapeDtypeStruct((t, D_MODEL), F32),
                   jax.ShapeDtypeStruct((t, LANES), F32),
                   jax.ShapeDtypeStruct((nt, 8, tm), jnp.int32),
                   jax.ShapeDtypeStruct((nt, 8, LANES), F32),
                   jax.ShapeDtypeStruct((8, LANES), F32),
                   jax.ShapeDtypeStruct((n_rows, PACK_W), jnp.uint32)],
        scratch_shapes=[pltpu.VMEM((tm, tm), BF16), pltpu.VMEM((LANES, LANES), BF16), pltpu.VMEM((8, LANES), F32),
                        pltpu.VMEM((NBUF, tm, PACK_W), jnp.uint32), pltpu.VMEM((NBUF, 8, tm), jnp.int32),
                        pltpu.SMEM((NBUF, 8, tm), jnp.int32), pltpu.VMEM((8, LANES), jnp.int32),
                        pltpu.SMEM((8, LANES), jnp.int32), pltpu.VMEM((MOE_BLK, PACK_W), jnp.uint32),
                        pltpu.VMEM((tm, D_MODEL), BF16), pltpu.VMEM((2, tm, D_MODEL), F32),
                        pltpu.SemaphoreType.DMA((NBUF,)), pltpu.SemaphoreType.DMA((NBUF,)),
                        pltpu.SemaphoreType.DMA],
        compiler_params=_cparams(("arbitrary",)),
        name="merge",
    )(x2d, ya, yb, g_mix, wg, bg, wa, wb, wo, g_ffn, wr, br)


def _moe_kernel(be_ref, bn_ref, bi_ref, xd_ref, wg_ref, wu_ref, wd_ref, yd_ref, wgb_ref, wub_ref, wdb_ref):
    del bi_ref
    i = pl.program_id(0)

    @pl.when(bn_ref[i] == 0)
    def _():
        yd_ref[...] = jnp.zeros_like(yd_ref)

    @pl.when(bn_ref[i] > 0)
    def _():
        prev = be_ref[jnp.maximum(i - 1, 0)]

        @pl.when(jnp.logical_or(i == 0, be_ref[i] != prev))
        def _():
            wgb_ref[...] = wg_ref[0].astype(BF16)
            wub_ref[...] = wu_ref[0].astype(BF16)
            wdb_ref[...] = wd_ref[0].astype(BF16)

        x = _unpack_rows(xd_ref[...]).astype(BF16)
        g = _dot(x, wgb_ref[...])
        u = _dot(x, wub_ref[...])
        a = (g * _sigmoid(g) * u).astype(BF16)
        yd_ref[...] = _pack_rows(_dot(a, wdb_ref[...]))


def _moe(blk_e, blk_n, blk_i, xdw, w_eg, w_eu, w_ed):
    n_blk = blk_e.shape[0]
    grid_spec = pltpu.PrefetchScalarGridSpec(
        num_scalar_prefetch=3,
        grid=(n_blk,),
        in_specs=[pl.BlockSpec((MOE_BLK, PACK_W), lambda i, be, bn, bi: (bi[i], 0)),
                  pl.BlockSpec((1, D_MODEL, D_FF_EXPERT), lambda i, be, bn, bi: (be[i], 0, 0)),
                  pl.BlockSpec((1, D_MODEL, D_FF_EXPERT), lambda i, be, bn, bi: (be[i], 0, 0)),
                  pl.BlockSpec((1, D_FF_EXPERT, D_MODEL), lambda i, be, bn, bi: (be[i], 0, 0))],
        out_specs=pl.BlockSpec((MOE_BLK, PACK_W), lambda i, be, bn, bi: (bi[i], 0)),
        scratch_shapes=[pltpu.VMEM((D_MODEL, D_FF_EXPERT), BF16), pltpu.VMEM((D_MODEL, D_FF_EXPERT), BF16),
                        pltpu.VMEM((D_FF_EXPERT, D_MODEL), BF16)],
    )
    return pl.pallas_call(
        _moe_kernel,
        grid_spec=grid_spec,
        out_shape=jax.ShapeDtypeStruct((n_blk * MOE_BLK, PACK_W), jnp.uint32),
        compiler_params=_cparams(("arbitrary",)),
        name="moe",
    )(blk_e, blk_n, blk_i, xdw, w_eg, w_eu, w_ed)


def _final_kernel(dcur_ref, dnxt_ref, x1_ref, p_ref, route_ref, yd_hbm, wpp_ref, wpg_ref, bpg_ref, gple_ref,
                  gfin_ref, o_ref, yw0_ref, yw1_ref, sem, *, tm):
    j = pl.program_id(0)
    last = pl.num_programs(0) - 1
    n = TOP_K * tm
    yw = (yw0_ref, yw1_ref)

    def gather(dref, half, slot):
        return [_row_copy(yd_hbm, yw[slot], sem.at[slot], dref[0, r // tm, half * tm + r % tm], r)
                for r in range(n)]

    def tile(half, slot):
        rows = slice(half * tm, (half + 1) * tm)
        rt = route_ref[rows, :]
        y0 = _unpack_rows(yw[slot][0:tm, :])
        y1 = _unpack_rows(yw[slot][tm:n, :])
        x2 = x1_ref[rows, :] + rt[:, 2:3] * y0 + rt[:, 3:4] * y1
        h3 = _rms(x2, gple_ref[...]).astype(BF16)
        pp = _dot(p_ref[rows, :].astype(BF16), wpp_ref[...])
        x3 = x2 + pp * _sigmoid(_dot(h3, wpg_ref[...]) + bpg_ref[...])
        o_ref[rows, :] = _rms(x3, gfin_ref[...])

    @pl.when(j == 0)
    def _():
        _start_all(gather(dcur_ref, 0, 0))

    for c in gather(dcur_ref, 0, 0):
        c.wait()
    _start_all(gather(dcur_ref, 1, 1))
    tile(0, 0)
    for c in gather(dcur_ref, 1, 1):
        c.wait()
    _start_all(gather(dnxt_ref, 0, 0))
    tile(1, 1)

    @pl.when(j == last)
    def _():
        for c in gather(dnxt_ref, 0, 0):
            c.wait()


def _final(dest, x1, p2d, route, ydw, wpp, wpg, bpg, g_ple, g_final):
    t = x1.shape[0]
    ns, _, two_tm = dest.shape
    tm = two_tm // 2
    n = TOP_K * tm
    row = lambda w: pl.BlockSpec((2 * tm, w), lambda i: (i, 0))
    full = lambda a: pl.BlockSpec(a.shape, lambda i: (0,) * a.ndim)
    return pl.pallas_call(
        functools.partial(_final_kernel, tm=tm),
        grid=(ns,),
        in_specs=[pl.BlockSpec((1, 8, two_tm), lambda i: (i, 0, 0), memory_space=pltpu.SMEM),
                  pl.BlockSpec((1, 8, two_tm), lambda i: (jnp.minimum(i + 1, ns - 1), 0, 0),
                               memory_space=pltpu.SMEM),
                  row(D_MODEL), row(PLE_DIM), row(LANES), pl.BlockSpec(memory_space=pl.ANY),
                  full(wpp), full(wpg), full(bpg), full(g_ple), full(g_final)],
        out_specs=row(D_MODEL),
        out_shape=jax.ShapeDtypeStruct((t, D_MODEL), F32),
        scratch_shapes=[pltpu.VMEM((n, PACK_W), jnp.uint32), pltpu.VMEM((n, PACK_W), jnp.uint32),
                        pltpu.SemaphoreType.DMA((2,))],
        compiler_params=_cparams(("arbitrary",)),
        name="final",
    )(dest, dest, x1, p2d, route, ydw, wpp, wpg, bpg, g_ple, g_final)


def _block_plan(alloc, state, n_blk):
    new = alloc[:, 0, :N_EXPERTS].astype(jnp.int32).reshape(-1)
    base = alloc[:, 1, :N_EXPERTS].astype(jnp.int32).reshape(-1)
    expert = jnp.tile(jnp.arange(N_EXPERTS, dtype=jnp.int32), alloc.shape[0])
    fill = state[0, :N_EXPERTS].astype(jnp.int32)
    last_blk = state[1, :N_EXPERTS].astype(jnp.int32)
    taken = state[2, 0].astype(jnp.int32)
    b = jnp.arange(n_blk, dtype=jnp.int32)
    opened = (base[None, :] <= b[:, None]) & (b[:, None] < (base + new)[None, :])
    e_of = jnp.sum(jnp.where(opened, expert[None, :], 0), axis=1)
    onehot = e_of[:, None] == jnp.arange(N_EXPERTS, dtype=jnp.int32)[None, :]
    pick = lambda v: jnp.sum(jnp.where(onehot, v[None, :], 0), axis=1)
    rows = jnp.where(b < taken, jnp.where(b == pick(last_blk), pick(fill), MOE_BLK), 0)
    key = jnp.where(b < taken, e_of, N_EXPERTS) * n_blk + b
    pos = jnp.sum(key[None, :] < key[:, None], axis=1)
    at = pos[None, :] == b[:, None]
    order = lambda v: jnp.sum(jnp.where(at, v[None, :], 0), axis=1).astype(jnp.int32)
    return order(jnp.where(b < taken, e_of, N_EXPERTS - 1)), order(rows), order(b)


def _row(v):
    return v.reshape(1, -1).astype(F32)


def kernel(x, p, rel_bias, g_mix, w_in, w_gate, b_gate, ssm_a_re, ssm_a_im, ssm_log_dt, ssm_b_re, ssm_b_im,
           ssm_c_re, ssm_c_im, ssm_d, w_glu, b_glu, sinks, w_br_ssm, w_br_attn, w_out, g_ffn, w_router_group,
           b_router_group, w_router_expert, b_router_expert, w_e_gate, w_e_up, w_e_down, g_ple, w_ple_gate,
           b_ple_gate, w_ple_proj, g_final):
    bsz, seq, dm = x.shape
    assert g_mix.shape[0] == 1, "one layer followed by the final norm"
    i = 0
    t = bsz * seq
    x2d = x.reshape(t, dm)
    bias = _bias_table(rel_bias)
    u, q, kv = _proj(x2d, _row(g_mix[i]), w_in[i].astype(BF16))
    ab_re, ab_im, bb_re, bb_im = _ssmprep(ssm_a_re[i], ssm_a_im[i], ssm_log_dt[i], ssm_b_re[i], ssm_b_im[i])
    ws_in, ws_out, ar, ai = _ssm_weights(ab_re, ab_im, bb_re, bb_im, ssm_c_re[i], ssm_c_im[i])
    y_ssm = _ssm(u.reshape(bsz, seq, D_SSM), ws_in, ws_out, ar, ai, _row(ssm_d[i]),
                 w_glu[i].astype(BF16), _row(b_glu[i]))
    y_att = _attn(q.reshape(bsz, seq, ATT_W), kv.reshape(bsz, seq, KV_W), bias, sinks[i].astype(F32))
    w_router = jnp.zeros((dm, LANES), F32)
    w_router = w_router.at[:, :N_GROUPS].set(w_router_group[i])
    w_router = w_router.at[:, N_GROUPS:N_GROUPS + N_EXPERTS].set(w_router_expert[i])
    b_router = jnp.zeros((1, LANES), F32)
    b_router = b_router.at[0, :N_GROUPS].set(b_router_group[i])
    b_router = b_router.at[0, N_GROUPS:N_GROUPS + N_EXPERTS].set(b_router_expert[i])
    n_blk = t * TOP_K // MOE_BLK + N_EXPERTS
    x1, route, dest, alloc, state, xdw = _merge(
        x2d, y_ssm.reshape(t, D_SSM), y_att.reshape(t, ATT_W), _row(g_mix[i]), w_gate[i].astype(BF16),
        _row(b_gate[i]), w_br_ssm[i].astype(BF16), w_br_attn[i].astype(BF16), w_out[i].astype(BF16),
        _row(g_ffn[i]), w_router.astype(BF16), b_router, n_blk)
    blk_e, blk_n, blk_i = _block_plan(alloc, state, n_blk)
    ydw = _moe(blk_e, blk_n, blk_i, xdw, w_e_gate[i], w_e_up[i], w_e_down[i])
    out = _final(dest, x1, p[i].reshape(t, PLE_DIM), route, ydw, w_ple_proj[i].astype(BF16),
                 w_ple_gate[i].astype(BF16), _row(b_ple_gate[i]), _row(g_ple[i]), _row(g_final))
    return out.reshape(bsz, seq, dm)
```

```python
import functools
import math

import numpy as np
import jax
import jax.numpy as jnp
from jax import lax
from jax.experimental import pallas as pl
from jax.experimental.pallas import tpu as pltpu

D_MODEL = 1024
D_SSM = 512
SSM_CH = 16
SSM_GROUPS = D_SSM // SSM_CH
SSM_STATE = 64
N_STATE = SSM_GROUPS * SSM_STATE
N_HEADS = 8
N_KV = 2
HEAD_DIM = 64
GROUP = N_HEADS // N_KV
ATT_W = N_HEADS * HEAD_DIM
KV_W = 2 * N_KV * HEAD_DIM
WINDOW = 128
BLOCK = 128
NUM_BUCKETS = 32
MAX_DIST = 128
D_IN = D_SSM + ATT_W + KV_W
N_GROUPS = 4
EXPERTS_PER_GROUP = 8
N_EXPERTS = N_GROUPS * EXPERTS_PER_GROUP
TOP_K = 2
D_FF_EXPERT = 512
MOE_BLK = 512
PLE_DIM = 256
EPS = 1e-6

LANES = 128
NEG = -1e30
PACK_W = D_MODEL // 2

BF16 = jnp.bfloat16
F32 = jnp.float32
VMEM_LIMIT = 56 * 1024 * 1024
NBUF = 3


def _cparams(sem):
    return pltpu.CompilerParams(dimension_semantics=sem, vmem_limit_bytes=VMEM_LIMIT)


def _rms(x, g):
    ms = jnp.mean(x * x, axis=-1, keepdims=True)
    return x * lax.rsqrt(ms + EPS) * g


def _sigmoid(x):
    return 1.0 / (1.0 + jnp.exp(-x))


def _dot(a, b):
    return jnp.dot(a, b, preferred_element_type=F32)


def _pack_rows(y):
    lo = pltpu.bitcast(y[:, :PACK_W].astype(BF16).astype(F32), jnp.uint32)
    hi = pltpu.bitcast(y[:, PACK_W:].astype(BF16).astype(F32), jnp.uint32)
    return hi | lax.shift_right_logical(lo, jnp.uint32(16))


def _unpack_rows(w):
    lo = pltpu.bitcast(lax.shift_left(w, jnp.uint32(16)), F32)
    hi = pltpu.bitcast(w & jnp.uint32(0xFFFF0000), F32)
    return jnp.concatenate([lo, hi], axis=1)


def _proj_kernel(x_ref, g_ref, w_ref, u_ref, q_ref, kv_ref):
    h = _rms(x_ref[...], g_ref[...]).astype(BF16)
    proj = _dot(h, w_ref[...])
    u_ref[...] = proj[:, :D_SSM].astype(BF16)
    q_ref[...] = (proj[:, D_SSM:D_SSM + ATT_W] * (HEAD_DIM ** -0.5)).astype(BF16)
    kv_ref[...] = proj[:, D_SSM + ATT_W:].astype(BF16)


def _proj(x2d, g_mix, w_in_b, tm=512):
    t = x2d.shape[0]
    row = lambda w: pl.BlockSpec((tm, w), lambda i: (i, 0))
    full = lambda a: pl.BlockSpec(a.shape, lambda i: (0,) * a.ndim)
    return pl.pallas_call(
        _proj_kernel,
        grid=(t // tm,),
        in_specs=[row(D_MODEL), full(g_mix), full(w_in_b)],
        out_specs=[row(D_SSM), row(ATT_W), row(KV_W)],
        out_shape=[jax.ShapeDtypeStruct((t, D_SSM), BF16),
                   jax.ShapeDtypeStruct((t, ATT_W), BF16),
                   jax.ShapeDtypeStruct((t, KV_W), BF16)],
        compiler_params=_cparams(("parallel",)),
        name="proj",
    )(x2d, g_mix, w_in_b)


def _ssmprep_kernel(are_ref, aim_ref, ldt_ref, bre_ref, bim_ref, abr_ref, abi_ref, bbr_ref, bbi_ref):
    a_re, a_im = are_ref[...], aim_ref[...]
    dt = jnp.exp(ldt_ref[...])
    mag = jnp.exp(a_re * dt)
    ab_re = mag * jnp.cos(a_im * dt)
    ab_im = mag * jnp.sin(a_im * dt)
    abr_ref[...] = ab_re
    abi_ref[...] = ab_im
    den = a_re * a_re + a_im * a_im
    c_re = ((ab_re - 1.0) * a_re + ab_im * a_im) / den
    c_im = (ab_im * a_re - (ab_re - 1.0) * a_im) / den
    for c in range(SSM_CH):
        b_re, b_im = bre_ref[c], bim_ref[c]
        bbr_ref[c] = c_re * b_re - c_im * b_im
        bbi_ref[c] = c_re * b_im + c_im * b_re


def _ssmprep(a_re, a_im, log_dt, b_re, b_im):
    g, p = a_re.shape
    bt_re = jnp.transpose(b_re, (2, 0, 1))
    bt_im = jnp.transpose(b_im, (2, 0, 1))
    gp = jax.ShapeDtypeStruct((g, p), F32)
    cgp = jax.ShapeDtypeStruct((SSM_CH, g, p), F32)
    return pl.pallas_call(
        _ssmprep_kernel, out_shape=[gp, gp, cgp, cgp], name="ssmprep",
    )(a_re, a_im, log_dt.reshape(g, 1), bt_re, bt_im)


def _ssm_weights(ab_re, ab_im, bb_re, bb_im, c_re, c_im):
    g, p = ab_re.shape
    kb = D_SSM // 256
    gk = g // kb
    eye = jnp.eye(gk, dtype=F32)

    def in_block(bb):
        b4 = jnp.transpose(bb, (1, 0, 2)).reshape(kb, gk, SSM_CH, p)
        return jnp.einsum("kgcp,gh->kgchp", b4, eye).reshape(kb, gk * SSM_CH, gk * p)

    w_in = jnp.concatenate([in_block(bb_re), in_block(bb_im)], axis=2).astype(BF16)
    nj = D_SSM // LANES
    gj = g // nj
    eyej = jnp.eye(gj, dtype=F32)

    def out_block(c):
        c4 = jnp.transpose(c, (0, 2, 1)).reshape(nj, gj, p, SSM_CH)
        return jnp.einsum("jgpc,gh->jgphc", c4, eyej).reshape(nj, gj * p, gj * SSM_CH)

    w_out = jnp.stack([out_block(c_re), out_block(-c_im)], axis=1).astype(BF16)
    return w_in, w_out, ab_re.reshape(1, g * p), ab_im.reshape(1, g * p)


def _gelu_tanh(x):
    return 0.5 * x * (1.0 + jnp.tanh(math.sqrt(2.0 / math.pi) * (x + 0.044715 * (x * x * x))))


def _ssm_kernel(u_ref, ul_ref, win_ref, wout_ref, ar_ref, ai_ref, d_ref, wglu_ref, bglu_ref, y_ref,
                perm_ref, permt_ref, bu0_ref, bu1_ref, xb0_ref, xb1_ref, st_ref, *, nb, lc):
    rows = nb * lc
    ns = N_STATE
    k = pl.program_id(0)
    bu = (bu0_ref, bu1_ref)
    xb = (xb0_ref, xb1_ref)

    @pl.when(k == 0)
    def _():
        r = lax.broadcasted_iota(jnp.int32, (rows, rows), 0)
        c = lax.broadcasted_iota(jnp.int32, (rows, rows), 1)
        sh = nb.bit_length() - 1
        perm_ref[...] = jnp.where(c == (r & (nb - 1)) * lc + (r >> sh), 1.0, 0.0).astype(BF16)
        permt_ref[...] = jnp.where(r == (c & (nb - 1)) * lc + (c >> sh), 1.0, 0.0).astype(BF16)
        bu1_ref[...] = jnp.zeros_like(bu1_ref)
        xb0_ref[...] = jnp.zeros_like(xb0_ref)
        xb1_ref[...] = jnp.zeros_like(xb1_ref)
        st_ref[...] = jnp.zeros_like(st_ref)

    kb = win_ref.shape[0]
    half = ns // kb
    nj = wout_ref.shape[0]
    kw = ns // nj
    ar, ai = ar_ref[...], ai_ref[...]
    xr, xi = st_ref[:, 0:ns], st_ref[:, ns:2 * ns]

    for h in range(2):
        part = slice(h * lc, (h + 1) * lc)
        u_new = _dot(perm_ref[...], u_ref[:, part, :].reshape(rows, D_SSM)).astype(BF16)
        for j in range(kb):
            res = _dot(u_new[:, 256 * j:256 * (j + 1)], win_ref[j])
            bu[h][:, half * j:half * (j + 1)] = res[:, :half]
            bu[h][:, ns + half * j:ns + half * (j + 1)] = res[:, half:]
        src, dst = bu[1 - h], xb[1 - h]
        for t in range(0, lc, 2):
            keep_r, keep_i = [], []
            for s in range(2):
                row = (t + s) * nb
                br = src[row:row + nb, 0:ns]
                bi = src[row:row + nb, ns:2 * ns]
                xr, xi = ar * xr - ai * xi + br, ar * xi + ai * xr + bi
                keep_r.append(xr)
                keep_i.append(xi)
            dst[t * nb:(t + 2) * nb, 0:ns] = jnp.concatenate(keep_r, axis=0).astype(BF16)
            dst[t * nb:(t + 2) * nb, ns:2 * ns] = jnp.concatenate(keep_i, axis=0).astype(BF16)
        ys = []
        for j in range(nj):
            ys.append(_dot(xb[h][:, kw * j:kw * (j + 1)], wout_ref[j, 0])
                      + _dot(xb[h][:, ns + kw * j:ns + kw * (j + 1)], wout_ref[j, 1]))
        y_tb = jnp.concatenate(ys, axis=1).astype(BF16)
        y = _dot(permt_ref[...], y_tb) + d_ref[...] * ul_ref[:, part, :].reshape(rows, D_SSM).astype(F32)
        y = _gelu_tanh(y)
        z = _dot(y.astype(BF16), wglu_ref[...]) + bglu_ref[...]
        y_ref[:, part, :] = (y * _sigmoid(z)).astype(BF16).reshape(nb, lc, D_SSM)

    st_ref[:, 0:ns] = xr
    st_ref[:, ns:2 * ns] = xi


def _ssm(u3, w_in, w_out, ar, ai, d_skip, w_glu_b, b_glu, lc=64):
    nb, s, _ = u3.shape
    rows = nb * lc
    nk = s // (2 * lc)
    assert nb & (nb - 1) == 0 and nb % 8 == 0 and s % (2 * lc) == 0 and lc % 16 == 0
    full = lambda a: pl.BlockSpec(a.shape, lambda i: (0,) * a.ndim)
    lead = pl.BlockSpec((nb, 2 * lc, D_SSM), lambda i: (0, jnp.minimum(i, nk - 1), 0))
    lag = pl.BlockSpec((nb, 2 * lc, D_SSM), lambda i: (0, jnp.maximum(i - 1, 0), 0))
    return pl.pallas_call(
        functools.partial(_ssm_kernel, nb=nb, lc=lc),
        grid=(nk + 1,),
        in_specs=[lead, lag, full(w_in), full(w_out), full(ar), full(ai), full(d_skip), full(w_glu_b), full(b_glu)],
        out_specs=lag,
        out_shape=jax.ShapeDtypeStruct((nb, s, D_SSM), BF16),
        scratch_shapes=[pltpu.VMEM((rows, rows), BF16), pltpu.VMEM((rows, rows), BF16),
                        pltpu.VMEM((rows, 2 * N_STATE), F32), pltpu.VMEM((rows, 2 * N_STATE), F32),
                        pltpu.VMEM((rows, 2 * N_STATE), BF16), pltpu.VMEM((rows, 2 * N_STATE), BF16),
                        pltpu.VMEM((nb, 2 * N_STATE), F32)],
        compiler_params=_cparams(("arbitrary",)),
        name="ssm",
    )(u3, u3, w_in, w_out, ar, ai, d_skip, w_glu_b, b_glu)


def _t5_bucket_np(rel):
    max_exact = NUM_BUCKETS // 2
    relf = np.maximum(rel, 1).astype(np.float32)
    large = max_exact + (np.log(relf / np.float32(max_exact)) / np.float32(math.log(MAX_DIST / max_exact))
                         * np.float32(NUM_BUCKETS - max_exact)).astype(np.int32)
    large = np.minimum(large, NUM_BUCKETS - 1)
    return np.where(rel < max_exact, rel, large)


def _bias_table(rel_bias):
    q_loc = np.arange(BLOCK)[:, None]
    c_loc = np.arange(2 * BLOCK)[None, :]
    rel = q_loc + BLOCK - c_loc
    valid = (rel >= 0) & (rel < WINDOW)
    bucket = _t5_bucket_np(np.maximum(rel, 0)).reshape(-1, 1)
    onehot = (jnp.asarray(bucket) == jnp.arange(NUM_BUCKETS)[None, :]).astype(F32)
    bias = jnp.dot(onehot, rel_bias.astype(F32), precision=lax.Precision.HIGHEST)
    bias = jnp.where(valid.reshape(-1, 1), bias, NEG)
    return jnp.transpose(bias, (1, 0)).reshape(N_HEADS, BLOCK, 2 * BLOCK)


def _attn_kernel(sink_ref, q_ref, kvp_ref, kvc_ref, bias_ref, o_ref, *, nq):
    n = pl.program_id(1)
    kv_all = jnp.concatenate([kvp_ref[0], kvc_ref[0]], axis=0)
    col = lax.broadcasted_iota(jnp.int32, (BLOCK, 2 * BLOCK), 1)
    no_prev = jnp.where(col < BLOCK, jnp.where(n == 0, NEG, 0.0), 0.0)
    for j in range(nq):
        q = q_ref[0, BLOCK * j:BLOCK * (j + 1), :]
        kv = kv_all[BLOCK * j:BLOCK * (j + 2), :]
        outs = []
        for h in range(N_HEADS):
            g = h // GROUP
            qh = q[:, HEAD_DIM * h:HEAD_DIM * (h + 1)]
            kh = kv[:, HEAD_DIM * g:HEAD_DIM * (g + 1)]
            vh = kv[:, N_KV * HEAD_DIM + HEAD_DIM * g:N_KV * HEAD_DIM + HEAD_DIM * (g + 1)]
            s = lax.dot_general(qh, kh, (((1,), (1,)), ((), ())), preferred_element_type=F32)
            s = s + bias_ref[h]
            if j == 0:
                s = s + no_prev
            sink = sink_ref[h]
            m = jnp.maximum(jnp.max(s, axis=-1, keepdims=True), sink)
            e = jnp.exp(s - m)
            den = jnp.sum(e, axis=-1, keepdims=True) + jnp.exp(sink - m)
            outs.append(_dot(e.astype(BF16), vh) / den)
        o_ref[0, BLOCK * j:BLOCK * (j + 1), :] = jnp.concatenate(outs, axis=1).astype(BF16)


def _attn(q3, kv3, bias, sinks, nq=4):
    b, s, _ = q3.shape
    qb = nq * BLOCK
    grid_spec = pltpu.PrefetchScalarGridSpec(
        num_scalar_prefetch=0,
        grid=(b, s // qb),
        in_specs=[pl.BlockSpec(memory_space=pltpu.SMEM),
                  pl.BlockSpec((1, qb, ATT_W), lambda i, n: (i, n, 0)),
                  pl.BlockSpec((1, BLOCK, KV_W), lambda i, n: (i, jnp.maximum(n * nq - 1, 0), 0)),
                  pl.BlockSpec((1, qb, KV_W), lambda i, n: (i, n, 0)),
                  pl.BlockSpec(bias.shape, lambda i, n: (0, 0, 0))],
        out_specs=pl.BlockSpec((1, qb, ATT_W), lambda i, n: (i, n, 0)),
    )
    return pl.pallas_call(
        functools.partial(_attn_kernel, nq=nq),
        grid_spec=grid_spec,
        out_shape=jax.ShapeDtypeStruct((b, s, ATT_W), BF16),
        compiler_params=_cparams(("parallel", "parallel")),
        name="attn",
    )(sinks, q3, kv3, kv3, bias)


def _route(logits):
    lane = lax.broadcasted_iota(jnp.int32, logits.shape, 1).astype(F32)
    big = float(LANES)
    gl = jnp.where(lane < N_GROUPS, logits, NEG)
    mg = jnp.max(gl, axis=-1, keepdims=True)
    gidx = jnp.min(jnp.where(gl == mg, lane, big), axis=-1, keepdims=True)
    gsum = jnp.sum(jnp.where(lane < N_GROUPS, jnp.exp(gl - mg), 0.0), axis=-1, keepdims=True)
    gp = 1.0 / gsum
    lo = N_GROUPS + EXPERTS_PER_GROUP * gidx
    el = jnp.where(lane >= lo, jnp.where(lane < lo + EXPERTS_PER_GROUP, logits, NEG), NEG)
    m1 = jnp.max(el, axis=-1, keepdims=True)
    i1 = jnp.min(jnp.where(el == m1, lane, big), axis=-1, keepdims=True)
    el2 = jnp.where(lane == i1, NEG, el)
    m2 = jnp.max(el2, axis=-1, keepdims=True)
    i2 = jnp.min(jnp.where(el2 == m2, lane, big), axis=-1, keepdims=True)
    t = jnp.exp(m2 - m1)
    w1 = gp / (1.0 + t)
    w2 = gp * t / (1.0 + t)
    return jnp.where(lane == 0, i1 - N_GROUPS,
                     jnp.where(lane == 1, i2 - N_GROUPS,
                               jnp.where(lane == 2, w1, jnp.where(lane == 3, w2, 0.0))))


def _row_copy(src_ref, dst_ref, sem, src_row, dst_row):
    return pltpu.make_async_copy(src_ref.at[pl.ds(src_row, 1)], dst_ref.at[pl.ds(dst_row, 1)], sem)


def _start_all(copies):
    for k, c in enumerate(copies):
        c.start(priority=k % 2)


def _merge_kernel(x_ref, ya_ref, yb_ref, gmix_ref, wg_ref, bg_ref, wa_ref, wb_ref, wo_ref, gffn_ref, wr_ref, br_ref,
                  x1_ref, route_ref, dest_ref, alloc_ref, state_ref, xd_hbm,
                  tri_ref, pre_ref, st_ref, hbuf_ref, destv_ref, dests_ref, stv_ref, sts_ref, zblk_ref,
                  hs_ref, gate_ref, sem_rows, sem_idx, sem_pad, *, tm, n_blk):
    i = pl.program_id(0)
    last = pl.num_programs(0) - 1
    slot = lax.rem(i, NBUF)
    prev = lax.rem(i + NBUF - 1, NBUF)
    n = TOP_K * tm
    dump0 = n_blk * MOE_BLK

    def row_copies(s, lo=0, hi=n):
        return [_row_copy(hbuf_ref.at[s], xd_hbm, sem_rows.at[s], r % tm, dests_ref[s, r // tm, r % tm])
                for r in range(lo, hi)]

    def wait_rows(s):
        for _ in range(n):
            _row_copy(hbuf_ref.at[s], xd_hbm, sem_rows.at[s], 0, dump0).wait()

    def index_copy(s):
        return pltpu.make_async_copy(destv_ref.at[s], dests_ref.at[s], sem_idx.at[s])

    @pl.when(i == 0)
    def _():
        r = lax.broadcasted_iota(jnp.int32, (tm, tm), 0)
        c = lax.broadcasted_iota(jnp.int32, (tm, tm), 1)
        tri_ref[...] = jnp.where(r > c, 1.0, 0.0).astype(BF16)
        r = lax.broadcasted_iota(jnp.int32, (LANES, LANES), 0)
        c = lax.broadcasted_iota(jnp.int32, (LANES, LANES), 1)
        pre_ref[...] = jnp.where(r < c, 1.0, 0.0).astype(BF16)
        r = lax.broadcasted_iota(jnp.int32, (8, LANES), 0)
        st_ref[...] = jnp.where(r == 0, float(MOE_BLK), 0.0)
        hbuf_ref[NBUF - 1] = jnp.zeros((tm, PACK_W), jnp.uint32)

        def spare(t, carry):
            dests_ref[NBUF - 1, 0, t] = dump0 + t
            dests_ref[NBUF - 1, 1, t] = dump0 + tm + t
            return carry

        lax.fori_loop(0, tm, spare, 0)

    @pl.when(i >= NBUF - 1)
    def _():
        wait_rows(slot)

    @pl.when(i >= 1)
    def _():
        index_copy(prev).wait()

    group = n // 4
    once = jnp.minimum(dests_ref[prev, 0, 0], 0) + 1

    def region(body):
        lax.fori_loop(0, once, lambda _, c: (body(), c)[1], 0)

    def gates_lo():
        _start_all(row_copies(prev, 0, group))
        h = _rms(x_ref[...], gmix_ref[...]).astype(BF16)
        hs_ref[...] = h
        gate_ref[0] = _sigmoid(_dot(h, wg_ref[:, 0:D_MODEL]) + bg_ref[:, 0:D_MODEL])

    def gates_hi():
        _start_all(row_copies(prev, group, 2 * group))
        gate_ref[1] = _sigmoid(_dot(hs_ref[...], wg_ref[:, D_MODEL:2 * D_MODEL]) + bg_ref[:, D_MODEL:2 * D_MODEL])

    def branches():
        _start_all(row_copies(prev, 2 * group, 3 * group))
        merged = gate_ref[0] * _dot(ya_ref[...], wa_ref[...]) + gate_ref[1] * _dot(yb_ref[...], wb_ref[...])
        hs_ref[...] = merged.astype(BF16)

    region(gates_lo)
    region(gates_hi)
    region(branches)
    _start_all(row_copies(prev, 3 * group, n))
    x1 = x_ref[...] + _dot(hs_ref[...], wo_ref[...])
    x1_ref[...] = x1
    h2 = _rms(x1, gffn_ref[...])
    hbuf_ref[slot] = _pack_rows(h2)
    route = _route(_dot(h2.astype(BF16), wr_ref[...]) + br_ref[...])
    route_ref[...] = route

    lane = lax.broadcasted_iota(jnp.int32, route.shape, 1).astype(F32)
    e1, e2 = route[:, 0:1], route[:, 1:2]
    is1, is2 = lane == e1, lane == e2
    member = jnp.where(is1, 1.0, jnp.where(is2, 1.0, 0.0))
    before = _dot(tri_ref[...], member.astype(BF16))
    cnt = jnp.sum(member, axis=0, keepdims=True)
    fill, blk, free = st_ref[0:1, :], st_ref[1:2, :], st_ref[2:3, :]
    need = fill + cnt
    new = jnp.floor((need + float(MOE_BLK - 1)) * (1.0 / MOE_BLK)) - 1.0
    base = free + _dot(jnp.broadcast_to(new, (8, LANES)).astype(BF16), pre_ref[...])[0:1, :]
    q = fill + before
    jb = jnp.floor(q * (1.0 / MOE_BLK))
    rowid = jnp.where(jb == 0.0, blk, base + jb - 1.0) * float(MOE_BLK) + (q - jb * float(MOE_BLK))
    d1 = jnp.sum(jnp.where(is1, rowid, 0.0), axis=-1, keepdims=True)
    d2 = jnp.sum(jnp.where(is2, rowid, 0.0), axis=-1, keepdims=True)
    slab = jnp.where(lane == 0, d1, jnp.where(lane == 1, d2, 0.0))
    d8 = jnp.transpose(slab, (1, 0))[0:8, :].astype(jnp.int32)
    destv_ref[slot] = d8
    dest_ref[0] = d8
    index_copy(slot).start()

    r8 = lax.broadcasted_iota(jnp.int32, (8, LANES), 0)
    alloc_ref[0] = jnp.where(r8 == 0, new, jnp.where(r8 == 1, base, 0.0))
    state = jnp.where(r8 == 0, need - new * float(MOE_BLK),
                      jnp.where(r8 == 1, jnp.where(new > 0.0, base + new - 1.0, blk),
                                jnp.where(r8 == 2, free + jnp.sum(new, axis=-1, keepdims=True), 0.0)))
    st_ref[...] = state
    state_ref[...] = state

    @pl.when(i == last)
    def _():
        index_copy(slot).wait()
        _start_all(row_copies(slot))
        for s in range(NBUF):
            wait_rows(s)
        stv_ref[...] = state.astype(jnp.int32)
        cp = pltpu.make_async_copy(stv_ref, sts_ref, sem_pad)
        cp.start()
        cp.wait()
        zblk_ref[...] = jnp.zeros_like(zblk_ref)

        def zero_rows(e, carry):
            first = sts_ref[1, e] * MOE_BLK
            lo = sts_ref[0, e]
            for g, nxt in ((1, 8), (8, 64), (64, MOE_BLK)):
                hi = jnp.minimum((lo + nxt - 1) // nxt * nxt, MOE_BLK)

                def run(j, g=g):
                    dst = xd_hbm.at[pl.ds(pl.multiple_of(first + j * g, g), g)]
                    return pltpu.make_async_copy(zblk_ref.at[pl.ds(0, g)], dst, sem_pad)

                def start(j, c, run=run):
                    run(j).start()
                    return c

                def wait(j, c, run=run):
                    run(j).wait()
                    return c

                lax.fori_loop(lo // g, hi // g, start, 0)
                lax.fori_loop(lo // g, hi // g, wait, 0)
                lo = hi
            return carry

        lax.fori_loop(0, N_EXPERTS, zero_rows, 0)

        def block_copy(b):
            return pltpu.make_async_copy(zblk_ref, xd_hbm.at[pl.ds(pl.multiple_of(b * MOE_BLK, MOE_BLK), MOE_BLK)],
                                         sem_pad)

        def zero_block(b, c):
            block_copy(b).start()
            block_copy(b).wait()
            return c

        lax.fori_loop(sts_ref[2, 0], n_blk, zero_block, 0)


def _merge(x2d, ya, yb, g_mix, wg, bg, wa, wb, wo, g_ffn, wr, br, n_blk, tm=512):
    t = x2d.shape[0]
    nt = t // tm
    n_rows = n_blk * MOE_BLK + TOP_K * tm
    row = lambda w: pl.BlockSpec((tm, w), lambda i: (i, 0))
    full = lambda a: pl.BlockSpec(a.shape, lambda i: (0,) * a.ndim)
    once = lambda a: pl.BlockSpec(a.shape, lambda i: (0,) * a.ndim, pipeline_mode=pl.Buffered(1))
    return pl.pallas_call(
        functools.partial(_merge_kernel, tm=tm, n_blk=n_blk),
        grid=(nt,),
        in_specs=[row(D_MODEL), row(D_SSM), row(ATT_W), full(g_mix), once(wg), full(bg),
                  once(wa), once(wb), once(wo), full(g_ffn), full(wr), full(br)],
        out_specs=[row(D_MODEL), row(LANES), pl.BlockSpec((1, 8, tm), lambda i: (i, 0, 0)),
                   pl.BlockSpec((1, 8, LANES), lambda i: (i, 0, 0)), pl.BlockSpec((8, LANES), lambda i: (0, 0)),
                   pl.BlockSpec(memory_space=pl.ANY)],
        out_shape=[jax.ShapeDtypeStruct((t, D_MODEL), F32),
                   jax.ShapeDtypeStruct((t, LANES), F32),
                   jax.ShapeDtypeStruct((nt, 8, tm), jnp.int32),
                   jax.ShapeDtypeStruct((nt, 8, LANES), F32),
                   jax.ShapeDtypeStruct((8, LANES), F32),
                   jax.ShapeDtypeStruct((n_rows, PACK_W), jnp.uint32)],
        scratch_shapes=[pltpu.VMEM((tm, tm), BF16), pltpu.VMEM((LANES, LANES), BF16), pltpu.VMEM((8, LANES), F32),
                        pltpu.VMEM((NBUF, tm, PACK_W), jnp.uint32), pltpu.VMEM((NBUF, 8, tm), jnp.int32),
                        pltpu.SMEM((NBUF, 8, tm), jnp.int32), pltpu.VMEM((8, LANES), jnp.int32),
                        pltpu.SMEM((8, LANES), jnp.int32), pltpu.VMEM((MOE_BLK, PACK_W), jnp.uint32),
                        pltpu.VMEM((tm, D_MODEL), BF16), pltpu.VMEM((2, tm, D_MODEL), F32),
                        pltpu.SemaphoreType.DMA((NBUF,)), pltpu.SemaphoreType.DMA((NBUF,)),
                        pltpu.SemaphoreType.DMA],
        compiler_params=_cparams(("arbitrary",)),
        name="merge",
    )(x2d, ya, yb, g_mix, wg, bg, wa, wb, wo, g_ffn, wr, br)


def _moe_kernel(be_ref, bn_ref, bi_ref, xd_ref, wg_ref, wu_ref, wd_ref, yd_ref, wgb_ref, wub_ref, wdb_ref):
    del bi_ref
    i = pl.program_id(0)

    @pl.when(bn_ref[i] == 0)
    def _():
        yd_ref[...] = jnp.zeros_like(yd_ref)

    @pl.when(bn_ref[i] > 0)
    def _():
        prev = be_ref[jnp.maximum(i - 1, 0)]

        @pl.when(jnp.logical_or(i == 0, be_ref[i] != prev))
        def _():
            wgb_ref[...] = wg_ref[0].astype(BF16)
            wub_ref[...] = wu_ref[0].astype(BF16)
            wdb_ref[...] = wd_ref[0].astype(BF16)

        x = _unpack_rows(xd_ref[...]).astype(BF16)
        g = _dot(x, wgb_ref[...])
        u = _dot(x, wub_ref[...])
        a = (g * _sigmoid(g) * u).astype(BF16)
        yd_ref[...] = _pack_rows(_dot(a, wdb_ref[...]))


def _moe(blk_e, blk_n, blk_i, xdw, w_eg, w_eu, w_ed):
    n_blk = blk_e.shape[0]
    grid_spec = pltpu.PrefetchScalarGridSpec(
        num_scalar_prefetch=3,
        grid=(n_blk,),
        in_specs=[pl.BlockSpec((MOE_BLK, PACK_W), lambda i, be, bn, bi: (bi[i], 0)),
                  pl.BlockSpec((1, D_MODEL, D_FF_EXPERT), lambda i, be, bn, bi: (be[i], 0, 0)),
                  pl.BlockSpec((1, D_MODEL, D_FF_EXPERT), lambda i, be, bn, bi: (be[i], 0, 0)),
                  pl.BlockSpec((1, D_FF_EXPERT, D_MODEL), lambda i, be, bn, bi: (be[i], 0, 0))],
        out_specs=pl.BlockSpec((MOE_BLK, PACK_W), lambda i, be, bn, bi: (bi[i], 0)),
        scratch_shapes=[pltpu.VMEM((D_MODEL, D_FF_EXPERT), BF16), pltpu.VMEM((D_MODEL, D_FF_EXPERT), BF16),
                        pltpu.VMEM((D_FF_EXPERT, D_MODEL), BF16)],
    )
    return pl.pallas_call(
        _moe_kernel,
        grid_spec=grid_spec,
        out_shape=jax.ShapeDtypeStruct((n_blk * MOE_BLK, PACK_W), jnp.uint32),
        compiler_params=_cparams(("arbitrary",)),
        name="moe",
    )(blk_e, blk_n, blk_i, xdw, w_eg, w_eu, w_ed)


def _final_kernel(dcur_ref, dnxt_ref, x1_ref, p_ref, route_ref, yd_hbm, wpp_ref, wpg_ref, bpg_ref, gple_ref,
                  gfin_ref, o_ref, yw0_ref, yw1_ref, sem, *, tm):
    j = pl.program_id(0)
    last = pl.num_programs(0) - 1
    n = TOP_K * tm
    yw = (yw0_ref, yw1_ref)

    def gather(dref, half, slot):
        return [_row_copy(yd_hbm, yw[slot], sem.at[slot], dref[0, r // tm, half * tm + r % tm], r)
                for r in range(n)]

    def tile(half, slot):
        rows = slice(half * tm, (half + 1) * tm)
        rt = route_ref[rows, :]
        y0 = _unpack_rows(yw[slot][0:tm, :])
        y1 = _unpack_rows(yw[slot][tm:n, :])
        x2 = x1_ref[rows, :] + rt[:, 2:3] * y0 + rt[:, 3:4] * y1
        h3 = _rms(x2, gple_ref[...]).astype(BF16)
        pp = _dot(p_ref[rows, :].astype(BF16), wpp_ref[...])
        x3 = x2 + pp * _sigmoid(_dot(h3, wpg_ref[...]) + bpg_ref[...])
        o_ref[rows, :] = _rms(x3, gfin_ref[...])

    @pl.when(j == 0)
    def _():
        _start_all(gather(dcur_ref, 0, 0))

    for c in gather(dcur_ref, 0, 0):
        c.wait()
    _start_all(gather(dcur_ref, 1, 1))
    tile(0, 0)
    for c in gather(dcur_ref, 1, 1):
        c.wait()
    _start_all(gather(dnxt_ref, 0, 0))
    tile(1, 1)

    @pl.when(j == last)
    def _():
        for c in gather(dnxt_ref, 0, 0):
            c.wait()


def _final(dest, x1, p2d, route, ydw, wpp, wpg, bpg, g_ple, g_final):
    t = x1.shape[0]
    ns, _, two_tm = dest.shape
    tm = two_tm // 2
    n = TOP_K * tm
    row = lambda w: pl.BlockSpec((2 * tm, w), lambda i: (i, 0))
    full = lambda a: pl.BlockSpec(a.shape, lambda i: (0,) * a.ndim)
    return pl.pallas_call(
        functools.partial(_final_kernel, tm=tm),
        grid=(ns,),
        in_specs=[pl.BlockSpec((1, 8, two_tm), lambda i: (i, 0, 0), memory_space=pltpu.SMEM),
                  pl.BlockSpec((1, 8, two_tm), lambda i: (jnp.minimum(i + 1, ns - 1), 0, 0),
                               memory_space=pltpu.SMEM),
                  row(D_MODEL), row(PLE_DIM), row(LANES), pl.BlockSpec(memory_space=pl.ANY),
                  full(wpp), full(wpg), full(bpg), full(g_ple), full(g_final)],
        out_specs=row(D_MODEL),
        out_shape=jax.ShapeDtypeStruct((t, D_MODEL), F32),
        scratch_shapes=[pltpu.VMEM((n, PACK_W), jnp.uint32), pltpu.VMEM((n, PACK_W), jnp.uint32),
                        pltpu.SemaphoreType.DMA((2,))],
        compiler_params=_cparams(("arbitrary",)),
        name="final",
    )(dest, dest, x1, p2d, route, ydw, wpp, wpg, bpg, g_ple, g_final)


def _block_plan(alloc, state, n_blk):
    new = alloc[:, 0, :N_EXPERTS].astype(jnp.int32).reshape(-1)
    base = alloc[:, 1, :N_EXPERTS].astype(jnp.int32).reshape(-1)
    expert = jnp.tile(jnp.arange(N_EXPERTS, dtype=jnp.int32), alloc.shape[0])
    fill = state[0, :N_EXPERTS].astype(jnp.int32)
    last_blk = state[1, :N_EXPERTS].astype(jnp.int32)
    taken = state[2, 0].astype(jnp.int32)
    b = jnp.arange(n_blk, dtype=jnp.int32)
    opened = (base[None, :] <= b[:, None]) & (b[:, None] < (base + new)[None, :])
    e_of = jnp.sum(jnp.where(opened, expert[None, :], 0), axis=1)
    onehot = e_of[:, None] == jnp.arange(N_EXPERTS, dtype=jnp.int32)[None, :]
    pick = lambda v: jnp.sum(jnp.where(onehot, v[None, :], 0), axis=1)
    rows = jnp.where(b < taken, jnp.where(b == pick(last_blk), pick(fill), MOE_BLK), 0)
    key = jnp.where(b < taken, e_of, N_EXPERTS) * n_blk + b
    pos = jnp.sum(key[None, :] < key[:, None], axis=1)
    at = pos[None, :] == b[:, None]
    order = lambda v: jnp.sum(jnp.where(at, v[None, :], 0), axis=1).astype(jnp.int32)
    return order(jnp.where(b < taken, e_of, N_EXPERTS - 1)), order(rows), order(b)


def _row(v):
    return v.reshape(1, -1).astype(F32)


def kernel(x, p, rel_bias, g_mix, w_in, w_gate, b_gate, ssm_a_re, ssm_a_im, ssm_log_dt, ssm_b_re, ssm_b_im,
           ssm_c_re, ssm_c_im, ssm_d, w_glu, b_glu, sinks, w_br_ssm, w_br_attn, w_out, g_ffn, w_router_group,
           b_router_group, w_router_expert, b_router_expert, w_e_gate, w_e_up, w_e_down, g_ple, w_ple_gate,
           b_ple_gate, w_ple_proj, g_final):
    bsz, seq, dm = x.shape
    assert g_mix.shape[0] == 1, "one layer followed by the final norm"
    i = 0
    t = bsz * seq
    x2d = x.reshape(t, dm)
    bias = _bias_table(rel_bias)
    u, q, kv = _proj(x2d, _row(g_mix[i]), w_in[i].astype(BF16))
    ab_re, ab_im, bb_re, bb_im = _ssmprep(ssm_a_re[i], ssm_a_im[i], ssm_log_dt[i], ssm_b_re[i], ssm_b_im[i])
    ws_in, ws_out, ar, ai = _ssm_weights(ab_re, ab_im, bb_re, bb_im, ssm_c_re[i], ssm_c_im[i])
    y_ssm = _ssm(u.reshape(bsz, seq, D_SSM), ws_in, ws_out, ar, ai, _row(ssm_d[i]),
                 w_glu[i].astype(BF16), _row(b_glu[i]))
    y_att = _attn(q.reshape(bsz, seq, ATT_W), kv.reshape(bsz, seq, KV_W), bias, sinks[i].astype(F32))
    w_router = jnp.zeros((dm, LANES), F32)
    w_router = w_router.at[:, :N_GROUPS].set(w_router_group[i])
    w_router = w_router.at[:, N_GROUPS:N_GROUPS + N_EXPERTS].set(w_router_expert[i])
    b_router = jnp.zeros((1, LANES), F32)
    b_router = b_router.at[0, :N_GROUPS].set(b_router_group[i])
    b_router = b_router.at[0, N_GROUPS:N_GROUPS + N_EXPERTS].set(b_router_expert[i])
    n_blk = t * TOP_K // MOE_BLK + N_EXPERTS
    x1, route, dest, alloc, state, xdw = _merge(
        x2d, y_ssm.reshape(t, D_SSM), y_att.reshape(t, ATT_W), _row(g_mix[i]), w_gate[i].astype(BF16),
        _row(b_gate[i]), w_br_ssm[i].astype(BF16), w_br_attn[i].astype(BF16), w_out[i].astype(BF16),
        _row(g_ffn[i]), w_router.astype(BF16), b_router, n_blk)
    blk_e, blk_n, blk_i = _block_plan(alloc, state, n_blk)
    ydw = _moe(blk_e, blk_n, blk_i, xdw, w_e_gate[i], w_e_up[i], w_e_down[i])
    out = _final(dest, x1, p[i].reshape(t, PLE_DIM), route, ydw, w_ple_proj[i].astype(BF16),
                 w_ple_gate[i].astype(BF16), _row(b_ple_gate[i]), _row(g_ple[i]), _row(g_final))
    return out.reshape(bsz, seq, dm)
```

```python
import functools
import math

import numpy as np
import jax
import jax.numpy as jnp
from jax import lax
from jax.experimental import pallas as pl
from jax.experimental.pallas import tpu as pltpu

D_MODEL = 1024
D_SSM = 512
SSM_CH = 16
SSM_GROUPS = D_SSM // SSM_CH
SSM_STATE = 64
N_STATE = SSM_GROUPS * SSM_STATE
N_HEADS = 8
N_KV = 2
HEAD_DIM = 64
GROUP = N_HEADS // N_KV
ATT_W = N_HEADS * HEAD_DIM
KV_W = 2 * N_KV * HEAD_DIM
WINDOW = 128
BLOCK = 128
NUM_BUCKETS = 32
MAX_DIST = 128
D_IN = D_SSM + ATT_W + KV_W
N_GROUPS = 4
EXPERTS_PER_GROUP = 8
N_EXPERTS = N_GROUPS * EXPERTS_PER_GROUP
TOP_K = 2
D_FF_EXPERT = 512
MOE_BLK = 512
PLE_DIM = 256
EPS = 1e-6

LANES = 128
NEG = -1e30
PACK_W = D_MODEL // 2

BF16 = jnp.bfloat16
F32 = jnp.float32
VMEM_LIMIT = 56 * 1024 * 1024
NBUF = 3


def _cparams(sem):
    return pltpu.CompilerParams(dimension_semantics=sem, vmem_limit_bytes=VMEM_LIMIT)


def _rms(x, g):
    ms = jnp.mean(x * x, axis=-1, keepdims=True)
    return x * lax.rsqrt(ms + EPS) * g


def _sigmoid(x):
    return 1.0 / (1.0 + jnp.exp(-x))


def _dot(a, b):
    return jnp.dot(a, b, preferred_element_type=F32)


def _pack_rows(y):
    lo = pltpu.bitcast(y[:, :PACK_W].astype(BF16).astype(F32), jnp.uint32)
    hi = pltpu.bitcast(y[:, PACK_W:].astype(BF16).astype(F32), jnp.uint32)
    return hi | lax.shift_right_logical(lo, jnp.uint32(16))


def _unpack_rows(w):
    lo = pltpu.bitcast(lax.shift_left(w, jnp.uint32(16)), F32)
    hi = pltpu.bitcast(w & jnp.uint32(0xFFFF0000), F32)
    return jnp.concatenate([lo, hi], axis=1)


def _proj_kernel(x_ref, g_ref, w_ref, u_ref, q_ref, kv_ref):
    h = _rms(x_ref[...], g_ref[...]).astype(BF16)
    proj = _dot(h, w_ref[...])
    u_ref[...] = proj[:, :D_SSM].astype(BF16)
    q_ref[...] = (proj[:, D_SSM:D_SSM + ATT_W] * (HEAD_DIM ** -0.5)).astype(BF16)
    kv_ref[...] = proj[:, D_SSM + ATT_W:].astype(BF16)


def _proj(x2d, g_mix, w_in_b, tm=1024):
    t = x2d.shape[0]
    row = lambda w: pl.BlockSpec((tm, w), lambda i: (i, 0))
    full = lambda a: pl.BlockSpec(a.shape, lambda i: (0,) * a.ndim)
    return pl.pallas_call(
        _proj_kernel,
        grid=(t // tm,),
        in_specs=[row(D_MODEL), full(g_mix), full(w_in_b)],
        out_specs=[row(D_SSM), row(ATT_W), row(KV_W)],
        out_shape=[jax.ShapeDtypeStruct((t, D_SSM), BF16),
                   jax.ShapeDtypeStruct((t, ATT_W), BF16),
                   jax.ShapeDtypeStruct((t, KV_W), BF16)],
        compiler_params=_cparams(("parallel",)),
        name="proj",
    )(x2d, g_mix, w_in_b)


def _ssmprep_kernel(are_ref, aim_ref, ldt_ref, bre_ref, bim_ref, abr_ref, abi_ref, bbr_ref, bbi_ref):
    a_re, a_im = are_ref[...], aim_ref[...]
    dt = jnp.exp(ldt_ref[...])
    mag = jnp.exp(a_re * dt)
    ab_re = mag * jnp.cos(a_im * dt)
    ab_im = mag * jnp.sin(a_im * dt)
    abr_ref[...] = ab_re
    abi_ref[...] = ab_im
    den = a_re * a_re + a_im * a_im
    c_re = ((ab_re - 1.0) * a_re + ab_im * a_im) / den
    c_im = (ab_im * a_re - (ab_re - 1.0) * a_im) / den
    for c in range(SSM_CH):
        b_re, b_im = bre_ref[c], bim_ref[c]
        bbr_ref[c] = c_re * b_re - c_im * b_im
        bbi_ref[c] = c_re * b_im + c_im * b_re


def _ssmprep(a_re, a_im, log_dt, b_re, b_im):
    g, p = a_re.shape
    bt_re = jnp.transpose(b_re, (2, 0, 1))
    bt_im = jnp.transpose(b_im, (2, 0, 1))
    gp = jax.ShapeDtypeStruct((g, p), F32)
    cgp = jax.ShapeDtypeStruct((SSM_CH, g, p), F32)
    return pl.pallas_call(
        _ssmprep_kernel, out_shape=[gp, gp, cgp, cgp], name="ssmprep",
    )(a_re, a_im, log_dt.reshape(g, 1), bt_re, bt_im)


def _ssm_weights(ab_re, ab_im, bb_re, bb_im, c_re, c_im):
    g, p = ab_re.shape
    kb = D_SSM // 256
    gk = g // kb
    eye = jnp.eye(gk, dtype=F32)

    def in_block(bb):
        b4 = jnp.transpose(bb, (1, 0, 2)).reshape(kb, gk, SSM_CH, p)
        return jnp.einsum("kgcp,gh->kgchp", b4, eye).reshape(kb, gk * SSM_CH, gk * p)

    w_in = jnp.concatenate([in_block(bb_re), in_block(bb_im)], axis=2).astype(BF16)
    nj = D_SSM // LANES
    gj = g // nj
    eyej = jnp.eye(gj, dtype=F32)

    def out_block(c):
        c4 = jnp.transpose(c, (0, 2, 1)).reshape(nj, gj, p, SSM_CH)
        return jnp.einsum("jgpc,gh->jgphc", c4, eyej).reshape(nj, gj * p, gj * SSM_CH)

    w_out = jnp.stack([out_block(c_re), out_block(-c_im)], axis=1).astype(BF16)
    return w_in, w_out, ab_re.reshape(1, g * p), ab_im.reshape(1, g * p)


def _gelu_tanh(x):
    return 0.5 * x * (1.0 + jnp.tanh(math.sqrt(2.0 / math.pi) * (x + 0.044715 * (x * x * x))))


def _ssm_kernel(u_ref, ul_ref, win_ref, wout_ref, ar_ref, ai_ref, d_ref, wglu_ref, bglu_ref, y_ref,
                perm_ref, permt_ref, bu0_ref, bu1_ref, xb0_ref, xb1_ref, st_ref, *, nb, lc):
    rows = nb * lc
    ns = N_STATE
    k = pl.program_id(0)
    bu = (bu0_ref, bu1_ref)
    xb = (xb0_ref, xb1_ref)

    @pl.when(k == 0)
    def _():
        r = lax.broadcasted_iota(jnp.int32, (rows, rows), 0)
        c = lax.broadcasted_iota(jnp.int32, (rows, rows), 1)
        sh = nb.bit_length() - 1
        perm_ref[...] = jnp.where(c == (r & (nb - 1)) * lc + (r >> sh), 1.0, 0.0).astype(BF16)
        permt_ref[...] = jnp.where(r == (c & (nb - 1)) * lc + (c >> sh), 1.0, 0.0).astype(BF16)
        bu1_ref[...] = jnp.zeros_like(bu1_ref)
        xb0_ref[...] = jnp.zeros_like(xb0_ref)
        xb1_ref[...] = jnp.zeros_like(xb1_ref)
        st_ref[...] = jnp.zeros_like(st_ref)

    kb = win_ref.shape[0]
    half = ns // kb
    nj = wout_ref.shape[0]
    kw = ns // nj
    ar, ai = ar_ref[...], ai_ref[...]
    xr, xi = st_ref[:, 0:ns], st_ref[:, ns:2 * ns]

    for h in range(2):
        part = slice(h * lc, (h + 1) * lc)
        u_new = _dot(perm_ref[...], u_ref[:, part, :].reshape(rows, D_SSM)).astype(BF16)
        for j in range(kb):
            res = _dot(u_new[:, 256 * j:256 * (j + 1)], win_ref[j])
            bu[h][:, half * j:half * (j + 1)] = res[:, :half]
            bu[h][:, ns + half * j:ns + half * (j + 1)] = res[:, half:]
        src, dst = bu[1 - h], xb[1 - h]
        for t in range(0, lc, 2):
            keep_r, keep_i = [], []
            for s in range(2):
                row = (t + s) * nb
                br = src[row:row + nb, 0:ns]
                bi = src[row:row + nb, ns:2 * ns]
                xr, xi = ar * xr - ai * xi + br, ar * xi + ai * xr + bi
                keep_r.append(xr)
                keep_i.append(xi)
            dst[t * nb:(t + 2) * nb, 0:ns] = jnp.concatenate(keep_r, axis=0).astype(BF16)
            dst[t * nb:(t + 2) * nb, ns:2 * ns] = jnp.concatenate(keep_i, axis=0).astype(BF16)
        ys = []
        for j in range(nj):
            ys.append(_dot(xb[h][:, kw * j:kw * (j + 1)], wout_ref[j, 0])
                      + _dot(xb[h][:, ns + kw * j:ns + kw * (j + 1)], wout_ref[j, 1]))
        y_tb = jnp.concatenate(ys, axis=1).astype(BF16)
        y = _dot(permt_ref[...], y_tb) + d_ref[...] * ul_ref[:, part, :].reshape(rows, D_SSM).astype(F32)
        y = _gelu_tanh(y)
        z = _dot(y.astype(BF16), wglu_ref[...]) + bglu_ref[...]
        y_ref[:, part, :] = (y * _sigmoid(z)).astype(BF16).reshape(nb, lc, D_SSM)

    st_ref[:, 0:ns] = xr
    st_ref[:, ns:2 * ns] = xi


def _ssm(u3, w_in, w_out, ar, ai, d_skip, w_glu_b, b_glu, lc=64):
    nb, s, _ = u3.shape
    rows = nb * lc
    nk = s // (2 * lc)
    assert nb & (nb - 1) == 0 and nb % 8 == 0 and s % (2 * lc) == 0 and lc % 16 == 0
    full = lambda a: pl.BlockSpec(a.shape, lambda i: (0,) * a.ndim)
    lead = pl.BlockSpec((nb, 2 * lc, D_SSM), lambda i: (0, jnp.minimum(i, nk - 1), 0))
    lag = pl.BlockSpec((nb, 2 * lc, D_SSM), lambda i: (0, jnp.maximum(i - 1, 0), 0))
    return pl.pallas_call(
        functools.partial(_ssm_kernel, nb=nb, lc=lc),
        grid=(nk + 1,),
        in_specs=[lead, lag, full(w_in), full(w_out), full(ar), full(ai), full(d_skip), full(w_glu_b), full(b_glu)],
        out_specs=lag,
        out_shape=jax.ShapeDtypeStruct((nb, s, D_SSM), BF16),
        scratch_shapes=[pltpu.VMEM((rows, rows), BF16), pltpu.VMEM((rows, rows), BF16),
                        pltpu.VMEM((rows, 2 * N_STATE), F32), pltpu.VMEM((rows, 2 * N_STATE), F32),
                        pltpu.VMEM((rows, 2 * N_STATE), BF16), pltpu.VMEM((rows, 2 * N_STATE), BF16),
                        pltpu.VMEM((nb, 2 * N_STATE), F32)],
        compiler_params=_cparams(("arbitrary",)),
        name="ssm",
    )(u3, u3, w_in, w_out, ar, ai, d_skip, w_glu_b, b_glu)


def _t5_bucket_np(rel):
    max_exact = NUM_BUCKETS // 2
    relf = np.maximum(rel, 1).astype(np.float32)
    large = max_exact + (np.log(relf / np.float32(max_exact)) / np.float32(math.log(MAX_DIST / max_exact))
                         * np.float32(NUM_BUCKETS - max_exact)).astype(np.int32)
    large = np.minimum(large, NUM_BUCKETS - 1)
    return np.where(rel < max_exact, rel, large)


def _bias_table(rel_bias):
    q_loc = np.arange(BLOCK)[:, None]
    c_loc = np.arange(2 * BLOCK)[None, :]
    rel = q_loc + BLOCK - c_loc
    valid = (rel >= 0) & (rel < WINDOW)
    bucket = _t5_bucket_np(np.maximum(rel, 0)).reshape(-1, 1)
    onehot = (jnp.asarray(bucket) == jnp.arange(NUM_BUCKETS)[None, :]).astype(F32)
    bias = jnp.dot(onehot, rel_bias.astype(F32), precision=lax.Precision.HIGHEST)
    bias = jnp.where(valid.reshape(-1, 1), bias, NEG)
    return jnp.transpose(bias, (1, 0)).reshape(N_HEADS, BLOCK, 2 * BLOCK)


def _attn_kernel(sink_ref, q_ref, kvp_ref, kvc_ref, bias_ref, o_ref, *, nq):
    n = pl.program_id(1)
    kv_all = jnp.concatenate([kvp_ref[0], kvc_ref[0]], axis=0)
    col = lax.broadcasted_iota(jnp.int32, (BLOCK, 2 * BLOCK), 1)
    no_prev = jnp.where(col < BLOCK, jnp.where(n == 0, NEG, 0.0), 0.0)
    for j in range(nq):
        q = q_ref[0, BLOCK * j:BLOCK * (j + 1), :]
        kv = kv_all[BLOCK * j:BLOCK * (j + 2), :]
        outs = []
        for h in range(N_HEADS):
            g = h // GROUP
            qh = q[:, HEAD_DIM * h:HEAD_DIM * (h + 1)]
            kh = kv[:, HEAD_DIM * g:HEAD_DIM * (g + 1)]
            vh = kv[:, N_KV * HEAD_DIM + HEAD_DIM * g:N_KV * HEAD_DIM + HEAD_DIM * (g + 1)]
            s = lax.dot_general(qh, kh, (((1,), (1,)), ((), ())), preferred_element_type=F32)
            s = s + bias_ref[h]
            if j == 0:
                s = s + no_prev
            sink = sink_ref[h]
            m = jnp.maximum(jnp.max(s, axis=-1, keepdims=True), sink)
            e = jnp.exp(s - m)
            den = jnp.sum(e, axis=-1, keepdims=True) + jnp.exp(sink - m)
            outs.append(_dot(e.astype(BF16), vh) / den)
        o_ref[0, BLOCK * j:BLOCK * (j + 1), :] = jnp.concatenate(outs, axis=1).astype(BF16)


def _attn(q3, kv3, bias, sinks, nq=2):
    b, s, _ = q3.shape
    qb = nq * BLOCK
    grid_spec = pltpu.PrefetchScalarGridSpec(
        num_scalar_prefetch=0,
        grid=(b, s // qb),
        in_specs=[pl.BlockSpec(memory_space=pltpu.SMEM),
                  pl.BlockSpec((1, qb, ATT_W), lambda i, n: (i, n, 0)),
                  pl.BlockSpec((1, BLOCK, KV_W), lambda i, n: (i, jnp.maximum(n * nq - 1, 0), 0)),
                  pl.BlockSpec((1, qb, KV_W), lambda i, n: (i, n, 0)),
                  pl.BlockSpec(bias.shape, lambda i, n: (0, 0, 0))],
        out_specs=pl.BlockSpec((1, qb, ATT_W), lambda i, n: (i, n, 0)),
    )
    return pl.pallas_call(
        functools.partial(_attn_kernel, nq=nq),
        grid_spec=grid_spec,
        out_shape=jax.ShapeDtypeStruct((b, s, ATT_W), BF16),
        compiler_params=_cparams(("parallel", "parallel")),
        name="attn",
    )(sinks, q3, kv3, kv3, bias)


def _route(logits):
    lane = lax.broadcasted_iota(jnp.int32, logits.shape, 1).astype(F32)
    big = float(LANES)
    gl = jnp.where(lane < N_GROUPS, logits, NEG)
    mg = jnp.max(gl, axis=-1, keepdims=True)
    gidx = jnp.min(jnp.where(gl == mg, lane, big), axis=-1, keepdims=True)
    gsum = jnp.sum(jnp.where(lane < N_GROUPS, jnp.exp(gl - mg), 0.0), axis=-1, keepdims=True)
    gp = 1.0 / gsum
    lo = N_GROUPS + EXPERTS_PER_GROUP * gidx
    el = jnp.where(lane >= lo, jnp.where(lane < lo + EXPERTS_PER_GROUP, logits, NEG), NEG)
    m1 = jnp.max(el, axis=-1, keepdims=True)
    i1 = jnp.min(jnp.where(el == m1, lane, big), axis=-1, keepdims=True)
    el2 = jnp.where(lane == i1, NEG, el)
    m2 = jnp.max(el2, axis=-1, keepdims=True)
    i2 = jnp.min(jnp.where(el2 == m2, lane, big), axis=-1, keepdims=True)
    t = jnp.exp(m2 - m1)
    w1 = gp / (1.0 + t)
    w2 = gp * t / (1.0 + t)
    return jnp.where(lane == 0, i1 - N_GROUPS,
                     jnp.where(lane == 1, i2 - N_GROUPS,
                               jnp.where(lane == 2, w1, jnp.where(lane == 3, w2, 0.0))))


def _row_copy(src_ref, dst_ref, sem, src_row, dst_row):
    return pltpu.make_async_copy(src_ref.at[pl.ds(src_row, 1)], dst_ref.at[pl.ds(dst_row, 1)], sem)


def _start_all(copies):
    for k, c in enumerate(copies):
        c.start(priority=k % 2)


def _merge_kernel(x_ref, ya_ref, yb_ref, gmix_ref, wg_ref, bg_ref, wa_ref, wb_ref, wo_ref, gffn_ref, wr_ref, br_ref,
                  x1_ref, route_ref, dest_ref, alloc_ref, state_ref, xd_hbm,
                  tri_ref, pre_ref, st_ref, hbuf_ref, destv_ref, dests_ref, stv_ref, sts_ref, zblk_ref,
                  hs_ref, gate_ref, sem_rows, sem_idx, sem_pad, *, tm, n_blk):
    i = pl.program_id(0)
    last = pl.num_programs(0) - 1
    slot = lax.rem(i, NBUF)
    prev = lax.rem(i + NBUF - 1, NBUF)
    n = TOP_K * tm
    dump0 = n_blk * MOE_BLK

    def row_copies(s, lo=0, hi=n):
        return [_row_copy(hbuf_ref.at[s], xd_hbm, sem_rows.at[s], r % tm, dests_ref[s, r // tm, r % tm])
                for r in range(lo, hi)]

    def wait_rows(s):
        for _ in range(n):
            _row_copy(hbuf_ref.at[s], xd_hbm, sem_rows.at[s], 0, dump0).wait()

    def index_copy(s):
        return pltpu.make_async_copy(destv_ref.at[s], dests_ref.at[s], sem_idx.at[s])

    @pl.when(i == 0)
    def _():
        r = lax.broadcasted_iota(jnp.int32, (tm, tm), 0)
        c = lax.broadcasted_iota(jnp.int32, (tm, tm), 1)
        tri_ref[...] = jnp.where(r > c, 1.0, 0.0).astype(BF16)
        r = lax.broadcasted_iota(jnp.int32, (LANES, LANES), 0)
        c = lax.broadcasted_iota(jnp.int32, (LANES, LANES), 1)
        pre_ref[...] = jnp.where(r < c, 1.0, 0.0).astype(BF16)
        r = lax.broadcasted_iota(jnp.int32, (8, LANES), 0)
        st_ref[...] = jnp.where(r == 0, float(MOE_BLK), 0.0)
        hbuf_ref[NBUF - 1] = jnp.zeros((tm, PACK_W), jnp.uint32)

        def spare(t, carry):
            dests_ref[NBUF - 1, 0, t] = dump0 + t
            dests_ref[NBUF - 1, 1, t] = dump0 + tm + t
            return carry

        lax.fori_loop(0, tm, spare, 0)

    @pl.when(i >= NBUF - 1)
    def _():
        wait_rows(slot)

    @pl.when(i >= 1)
    def _():
        index_copy(prev).wait()

    group = n // 4
    once = jnp.minimum(dests_ref[prev, 0, 0], 0) + 1

    def region(body):
        lax.fori_loop(0, once, lambda _, c: (body(), c)[1], 0)

    def gates_lo():
        _start_all(row_copies(prev, 0, group))
        h = _rms(x_ref[...], gmix_ref[...]).astype(BF16)
        hs_ref[...] = h
        gate_ref[0] = _sigmoid(_dot(h, wg_ref[:, 0:D_MODEL]) + bg_ref[:, 0:D_MODEL])

    def gates_hi():
        _start_all(row_copies(prev, group, 2 * group))
        gate_ref[1] = _sigmoid(_dot(hs_ref[...], wg_ref[:, D_MODEL:2 * D_MODEL]) + bg_ref[:, D_MODEL:2 * D_MODEL])

    def branches():
        _start_all(row_copies(prev, 2 * group, 3 * group))
        merged = gate_ref[0] * _dot(ya_ref[...], wa_ref[...]) + gate_ref[1] * _dot(yb_ref[...], wb_ref[...])
        hs_ref[...] = merged.astype(BF16)

    region(gates_lo)
    region(gates_hi)
    region(branches)
    _start_all(row_copies(prev, 3 * group, n))
    x1 = x_ref[...] + _dot(hs_ref[...], wo_ref[...])
    x1_ref[...] = x1
    h2 = _rms(x1, gffn_ref[...])
    hbuf_ref[slot] = _pack_rows(h2)
    route = _route(_dot(h2.astype(BF16), wr_ref[...]) + br_ref[...])
    route_ref[...] = route

    lane = lax.broadcasted_iota(jnp.int32, route.shape, 1).astype(F32)
    e1, e2 = route[:, 0:1], route[:, 1:2]
    is1, is2 = lane == e1, lane == e2
    member = jnp.where(is1, 1.0, jnp.where(is2, 1.0, 0.0))
    before = _dot(tri_ref[...], member.astype(BF16))
    cnt = jnp.sum(member, axis=0, keepdims=True)
    fill, blk, free = st_ref[0:1, :], st_ref[1:2, :], st_ref[2:3, :]
    need = fill + cnt
    new = jnp.floor((need + float(MOE_BLK - 1)) * (1.0 / MOE_BLK)) - 1.0
    base = free + _dot(jnp.broadcast_to(new, (8, LANES)).astype(BF16), pre_ref[...])[0:1, :]
    q = fill + before
    jb = jnp.floor(q * (1.0 / MOE_BLK))
    rowid = jnp.where(jb == 0.0, blk, base + jb - 1.0) * float(MOE_BLK) + (q - jb * float(MOE_BLK))
    d1 = jnp.sum(jnp.where(is1, rowid, 0.0), axis=-1, keepdims=True)
    d2 = jnp.sum(jnp.where(is2, rowid, 0.0), axis=-1, keepdims=True)
    slab = jnp.where(lane == 0, d1, jnp.where(lane == 1, d2, 0.0))
    d8 = jnp.transpose(slab, (1, 0))[0:8, :].astype(jnp.int32)
    destv_ref[slot] = d8
    dest_ref[0] = d8
    index_copy(slot).start()

    r8 = lax.broadcasted_iota(jnp.int32, (8, LANES), 0)
    alloc_ref[0] = jnp.where(r8 == 0, new, jnp.where(r8 == 1, base, 0.0))
    state = jnp.where(r8 == 0, need - new * float(MOE_BLK),
                      jnp.where(r8 == 1, jnp.where(new > 0.0, base + new - 1.0, blk),
                                jnp.where(r8 == 2, free + jnp.sum(new, axis=-1, keepdims=True), 0.0)))
    st_ref[...] = state
    state_ref[...] = state

    @pl.when(i == last)
    def _():
        index_copy(slot).wait()
        _start_all(row_copies(slot))
        for s in range(NBUF):
            wait_rows(s)
        stv_ref[...] = state.astype(jnp.int32)
        cp = pltpu.make_async_copy(stv_ref, sts_ref, sem_pad)
        cp.start()
        cp.wait()
        zblk_ref[...] = jnp.zeros_like(zblk_ref)

        def zero_rows(e, carry):
            first = sts_ref[1, e] * MOE_BLK
            lo = sts_ref[0, e]
            for g, nxt in ((1, 8), (8, 64), (64, MOE_BLK)):
                hi = jnp.minimum((lo + nxt - 1) // nxt * nxt, MOE_BLK)

                def run(j, g=g):
                    dst = xd_hbm.at[pl.ds(pl.multiple_of(first + j * g, g), g)]
                    return pltpu.make_async_copy(zblk_ref.at[pl.ds(0, g)], dst, sem_pad)

                def start(j, c, run=run):
                    run(j).start()
                    return c

                def wait(j, c, run=run):
                    run(j).wait()
                    return c

                lax.fori_loop(lo // g, hi // g, start, 0)
                lax.fori_loop(lo // g, hi // g, wait, 0)
                lo = hi
            return carry

        lax.fori_loop(0, N_EXPERTS, zero_rows, 0)

        def block_copy(b):
            return pltpu.make_async_copy(zblk_ref, xd_hbm.at[pl.ds(pl.multiple_of(b * MOE_BLK, MOE_BLK), MOE_BLK)],
                                         sem_pad)

        def zero_block(b, c):
            block_copy(b).start()
            block_copy(b).wait()
            return c

        lax.fori_loop(sts_ref[2, 0], n_blk, zero_block, 0)


def _merge(x2d, ya, yb, g_mix, wg, bg, wa, wb, wo, g_ffn, wr, br, n_blk, tm=512):
    t = x2d.shape[0]
    nt = t // tm
    n_rows = n_blk * MOE_BLK + TOP_K * tm
    row = lambda w: pl.BlockSpec((tm, w), lambda i: (i, 0))
    full = lambda a: pl.BlockSpec(a.shape, lambda i: (0,) * a.ndim)
    once = lambda a: pl.BlockSpec(a.shape, lambda i: (0,) * a.ndim, pipeline_mode=pl.Buffered(1))
    return pl.pallas_call(
        functools.partial(_merge_kernel, tm=tm, n_blk=n_blk),
        grid=(nt,),
        in_specs=[row(D_MODEL), row(D_SSM), row(ATT_W), full(g_mix), once(wg), full(bg),
                  once(wa), once(wb), once(wo), full(g_ffn), full(wr), full(br)],
        out_specs=[row(D_MODEL), row(LANES), pl.BlockSpec((1, 8, tm), lambda i: (i, 0, 0)),
                   pl.BlockSpec((1, 8, LANES), lambda i: (i, 0, 0)), pl.BlockSpec((8, LANES), lambda i: (0, 0)),
                   pl.BlockSpec(memory_space=pl.ANY)],
        out_shape=[jax.ShapeDtypeStruct((t, D_MODEL), F32),
                   jax.ShapeDtypeStruct((t, LANES), F32),
                   jax.ShapeDtypeStruct((nt, 8, tm), jnp.int32),
                   jax.ShapeDtypeStruct((nt, 8, LANES), F32),
                   jax.ShapeDtypeStruct((8, LANES), F32),
                   jax.ShapeDtypeStruct((n_rows, PACK_W), jnp.uint32)],
        scratch_shapes=[pltpu.VMEM((tm, tm), BF16), pltpu.VMEM((LANES, LANES), BF16), pltpu.VMEM((8, LANES), F32),
                        pltpu.VMEM((NBUF, tm, PACK_W), jnp.uint32), pltpu.VMEM((NBUF, 8, tm), jnp.int32),
                        pltpu.SMEM((NBUF, 8, tm), jnp.int32), pltpu.VMEM((8, LANES), jnp.int32),
                        pltpu.SMEM((8, LANES), jnp.int32), pltpu.VMEM((MOE_BLK, PACK_W), jnp.uint32),
                        pltpu.VMEM((tm, D_MODEL), BF16), pltpu.VMEM((2, tm, D_MODEL), F32),
                        pltpu.SemaphoreType.DMA((NBUF,)), pltpu.SemaphoreType.DMA((NBUF,)),
                        pltpu.SemaphoreType.DMA],
        compiler_params=_cparams(("arbitrary",)),
        name="merge",
    )(x2d, ya, yb, g_mix, wg, bg, wa, wb, wo, g_ffn, wr, br)


def _moe_kernel(be_ref, bn_ref, bi_ref, xd_ref, wg_ref, wu_ref, wd_ref, yd_ref, wgb_ref, wub_ref, wdb_ref):
    del bi_ref
    i = pl.program_id(0)

    @pl.when(bn_ref[i] == 0)
    def _():
        yd_ref[...] = jnp.zeros_like(yd_ref)

    @pl.when(bn_ref[i] > 0)
    def _():
        prev = be_ref[jnp.maximum(i - 1, 0)]

        @pl.when(jnp.logical_or(i == 0, be_ref[i] != prev))
        def _():
            wgb_ref[...] = wg_ref[0].astype(BF16)
            wub_ref[...] = wu_ref[0].astype(BF16)
            wdb_ref[...] = wd_ref[0].astype(BF16)

        x = _unpack_rows(xd_ref[...]).astype(BF16)
        g = _dot(x, wgb_ref[...])
        u = _dot(x, wub_ref[...])
        a = (g * _sigmoid(g) * u).astype(BF16)
        yd_ref[...] = _pack_rows(_dot(a, wdb_ref[...]))


def _moe(blk_e, blk_n, blk_i, xdw, w_eg, w_eu, w_ed):
    n_blk = blk_e.shape[0]
    grid_spec = pltpu.PrefetchScalarGridSpec(
        num_scalar_prefetch=3,
        grid=(n_blk,),
        in_specs=[pl.BlockSpec((MOE_BLK, PACK_W), lambda i, be, bn, bi: (bi[i], 0)),
                  pl.BlockSpec((1, D_MODEL, D_FF_EXPERT), lambda i, be, bn, bi: (be[i], 0, 0)),
                  pl.BlockSpec((1, D_MODEL, D_FF_EXPERT), lambda i, be, bn, bi: (be[i], 0, 0)),
                  pl.BlockSpec((1, D_FF_EXPERT, D_MODEL), lambda i, be, bn, bi: (be[i], 0, 0))],
        out_specs=pl.BlockSpec((MOE_BLK, PACK_W), lambda i, be, bn, bi: (bi[i], 0)),
        scratch_shapes=[pltpu.VMEM((D_MODEL, D_FF_EXPERT), BF16), pltpu.VMEM((D_MODEL, D_FF_EXPERT), BF16),
                        pltpu.VMEM((D_FF_EXPERT, D_MODEL), BF16)],
    )
    return pl.pallas_call(
        _moe_kernel,
        grid_spec=grid_spec,
        out_shape=jax.ShapeDtypeStruct((n_blk * MOE_BLK, PACK_W), jnp.uint32),
        compiler_params=_cparams(("arbitrary",)),
        name="moe",
    )(blk_e, blk_n, blk_i, xdw, w_eg, w_eu, w_ed)


def _final_kernel(dcur_ref, dnxt_ref, x1_ref, p_ref, route_ref, yd_hbm, wpp_ref, wpg_ref, bpg_ref, gple_ref,
                  gfin_ref, o_ref, yw0_ref, yw1_ref, sem, *, tm):
    j = pl.program_id(0)
    last = pl.num_programs(0) - 1
    n = TOP_K * tm
    yw = (yw0_ref, yw1_ref)

    def gather(dref, half, slot):
        return [_row_copy(yd_hbm, yw[slot], sem.at[slot], dref[0, r // tm, half * tm + r % tm], r)
                for r in range(n)]

    def tile(half, slot):
        rows = slice(half * tm, (half + 1) * tm)
        rt = route_ref[rows, :]
        y0 = _unpack_rows(yw[slot][0:tm, :])
        y1 = _unpack_rows(yw[slot][tm:n, :])
        x2 = x1_ref[rows, :] + rt[:, 2:3] * y0 + rt[:, 3:4] * y1
        h3 = _rms(x2, gple_ref[...]).astype(BF16)
        pp = _dot(p_ref[rows, :].astype(BF16), wpp_ref[...])
        x3 = x2 + pp * _sigmoid(_dot(h3, wpg_ref[...]) + bpg_ref[...])
        o_ref[rows, :] = _rms(x3, gfin_ref[...])

    @pl.when(j == 0)
    def _():
        _start_all(gather(dcur_ref, 0, 0))

    for c in gather(dcur_ref, 0, 0):
        c.wait()
    _start_all(gather(dcur_ref, 1, 1))
    tile(0, 0)
    for c in gather(dcur_ref, 1, 1):
        c.wait()
    _start_all(gather(dnxt_ref, 0, 0))
    tile(1, 1)

    @pl.when(j == last)
    def _():
        for c in gather(dnxt_ref, 0, 0):
            c.wait()


def _final(dest, x1, p2d, route, ydw, wpp, wpg, bpg, g_ple, g_final):
    t = x1.shape[0]
    ns, _, two_tm = dest.shape
    tm = two_tm // 2
    n = TOP_K * tm
    row = lambda w: pl.BlockSpec((2 * tm, w), lambda i: (i, 0))
    full = lambda a: pl.BlockSpec(a.shape, lambda i: (0,) * a.ndim)
    return pl.pallas_call(
        functools.partial(_final_kernel, tm=tm),
        grid=(ns,),
        in_specs=[pl.BlockSpec((1, 8, two_tm), lambda i: (i, 0, 0), memory_space=pltpu.SMEM),
                  pl.BlockSpec((1, 8, two_tm), lambda i: (jnp.minimum(i + 1, ns - 1), 0, 0),
                               memory_space=pltpu.SMEM),
                  row(D_MODEL), row(PLE_DIM), row(LANES), pl.BlockSpec(memory_space=pl.ANY),
                  full(wpp), full(wpg), full(bpg), full(g_ple), full(g_final)],
        out_specs=row(D_MODEL),
        out_shape=jax.ShapeDtypeStruct((t, D_MODEL), F32),
        scratch_shapes=[pltpu.VMEM((n, PACK_W), jnp.uint32), pltpu.VMEM((n, PACK_W), jnp.uint32),
                        pltpu.SemaphoreType.DMA((2,))],
        compiler_params=_cparams(("arbitrary",)),
        name="final",
    )(dest, dest, x1, p2d, route, ydw, wpp, wpg, bpg, g_ple, g_final)


def _block_plan(alloc, state, n_blk):
    new = alloc[:, 0, :N_EXPERTS].astype(jnp.int32).reshape(-1)
    base = alloc[:, 1, :N_EXPERTS].astype(jnp.int32).reshape(-1)
    expert = jnp.tile(jnp.arange(N_EXPERTS, dtype=jnp.int32), alloc.shape[0])
    fill = state[0, :N_EXPERTS].astype(jnp.int32)
    last_blk = state[1, :N_EXPERTS].astype(jnp.int32)
    taken = state[2, 0].astype(jnp.int32)
    b = jnp.arange(n_blk, dtype=jnp.int32)
    opened = (base[None, :] <= b[:, None]) & (b[:, None] < (base + new)[None, :])
    e_of = jnp.sum(jnp.where(opened, expert[None, :], 0), axis=1)
    onehot = e_of[:, None] == jnp.arange(N_EXPERTS, dtype=jnp.int32)[None, :]
    pick = lambda v: jnp.sum(jnp.where(onehot, v[None, :], 0), axis=1)
    rows = jnp.where(b < taken, jnp.where(b == pick(last_blk), pick(fill), MOE_BLK), 0)
    key = jnp.where(b < taken, e_of, N_EXPERTS) * n_blk + b
    pos = jnp.sum(key[None, :] < key[:, None], axis=1)
    at = pos[None, :] == b[:, None]
    order = lambda v: jnp.sum(jnp.where(at, v[None, :], 0), axis=1).astype(jnp.int32)
    return order(jnp.where(b < taken, e_of, N_EXPERTS - 1)), order(rows), order(b)


def _row(v):
    return v.reshape(1, -1).astype(F32)


def kernel(x, p, rel_bias, g_mix, w_in, w_gate, b_gate, ssm_a_re, ssm_a_im, ssm_log_dt, ssm_b_re, ssm_b_im,
           ssm_c_re, ssm_c_im, ssm_d, w_glu, b_glu, sinks, w_br_ssm, w_br_attn, w_out, g_ffn, w_router_group,
           b_router_group, w_router_expert, b_router_expert, w_e_gate, w_e_up, w_e_down, g_ple, w_ple_gate,
           b_ple_gate, w_ple_proj, g_final):
    bsz, seq, dm = x.shape
    assert g_mix.shape[0] == 1, "one layer followed by the final norm"
    i = 0
    t = bsz * seq
    x2d = x.reshape(t, dm)
    bias = _bias_table(rel_bias)
    u, q, kv = _proj(x2d, _row(g_mix[i]), w_in[i].astype(BF16))
    ab_re, ab_im, bb_re, bb_im = _ssmprep(ssm_a_re[i], ssm_a_im[i], ssm_log_dt[i], ssm_b_re[i], ssm_b_im[i])
    ws_in, ws_out, ar, ai = _ssm_weights(ab_re, ab_im, bb_re, bb_im, ssm_c_re[i], ssm_c_im[i])
    y_ssm = _ssm(u.reshape(bsz, seq, D_SSM), ws_in, ws_out, ar, ai, _row(ssm_d[i]),
                 w_glu[i].astype(BF16), _row(b_glu[i]))
    y_att = _attn(q.reshape(bsz, seq, ATT_W), kv.reshape(bsz, seq, KV_W), bias, sinks[i].astype(F32))
    w_router = jnp.zeros((dm, LANES), F32)
    w_router = w_router.at[:, :N_GROUPS].set(w_router_group[i])
    w_router = w_router.at[:, N_GROUPS:N_GROUPS + N_EXPERTS].set(w_router_expert[i])
    b_router = jnp.zeros((1, LANES), F32)
    b_router = b_router.at[0, :N_GROUPS].set(b_router_group[i])
    b_router = b_router.at[0, N_GROUPS:N_GROUPS + N_EXPERTS].set(b_router_expert[i])
    n_blk = t * TOP_K // MOE_BLK + N_EXPERTS
    x1, route, dest, alloc, state, xdw = _merge(
        x2d, y_ssm.reshape(t, D_SSM), y_att.reshape(t, ATT_W), _row(g_mix[i]), w_gate[i].astype(BF16),
        _row(b_gate[i]), w_br_ssm[i].astype(BF16), w_br_attn[i].astype(BF16), w_out[i].astype(BF16),
        _row(g_ffn[i]), w_router.astype(BF16), b_router, n_blk)
    blk_e, blk_n, blk_i = _block_plan(alloc, state, n_blk)
    ydw = _moe(blk_e, blk_n, blk_i, xdw, w_e_gate[i], w_e_up[i], w_e_down[i])
    out = _final(dest, x1, p[i].reshape(t, PLE_DIM), route, ydw, w_ple_proj[i].astype(BF16),
                 w_ple_gate[i].astype(BF16), _row(b_ple_gate[i]), _row(g_ple[i]), _row(g_final))
    return out.reshape(bsz, seq, dm)
```

```python
import functools
import math

import numpy as np
import jax
import jax.numpy as jnp
from jax import lax
from jax.experimental import pallas as pl
from jax.experimental.pallas import tpu as pltpu

D_MODEL = 1024
D_SSM = 512
SSM_CH = 16
SSM_GROUPS = D_SSM // SSM_CH
SSM_STATE = 64
N_STATE = SSM_GROUPS * SSM_STATE
N_HEADS = 8
N_KV = 2
HEAD_DIM = 64
GROUP = N_HEADS // N_KV
ATT_W = N_HEADS * HEAD_DIM
KV_W = 2 * N_KV * HEAD_DIM
WINDOW = 128
BLOCK = 128
NUM_BUCKETS = 32
MAX_DIST = 128
D_IN = D_SSM + ATT_W + KV_W
N_GROUPS = 4
EXPERTS_PER_GROUP = 8
N_EXPERTS = N_GROUPS * EXPERTS_PER_GROUP
TOP_K = 2
D_FF_EXPERT = 512
MOE_BLK = 512
PLE_DIM = 256
EPS = 1e-6

LANES = 128
NEG = -1e30
PACK_W = D_MODEL // 2

BF16 = jnp.bfloat16
F32 = jnp.float32
VMEM_LIMIT = 56 * 1024 * 1024
NBUF = 3


def _cparams(sem):
    return pltpu.CompilerParams(dimension_semantics=sem, vmem_limit_bytes=VMEM_LIMIT)


def _rms(x, g):
    ms = jnp.mean(x * x, axis=-1, keepdims=True)
    return x * lax.rsqrt(ms + EPS) * g


def _sigmoid(x):
    return 1.0 / (1.0 + jnp.exp(-x))


def _dot(a, b):
    return jnp.dot(a, b, preferred_element_type=F32)


def _pack_rows(y):
    lo = pltpu.bitcast(y[:, :PACK_W].astype(BF16).astype(F32), jnp.uint32)
    hi = pltpu.bitcast(y[:, PACK_W:].astype(BF16).astype(F32), jnp.uint32)
    return hi | lax.shift_right_logical(lo, jnp.uint32(16))


def _unpack_rows(w):
    lo = pltpu.bitcast(lax.shift_left(w, jnp.uint32(16)), F32)
    hi = pltpu.bitcast(w & jnp.uint32(0xFFFF0000), F32)
    return jnp.concatenate([lo, hi], axis=1)


def _proj_kernel(x_ref, g_ref, w_ref, u_ref, q_ref, kv_ref):
    h = _rms(x_ref[...], g_ref[...]).astype(BF16)
    proj = _dot(h, w_ref[...])
    u_ref[...] = proj[:, :D_SSM].astype(BF16)
    q_ref[...] = (proj[:, D_SSM:D_SSM + ATT_W] * (HEAD_DIM ** -0.5)).astype(BF16)
    kv_ref[...] = proj[:, D_SSM + ATT_W:].astype(BF16)


def _proj(x2d, g_mix, w_in_b, tm=1024):
    t = x2d.shape[0]
    row = lambda w: pl.BlockSpec((tm, w), lambda i: (i, 0))
    full = lambda a: pl.BlockSpec(a.shape, lambda i: (0,) * a.ndim)
    return pl.pallas_call(
        _proj_kernel,
        grid=(t // tm,),
        in_specs=[row(D_MODEL), full(g_mix), full(w_in_b)],
        out_specs=[row(D_SSM), row(ATT_W), row(KV_W)],
        out_shape=[jax.ShapeDtypeStruct((t, D_SSM), BF16),
                   jax.ShapeDtypeStruct((t, ATT_W), BF16),
                   jax.ShapeDtypeStruct((t, KV_W), BF16)],
        compiler_params=_cparams(("parallel",)),
        name="proj",
    )(x2d, g_mix, w_in_b)


def _ssmprep_kernel(are_ref, aim_ref, ldt_ref, bre_ref, bim_ref, abr_ref, abi_ref, bbr_ref, bbi_ref):
    a_re, a_im = are_ref[...], aim_ref[...]
    dt = jnp.exp(ldt_ref[...])
    mag = jnp.exp(a_re * dt)
    ab_re = mag * jnp.cos(a_im * dt)
    ab_im = mag * jnp.sin(a_im * dt)
    abr_ref[...] = ab_re
    abi_ref[...] = ab_im
    den = a_re * a_re + a_im * a_im
    c_re = ((ab_re - 1.0) * a_re + ab_im * a_im) / den
    c_im = (ab_im * a_re - (ab_re - 1.0) * a_im) / den
    for c in range(SSM_CH):
        b_re, b_im = bre_ref[c], bim_ref[c]
        bbr_ref[c] = c_re * b_re - c_im * b_im
        bbi_ref[c] = c_re * b_im + c_im * b_re


def _ssmprep(a_re, a_im, log_dt, b_re, b_im):
    g, p = a_re.shape
    bt_re = jnp.transpose(b_re, (2, 0, 1))
    bt_im = jnp.transpose(b_im, (2, 0, 1))
    gp = jax.ShapeDtypeStruct((g, p), F32)
    cgp = jax.ShapeDtypeStruct((SSM_CH, g, p), F32)
    return pl.pallas_call(
        _ssmprep_kernel, out_shape=[gp, gp, cgp, cgp], name="ssmprep",
    )(a_re, a_im, log_dt.reshape(g, 1), bt_re, bt_im)


def _ssm_weights(ab_re, ab_im, bb_re, bb_im, c_re, c_im):
    g, p = ab_re.shape
    kb = D_SSM // 256
    gk = g // kb
    eye = jnp.eye(gk, dtype=F32)

    def in_block(bb):
        b4 = jnp.transpose(bb, (1, 0, 2)).reshape(kb, gk, SSM_CH, p)
        return jnp.einsum("kgcp,gh->kgchp", b4, eye).reshape(kb, gk * SSM_CH, gk * p)

    w_in = jnp.concatenate([in_block(bb_re), in_block(bb_im)], axis=2).astype(BF16)
    nj = D_SSM // LANES
    gj = g // nj
    eyej = jnp.eye(gj, dtype=F32)

    def out_block(c):
        c4 = jnp.transpose(c, (0, 2, 1)).reshape(nj, gj, p, SSM_CH)
        return jnp.einsum("jgpc,gh->jgphc", c4, eyej).reshape(nj, gj * p, gj * SSM_CH)

    w_out = jnp.stack([out_block(c_re), out_block(-c_im)], axis=1).astype(BF16)
    return w_in, w_out, ab_re.reshape(1, g * p), ab_im.reshape(1, g * p)


def _gelu_tanh(x):
    return 0.5 * x * (1.0 + jnp.tanh(math.sqrt(2.0 / math.pi) * (x + 0.044715 * (x * x * x))))


def _ssm_kernel(u_ref, ul_ref, win_ref, wout_ref, ar_ref, ai_ref, d_ref, wglu_ref, bglu_ref, y_ref,
                perm_ref, permt_ref, bu0_ref, bu1_ref, xb0_ref, xb1_ref, st_ref, *, nb, lc):
    rows = nb * lc
    ns = N_STATE
    k = pl.program_id(0)
    bu = (bu0_ref, bu1_ref)
    xb = (xb0_ref, xb1_ref)

    @pl.when(k == 0)
    def _():
        r = lax.broadcasted_iota(jnp.int32, (rows, rows), 0)
        c = lax.broadcasted_iota(jnp.int32, (rows, rows), 1)
        sh = nb.bit_length() - 1
        perm_ref[...] = jnp.where(c == (r & (nb - 1)) * lc + (r >> sh), 1.0, 0.0).astype(BF16)
        permt_ref[...] = jnp.where(r == (c & (nb - 1)) * lc + (c >> sh), 1.0, 0.0).astype(BF16)
        bu1_ref[...] = jnp.zeros_like(bu1_ref)
        xb0_ref[...] = jnp.zeros_like(xb0_ref)
        xb1_ref[...] = jnp.zeros_like(xb1_ref)
        st_ref[...] = jnp.zeros_like(st_ref)

    kb = win_ref.shape[0]
    half = ns // kb
    nj = wout_ref.shape[0]
    kw = ns // nj
    ar, ai = ar_ref[...], ai_ref[...]
    xr, xi = st_ref[:, 0:ns], st_ref[:, ns:2 * ns]

    for h in range(2):
        part = slice(h * lc, (h + 1) * lc)
        u_new = _dot(perm_ref[...], u_ref[:, part, :].reshape(rows, D_SSM)).astype(BF16)
        for j in range(kb):
            res = _dot(u_new[:, 256 * j:256 * (j + 1)], win_ref[j])
            bu[h][:, half * j:half * (j + 1)] = res[:, :half]
            bu[h][:, ns + half * j:ns + half * (j + 1)] = res[:, half:]
        src, dst = bu[1 - h], xb[1 - h]
        for t in range(0, lc, 2):
            keep_r, keep_i = [], []
            for s in range(2):
                row = (t + s) * nb
                br = src[row:row + nb, 0:ns]
                bi = src[row:row + nb, ns:2 * ns]
                xr, xi = ar * xr - ai * xi + br, ar * xi + ai * xr + bi
                keep_r.append(xr)
                keep_i.append(xi)
            dst[t * nb:(t + 2) * nb, 0:ns] = jnp.concatenate(keep_r, axis=0).astype(BF16)
            dst[t * nb:(t + 2) * nb, ns:2 * ns] = jnp.concatenate(keep_i, axis=0).astype(BF16)
        ys = []
        for j in range(nj):
            ys.append(_dot(xb[h][:, kw * j:kw * (j + 1)], wout_ref[j, 0])
                      + _dot(xb[h][:, ns + kw * j:ns + kw * (j + 1)], wout_ref[j, 1]))
        y_tb = jnp.concatenate(ys, axis=1).astype(BF16)
        y = _dot(permt_ref[...], y_tb) + d_ref[...] * ul_ref[:, part, :].reshape(rows, D_SSM).astype(F32)
        y = _gelu_tanh(y)
        z = _dot(y.astype(BF16), wglu_ref[...]) + bglu_ref[...]
        y_ref[:, part, :] = (y * _sigmoid(z)).astype(BF16).reshape(nb, lc, D_SSM)

    st_ref[:, 0:ns] = xr
    st_ref[:, ns:2 * ns] = xi


def _ssm(u3, w_in, w_out, ar, ai, d_skip, w_glu_b, b_glu, lc=64):
    nb, s, _ = u3.shape
    rows = nb * lc
    nk = s // (2 * lc)
    assert nb & (nb - 1) == 0 and nb % 8 == 0 and s % (2 * lc) == 0 and lc % 16 == 0
    full = lambda a: pl.BlockSpec(a.shape, lambda i: (0,) * a.ndim)
    lead = pl.BlockSpec((nb, 2 * lc, D_SSM), lambda i: (0, jnp.minimum(i, nk - 1), 0))
    lag = pl.BlockSpec((nb, 2 * lc, D_SSM), lambda i: (0, jnp.maximum(i - 1, 0), 0))
    return pl.pallas_call(
        functools.partial(_ssm_kernel, nb=nb, lc=lc),
        grid=(nk + 1,),
        in_specs=[lead, lag, full(w_in), full(w_out), full(ar), full(ai), full(d_skip), full(w_glu_b), full(b_glu)],
        out_specs=lag,
        out_shape=jax.ShapeDtypeStruct((nb, s, D_SSM), BF16),
        scratch_shapes=[pltpu.VMEM((rows, rows), BF16), pltpu.VMEM((rows, rows), BF16),
                        pltpu.VMEM((rows, 2 * N_STATE), F32), pltpu.VMEM((rows, 2 * N_STATE), F32),
                        pltpu.VMEM((rows, 2 * N_STATE), BF16), pltpu.VMEM((rows, 2 * N_STATE), BF16),
                        pltpu.VMEM((nb, 2 * N_STATE), F32)],
        compiler_params=_cparams(("arbitrary",)),
        name="ssm",
    )(u3, u3, w_in, w_out, ar, ai, d_skip, w_glu_b, b_glu)


def _t5_bucket_np(rel):
    max_exact = NUM_BUCKETS // 2
    relf = np.maximum(rel, 1).astype(np.float32)
    large = max_exact + (np.log(relf / np.float32(max_exact)) / np.float32(math.log(MAX_DIST / max_exact))
                         * np.float32(NUM_BUCKETS - max_exact)).astype(np.int32)
    large = np.minimum(large, NUM_BUCKETS - 1)
    return np.where(rel < max_exact, rel, large)


def _bias_table(rel_bias):
    q_loc = np.arange(BLOCK)[:, None]
    c_loc = np.arange(2 * BLOCK)[None, :]
    rel = q_loc + BLOCK - c_loc
    valid = (rel >= 0) & (rel < WINDOW)
    bucket = _t5_bucket_np(np.maximum(rel, 0)).reshape(-1, 1)
    onehot = (jnp.asarray(bucket) == jnp.arange(NUM_BUCKETS)[None, :]).astype(F32)
    bias = jnp.dot(onehot, rel_bias.astype(F32), precision=lax.Precision.HIGHEST)
    bias = jnp.where(valid.reshape(-1, 1), bias, NEG)
    return jnp.transpose(bias, (1, 0)).reshape(N_HEADS, BLOCK, 2 * BLOCK)


def _attn_kernel(sink_ref, q_ref, kvp_ref, kvc_ref, bias_ref, o_ref, *, nq):
    n = pl.program_id(1)
    kv_all = jnp.concatenate([kvp_ref[0], kvc_ref[0]], axis=0)
    col = lax.broadcasted_iota(jnp.int32, (BLOCK, 2 * BLOCK), 1)
    no_prev = jnp.where(col < BLOCK, jnp.where(n == 0, NEG, 0.0), 0.0)
    for j in range(nq):
        q = q_ref[0, BLOCK * j:BLOCK * (j + 1), :]
        kv = kv_all[BLOCK * j:BLOCK * (j + 2), :]
        outs = []
        for h in range(N_HEADS):
            g = h // GROUP
            qh = q[:, HEAD_DIM * h:HEAD_DIM * (h + 1)]
            kh = kv[:, HEAD_DIM * g:HEAD_DIM * (g + 1)]
            vh = kv[:, N_KV * HEAD_DIM + HEAD_DIM * g:N_KV * HEAD_DIM + HEAD_DIM * (g + 1)]
            s = lax.dot_general(qh, kh, (((1,), (1,)), ((), ())), preferred_element_type=F32)
            s = s + bias_ref[h]
            if j == 0:
                s = s + no_prev
            sink = sink_ref[h]
            m = jnp.maximum(jnp.max(s, axis=-1, keepdims=True), sink)
            e = jnp.exp(s - m)
            den = jnp.sum(e, axis=-1, keepdims=True) + jnp.exp(sink - m)
            outs.append(_dot(e.astype(BF16), vh) / den)
        o_ref[0, BLOCK * j:BLOCK * (j + 1), :] = jnp.concatenate(outs, axis=1).astype(BF16)


def _attn(q3, kv3, bias, sinks, nq=2):
    b, s, _ = q3.shape
    qb = nq * BLOCK
    grid_spec = pltpu.PrefetchScalarGridSpec(
        num_scalar_prefetch=0,
        grid=(b, s // qb),
        in_specs=[pl.BlockSpec(memory_space=pltpu.SMEM),
                  pl.BlockSpec((1, qb, ATT_W), lambda i, n: (i, n, 0)),
                  pl.BlockSpec((1, BLOCK, KV_W), lambda i, n: (i, jnp.maximum(n * nq - 1, 0), 0)),
                  pl.BlockSpec((1, qb, KV_W), lambda i, n: (i, n, 0)),
                  pl.BlockSpec(bias.shape, lambda i, n: (0, 0, 0))],
        out_specs=pl.BlockSpec((1, qb, ATT_W), lambda i, n: (i, n, 0)),
    )
    return pl.pallas_call(
        functools.partial(_attn_kernel, nq=nq),
        grid_spec=grid_spec,
        out_shape=jax.ShapeDtypeStruct((b, s, ATT_W), BF16),
        compiler_params=_cparams(("parallel", "parallel")),
        name="attn",
    )(sinks, q3, kv3, kv3, bias)


EXPERT_ROW0 = 8
assert EXPERTS_PER_GROUP == 8 and N_GROUPS <= EXPERT_ROW0


def _route_t(lt):
    m = lt.shape[1]
    row = lax.broadcasted_iota(jnp.int32, (8, m), 0).astype(F32)
    big = 8.0
    gl = jnp.where(row < N_GROUPS, lt[0:8, :], NEG)
    mg = jnp.max(gl, axis=0, keepdims=True)
    gidx = jnp.min(jnp.where(gl == mg, row, big), axis=0, keepdims=True)
    gp = 1.0 / jnp.sum(jnp.where(row < N_GROUPS, jnp.exp(gl - mg), 0.0), axis=0, keepdims=True)
    el = lt[EXPERT_ROW0:EXPERT_ROW0 + 8, :]
    for g in range(1, N_GROUPS):
        el = jnp.where(gidx == float(g), lt[EXPERT_ROW0 + 8 * g:EXPERT_ROW0 + 8 * (g + 1), :], el)
    m1 = jnp.max(el, axis=0, keepdims=True)
    i1 = jnp.min(jnp.where(el == m1, row, big), axis=0, keepdims=True)
    el2 = jnp.where(row == i1, NEG, el)
    m2 = jnp.max(el2, axis=0, keepdims=True)
    i2 = jnp.min(jnp.where(el2 == m2, row, big), axis=0, keepdims=True)
    t = jnp.exp(m2 - m1)
    first = gidx * float(EXPERTS_PER_GROUP)
    return first + i1, first + i2, gp / (1.0 + t), gp * t / (1.0 + t)


def _row_copy(src_ref, dst_ref, sem, src_row, dst_row):
    return pltpu.make_async_copy(src_ref.at[pl.ds(src_row, 1)], dst_ref.at[pl.ds(dst_row, 1)], sem)


def _start_all(copies):
    for k, c in enumerate(copies):
        c.start(priority=k % 2)


def _merge_kernel(x_ref, ya_ref, yb_ref, gmix_ref, wg_ref, bg_ref, wa_ref, wb_ref, wo_ref, gffn_ref, wr_ref, br_ref,
                  x1_ref, route_ref, dest_ref, alloc_ref, state_ref, xd_hbm,
                  tri_ref, pre_ref, st_ref, hbuf_ref, destv_ref, dests_ref, stv_ref, sts_ref, zblk_ref,
                  hs_ref, gate_ref, sem_rows, sem_idx, sem_pad, *, tm, n_blk):
    i = pl.program_id(0)
    last = pl.num_programs(0) - 1
    slot = lax.rem(i, NBUF)
    prev = lax.rem(i + NBUF - 1, NBUF)
    n = TOP_K * tm
    dump0 = n_blk * MOE_BLK

    def row_copies(s, lo=0, hi=n):
        return [_row_copy(hbuf_ref.at[s], xd_hbm, sem_rows.at[s], r % tm, dests_ref[s, r // tm, r % tm])
                for r in range(lo, hi)]

    def wait_rows(s):
        for _ in range(n):
            _row_copy(hbuf_ref.at[s], xd_hbm, sem_rows.at[s], 0, dump0).wait()

    def index_copy(s):
        return pltpu.make_async_copy(destv_ref.at[s], dests_ref.at[s], sem_idx.at[s])

    @pl.when(i == 0)
    def _():
        r = lax.broadcasted_iota(jnp.int32, (tm, tm), 0)
        c = lax.broadcasted_iota(jnp.int32, (tm, tm), 1)
        tri_ref[...] = jnp.where(r < c, 1.0, 0.0).astype(BF16)
        r = lax.broadcasted_iota(jnp.int32, (LANES, LANES), 0)
        c = lax.broadcasted_iota(jnp.int32, (LANES, LANES), 1)
        pre_ref[...] = jnp.where(r > c, 1.0, 0.0).astype(BF16)
        c = lax.broadcasted_iota(jnp.int32, (N_EXPERTS, LANES), 1)
        st_ref[...] = jnp.where(c == 0, float(MOE_BLK), 0.0)
        hbuf_ref[NBUF - 1] = jnp.zeros((tm, PACK_W), jnp.uint32)

        def spare(t, carry):
            dests_ref[NBUF - 1, 0, t] = dump0 + t
            dests_ref[NBUF - 1, 1, t] = dump0 + tm + t
            return carry

        lax.fori_loop(0, tm, spare, 0)

    @pl.when(i >= NBUF - 1)
    def _():
        wait_rows(slot)

    @pl.when(i >= 1)
    def _():
        index_copy(prev).wait()

    group = n // 4
    once = jnp.minimum(dests_ref[prev, 0, 0], 0) + 1

    def region(body):
        lax.fori_loop(0, once, lambda _, c: (body(), c)[1], 0)

    def gates_lo():
        _start_all(row_copies(prev, 0, group))
        h = _rms(x_ref[...], gmix_ref[...]).astype(BF16)
        hs_ref[...] = h
        gate_ref[0] = _sigmoid(_dot(h, wg_ref[:, 0:D_MODEL]) + bg_ref[:, 0:D_MODEL])

    def gates_hi():
        _start_all(row_copies(prev, group, 2 * group))
        gate_ref[1] = _sigmoid(_dot(hs_ref[...], wg_ref[:, D_MODEL:2 * D_MODEL]) + bg_ref[:, D_MODEL:2 * D_MODEL])

    def branches():
        _start_all(row_copies(prev, 2 * group, 3 * group))
        merged = gate_ref[0] * _dot(ya_ref[...], wa_ref[...]) + gate_ref[1] * _dot(yb_ref[...], wb_ref[...])
        hs_ref[...] = merged.astype(BF16)

    region(gates_lo)
    region(gates_hi)
    region(branches)
    _start_all(row_copies(prev, 3 * group, n))
    x1 = x_ref[...] + _dot(hs_ref[...], wo_ref[...])
    x1_ref[...] = x1
    h2 = _rms(x1, gffn_ref[...])
    hbuf_ref[slot] = _pack_rows(h2)
    lt = lax.dot_general(wr_ref[...], h2.astype(BF16), (((1,), (1,)), ((), ())),
                         preferred_element_type=F32) + br_ref[...]
    e1, e2, w1, w2 = _route_t(lt)
    r8 = lax.broadcasted_iota(jnp.int32, (8, tm), 0)
    rt8 = jnp.where(r8 == 0, e1, jnp.where(r8 == 1, e2, jnp.where(r8 == 2, w1, jnp.where(r8 == 3, w2, 0.0))))
    route_ref[...] = jnp.transpose(jnp.concatenate([rt8, jnp.zeros((LANES - 8, tm), F32)], axis=0), (1, 0))

    erow = lax.broadcasted_iota(jnp.int32, (N_EXPERTS, tm), 0).astype(F32)
    is1, is2 = erow == e1, erow == e2
    member = jnp.where(is1, 1.0, jnp.where(is2, 1.0, 0.0))
    before = _dot(member.astype(BF16), tri_ref[...])
    cnt = jnp.sum(member, axis=1, keepdims=True)
    st = st_ref[...]
    fill, blk, free = st[:, 0:1], st[:, 1:2], st[:, 2:3]
    need = fill + cnt
    new = jnp.floor((need + float(MOE_BLK - 1)) * (1.0 / MOE_BLK)) - 1.0
    new_pad = jnp.concatenate([jnp.broadcast_to(new, (N_EXPERTS, LANES)),
                               jnp.zeros((LANES - N_EXPERTS, LANES), F32)], axis=0).astype(BF16)
    base = free + _dot(pre_ref[...], new_pad)[0:N_EXPERTS, 0:1]
    q = fill + before
    jb = jnp.floor(q * (1.0 / MOE_BLK))
    rowid = jnp.where(jb == 0.0, blk, base + jb - 1.0) * float(MOE_BLK) + (q - jb * float(MOE_BLK))
    d1 = jnp.sum(jnp.where(is1, rowid, 0.0), axis=0, keepdims=True)
    d2 = jnp.sum(jnp.where(is2, rowid, 0.0), axis=0, keepdims=True)
    d8 = jnp.where(r8 == 0, d1, jnp.where(r8 == 1, d2, 0.0)).astype(jnp.int32)
    destv_ref[slot] = d8
    dest_ref[0] = d8
    index_copy(slot).start()

    lane = lax.broadcasted_iota(jnp.int32, (N_EXPERTS, LANES), 1)
    alloc_ref[0] = jnp.where(lane == 0, new, jnp.where(lane == 1, base, 0.0))
    state = jnp.where(lane == 0, need - new * float(MOE_BLK),
                      jnp.where(lane == 1, jnp.where(new > 0.0, base + new - 1.0, blk),
                                jnp.where(lane == 2, free + jnp.sum(new, axis=0, keepdims=True), 0.0)))
    st_ref[...] = state
    state_ref[...] = state

    @pl.when(i == last)
    def _():
        index_copy(slot).wait()
        _start_all(row_copies(slot))
        for s in range(NBUF):
            wait_rows(s)
        stv_ref[...] = state.astype(jnp.int32)
        cp = pltpu.make_async_copy(stv_ref, sts_ref, sem_pad)
        cp.start()
        cp.wait()
        zblk_ref[...] = jnp.zeros_like(zblk_ref)

        def zero_rows(e, carry):
            first = sts_ref[e, 1] * MOE_BLK
            lo = sts_ref[e, 0]
            for g, nxt in ((1, 8), (8, 64), (64, MOE_BLK)):
                hi = jnp.minimum((lo + nxt - 1) // nxt * nxt, MOE_BLK)

                def run(j, g=g):
                    dst = xd_hbm.at[pl.ds(pl.multiple_of(first + j * g, g), g)]
                    return pltpu.make_async_copy(zblk_ref.at[pl.ds(0, g)], dst, sem_pad)

                def start(j, c, run=run):
                    run(j).start()
                    return c

                def wait(j, c, run=run):
                    run(j).wait()
                    return c

                lax.fori_loop(lo // g, hi // g, start, 0)
                lax.fori_loop(lo // g, hi // g, wait, 0)
                lo = hi
            return carry

        lax.fori_loop(0, N_EXPERTS, zero_rows, 0)

        def block_copy(b):
            return pltpu.make_async_copy(zblk_ref, xd_hbm.at[pl.ds(pl.multiple_of(b * MOE_BLK, MOE_BLK), MOE_BLK)],
                                         sem_pad)

        def zero_block(b, c):
            block_copy(b).start()
            block_copy(b).wait()
            return c

        lax.fori_loop(sts_ref[0, 2], n_blk, zero_block, 0)


def _merge(x2d, ya, yb, g_mix, wg, bg, wa, wb, wo, g_ffn, wr, br, n_blk, tm=512):
    t = x2d.shape[0]
    nt = t // tm
    n_rows = n_blk * MOE_BLK + TOP_K * tm
    row = lambda w: pl.BlockSpec((tm, w), lambda i: (i, 0))
    full = lambda a: pl.BlockSpec(a.shape, lambda i: (0,) * a.ndim)
    once = lambda a: pl.BlockSpec(a.shape, lambda i: (0,) * a.ndim, pipeline_mode=pl.Buffered(1))
    return pl.pallas_call(
        functools.partial(_merge_kernel, tm=tm, n_blk=n_blk),
        grid=(nt,),
        in_specs=[row(D_MODEL), row(D_SSM), row(ATT_W), full(g_mix), once(wg), full(bg),
                  once(wa), once(wb), once(wo), full(g_ffn), full(wr), full(br)],
        out_specs=[row(D_MODEL), row(LANES), pl.BlockSpec((1, 8, tm), lambda i: (i, 0, 0)),
                   pl.BlockSpec((1, N_EXPERTS, LANES), lambda i: (i, 0, 0)),
                   pl.BlockSpec((N_EXPERTS, LANES), lambda i: (0, 0)),
                   pl.BlockSpec(memory_space=pl.ANY)],
        out_shape=[jax.ShapeDtypeStruct((t, D_MODEL), F32),
                   jax.ShapeDtypeStruct((t, LANES), F32),
                   jax.ShapeDtypeStruct((nt, 8, tm), jnp.int32),
                   jax.ShapeDtypeStruct((nt, N_EXPERTS, LANES), F32),
                   jax.ShapeDtypeStruct((N_EXPERTS, LANES), F32),
                   jax.ShapeDtypeStruct((n_rows, PACK_W), jnp.uint32)],
        scratch_shapes=[pltpu.VMEM((tm, tm), BF16), pltpu.VMEM((LANES, LANES), BF16), pltpu.VMEM((N_EXPERTS, LANES), F32),
                        pltpu.VMEM((NBUF, tm, PACK_W), jnp.uint32), pltpu.VMEM((NBUF, 8, tm), jnp.int32),
                        pltpu.SMEM((NBUF, 8, tm), jnp.int32), pltpu.VMEM((N_EXPERTS, LANES), jnp.int32),
                        pltpu.SMEM((N_EXPERTS, LANES), jnp.int32), pltpu.VMEM((MOE_BLK, PACK_W), jnp.uint32),
                        pltpu.VMEM((tm, D_MODEL), BF16), pltpu.VMEM((2, tm, D_MODEL), F32),
                        pltpu.SemaphoreType.DMA((NBUF,)), pltpu.SemaphoreType.DMA((NBUF,)),
                        pltpu.SemaphoreType.DMA],
        compiler_params=_cparams(("arbitrary",)),
        name="merge",
    )(x2d, ya, yb, g_mix, wg, bg, wa, wb, wo, g_ffn, wr, br)


def _moe_kernel(be_ref, bn_ref, bi_ref, xd_ref, wg_ref, wu_ref, wd_ref, yd_ref, wgb_ref, wub_ref, wdb_ref):
    del bi_ref
    i = pl.program_id(0)

    @pl.when(bn_ref[i] == 0)
    def _():
        yd_ref[...] = jnp.zeros_like(yd_ref)

    @pl.when(bn_ref[i] > 0)
    def _():
        prev = be_ref[jnp.maximum(i - 1, 0)]

        @pl.when(jnp.logical_or(i == 0, be_ref[i] != prev))
        def _():
            wgb_ref[...] = wg_ref[0].astype(BF16)
            wub_ref[...] = wu_ref[0].astype(BF16)
            wdb_ref[...] = wd_ref[0].astype(BF16)

        x = _unpack_rows(xd_ref[...]).astype(BF16)
        g = _dot(x, wgb_ref[...])
        u = _dot(x, wub_ref[...])
        a = (g * _sigmoid(g) * u).astype(BF16)
        yd_ref[...] = _pack_rows(_dot(a, wdb_ref[...]))


def _moe(blk_e, blk_n, blk_i, xdw, w_eg, w_eu, w_ed):
    n_blk = blk_e.shape[0]
    grid_spec = pltpu.PrefetchScalarGridSpec(
        num_scalar_prefetch=3,
        grid=(n_blk,),
        in_specs=[pl.BlockSpec((MOE_BLK, PACK_W), lambda i, be, bn, bi: (bi[i], 0)),
                  pl.BlockSpec((1, D_MODEL, D_FF_EXPERT), lambda i, be, bn, bi: (be[i], 0, 0)),
                  pl.BlockSpec((1, D_MODEL, D_FF_EXPERT), lambda i, be, bn, bi: (be[i], 0, 0)),
                  pl.BlockSpec((1, D_FF_EXPERT, D_MODEL), lambda i, be, bn, bi: (be[i], 0, 0))],
        out_specs=pl.BlockSpec((MOE_BLK, PACK_W), lambda i, be, bn, bi: (bi[i], 0)),
        scratch_shapes=[pltpu.VMEM((D_MODEL, D_FF_EXPERT), BF16), pltpu.VMEM((D_MODEL, D_FF_EXPERT), BF16),
                        pltpu.VMEM((D_FF_EXPERT, D_MODEL), BF16)],
    )
    return pl.pallas_call(
        _moe_kernel,
        grid_spec=grid_spec,
        out_shape=jax.ShapeDtypeStruct((n_blk * MOE_BLK, PACK_W), jnp.uint32),
        compiler_params=_cparams(("arbitrary",)),
        name="moe",
    )(blk_e, blk_n, blk_i, xdw, w_eg, w_eu, w_ed)


def _final_kernel(dcur_ref, dnxt_ref, x1_ref, p_ref, route_ref, yd_hbm, wpp_ref, wpg_ref, bpg_ref, gple_ref,
                  gfin_ref, o_ref, yw0_ref, yw1_ref, sem, *, tm):
    j = pl.program_id(0)
    last = pl.num_programs(0) - 1
    n = TOP_K * tm
    yw = (yw0_ref, yw1_ref)

    def gather(dref, half, slot):
        return [_row_copy(yd_hbm, yw[slot], sem.at[slot], dref[0, r // tm, half * tm + r % tm], r)
                for r in range(n)]

    def tile(half, slot):
        rows = slice(half * tm, (half + 1) * tm)
        rt = route_ref[rows, :]
        y0 = _unpack_rows(yw[slot][0:tm, :])
        y1 = _unpack_rows(yw[slot][tm:n, :])
        x2 = x1_ref[rows, :] + rt[:, 2:3] * y0 + rt[:, 3:4] * y1
        h3 = _rms(x2, gple_ref[...]).astype(BF16)
        pp = _dot(p_ref[rows, :].astype(BF16), wpp_ref[...])
        x3 = x2 + pp * _sigmoid(_dot(h3, wpg_ref[...]) + bpg_ref[...])
        o_ref[rows, :] = _rms(x3, gfin_ref[...])

    @pl.when(j == 0)
    def _():
        _start_all(gather(dcur_ref, 0, 0))

    for c in gather(dcur_ref, 0, 0):
        c.wait()
    _start_all(gather(dcur_ref, 1, 1))
    tile(0, 0)
    for c in gather(dcur_ref, 1, 1):
        c.wait()
    _start_all(gather(dnxt_ref, 0, 0))
    tile(1, 1)

    @pl.when(j == last)
    def _():
        for c in gather(dnxt_ref, 0, 0):
            c.wait()


def _final(dest, x1, p2d, route, ydw, wpp, wpg, bpg, g_ple, g_final):
    t = x1.shape[0]
    ns, _, two_tm = dest.shape
    tm = two_tm // 2
    n = TOP_K * tm
    row = lambda w: pl.BlockSpec((2 * tm, w), lambda i: (i, 0))
    full = lambda a: pl.BlockSpec(a.shape, lambda i: (0,) * a.ndim)
    return pl.pallas_call(
        functools.partial(_final_kernel, tm=tm),
        grid=(ns,),
        in_specs=[pl.BlockSpec((1, 8, two_tm), lambda i: (i, 0, 0), memory_space=pltpu.SMEM),
                  pl.BlockSpec((1, 8, two_tm), lambda i: (jnp.minimum(i + 1, ns - 1), 0, 0),
                               memory_space=pltpu.SMEM),
                  row(D_MODEL), row(PLE_DIM), row(LANES), pl.BlockSpec(memory_space=pl.ANY),
                  full(wpp), full(wpg), full(bpg), full(g_ple), full(g_final)],
        out_specs=row(D_MODEL),
        out_shape=jax.ShapeDtypeStruct((t, D_MODEL), F32),
        scratch_shapes=[pltpu.VMEM((n, PACK_W), jnp.uint32), pltpu.VMEM((n, PACK_W), jnp.uint32),
                        pltpu.SemaphoreType.DMA((2,))],
        compiler_params=_cparams(("arbitrary",)),
        name="final",
    )(dest, dest, x1, p2d, route, ydw, wpp, wpg, bpg, g_ple, g_final)


def _block_plan(alloc, state, n_blk):
    new = alloc[:, :, 0].astype(jnp.int32).reshape(-1)
    base = alloc[:, :, 1].astype(jnp.int32).reshape(-1)
    expert = jnp.tile(jnp.arange(N_EXPERTS, dtype=jnp.int32), alloc.shape[0])
    fill = state[:, 0].astype(jnp.int32)
    last_blk = state[:, 1].astype(jnp.int32)
    taken = state[0, 2].astype(jnp.int32)
    b = jnp.arange(n_blk, dtype=jnp.int32)
    opened = (base[None, :] <= b[:, None]) & (b[:, None] < (base + new)[None, :])
    e_of = jnp.sum(jnp.where(opened, expert[None, :], 0), axis=1)
    onehot = e_of[:, None] == jnp.arange(N_EXPERTS, dtype=jnp.int32)[None, :]
    pick = lambda v: jnp.sum(jnp.where(onehot, v[None, :], 0), axis=1)
    rows = jnp.where(b < taken, jnp.where(b == pick(last_blk), pick(fill), MOE_BLK), 0)
    key = jnp.where(b < taken, e_of, N_EXPERTS) * n_blk + b
    pos = jnp.sum(key[None, :] < key[:, None], axis=1)
    at = pos[None, :] == b[:, None]
    order = lambda v: jnp.sum(jnp.where(at, v[None, :], 0), axis=1).astype(jnp.int32)
    return order(jnp.where(b < taken, e_of, N_EXPERTS - 1)), order(rows), order(b)


def _row(v):
    return v.reshape(1, -1).astype(F32)


def kernel(x, p, rel_bias, g_mix, w_in, w_gate, b_gate, ssm_a_re, ssm_a_im, ssm_log_dt, ssm_b_re, ssm_b_im,
           ssm_c_re, ssm_c_im, ssm_d, w_glu, b_glu, sinks, w_br_ssm, w_br_attn, w_out, g_ffn, w_router_group,
           b_router_group, w_router_expert, b_router_expert, w_e_gate, w_e_up, w_e_down, g_ple, w_ple_gate,
           b_ple_gate, w_ple_proj, g_final):
    bsz, seq, dm = x.shape
    assert g_mix.shape[0] == 1, "one layer followed by the final norm"
    i = 0
    t = bsz * seq
    x2d = x.reshape(t, dm)
    bias = _bias_table(rel_bias)
    u, q, kv = _proj(x2d, _row(g_mix[i]), w_in[i].astype(BF16))
    ab_re, ab_im, bb_re, bb_im = _ssmprep(ssm_a_re[i], ssm_a_im[i], ssm_log_dt[i], ssm_b_re[i], ssm_b_im[i])
    ws_in, ws_out, ar, ai = _ssm_weights(ab_re, ab_im, bb_re, bb_im, ssm_c_re[i], ssm_c_im[i])
    y_ssm = _ssm(u.reshape(bsz, seq, D_SSM), ws_in, ws_out, ar, ai, _row(ssm_d[i]),
                 w_glu[i].astype(BF16), _row(b_glu[i]))
    y_att = _attn(q.reshape(bsz, seq, ATT_W), kv.reshape(bsz, seq, KV_W), bias, sinks[i].astype(F32))
    w_router = jnp.zeros((LANES, dm), F32)
    w_router = w_router.at[:N_GROUPS].set(w_router_group[i].T)
    w_router = w_router.at[EXPERT_ROW0:EXPERT_ROW0 + N_EXPERTS].set(w_router_expert[i].T)
    b_router = jnp.zeros((LANES, 1), F32)
    b_router = b_router.at[:N_GROUPS, 0].set(b_router_group[i])
    b_router = b_router.at[EXPERT_ROW0:EXPERT_ROW0 + N_EXPERTS, 0].set(b_router_expert[i])
    n_blk = t * TOP_K // MOE_BLK + N_EXPERTS
    x1, route, dest, alloc, state, xdw = _merge(
        x2d, y_ssm.reshape(t, D_SSM), y_att.reshape(t, ATT_W), _row(g_mix[i]), w_gate[i].astype(BF16),
        _row(b_gate[i]), w_br_ssm[i].astype(BF16), w_br_attn[i].astype(BF16), w_out[i].astype(BF16),
        _row(g_ffn[i]), w_router.astype(BF16), b_router, n_blk)
    blk_e, blk_n, blk_i = _block_plan(alloc, state, n_blk)
    ydw = _moe(blk_e, blk_n, blk_i, xdw, w_e_gate[i], w_e_up[i], w_e_down[i])
    out = _final(dest, x1, p[i].reshape(t, PLE_DIM), route, ydw, w_ple_proj[i].astype(BF16),
                 w_ple_gate[i].astype(BF16), _row(b_ple_gate[i]), _row(g_ple[i]), _row(g_final))
    return out.reshape(bsz, seq, dm)
```

```python
import functools
import math

import numpy as np
import jax
import jax.numpy as jnp
from jax import lax
from jax.experimental import pallas as pl
from jax.experimental.pallas import tpu as pltpu

D_MODEL = 1024
D_SSM = 512
SSM_CH = 16
SSM_GROUPS = D_SSM // SSM_CH
SSM_STATE = 64
N_STATE = SSM_GROUPS * SSM_STATE
N_HEADS = 8
N_KV = 2
HEAD_DIM = 64
GROUP = N_HEADS // N_KV
ATT_W = N_HEADS * HEAD_DIM
KV_W = 2 * N_KV * HEAD_DIM
WINDOW = 128
BLOCK = 128
NUM_BUCKETS = 32
MAX_DIST = 128
D_IN = D_SSM + ATT_W + KV_W
N_GROUPS = 4
EXPERTS_PER_GROUP = 8
N_EXPERTS = N_GROUPS * EXPERTS_PER_GROUP
TOP_K = 2
D_FF_EXPERT = 512
MOE_BLK = 512
PLE_DIM = 256
EPS = 1e-6

LANES = 128
NEG = -1e30
PACK_W = D_MODEL // 2

BF16 = jnp.bfloat16
F32 = jnp.float32
VMEM_LIMIT = 56 * 1024 * 1024
NBUF = 3


def _cparams(sem):
    return pltpu.CompilerParams(dimension_semantics=sem, vmem_limit_bytes=VMEM_LIMIT)


def _rms(x, g):
    ms = jnp.mean(x * x, axis=-1, keepdims=True)
    return x * lax.rsqrt(ms + EPS) * g


def _sigmoid(x):
    return 1.0 / (1.0 + jnp.exp(-x))


def _dot(a, b):
    return jnp.dot(a, b, preferred_element_type=F32)


def _pack_rows(y):
    lo = pltpu.bitcast(y[:, :PACK_W].astype(BF16).astype(F32), jnp.uint32)
    hi = pltpu.bitcast(y[:, PACK_W:].astype(BF16).astype(F32), jnp.uint32)
    return hi | lax.shift_right_logical(lo, jnp.uint32(16))


def _unpack_rows(w):
    lo = pltpu.bitcast(lax.shift_left(w, jnp.uint32(16)), F32)
    hi = pltpu.bitcast(w & jnp.uint32(0xFFFF0000), F32)
    return jnp.concatenate([lo, hi], axis=1)


def _proj_kernel(x_ref, g_ref, w_ref, u_ref, q_ref, kv_ref):
    h = _rms(x_ref[...], g_ref[...]).astype(BF16)
    proj = _dot(h, w_ref[...])
    u_ref[...] = proj[:, :D_SSM].astype(BF16)
    q_ref[...] = (proj[:, D_SSM:D_SSM + ATT_W] * (HEAD_DIM ** -0.5)).astype(BF16)
    kv_ref[...] = proj[:, D_SSM + ATT_W:].astype(BF16)


def _proj(x2d, g_mix, w_in_b, tm=1024):
    t = x2d.shape[0]
    row = lambda w: pl.BlockSpec((tm, w), lambda i: (i, 0))
    full = lambda a: pl.BlockSpec(a.shape, lambda i: (0,) * a.ndim)
    return pl.pallas_call(
        _proj_kernel,
        grid=(t // tm,),
        in_specs=[row(D_MODEL), full(g_mix), full(w_in_b)],
        out_specs=[row(D_SSM), row(ATT_W), row(KV_W)],
        out_shape=[jax.ShapeDtypeStruct((t, D_SSM), BF16),
                   jax.ShapeDtypeStruct((t, ATT_W), BF16),
                   jax.ShapeDtypeStruct((t, KV_W), BF16)],
        compiler_params=_cparams(("parallel",)),
        name="proj",
    )(x2d, g_mix, w_in_b)


def _ssmprep_kernel(are_ref, aim_ref, ldt_ref, bre_ref, bim_ref, abr_ref, abi_ref, bbr_ref, bbi_ref):
    a_re, a_im = are_ref[...], aim_ref[...]
    dt = jnp.exp(ldt_ref[...])
    mag = jnp.exp(a_re * dt)
    ab_re = mag * jnp.cos(a_im * dt)
    ab_im = mag * jnp.sin(a_im * dt)
    abr_ref[...] = ab_re
    abi_ref[...] = ab_im
    den = a_re * a_re + a_im * a_im
    c_re = ((ab_re - 1.0) * a_re + ab_im * a_im) / den
    c_im = (ab_im * a_re - (ab_re - 1.0) * a_im) / den
    for c in range(SSM_CH):
        b_re, b_im = bre_ref[c], bim_ref[c]
        bbr_ref[c] = c_re * b_re - c_im * b_im
        bbi_ref[c] = c_re * b_im + c_im * b_re


def _ssmprep(a_re, a_im, log_dt, b_re, b_im):
    g, p = a_re.shape
    bt_re = jnp.transpose(b_re, (2, 0, 1))
    bt_im = jnp.transpose(b_im, (2, 0, 1))
    gp = jax.ShapeDtypeStruct((g, p), F32)
    cgp = jax.ShapeDtypeStruct((SSM_CH, g, p), F32)
    return pl.pallas_call(
        _ssmprep_kernel, out_shape=[gp, gp, cgp, cgp], name="ssmprep",
    )(a_re, a_im, log_dt.reshape(g, 1), bt_re, bt_im)


def _ssm_weights(ab_re, ab_im, bb_re, bb_im, c_re, c_im):
    g, p = ab_re.shape
    kb = D_SSM // 256
    gk = g // kb
    eye = jnp.eye(gk, dtype=F32)

    def in_block(bb):
        b4 = jnp.transpose(bb, (1, 0, 2)).reshape(kb, gk, SSM_CH, p)
        return jnp.einsum("kgcp,gh->kgchp", b4, eye).reshape(kb, gk * SSM_CH, gk * p)

    w_in = jnp.concatenate([in_block(bb_re), in_block(bb_im)], axis=2).astype(BF16)
    nj = D_SSM // LANES
    gj = g // nj
    eyej = jnp.eye(gj, dtype=F32)

    def out_block(c):
        c4 = jnp.transpose(c, (0, 2, 1)).reshape(nj, gj, p, SSM_CH)
        return jnp.einsum("jgpc,gh->jgphc", c4, eyej).reshape(nj, gj * p, gj * SSM_CH)

    w_out = jnp.stack([out_block(c_re), out_block(-c_im)], axis=1).astype(BF16)
    return w_in, w_out, ab_re.reshape(1, g * p), ab_im.reshape(1, g * p)


def _gelu_tanh(x):
    return 0.5 * x * (1.0 + jnp.tanh(math.sqrt(2.0 / math.pi) * (x + 0.044715 * (x * x * x))))


def _ssm_kernel(u_ref, ul_ref, win_ref, wout_ref, ar_ref, ai_ref, d_ref, wglu_ref, bglu_ref, y_ref,
                perm_ref, permt_ref, bu0_ref, bu1_ref, xb0_ref, xb1_ref, st_ref, *, nb, lc):
    rows = nb * lc
    ns = N_STATE
    k = pl.program_id(0)
    bu = (bu0_ref, bu1_ref)
    xb = (xb0_ref, xb1_ref)

    @pl.when(k == 0)
    def _():
        r = lax.broadcasted_iota(jnp.int32, (rows, rows), 0)
        c = lax.broadcasted_iota(jnp.int32, (rows, rows), 1)
        sh = nb.bit_length() - 1
        perm_ref[...] = jnp.where(c == (r & (nb - 1)) * lc + (r >> sh), 1.0, 0.0).astype(BF16)
        permt_ref[...] = jnp.where(r == (c & (nb - 1)) * lc + (c >> sh), 1.0, 0.0).astype(BF16)
        bu1_ref[...] = jnp.zeros_like(bu1_ref)
        xb0_ref[...] = jnp.zeros_like(xb0_ref)
        xb1_ref[...] = jnp.zeros_like(xb1_ref)
        st_ref[...] = jnp.zeros_like(st_ref)

    kb = win_ref.shape[0]
    half = ns // kb
    nj = wout_ref.shape[0]
    kw = ns // nj
    ar, ai = ar_ref[...], ai_ref[...]
    xr, xi = st_ref[:, 0:ns], st_ref[:, ns:2 * ns]

    for h in range(2):
        part = slice(h * lc, (h + 1) * lc)
        u_new = _dot(perm_ref[...], u_ref[:, part, :].reshape(rows, D_SSM)).astype(BF16)
        for j in range(kb):
            res = _dot(u_new[:, 256 * j:256 * (j + 1)], win_ref[j])
            bu[h][:, half * j:half * (j + 1)] = res[:, :half]
            bu[h][:, ns + half * j:ns + half * (j + 1)] = res[:, half:]
        src, dst = bu[1 - h], xb[1 - h]
        for t in range(0, lc, 2):
            keep_r, keep_i = [], []
            for s in range(2):
                row = (t + s) * nb
                br = src[row:row + nb, 0:ns]
                bi = src[row:row + nb, ns:2 * ns]
                xr, xi = ar * xr - ai * xi + br, ar * xi + ai * xr + bi
                keep_r.append(xr)
                keep_i.append(xi)
            dst[t * nb:(t + 2) * nb, 0:ns] = jnp.concatenate(keep_r, axis=0).astype(BF16)
            dst[t * nb:(t + 2) * nb, ns:2 * ns] = jnp.concatenate(keep_i, axis=0).astype(BF16)
        ys = []
        for j in range(nj):
            ys.append(_dot(xb[h][:, kw * j:kw * (j + 1)], wout_ref[j, 0])
                      + _dot(xb[h][:, ns + kw * j:ns + kw * (j + 1)], wout_ref[j, 1]))
        y_tb = jnp.concatenate(ys, axis=1).astype(BF16)
        y = _dot(permt_ref[...], y_tb) + d_ref[...] * ul_ref[:, part, :].reshape(rows, D_SSM).astype(F32)
        y = _gelu_tanh(y)
        z = _dot(y.astype(BF16), wglu_ref[...]) + bglu_ref[...]
        y_ref[:, part, :] = (y * _sigmoid(z)).astype(BF16).reshape(nb, lc, D_SSM)

    st_ref[:, 0:ns] = xr
    st_ref[:, ns:2 * ns] = xi


def _ssm(u3, w_in, w_out, ar, ai, d_skip, w_glu_b, b_glu, lc=64):
    nb, s, _ = u3.shape
    rows = nb * lc
    nk = s // (2 * lc)
    assert nb & (nb - 1) == 0 and nb % 8 == 0 and s % (2 * lc) == 0 and lc % 16 == 0
    full = lambda a: pl.BlockSpec(a.shape, lambda i: (0,) * a.ndim)
    lead = pl.BlockSpec((nb, 2 * lc, D_SSM), lambda i: (0, jnp.minimum(i, nk - 1), 0))
    lag = pl.BlockSpec((nb, 2 * lc, D_SSM), lambda i: (0, jnp.maximum(i - 1, 0), 0))
    return pl.pallas_call(
        functools.partial(_ssm_kernel, nb=nb, lc=lc),
        grid=(nk + 1,),
        in_specs=[lead, lag, full(w_in), full(w_out), full(ar), full(ai), full(d_skip), full(w_glu_b), full(b_glu)],
        out_specs=lag,
        out_shape=jax.ShapeDtypeStruct((nb, s, D_SSM), BF16),
        scratch_shapes=[pltpu.VMEM((rows, rows), BF16), pltpu.VMEM((rows, rows), BF16),
                        pltpu.VMEM((rows, 2 * N_STATE), F32), pltpu.VMEM((rows, 2 * N_STATE), F32),
                        pltpu.VMEM((rows, 2 * N_STATE), BF16), pltpu.VMEM((rows, 2 * N_STATE), BF16),
                        pltpu.VMEM((nb, 2 * N_STATE), F32)],
        compiler_params=_cparams(("arbitrary",)),
        name="ssm",
    )(u3, u3, w_in, w_out, ar, ai, d_skip, w_glu_b, b_glu)


def _t5_bucket_np(rel):
    max_exact = NUM_BUCKETS // 2
    relf = np.maximum(rel, 1).astype(np.float32)
    large = max_exact + (np.log(relf / np.float32(max_exact)) / np.float32(math.log(MAX_DIST / max_exact))
                         * np.float32(NUM_BUCKETS - max_exact)).astype(np.int32)
    large = np.minimum(large, NUM_BUCKETS - 1)
    return np.where(rel < max_exact, rel, large)


def _bias_table(rel_bias):
    q_loc = np.arange(BLOCK)[:, None]
    c_loc = np.arange(2 * BLOCK)[None, :]
    rel = q_loc + BLOCK - c_loc
    valid = (rel >= 0) & (rel < WINDOW)
    bucket = _t5_bucket_np(np.maximum(rel, 0)).reshape(-1, 1)
    onehot = (jnp.asarray(bucket) == jnp.arange(NUM_BUCKETS)[None, :]).astype(F32)
    bias = jnp.dot(onehot, rel_bias.astype(F32), precision=lax.Precision.HIGHEST)
    bias = jnp.where(valid.reshape(-1, 1), bias, NEG)
    return jnp.transpose(bias, (1, 0)).reshape(N_HEADS, BLOCK, 2 * BLOCK)


def _attn_kernel(sink_ref, q_ref, kvp_ref, kvc_ref, bias_ref, o_ref, *, nq):
    n = pl.program_id(1)
    kv_all = jnp.concatenate([kvp_ref[0], kvc_ref[0]], axis=0)
    col = lax.broadcasted_iota(jnp.int32, (BLOCK, 2 * BLOCK), 1)
    no_prev = jnp.where(col < BLOCK, jnp.where(n == 0, NEG, 0.0), 0.0)
    for j in range(nq):
        q = q_ref[0, BLOCK * j:BLOCK * (j + 1), :]
        kv = kv_all[BLOCK * j:BLOCK * (j + 2), :]
        outs = []
        for h in range(N_HEADS):
            g = h // GROUP
            qh = q[:, HEAD_DIM * h:HEAD_DIM * (h + 1)]
            kh = kv[:, HEAD_DIM * g:HEAD_DIM * (g + 1)]
            vh = kv[:, N_KV * HEAD_DIM + HEAD_DIM * g:N_KV * HEAD_DIM + HEAD_DIM * (g + 1)]
            s = lax.dot_general(qh, kh, (((1,), (1,)), ((), ())), preferred_element_type=F32)
            s = s + bias_ref[h]
            if j == 0:
                s = s + no_prev
            sink = sink_ref[h]
            m = jnp.maximum(jnp.max(s, axis=-1, keepdims=True), sink)
            e = jnp.exp(s - m)
            den = jnp.sum(e, axis=-1, keepdims=True) + jnp.exp(sink - m)
            outs.append(_dot(e.astype(BF16), vh) / den)
        o_ref[0, BLOCK * j:BLOCK * (j + 1), :] = jnp.concatenate(outs, axis=1).astype(BF16)


def _attn(q3, kv3, bias, sinks, nq=2):
    b, s, _ = q3.shape
    qb = nq * BLOCK
    grid_spec = pltpu.PrefetchScalarGridSpec(
        num_scalar_prefetch=0,
        grid=(b, s // qb),
        in_specs=[pl.BlockSpec(memory_space=pltpu.SMEM),
                  pl.BlockSpec((1, qb, ATT_W), lambda i, n: (i, n, 0)),
                  pl.BlockSpec((1, BLOCK, KV_W), lambda i, n: (i, jnp.maximum(n * nq - 1, 0), 0)),
                  pl.BlockSpec((1, qb, KV_W), lambda i, n: (i, n, 0)),
                  pl.BlockSpec(bias.shape, lambda i, n: (0, 0, 0))],
        out_specs=pl.BlockSpec((1, qb, ATT_W), lambda i, n: (i, n, 0)),
    )
    return pl.pallas_call(
        functools.partial(_attn_kernel, nq=nq),
        grid_spec=grid_spec,
        out_shape=jax.ShapeDtypeStruct((b, s, ATT_W), BF16),
        compiler_params=_cparams(("parallel", "parallel")),
        name="attn",
    )(sinks, q3, kv3, kv3, bias)


EXPERT_ROW0 = 8
assert EXPERTS_PER_GROUP == 8 and N_GROUPS <= EXPERT_ROW0


def _route_t(lt):
    m = lt.shape[1]
    row = lax.broadcasted_iota(jnp.int32, (8, m), 0).astype(F32)
    big = 8.0
    gl = jnp.where(row < N_GROUPS, lt[0:8, :], NEG)
    mg = jnp.max(gl, axis=0, keepdims=True)
    gidx = jnp.min(jnp.where(gl == mg, row, big), axis=0, keepdims=True)
    gp = 1.0 / jnp.sum(jnp.where(row < N_GROUPS, jnp.exp(gl - mg), 0.0), axis=0, keepdims=True)
    el = lt[EXPERT_ROW0:EXPERT_ROW0 + 8, :]
    for g in range(1, N_GROUPS):
        el = jnp.where(gidx == float(g), lt[EXPERT_ROW0 + 8 * g:EXPERT_ROW0 + 8 * (g + 1), :], el)
    m1 = jnp.max(el, axis=0, keepdims=True)
    i1 = jnp.min(jnp.where(el == m1, row, big), axis=0, keepdims=True)
    el2 = jnp.where(row == i1, NEG, el)
    m2 = jnp.max(el2, axis=0, keepdims=True)
    i2 = jnp.min(jnp.where(el2 == m2, row, big), axis=0, keepdims=True)
    t = jnp.exp(m2 - m1)
    first = gidx * float(EXPERTS_PER_GROUP)
    return first + i1, first + i2, gp / (1.0 + t), gp * t / (1.0 + t)


def _row_copy(src_ref, dst_ref, sem, src_row, dst_row):
    return pltpu.make_async_copy(src_ref.at[pl.ds(src_row, 1)], dst_ref.at[pl.ds(dst_row, 1)], sem)


def _start_all(copies):
    for k, c in enumerate(copies):
        c.start(priority=k % 2)


def _merge_kernel(x_ref, ya_ref, yb_ref, gmix_ref, wg_ref, bg_ref, wa_ref, wb_ref, wo_ref, gffn_ref, wr_ref, br_ref,
                  x1_ref, route_ref, dest_ref, alloc_ref, state_ref, xd_hbm,
                  tri_ref, pre_ref, st_ref, hbuf_ref, destv_ref, dests_ref, stv_ref, sts_ref, zblk_ref,
                  hs_ref, gate_ref, sem_rows, sem_idx, sem_pad, *, tm, n_blk):
    i = pl.program_id(0)
    last = pl.num_programs(0) - 1
    slot = lax.rem(i, NBUF)
    prev = lax.rem(i + NBUF - 1, NBUF)
    n = TOP_K * tm
    dump0 = n_blk * MOE_BLK

    def row_copies(s, lo=0, hi=n):
        return [_row_copy(hbuf_ref.at[s], xd_hbm, sem_rows.at[s], r % tm, dests_ref[s, r // tm, r % tm])
                for r in range(lo, hi)]

    def wait_rows(s):
        for _ in range(n):
            _row_copy(hbuf_ref.at[s], xd_hbm, sem_rows.at[s], 0, dump0).wait()

    def index_copy(s):
        return pltpu.make_async_copy(destv_ref.at[s], dests_ref.at[s], sem_idx.at[s])

    @pl.when(i == 0)
    def _():
        r = lax.broadcasted_iota(jnp.int32, (tm, tm), 0)
        c = lax.broadcasted_iota(jnp.int32, (tm, tm), 1)
        tri_ref[...] = jnp.where(r < c, 1.0, 0.0).astype(BF16)
        r = lax.broadcasted_iota(jnp.int32, (LANES, LANES), 0)
        c = lax.broadcasted_iota(jnp.int32, (LANES, LANES), 1)
        pre_ref[...] = jnp.where(r > c, 1.0, 0.0).astype(BF16)
        c = lax.broadcasted_iota(jnp.int32, (N_EXPERTS, LANES), 1)
        st_ref[...] = jnp.where(c == 0, float(MOE_BLK), 0.0)
        hbuf_ref[NBUF - 1] = jnp.zeros((tm, PACK_W), jnp.uint32)

        def spare(t, carry):
            dests_ref[NBUF - 1, 0, t] = dump0 + t
            dests_ref[NBUF - 1, 1, t] = dump0 + tm + t
            return carry

        lax.fori_loop(0, tm, spare, 0)

    @pl.when(i >= NBUF - 1)
    def _():
        wait_rows(slot)

    @pl.when(i >= 1)
    def _():
        index_copy(prev).wait()

    group = n // 6
    once = jnp.minimum(dests_ref[prev, 0, 0], 0) + 1

    def region(body):
        lax.fori_loop(0, once, lambda _, c: (body(), c)[1], 0)

    def gates_lo():
        _start_all(row_copies(prev, 0, group))
        h = _rms(x_ref[...], gmix_ref[...]).astype(BF16)
        hs_ref[...] = h
        gate_ref[0] = _sigmoid(_dot(h, wg_ref[:, 0:D_MODEL]) + bg_ref[:, 0:D_MODEL])

    def gates_hi():
        _start_all(row_copies(prev, group, 2 * group))
        gate_ref[1] = _sigmoid(_dot(hs_ref[...], wg_ref[:, D_MODEL:2 * D_MODEL]) + bg_ref[:, D_MODEL:2 * D_MODEL])

    def branches():
        _start_all(row_copies(prev, 2 * group, 3 * group))
        merged = gate_ref[0] * _dot(ya_ref[...], wa_ref[...]) + gate_ref[1] * _dot(yb_ref[...], wb_ref[...])
        hs_ref[...] = merged.astype(BF16)

    region(gates_lo)
    region(gates_hi)
    region(branches)
    _start_all(row_copies(prev, 3 * group, n))
    x1 = x_ref[...] + _dot(hs_ref[...], wo_ref[...])
    x1_ref[...] = x1
    h2 = _rms(x1, gffn_ref[...])
    hbuf_ref[slot] = _pack_rows(h2)
    lt = lax.dot_general(wr_ref[...], h2.astype(BF16), (((1,), (1,)), ((), ())),
                         preferred_element_type=F32) + br_ref[...]
    e1, e2, w1, w2 = _route_t(lt)
    r8 = lax.broadcasted_iota(jnp.int32, (8, tm), 0)
    rt8 = jnp.where(r8 == 0, e1, jnp.where(r8 == 1, e2, jnp.where(r8 == 2, w1, jnp.where(r8 == 3, w2, 0.0))))
    route_ref[...] = jnp.transpose(jnp.concatenate([rt8, jnp.zeros((LANES - 8, tm), F32)], axis=0), (1, 0))

    erow = lax.broadcasted_iota(jnp.int32, (N_EXPERTS, tm), 0).astype(F32)
    is1, is2 = erow == e1, erow == e2
    member = jnp.where(is1, 1.0, jnp.where(is2, 1.0, 0.0))
    before = _dot(member.astype(BF16), tri_ref[...])
    cnt = jnp.sum(member, axis=1, keepdims=True)
    st = st_ref[...]
    fill, blk, free = st[:, 0:1], st[:, 1:2], st[:, 2:3]
    need = fill + cnt
    new = jnp.floor((need + float(MOE_BLK - 1)) * (1.0 / MOE_BLK)) - 1.0
    new_pad = jnp.concatenate([jnp.broadcast_to(new, (N_EXPERTS, LANES)),
                               jnp.zeros((LANES - N_EXPERTS, LANES), F32)], axis=0).astype(BF16)
    base = free + _dot(pre_ref[...], new_pad)[0:N_EXPERTS, 0:1]
    q = fill + before
    jb = jnp.floor(q * (1.0 / MOE_BLK))
    rowid = jnp.where(jb == 0.0, blk, base + jb - 1.0) * float(MOE_BLK) + (q - jb * float(MOE_BLK))
    d1 = jnp.sum(jnp.where(is1, rowid, 0.0), axis=0, keepdims=True)
    d2 = jnp.sum(jnp.where(is2, rowid, 0.0), axis=0, keepdims=True)
    d8 = jnp.where(r8 == 0, d1, jnp.where(r8 == 1, d2, 0.0)).astype(jnp.int32)
    destv_ref[slot] = d8
    dest_ref[0] = d8
    index_copy(slot).start()

    lane = lax.broadcasted_iota(jnp.int32, (N_EXPERTS, LANES), 1)
    alloc_ref[0] = jnp.where(lane == 0, new, jnp.where(lane == 1, base, 0.0))
    state = jnp.where(lane == 0, need - new * float(MOE_BLK),
                      jnp.where(lane == 1, jnp.where(new > 0.0, base + new - 1.0, blk),
                                jnp.where(lane == 2, free + jnp.sum(new, axis=0, keepdims=True), 0.0)))
    st_ref[...] = state
    state_ref[...] = state

    @pl.when(i == last)
    def _():
        index_copy(slot).wait()
        _start_all(row_copies(slot))
        for s in range(NBUF):
            wait_rows(s)
        stv_ref[...] = state.astype(jnp.int32)
        cp = pltpu.make_async_copy(stv_ref, sts_ref, sem_pad)
        cp.start()
        cp.wait()
        zblk_ref[...] = jnp.zeros_like(zblk_ref)

        def zero_rows(e, carry):
            first = sts_ref[e, 1] * MOE_BLK
            lo = sts_ref[e, 0]
            for g, nxt in ((1, 8), (8, 64), (64, MOE_BLK)):
                hi = jnp.minimum((lo + nxt - 1) // nxt * nxt, MOE_BLK)

                def run(j, g=g):
                    dst = xd_hbm.at[pl.ds(pl.multiple_of(first + j * g, g), g)]
                    return pltpu.make_async_copy(zblk_ref.at[pl.ds(0, g)], dst, sem_pad)

                def start(j, c, run=run):
                    run(j).start()
                    return c

                def wait(j, c, run=run):
                    run(j).wait()
                    return c

                lax.fori_loop(lo // g, hi // g, start, 0)
                lax.fori_loop(lo // g, hi // g, wait, 0)
                lo = hi
            return carry

        lax.fori_loop(0, N_EXPERTS, zero_rows, 0)

        def block_copy(b):
            return pltpu.make_async_copy(zblk_ref, xd_hbm.at[pl.ds(pl.multiple_of(b * MOE_BLK, MOE_BLK), MOE_BLK)],
                                         sem_pad)

        def zero_block(b, c):
            block_copy(b).start()
            block_copy(b).wait()
            return c

        lax.fori_loop(sts_ref[0, 2], n_blk, zero_block, 0)


def _merge(x2d, ya, yb, g_mix, wg, bg, wa, wb, wo, g_ffn, wr, br, n_blk, tm=512):
    t = x2d.shape[0]
    nt = t // tm
    n_rows = n_blk * MOE_BLK + TOP_K * tm
    row = lambda w: pl.BlockSpec((tm, w), lambda i: (i, 0))
    full = lambda a: pl.BlockSpec(a.shape, lambda i: (0,) * a.ndim)
    once = lambda a: pl.BlockSpec(a.shape, lambda i: (0,) * a.ndim, pipeline_mode=pl.Buffered(1))
    return pl.pallas_call(
        functools.partial(_merge_kernel, tm=tm, n_blk=n_blk),
        grid=(nt,),
        in_specs=[row(D_MODEL), row(D_SSM), row(ATT_W), full(g_mix), once(wg), full(bg),
                  once(wa), once(wb), once(wo), full(g_ffn), full(wr), full(br)],
        out_specs=[row(D_MODEL), row(LANES), pl.BlockSpec((1, 8, tm), lambda i: (i, 0, 0)),
                   pl.BlockSpec((1, N_EXPERTS, LANES), lambda i: (i, 0, 0)),
                   pl.BlockSpec((N_EXPERTS, LANES), lambda i: (0, 0)),
                   pl.BlockSpec(memory_space=pl.ANY)],
        out_shape=[jax.ShapeDtypeStruct((t, D_MODEL), F32),
                   jax.ShapeDtypeStruct((t, LANES), F32),
                   jax.ShapeDtypeStruct((nt, 8, tm), jnp.int32),
                   jax.ShapeDtypeStruct((nt, N_EXPERTS, LANES), F32),
                   jax.ShapeDtypeStruct((N_EXPERTS, LANES), F32),
                   jax.ShapeDtypeStruct((n_rows, PACK_W), jnp.uint32)],
        scratch_shapes=[pltpu.VMEM((tm, tm), BF16), pltpu.VMEM((LANES, LANES), BF16), pltpu.VMEM((N_EXPERTS, LANES), F32),
                        pltpu.VMEM((NBUF, tm, PACK_W), jnp.uint32), pltpu.VMEM((NBUF, 8, tm), jnp.int32),
                        pltpu.SMEM((NBUF, 8, tm), jnp.int32), pltpu.VMEM((N_EXPERTS, LANES), jnp.int32),
                        pltpu.SMEM((N_EXPERTS, LANES), jnp.int32), pltpu.VMEM((MOE_BLK, PACK_W), jnp.uint32),
                        pltpu.VMEM((tm, D_MODEL), BF16), pltpu.VMEM((2, tm, D_MODEL), F32),
                        pltpu.SemaphoreType.DMA((NBUF,)), pltpu.SemaphoreType.DMA((NBUF,)),
                        pltpu.SemaphoreType.DMA],
        compiler_params=_cparams(("arbitrary",)),
        name="merge",
    )(x2d, ya, yb, g_mix, wg, bg, wa, wb, wo, g_ffn, wr, br)


def _moe_kernel(be_ref, bn_ref, bi_ref, xd_ref, wg_ref, wu_ref, wd_ref, yd_ref, wgb_ref, wub_ref, wdb_ref):
    del bi_ref
    i = pl.program_id(0)

    @pl.when(bn_ref[i] == 0)
    def _():
        yd_ref[...] = jnp.zeros_like(yd_ref)

    @pl.when(bn_ref[i] > 0)
    def _():
        prev = be_ref[jnp.maximum(i - 1, 0)]

        @pl.when(jnp.logical_or(i == 0, be_ref[i] != prev))
        def _():
            wgb_ref[...] = wg_ref[0].astype(BF16)
            wub_ref[...] = wu_ref[0].astype(BF16)
            wdb_ref[...] = wd_ref[0].astype(BF16)

        x = _unpack_rows(xd_ref[...]).astype(BF16)
        g = _dot(x, wgb_ref[...])
        u = _dot(x, wub_ref[...])
        a = (g * _sigmoid(g) * u).astype(BF16)
        yd_ref[...] = _pack_rows(_dot(a, wdb_ref[...]))


def _moe(blk_e, blk_n, blk_i, xdw, w_eg, w_eu, w_ed):
    n_blk = blk_e.shape[0]
    grid_spec = pltpu.PrefetchScalarGridSpec(
        num_scalar_prefetch=3,
        grid=(n_blk,),
        in_specs=[pl.BlockSpec((MOE_BLK, PACK_W), lambda i, be, bn, bi: (bi[i], 0)),
                  pl.BlockSpec((1, D_MODEL, D_FF_EXPERT), lambda i, be, bn, bi: (be[i], 0, 0)),
                  pl.BlockSpec((1, D_MODEL, D_FF_EXPERT), lambda i, be, bn, bi: (be[i], 0, 0)),
                  pl.BlockSpec((1, D_FF_EXPERT, D_MODEL), lambda i, be, bn, bi: (be[i], 0, 0))],
        out_specs=pl.BlockSpec((MOE_BLK, PACK_W), lambda i, be, bn, bi: (bi[i], 0)),
        scratch_shapes=[pltpu.VMEM((D_MODEL, D_FF_EXPERT), BF16), pltpu.VMEM((D_MODEL, D_FF_EXPERT), BF16),
                        pltpu.VMEM((D_FF_EXPERT, D_MODEL), BF16)],
    )
    return pl.pallas_call(
        _moe_kernel,
        grid_spec=grid_spec,
        out_shape=jax.ShapeDtypeStruct((n_blk * MOE_BLK, PACK_W), jnp.uint32),
        compiler_params=_cparams(("arbitrary",)),
        name="moe",
    )(blk_e, blk_n, blk_i, xdw, w_eg, w_eu, w_ed)


def _final_kernel(dcur_ref, dnxt_ref, x1_ref, p_ref, route_ref, yd_hbm, wpp_ref, wpg_ref, bpg_ref, gple_ref,
                  gfin_ref, o_ref, yw0_ref, yw1_ref, sem, *, tm):
    j = pl.program_id(0)
    last = pl.num_programs(0) - 1
    n = TOP_K * tm
    yw = (yw0_ref, yw1_ref)

    def gather(dref, half, slot):
        return [_row_copy(yd_hbm, yw[slot], sem.at[slot], dref[0, r // tm, half * tm + r % tm], r)
                for r in range(n)]

    def tile(half, slot):
        rows = slice(half * tm, (half + 1) * tm)
        rt = route_ref[rows, :]
        y0 = _unpack_rows(yw[slot][0:tm, :])
        y1 = _unpack_rows(yw[slot][tm:n, :])
        x2 = x1_ref[rows, :] + rt[:, 2:3] * y0 + rt[:, 3:4] * y1
        h3 = _rms(x2, gple_ref[...]).astype(BF16)
        pp = _dot(p_ref[rows, :].astype(BF16), wpp_ref[...])
        x3 = x2 + pp * _sigmoid(_dot(h3, wpg_ref[...]) + bpg_ref[...])
        o_ref[rows, :] = _rms(x3, gfin_ref[...])

    @pl.when(j == 0)
    def _():
        _start_all(gather(dcur_ref, 0, 0))

    for c in gather(dcur_ref, 0, 0):
        c.wait()
    _start_all(gather(dcur_ref, 1, 1))
    tile(0, 0)
    for c in gather(dcur_ref, 1, 1):
        c.wait()
    _start_all(gather(dnxt_ref, 0, 0))
    tile(1, 1)

    @pl.when(j == last)
    def _():
        for c in gather(dnxt_ref, 0, 0):
            c.wait()


def _final(dest, x1, p2d, route, ydw, wpp, wpg, bpg, g_ple, g_final):
    t = x1.shape[0]
    ns, _, two_tm = dest.shape
    tm = two_tm // 2
    n = TOP_K * tm
    row = lambda w: pl.BlockSpec((2 * tm, w), lambda i: (i, 0))
    full = lambda a: pl.BlockSpec(a.shape, lambda i: (0,) * a.ndim)
    return pl.pallas_call(
        functools.partial(_final_kernel, tm=tm),
        grid=(ns,),
        in_specs=[pl.BlockSpec((1, 8, two_tm), lambda i: (i, 0, 0), memory_space=pltpu.SMEM),
                  pl.BlockSpec((1, 8, two_tm), lambda i: (jnp.minimum(i + 1, ns - 1), 0, 0),
                               memory_space=pltpu.SMEM),
                  row(D_MODEL), row(PLE_DIM), row(LANES), pl.BlockSpec(memory_space=pl.ANY),
                  full(wpp), full(wpg), full(bpg), full(g_ple), full(g_final)],
        out_specs=row(D_MODEL),
        out_shape=jax.ShapeDtypeStruct((t, D_MODEL), F32),
        scratch_shapes=[pltpu.VMEM((n, PACK_W), jnp.uint32), pltpu.VMEM((n, PACK_W), jnp.uint32),
                        pltpu.SemaphoreType.DMA((2,))],
        compiler_params=_cparams(("arbitrary",)),
        name="final",
    )(dest, dest, x1, p2d, route, ydw, wpp, wpg, bpg, g_ple, g_final)


def _block_plan(alloc, state, n_blk):
    new = alloc[:, :, 0].astype(jnp.int32).reshape(-1)
    base = alloc[:, :, 1].astype(jnp.int32).reshape(-1)
    expert = jnp.tile(jnp.arange(N_EXPERTS, dtype=jnp.int32), alloc.shape[0])
    fill = state[:, 0].astype(jnp.int32)
    last_blk = state[:, 1].astype(jnp.int32)
    taken = state[0, 2].astype(jnp.int32)
    b = jnp.arange(n_blk, dtype=jnp.int32)
    opened = (base[None, :] <= b[:, None]) & (b[:, None] < (base + new)[None, :])
    e_of = jnp.sum(jnp.where(opened, expert[None, :], 0), axis=1)
    onehot = e_of[:, None] == jnp.arange(N_EXPERTS, dtype=jnp.int32)[None, :]
    pick = lambda v: jnp.sum(jnp.where(onehot, v[None, :], 0), axis=1)
    rows = jnp.where(b < taken, jnp.where(b == pick(last_blk), pick(fill), MOE_BLK), 0)
    key = jnp.where(b < taken, e_of, N_EXPERTS) * n_blk + b
    pos = jnp.sum(key[None, :] < key[:, None], axis=1)
    at = pos[None, :] == b[:, None]
    order = lambda v: jnp.sum(jnp.where(at, v[None, :], 0), axis=1).astype(jnp.int32)
    return order(jnp.where(b < taken, e_of, N_EXPERTS - 1)), order(rows), order(b)


def _row(v):
    return v.reshape(1, -1).astype(F32)


def kernel(x, p, rel_bias, g_mix, w_in, w_gate, b_gate, ssm_a_re, ssm_a_im, ssm_log_dt, ssm_b_re, ssm_b_im,
           ssm_c_re, ssm_c_im, ssm_d, w_glu, b_glu, sinks, w_br_ssm, w_br_attn, w_out, g_ffn, w_router_group,
           b_router_group, w_router_expert, b_router_expert, w_e_gate, w_e_up, w_e_down, g_ple, w_ple_gate,
           b_ple_gate, w_ple_proj, g_final):
    bsz, seq, dm = x.shape
    assert g_mix.shape[0] == 1, "one layer followed by the final norm"
    i = 0
    t = bsz * seq
    x2d = x.reshape(t, dm)
    bias = _bias_table(rel_bias)
    u, q, kv = _proj(x2d, _row(g_mix[i]), w_in[i].astype(BF16))
    ab_re, ab_im, bb_re, bb_im = _ssmprep(ssm_a_re[i], ssm_a_im[i], ssm_log_dt[i], ssm_b_re[i], ssm_b_im[i])
    ws_in, ws_out, ar, ai = _ssm_weights(ab_re, ab_im, bb_re, bb_im, ssm_c_re[i], ssm_c_im[i])
    y_ssm = _ssm(u.reshape(bsz, seq, D_SSM), ws_in, ws_out, ar, ai, _row(ssm_d[i]),
                 w_glu[i].astype(BF16), _row(b_glu[i]))
    y_att = _attn(q.reshape(bsz, seq, ATT_W), kv.reshape(bsz, seq, KV_W), bias, sinks[i].astype(F32))
    w_router = jnp.zeros((LANES, dm), F32)
    w_router = w_router.at[:N_GROUPS].set(w_router_group[i].T)
    w_router = w_router.at[EXPERT_ROW0:EXPERT_ROW0 + N_EXPERTS].set(w_router_expert[i].T)
    b_router = jnp.zeros((LANES, 1), F32)
    b_router = b_router.at[:N_GROUPS, 0].set(b_router_group[i])
    b_router = b_router.at[EXPERT_ROW0:EXPERT_ROW0 + N_EXPERTS, 0].set(b_router_expert[i])
    n_blk = t * TOP_K // MOE_BLK + N_EXPERTS
    x1, route, dest, alloc, state, xdw = _merge(
        x2d, y_ssm.reshape(t, D_SSM), y_att.reshape(t, ATT_W), _row(g_mix[i]), w_gate[i].astype(BF16),
        _row(b_gate[i]), w_br_ssm[i].astype(BF16), w_br_attn[i].astype(BF16), w_out[i].astype(BF16),
        _row(g_ffn[i]), w_router.astype(BF16), b_router, n_blk)
    blk_e, blk_n, blk_i = _block_plan(alloc, state, n_blk)
    ydw = _moe(blk_e, blk_n, blk_i, xdw, w_e_gate[i], w_e_up[i], w_e_down[i])
    out = _final(dest, x1, p[i].reshape(t, PLE_DIM), route, ydw, w_ple_proj[i].astype(BF16),
                 w_ple_gate[i].astype(BF16), _row(b_ple_gate[i]), _row(g_ple[i]), _row(g_final))
    return out.reshape(bsz, seq, dm)
```

```python
import functools
import math

import numpy as np
import jax
import jax.numpy as jnp
from jax import lax
from jax.experimental import pallas as pl
from jax.experimental.pallas import tpu as pltpu

D_MODEL = 1024
D_SSM = 512
SSM_CH = 16
SSM_GROUPS = D_SSM // SSM_CH
SSM_STATE = 64
N_STATE = SSM_GROUPS * SSM_STATE
N_HEADS = 8
N_KV = 2
HEAD_DIM = 64
GROUP = N_HEADS // N_KV
ATT_W = N_HEADS * HEAD_DIM
KV_W = 2 * N_KV * HEAD_DIM
WINDOW = 128
BLOCK = 128
NUM_BUCKETS = 32
MAX_DIST = 128
D_IN = D_SSM + ATT_W + KV_W
N_GROUPS = 4
EXPERTS_PER_GROUP = 8
N_EXPERTS = N_GROUPS * EXPERTS_PER_GROUP
TOP_K = 2
D_FF_EXPERT = 512
MOE_BLK = 512
PLE_DIM = 256
EPS = 1e-6

LANES = 128
NEG = -1e30
PACK_W = D_MODEL // 2

BF16 = jnp.bfloat16
F32 = jnp.float32
VMEM_LIMIT = 56 * 1024 * 1024
NBUF = 3


def _cparams(sem):
    return pltpu.CompilerParams(dimension_semantics=sem, vmem_limit_bytes=VMEM_LIMIT)


def _rms(x, g):
    ms = jnp.mean(x * x, axis=-1, keepdims=True)
    return x * lax.rsqrt(ms + EPS) * g


def _sigmoid(x):
    return 1.0 / (1.0 + jnp.exp(-x))


def _dot(a, b):
    return jnp.dot(a, b, preferred_element_type=F32)


def _pack_rows(y):
    lo = pltpu.bitcast(y[:, :PACK_W].astype(BF16).astype(F32), jnp.uint32)
    hi = pltpu.bitcast(y[:, PACK_W:].astype(BF16).astype(F32), jnp.uint32)
    return hi | lax.shift_right_logical(lo, jnp.uint32(16))


def _unpack_rows(w):
    lo = pltpu.bitcast(lax.shift_left(w, jnp.uint32(16)), F32)
    hi = pltpu.bitcast(w & jnp.uint32(0xFFFF0000), F32)
    return jnp.concatenate([lo, hi], axis=1)


def _proj_kernel(x_ref, g_ref, w_ref, u_ref, q_ref, kv_ref):
    h = _rms(x_ref[...], g_ref[...]).astype(BF16)
    proj = _dot(h, w_ref[...])
    u_ref[...] = proj[:, :D_SSM].astype(BF16)
    q_ref[...] = (proj[:, D_SSM:D_SSM + ATT_W] * (HEAD_DIM ** -0.5)).astype(BF16)
    kv_ref[...] = proj[:, D_SSM + ATT_W:].astype(BF16)


def _proj(x2d, g_mix, w_in_b, tm=1024):
    t = x2d.shape[0]
    row = lambda w: pl.BlockSpec((tm, w), lambda i: (i, 0))
    full = lambda a: pl.BlockSpec(a.shape, lambda i: (0,) * a.ndim)
    return pl.pallas_call(
        _proj_kernel,
        grid=(t // tm,),
        in_specs=[row(D_MODEL), full(g_mix), full(w_in_b)],
        out_specs=[row(D_SSM), row(ATT_W), row(KV_W)],
        out_shape=[jax.ShapeDtypeStruct((t, D_SSM), BF16),
                   jax.ShapeDtypeStruct((t, ATT_W), BF16),
                   jax.ShapeDtypeStruct((t, KV_W), BF16)],
        compiler_params=_cparams(("parallel",)),
        name="proj",
    )(x2d, g_mix, w_in_b)


def _ssmprep_kernel(are_ref, aim_ref, ldt_ref, bre_ref, bim_ref, abr_ref, abi_ref, bbr_ref, bbi_ref):
    a_re, a_im = are_ref[...], aim_ref[...]
    dt = jnp.exp(ldt_ref[...])
    mag = jnp.exp(a_re * dt)
    ab_re = mag * jnp.cos(a_im * dt)
    ab_im = mag * jnp.sin(a_im * dt)
    abr_ref[...] = ab_re
    abi_ref[...] = ab_im
    den = a_re * a_re + a_im * a_im
    c_re = ((ab_re - 1.0) * a_re + ab_im * a_im) / den
    c_im = (ab_im * a_re - (ab_re - 1.0) * a_im) / den
    for c in range(SSM_CH):
        b_re, b_im = bre_ref[c], bim_ref[c]
        bbr_ref[c] = c_re * b_re - c_im * b_im
        bbi_ref[c] = c_re * b_im + c_im * b_re


def _ssmprep(a_re, a_im, log_dt, b_re, b_im):
    g, p = a_re.shape
    bt_re = jnp.transpose(b_re, (2, 0, 1))
    bt_im = jnp.transpose(b_im, (2, 0, 1))
    gp = jax.ShapeDtypeStruct((g, p), F32)
    cgp = jax.ShapeDtypeStruct((SSM_CH, g, p), F32)
    return pl.pallas_call(
        _ssmprep_kernel, out_shape=[gp, gp, cgp, cgp], name="ssmprep",
    )(a_re, a_im, log_dt.reshape(g, 1), bt_re, bt_im)


def _ssm_weights(ab_re, ab_im, bb_re, bb_im, c_re, c_im):
    g, p = ab_re.shape
    kb = D_SSM // 256
    gk = g // kb
    eye = jnp.eye(gk, dtype=F32)

    def in_block(bb):
        b4 = jnp.transpose(bb, (1, 0, 2)).reshape(kb, gk, SSM_CH, p)
        return jnp.einsum("kgcp,gh->kgchp", b4, eye).reshape(kb, gk * SSM_CH, gk * p)

    w_in = jnp.concatenate([in_block(bb_re), in_block(bb_im)], axis=2).astype(BF16)
    nj = D_SSM // LANES
    gj = g // nj
    eyej = jnp.eye(gj, dtype=F32)

    def out_block(c):
        c4 = jnp.transpose(c, (0, 2, 1)).reshape(nj, gj, p, SSM_CH)
        return jnp.einsum("jgpc,gh->jgphc", c4, eyej).reshape(nj, gj * p, gj * SSM_CH)

    w_out = jnp.stack([out_block(c_re), out_block(-c_im)], axis=1).astype(BF16)
    return w_in, w_out, ab_re.reshape(1, g * p), ab_im.reshape(1, g * p)


def _gelu_tanh(x):
    return 0.5 * x * (1.0 + jnp.tanh(math.sqrt(2.0 / math.pi) * (x + 0.044715 * (x * x * x))))


def _ssm_kernel(u_ref, ul_ref, win_ref, wout_ref, ar_ref, ai_ref, d_ref, wglu_ref, bglu_ref, y_ref,
                bu0_ref, bu1_ref, xb0_ref, xb1_ref, st_ref, *, nb, lc):
    rows = nb * lc
    ns = N_STATE
    k = pl.program_id(0)
    bu = (bu0_ref, bu1_ref)
    xb = (xb0_ref, xb1_ref)

    @pl.when(k == 0)
    def _():
        bu1_ref[...] = jnp.zeros_like(bu1_ref)
        xb0_ref[...] = jnp.zeros_like(xb0_ref)
        xb1_ref[...] = jnp.zeros_like(xb1_ref)
        st_ref[...] = jnp.zeros_like(st_ref)

    kb = win_ref.shape[0]
    half = ns // kb
    nj = wout_ref.shape[0]
    kw = ns // nj
    ar, ai = ar_ref[...], ai_ref[...]
    xr, xi = st_ref[:, 0:ns], st_ref[:, ns:2 * ns]

    for h in range(2):
        part = slice(h * lc, (h + 1) * lc)
        u_new = pltpu.einshape("btd->(tb)d", u_ref[:, part, :].astype(F32)).astype(BF16)
        for j in range(kb):
            res = _dot(u_new[:, 256 * j:256 * (j + 1)], win_ref[j])
            bu[h][:, half * j:half * (j + 1)] = res[:, :half]
            bu[h][:, ns + half * j:ns + half * (j + 1)] = res[:, half:]
        src, dst = bu[1 - h], xb[1 - h]
        for t in range(0, lc, 2):
            keep_r, keep_i = [], []
            for s in range(2):
                row = (t + s) * nb
                br = src[row:row + nb, 0:ns]
                bi = src[row:row + nb, ns:2 * ns]
                xr, xi = ar * xr - ai * xi + br, ar * xi + ai * xr + bi
                keep_r.append(xr)
                keep_i.append(xi)
            dst[t * nb:(t + 2) * nb, 0:ns] = jnp.concatenate(keep_r, axis=0).astype(BF16)
            dst[t * nb:(t + 2) * nb, ns:2 * ns] = jnp.concatenate(keep_i, axis=0).astype(BF16)
        ys = []
        for j in range(nj):
            ys.append(_dot(xb[h][:, kw * j:kw * (j + 1)], wout_ref[j, 0])
                      + _dot(xb[h][:, ns + kw * j:ns + kw * (j + 1)], wout_ref[j, 1]))
        y_bt = pltpu.einshape("(tb)d->btd", jnp.concatenate(ys, axis=1), b=nb).reshape(rows, D_SSM)
        y = y_bt + d_ref[...] * ul_ref[:, part, :].reshape(rows, D_SSM).astype(F32)
        y = _gelu_tanh(y)
        z = _dot(y.astype(BF16), wglu_ref[...]) + bglu_ref[...]
        y_ref[:, part, :] = (y * _sigmoid(z)).astype(BF16).reshape(nb, lc, D_SSM)

    st_ref[:, 0:ns] = xr
    st_ref[:, ns:2 * ns] = xi


def _ssm(u3, w_in, w_out, ar, ai, d_skip, w_glu_b, b_glu, lc=64):
    nb, s, _ = u3.shape
    rows = nb * lc
    nk = s // (2 * lc)
    assert nb & (nb - 1) == 0 and nb % 8 == 0 and s % (2 * lc) == 0 and lc % 16 == 0
    full = lambda a: pl.BlockSpec(a.shape, lambda i: (0,) * a.ndim)
    lead = pl.BlockSpec((nb, 2 * lc, D_SSM), lambda i: (0, jnp.minimum(i, nk - 1), 0))
    lag = pl.BlockSpec((nb, 2 * lc, D_SSM), lambda i: (0, jnp.maximum(i - 1, 0), 0))
    return pl.pallas_call(
        functools.partial(_ssm_kernel, nb=nb, lc=lc),
        grid=(nk + 1,),
        in_specs=[lead, lag, full(w_in), full(w_out), full(ar), full(ai), full(d_skip), full(w_glu_b), full(b_glu)],
        out_specs=lag,
        out_shape=jax.ShapeDtypeStruct((nb, s, D_SSM), BF16),
        scratch_shapes=[pltpu.VMEM((rows, 2 * N_STATE), F32), pltpu.VMEM((rows, 2 * N_STATE), F32),
                        pltpu.VMEM((rows, 2 * N_STATE), BF16), pltpu.VMEM((rows, 2 * N_STATE), BF16),
                        pltpu.VMEM((nb, 2 * N_STATE), F32)],
        compiler_params=_cparams(("arbitrary",)),
        name="ssm",
    )(u3, u3, w_in, w_out, ar, ai, d_skip, w_glu_b, b_glu)


def _t5_bucket_np(rel):
    max_exact = NUM_BUCKETS // 2
    relf = np.maximum(rel, 1).astype(np.float32)
    large = max_exact + (np.log(relf / np.float32(max_exact)) / np.float32(math.log(MAX_DIST / max_exact))
                         * np.float32(NUM_BUCKETS - max_exact)).astype(np.int32)
    large = np.minimum(large, NUM_BUCKETS - 1)
    return np.where(rel < max_exact, rel, large)


def _bias_table(rel_bias):
    q_loc = np.arange(BLOCK)[:, None]
    c_loc = np.arange(2 * BLOCK)[None, :]
    rel = q_loc + BLOCK - c_loc
    valid = (rel >= 0) & (rel < WINDOW)
    bucket = _t5_bucket_np(np.maximum(rel, 0)).reshape(-1, 1)
    onehot = (jnp.asarray(bucket) == jnp.arange(NUM_BUCKETS)[None, :]).astype(F32)
    bias = jnp.dot(onehot, rel_bias.astype(F32), precision=lax.Precision.HIGHEST)
    bias = jnp.where(valid.reshape(-1, 1), bias, NEG)
    return jnp.transpose(bias, (1, 0)).reshape(N_HEADS, BLOCK, 2 * BLOCK)


def _attn_kernel(sink_ref, q_ref, kvp_ref, kvc_ref, bias_ref, o_ref, *, nq):
    n = pl.program_id(1)
    kv_all = jnp.concatenate([kvp_ref[0], kvc_ref[0]], axis=0)
    col = lax.broadcasted_iota(jnp.int32, (BLOCK, 2 * BLOCK), 1)
    no_prev = jnp.where(col < BLOCK, jnp.where(n == 0, NEG, 0.0), 0.0)
    for j in range(nq):
        q = q_ref[0, BLOCK * j:BLOCK * (j + 1), :]
        kv = kv_all[BLOCK * j:BLOCK * (j + 2), :]
        outs = []
        for h in range(N_HEADS):
            g = h // GROUP
            qh = q[:, HEAD_DIM * h:HEAD_DIM * (h + 1)]
            kh = kv[:, HEAD_DIM * g:HEAD_DIM * (g + 1)]
            vh = kv[:, N_KV * HEAD_DIM + HEAD_DIM * g:N_KV * HEAD_DIM + HEAD_DIM * (g + 1)]
            s = lax.dot_general(qh, kh, (((1,), (1,)), ((), ())), preferred_element_type=F32)
            s = s + bias_ref[h]
            if j == 0:
                s = s + no_prev
            sink = sink_ref[h]
            m = jnp.maximum(jnp.max(s, axis=-1, keepdims=True), sink)
            e = jnp.exp(s - m)
            den = jnp.sum(e, axis=-1, keepdims=True) + jnp.exp(sink - m)
            outs.append(_dot(e.astype(BF16), vh) / den)
        o_ref[0, BLOCK * j:BLOCK * (j + 1), :] = jnp.concatenate(outs, axis=1).astype(BF16)


def _attn(q3, kv3, bias, sinks, nq=2):
    b, s, _ = q3.shape
    qb = nq * BLOCK
    grid_spec = pltpu.PrefetchScalarGridSpec(
        num_scalar_prefetch=0,
        grid=(b, s // qb),
        in_specs=[pl.BlockSpec(memory_space=pltpu.SMEM),
                  pl.BlockSpec((1, qb, ATT_W), lambda i, n: (i, n, 0)),
                  pl.BlockSpec((1, BLOCK, KV_W), lambda i, n: (i, jnp.maximum(n * nq - 1, 0), 0)),
                  pl.BlockSpec((1, qb, KV_W), lambda i, n: (i, n, 0)),
                  pl.BlockSpec(bias.shape, lambda i, n: (0, 0, 0))],
        out_specs=pl.BlockSpec((1, qb, ATT_W), lambda i, n: (i, n, 0)),
    )
    return pl.pallas_call(
        functools.partial(_attn_kernel, nq=nq),
        grid_spec=grid_spec,
        out_shape=jax.ShapeDtypeStruct((b, s, ATT_W), BF16),
        compiler_params=_cparams(("parallel", "parallel")),
        name="attn",
    )(sinks, q3, kv3, kv3, bias)


EXPERT_ROW0 = 8
assert EXPERTS_PER_GROUP == 8 and N_GROUPS <= EXPERT_ROW0


def _route_t(lt):
    m = lt.shape[1]
    row = lax.broadcasted_iota(jnp.int32, (8, m), 0).astype(F32)
    big = 8.0
    gl = jnp.where(row < N_GROUPS, lt[0:8, :], NEG)
    mg = jnp.max(gl, axis=0, keepdims=True)
    gidx = jnp.min(jnp.where(gl == mg, row, big), axis=0, keepdims=True)
    gp = 1.0 / jnp.sum(jnp.where(row < N_GROUPS, jnp.exp(gl - mg), 0.0), axis=0, keepdims=True)
    el = lt[EXPERT_ROW0:EXPERT_ROW0 + 8, :]
    for g in range(1, N_GROUPS):
        el = jnp.where(gidx == float(g), lt[EXPERT_ROW0 + 8 * g:EXPERT_ROW0 + 8 * (g + 1), :], el)
    m1 = jnp.max(el, axis=0, keepdims=True)
    i1 = jnp.min(jnp.where(el == m1, row, big), axis=0, keepdims=True)
    el2 = jnp.where(row == i1, NEG, el)
    m2 = jnp.max(el2, axis=0, keepdims=True)
    i2 = jnp.min(jnp.where(el2 == m2, row, big), axis=0, keepdims=True)
    t = jnp.exp(m2 - m1)
    first = gidx * float(EXPERTS_PER_GROUP)
    return first + i1, first + i2, gp / (1.0 + t), gp * t / (1.0 + t)


def _row_copy(src_ref, dst_ref, sem, src_row, dst_row):
    return pltpu.make_async_copy(src_ref.at[pl.ds(src_row, 1)], dst_ref.at[pl.ds(dst_row, 1)], sem)


def _start_all(copies):
    for k, c in enumerate(copies):
        c.start(priority=k % 2)


def _merge_kernel(x_ref, ya_ref, yb_ref, gmix_ref, wg_ref, bg_ref, wa_ref, wb_ref, wo_ref, gffn_ref, wr_ref, br_ref,
                  x1_ref, route_ref, dest_ref, alloc_ref, state_ref, xd_hbm,
                  tri_ref, pre_ref, st_ref, hbuf_ref, destv_ref, dests_ref, stv_ref, sts_ref, zblk_ref,
                  hs_ref, gate_ref, sem_rows, sem_idx, sem_pad, *, tm, n_blk):
    i = pl.program_id(0)
    last = pl.num_programs(0) - 1
    slot = lax.rem(i, NBUF)
    prev = lax.rem(i + NBUF - 1, NBUF)
    n = TOP_K * tm
    dump0 = n_blk * MOE_BLK

    def row_copies(s, lo=0, hi=n):
        return [_row_copy(hbuf_ref.at[s], xd_hbm, sem_rows.at[s], r % tm, dests_ref[s, r // tm, r % tm])
                for r in range(lo, hi)]

    def wait_rows(s):
        for _ in range(n):
            _row_copy(hbuf_ref.at[s], xd_hbm, sem_rows.at[s], 0, dump0).wait()

    def index_copy(s):
        return pltpu.make_async_copy(destv_ref.at[s], dests_ref.at[s], sem_idx.at[s])

    @pl.when(i == 0)
    def _():
        r = lax.broadcasted_iota(jnp.int32, (tm, tm), 0)
        c = lax.broadcasted_iota(jnp.int32, (tm, tm), 1)
        tri_ref[...] = jnp.where(r < c, 1.0, 0.0).astype(BF16)
        r = lax.broadcasted_iota(jnp.int32, (LANES, LANES), 0)
        c = lax.broadcasted_iota(jnp.int32, (LANES, LANES), 1)
        pre_ref[...] = jnp.where(r > c, 1.0, 0.0).astype(BF16)
        c = lax.broadcasted_iota(jnp.int32, (N_EXPERTS, LANES), 1)
        st_ref[...] = jnp.where(c == 0, float(MOE_BLK), 0.0)
        hbuf_ref[NBUF - 1] = jnp.zeros((tm, PACK_W), jnp.uint32)

        def spare(t, carry):
            dests_ref[NBUF - 1, 0, t] = dump0 + t
            dests_ref[NBUF - 1, 1, t] = dump0 + tm + t
            return carry

        lax.fori_loop(0, tm, spare, 0)

    @pl.when(i >= NBUF - 1)
    def _():
        wait_rows(slot)

    @pl.when(i >= 1)
    def _():
        index_copy(prev).wait()

    group = n // 6
    once = jnp.minimum(dests_ref[prev, 0, 0], 0) + 1

    def region(body):
        lax.fori_loop(0, once, lambda _, c: (body(), c)[1], 0)

    def gates_lo():
        _start_all(row_copies(prev, 0, group))
        h = _rms(x_ref[...], gmix_ref[...]).astype(BF16)
        hs_ref[...] = h
        gate_ref[0] = _sigmoid(_dot(h, wg_ref[:, 0:D_MODEL]) + bg_ref[:, 0:D_MODEL])

    def gates_hi():
        _start_all(row_copies(prev, group, 2 * group))
        gate_ref[1] = _sigmoid(_dot(hs_ref[...], wg_ref[:, D_MODEL:2 * D_MODEL]) + bg_ref[:, D_MODEL:2 * D_MODEL])

    def branches():
        _start_all(row_copies(prev, 2 * group, 3 * group))
        merged = gate_ref[0] * _dot(ya_ref[...], wa_ref[...]) + gate_ref[1] * _dot(yb_ref[...], wb_ref[...])
        hs_ref[...] = merged.astype(BF16)

    region(gates_lo)
    region(gates_hi)
    region(branches)
    _start_all(row_copies(prev, 3 * group, n))
    x1 = x_ref[...] + _dot(hs_ref[...], wo_ref[...])
    x1_ref[...] = x1
    h2 = _rms(x1, gffn_ref[...])
    hbuf_ref[slot] = _pack_rows(h2)
    lt = lax.dot_general(wr_ref[...], h2.astype(BF16), (((1,), (1,)), ((), ())),
                         preferred_element_type=F32) + br_ref[...]
    e1, e2, w1, w2 = _route_t(lt)
    r8 = lax.broadcasted_iota(jnp.int32, (8, tm), 0)
    rt8 = jnp.where(r8 == 0, e1, jnp.where(r8 == 1, e2, jnp.where(r8 == 2, w1, jnp.where(r8 == 3, w2, 0.0))))
    route_ref[...] = jnp.transpose(jnp.concatenate([rt8, jnp.zeros((LANES - 8, tm), F32)], axis=0), (1, 0))

    erow = lax.broadcasted_iota(jnp.int32, (N_EXPERTS, tm), 0).astype(F32)
    is1, is2 = erow == e1, erow == e2
    member = jnp.where(is1, 1.0, jnp.where(is2, 1.0, 0.0))
    before = _dot(member.astype(BF16), tri_ref[...])
    cnt = jnp.sum(member, axis=1, keepdims=True)
    st = st_ref[...]
    fill, blk, free = st[:, 0:1], st[:, 1:2], st[:, 2:3]
    need = fill + cnt
    new = jnp.floor((need + float(MOE_BLK - 1)) * (1.0 / MOE_BLK)) - 1.0
    new_pad = jnp.concatenate([jnp.broadcast_to(new, (N_EXPERTS, LANES)),
                               jnp.zeros((LANES - N_EXPERTS, LANES), F32)], axis=0).astype(BF16)
    base = free + _dot(pre_ref[...], new_pad)[0:N_EXPERTS, 0:1]
    q = fill + before
    jb = jnp.floor(q * (1.0 / MOE_BLK))
    rowid = jnp.where(jb == 0.0, blk, base + jb - 1.0) * float(MOE_BLK) + (q - jb * float(MOE_BLK))
    d1 = jnp.sum(jnp.where(is1, rowid, 0.0), axis=0, keepdims=True)
    d2 = jnp.sum(jnp.where(is2, rowid, 0.0), axis=0, keepdims=True)
    d8 = jnp.where(r8 == 0, d1, jnp.where(r8 == 1, d2, 0.0)).astype(jnp.int32)
    destv_ref[slot] = d8
    dest_ref[0] = d8
    index_copy(slot).start()

    lane = lax.broadcasted_iota(jnp.int32, (N_EXPERTS, LANES), 1)
    alloc_ref[0] = jnp.where(lane == 0, new, jnp.where(lane == 1, base, 0.0))
    state = jnp.where(lane == 0, need - new * float(MOE_BLK),
                      jnp.where(lane == 1, jnp.where(new > 0.0, base + new - 1.0, blk),
                                jnp.where(lane == 2, free + jnp.sum(new, axis=0, keepdims=True), 0.0)))
    st_ref[...] = state
    state_ref[...] = state

    @pl.when(i == last)
    def _():
        index_copy(slot).wait()
        _start_all(row_copies(slot))
        for s in range(NBUF):
            wait_rows(s)
        stv_ref[...] = state.astype(jnp.int32)
        cp = pltpu.make_async_copy(stv_ref, sts_ref, sem_pad)
        cp.start()
        cp.wait()
        zblk_ref[...] = jnp.zeros_like(zblk_ref)

        def zero_rows(e, carry):
            first = sts_ref[e, 1] * MOE_BLK
            lo = sts_ref[e, 0]
            for g, nxt in ((1, 8), (8, 64), (64, MOE_BLK)):
                hi = jnp.minimum((lo + nxt - 1) // nxt * nxt, MOE_BLK)

                def run(j, g=g):
                    dst = xd_hbm.at[pl.ds(pl.multiple_of(first + j * g, g), g)]
                    return pltpu.make_async_copy(zblk_ref.at[pl.ds(0, g)], dst, sem_pad)

                def start(j, c, run=run):
                    run(j).start()
                    return c

                def wait(j, c, run=run):
                    run(j).wait()
                    return c

                lax.fori_loop(lo // g, hi // g, start, 0)
                lax.fori_loop(lo // g, hi // g, wait, 0)
                lo = hi
            return carry

        lax.fori_loop(0, N_EXPERTS, zero_rows, 0)

        def block_copy(b):
            return pltpu.make_async_copy(zblk_ref, xd_hbm.at[pl.ds(pl.multiple_of(b * MOE_BLK, MOE_BLK), MOE_BLK)],
                                         sem_pad)

        def zero_block(b, c):
            block_copy(b).start()
            block_copy(b).wait()
            return c

        lax.fori_loop(sts_ref[0, 2], n_blk, zero_block, 0)


def _merge(x2d, ya, yb, g_mix, wg, bg, wa, wb, wo, g_ffn, wr, br, n_blk, tm=512):
    t = x2d.shape[0]
    nt = t // tm
    n_rows = n_blk * MOE_BLK + TOP_K * tm
    row = lambda w: pl.BlockSpec((tm, w), lambda i: (i, 0))
    full = lambda a: pl.BlockSpec(a.shape, lambda i: (0,) * a.ndim)
    once = lambda a: pl.BlockSpec(a.shape, lambda i: (0,) * a.ndim, pipeline_mode=pl.Buffered(1))
    return pl.pallas_call(
        functools.partial(_merge_kernel, tm=tm, n_blk=n_blk),
        grid=(nt,),
        in_specs=[row(D_MODEL), row(D_SSM), row(ATT_W), full(g_mix), once(wg), full(bg),
                  once(wa), once(wb), once(wo), full(g_ffn), full(wr), full(br)],
        out_specs=[row(D_MODEL), row(LANES), pl.BlockSpec((1, 8, tm), lambda i: (i, 0, 0)),
                   pl.BlockSpec((1, N_EXPERTS, LANES), lambda i: (i, 0, 0)),
                   pl.BlockSpec((N_EXPERTS, LANES), lambda i: (0, 0)),
                   pl.BlockSpec(memory_space=pl.ANY)],
        out_shape=[jax.ShapeDtypeStruct((t, D_MODEL), F32),
                   jax.ShapeDtypeStruct((t, LANES), F32),
                   jax.ShapeDtypeStruct((nt, 8, tm), jnp.int32),
                   jax.ShapeDtypeStruct((nt, N_EXPERTS, LANES), F32),
                   jax.ShapeDtypeStruct((N_EXPERTS, LANES), F32),
                   jax.ShapeDtypeStruct((n_rows, PACK_W), jnp.uint32)],
        scratch_shapes=[pltpu.VMEM((tm, tm), BF16), pltpu.VMEM((LANES, LANES), BF16), pltpu.VMEM((N_EXPERTS, LANES), F32),
                        pltpu.VMEM((NBUF, tm, PACK_W), jnp.uint32), pltpu.VMEM((NBUF, 8, tm), jnp.int32),
                        pltpu.SMEM((NBUF, 8, tm), jnp.int32), pltpu.VMEM((N_EXPERTS, LANES), jnp.int32),
                        pltpu.SMEM((N_EXPERTS, LANES), jnp.int32), pltpu.VMEM((MOE_BLK, PACK_W), jnp.uint32),
                        pltpu.VMEM((tm, D_MODEL), BF16), pltpu.VMEM((2, tm, D_MODEL), F32),
                        pltpu.SemaphoreType.DMA((NBUF,)), pltpu.SemaphoreType.DMA((NBUF,)),
                        pltpu.SemaphoreType.DMA],
        compiler_params=_cparams(("arbitrary",)),
        name="merge",
    )(x2d, ya, yb, g_mix, wg, bg, wa, wb, wo, g_ffn, wr, br)


def _moe_kernel(be_ref, bn_ref, bi_ref, xd_ref, wg_ref, wu_ref, wd_ref, yd_ref, wgb_ref, wub_ref, wdb_ref):
    del bi_ref
    i = pl.program_id(0)

    @pl.when(bn_ref[i] == 0)
    def _():
        yd_ref[...] = jnp.zeros_like(yd_ref)

    @pl.when(bn_ref[i] > 0)
    def _():
        prev = be_ref[jnp.maximum(i - 1, 0)]

        @pl.when(jnp.logical_or(i == 0, be_ref[i] != prev))
        def _():
            wgb_ref[...] = wg_ref[0].astype(BF16)
            wub_ref[...] = wu_ref[0].astype(BF16)
            wdb_ref[...] = wd_ref[0].astype(BF16)

        x = _unpack_rows(xd_ref[...]).astype(BF16)
        g = _dot(x, wgb_ref[...])
        u = _dot(x, wub_ref[...])
        a = (g * _sigmoid(g) * u).astype(BF16)
        yd_ref[...] = _pack_rows(_dot(a, wdb_ref[...]))


def _moe(blk_e, blk_n, blk_i, xdw, w_eg, w_eu, w_ed):
    n_blk = blk_e.shape[0]
    grid_spec = pltpu.PrefetchScalarGridSpec(
        num_scalar_prefetch=3,
        grid=(n_blk,),
        in_specs=[pl.BlockSpec((MOE_BLK, PACK_W), lambda i, be, bn, bi: (bi[i], 0)),
                  pl.BlockSpec((1, D_MODEL, D_FF_EXPERT), lambda i, be, bn, bi: (be[i], 0, 0)),
                  pl.BlockSpec((1, D_MODEL, D_FF_EXPERT), lambda i, be, bn, bi: (be[i], 0, 0)),
                  pl.BlockSpec((1, D_FF_EXPERT, D_MODEL), lambda i, be, bn, bi: (be[i], 0, 0))],
        out_specs=pl.BlockSpec((MOE_BLK, PACK_W), lambda i, be, bn, bi: (bi[i], 0)),
        scratch_shapes=[pltpu.VMEM((D_MODEL, D_FF_EXPERT), BF16), pltpu.VMEM((D_MODEL, D_FF_EXPERT), BF16),
                        pltpu.VMEM((D_FF_EXPERT, D_MODEL), BF16)],
    )
    return pl.pallas_call(
        _moe_kernel,
        grid_spec=grid_spec,
        out_shape=jax.ShapeDtypeStruct((n_blk * MOE_BLK, PACK_W), jnp.uint32),
        compiler_params=_cparams(("arbitrary",)),
        name="moe",
    )(blk_e, blk_n, blk_i, xdw, w_eg, w_eu, w_ed)


def _final_kernel(dcur_ref, dnxt_ref, x1_ref, p_ref, route_ref, yd_hbm, wpp_ref, wpg_ref, bpg_ref, gple_ref,
                  gfin_ref, o_ref, yw0_ref, yw1_ref, sem, *, tm):
    j = pl.program_id(0)
    last = pl.num_programs(0) - 1
    n = TOP_K * tm
    yw = (yw0_ref, yw1_ref)

    def gather(dref, half, slot):
        return [_row_copy(yd_hbm, yw[slot], sem.at[slot], dref[0, r // tm, half * tm + r % tm], r)
                for r in range(n)]

    def tile(half, slot):
        rows = slice(half * tm, (half + 1) * tm)
        rt = route_ref[rows, :]
        y0 = _unpack_rows(yw[slot][0:tm, :])
        y1 = _unpack_rows(yw[slot][tm:n, :])
        x2 = x1_ref[rows, :] + rt[:, 2:3] * y0 + rt[:, 3:4] * y1
        h3 = _rms(x2, gple_ref[...]).astype(BF16)
        pp = _dot(p_ref[rows, :].astype(BF16), wpp_ref[...])
        x3 = x2 + pp * _sigmoid(_dot(h3, wpg_ref[...]) + bpg_ref[...])
        o_ref[rows, :] = _rms(x3, gfin_ref[...])

    @pl.when(j == 0)
    def _():
        _start_all(gather(dcur_ref, 0, 0))

    for c in gather(dcur_ref, 0, 0):
        c.wait()
    _start_all(gather(dcur_ref, 1, 1))
    tile(0, 0)
    for c in gather(dcur_ref, 1, 1):
        c.wait()
    _start_all(gather(dnxt_ref, 0, 0))
    tile(1, 1)

    @pl.when(j == last)
    def _():
        for c in gather(dnxt_ref, 0, 0):
            c.wait()


def _final(dest, x1, p2d, route, ydw, wpp, wpg, bpg, g_ple, g_final):
    t = x1.shape[0]
    ns, _, two_tm = dest.shape
    tm = two_tm // 2
    n = TOP_K * tm
    row = lambda w: pl.BlockSpec((2 * tm, w), lambda i: (i, 0))
    full = lambda a: pl.BlockSpec(a.shape, lambda i: (0,) * a.ndim)
    return pl.pallas_call(
        functools.partial(_final_kernel, tm=tm),
        grid=(ns,),
        in_specs=[pl.BlockSpec((1, 8, two_tm), lambda i: (i, 0, 0), memory_space=pltpu.SMEM),
                  pl.BlockSpec((1, 8, two_tm), lambda i: (jnp.minimum(i + 1, ns - 1), 0, 0),
                               memory_space=pltpu.SMEM),
                  row(D_MODEL), row(PLE_DIM), row(LANES), pl.BlockSpec(memory_space=pl.ANY),
                  full(wpp), full(wpg), full(bpg), full(g_ple), full(g_final)],
        out_specs=row(D_MODEL),
        out_shape=jax.ShapeDtypeStruct((t, D_MODEL), F32),
        scratch_shapes=[pltpu.VMEM((n, PACK_W), jnp.uint32), pltpu.VMEM((n, PACK_W), jnp.uint32),
                        pltpu.SemaphoreType.DMA((2,))],
        compiler_params=_cparams(("arbitrary",)),
        name="final",
    )(dest, dest, x1, p2d, route, ydw, wpp, wpg, bpg, g_ple, g_final)


def _block_plan(alloc, state, n_blk):
    new = alloc[:, :, 0].astype(jnp.int32).reshape(-1)
    base = alloc[:, :, 1].astype(jnp.int32).reshape(-1)
    expert = jnp.tile(jnp.arange(N_EXPERTS, dtype=jnp.int32), alloc.shape[0])
    fill = state[:, 0].astype(jnp.int32)
    last_blk = state[:, 1].astype(jnp.int32)
    taken = state[0, 2].astype(jnp.int32)
    b = jnp.arange(n_blk, dtype=jnp.int32)
    opened = (base[None, :] <= b[:, None]) & (b[:, None] < (base + new)[None, :])
    e_of = jnp.sum(jnp.where(opened, expert[None, :], 0), axis=1)
    onehot = e_of[:, None] == jnp.arange(N_EXPERTS, dtype=jnp.int32)[None, :]
    pick = lambda v: jnp.sum(jnp.where(onehot, v[None, :], 0), axis=1)
    rows = jnp.where(b < taken, jnp.where(b == pick(last_blk), pick(fill), MOE_BLK), 0)
    key = jnp.where(b < taken, e_of, N_EXPERTS) * n_blk + b
    pos = jnp.sum(key[None, :] < key[:, None], axis=1)
    at = pos[None, :] == b[:, None]
    order = lambda v: jnp.sum(jnp.where(at, v[None, :], 0), axis=1).astype(jnp.int32)
    return order(jnp.where(b < taken, e_of, N_EXPERTS - 1)), order(rows), order(b)


def _row(v):
    return v.reshape(1, -1).astype(F32)


def kernel(x, p, rel_bias, g_mix, w_in, w_gate, b_gate, ssm_a_re, ssm_a_im, ssm_log_dt, ssm_b_re, ssm_b_im,
           ssm_c_re, ssm_c_im, ssm_d, w_glu, b_glu, sinks, w_br_ssm, w_br_attn, w_out, g_ffn, w_router_group,
           b_router_group, w_router_expert, b_router_expert, w_e_gate, w_e_up, w_e_down, g_ple, w_ple_gate,
           b_ple_gate, w_ple_proj, g_final):
    bsz, seq, dm = x.shape
    assert g_mix.shape[0] == 1, "one layer followed by the final norm"
    i = 0
    t = bsz * seq
    x2d = x.reshape(t, dm)
    bias = _bias_table(rel_bias)
    u, q, kv = _proj(x2d, _row(g_mix[i]), w_in[i].astype(BF16))
    ab_re, ab_im, bb_re, bb_im = _ssmprep(ssm_a_re[i], ssm_a_im[i], ssm_log_dt[i], ssm_b_re[i], ssm_b_im[i])
    ws_in, ws_out, ar, ai = _ssm_weights(ab_re, ab_im, bb_re, bb_im, ssm_c_re[i], ssm_c_im[i])
    y_ssm = _ssm(u.reshape(bsz, seq, D_SSM), ws_in, ws_out, ar, ai, _row(ssm_d[i]),
                 w_glu[i].astype(BF16), _row(b_glu[i]))
    y_att = _attn(q.reshape(bsz, seq, ATT_W), kv.reshape(bsz, seq, KV_W), bias, sinks[i].astype(F32))
    w_router = jnp.zeros((LANES, dm), F32)
    w_router = w_router.at[:N_GROUPS].set(w_router_group[i].T)
    w_router = w_router.at[EXPERT_ROW0:EXPERT_ROW0 + N_EXPERTS].set(w_router_expert[i].T)
    b_router = jnp.zeros((LANES, 1), F32)
    b_router = b_router.at[:N_GROUPS, 0].set(b_router_group[i])
    b_router = b_router.at[EXPERT_ROW0:EXPERT_ROW0 + N_EXPERTS, 0].set(b_router_expert[i])
    n_blk = t * TOP_K // MOE_BLK + N_EXPERTS
    x1, route, dest, alloc, state, xdw = _merge(
        x2d, y_ssm.reshape(t, D_SSM), y_att.reshape(t, ATT_W), _row(g_mix[i]), w_gate[i].astype(BF16),
        _row(b_gate[i]), w_br_ssm[i].astype(BF16), w_br_attn[i].astype(BF16), w_out[i].astype(BF16),
        _row(g_ffn[i]), w_router.astype(BF16), b_router, n_blk)
    blk_e, blk_n, blk_i = _block_plan(alloc, state, n_blk)
    ydw = _moe(blk_e, blk_n, blk_i, xdw, w_e_gate[i], w_e_up[i], w_e_down[i])
    out = _final(dest, x1, p[i].reshape(t, PLE_DIM), route, ydw, w_ple_proj[i].astype(BF16),
                 w_ple_gate[i].astype(BF16), _row(b_ple_gate[i]), _row(g_ple[i]), _row(g_final))
    return out.reshape(bsz, seq, dm)
```

```python
import functools
import math

import numpy as np
import jax
import jax.numpy as jnp
from jax import lax
from jax.experimental import pallas as pl
from jax.experimental.pallas import tpu as pltpu

D_MODEL = 1024
D_SSM = 512
SSM_CH = 16
SSM_GROUPS = D_SSM // SSM_CH
SSM_STATE = 64
N_STATE = SSM_GROUPS * SSM_STATE
N_HEADS = 8
N_KV = 2
HEAD_DIM = 64
GROUP = N_HEADS // N_KV
ATT_W = N_HEADS * HEAD_DIM
KV_W = 2 * N_KV * HEAD_DIM
WINDOW = 128
BLOCK = 128
NUM_BUCKETS = 32
MAX_DIST = 128
D_IN = D_SSM + ATT_W + KV_W
N_GROUPS = 4
EXPERTS_PER_GROUP = 8
N_EXPERTS = N_GROUPS * EXPERTS_PER_GROUP
TOP_K = 2
D_FF_EXPERT = 512
MOE_BLK = 512
PLE_DIM = 256
EPS = 1e-6

LANES = 128
NEG = -1e30
PACK_W = D_MODEL // 2

BF16 = jnp.bfloat16
F32 = jnp.float32
VMEM_LIMIT = 56 * 1024 * 1024
NBUF = 3


def _cparams(sem):
    return pltpu.CompilerParams(dimension_semantics=sem, vmem_limit_bytes=VMEM_LIMIT)


def _rms(x, g):
    ms = jnp.mean(x * x, axis=-1, keepdims=True)
    return x * lax.rsqrt(ms + EPS) * g


def _sigmoid(x):
    return 1.0 / (1.0 + jnp.exp(-x))


def _dot(a, b):
    return jnp.dot(a, b, preferred_element_type=F32)


def _pack_rows(y):
    lo = pltpu.bitcast(y[:, :PACK_W].astype(BF16).astype(F32), jnp.uint32)
    hi = pltpu.bitcast(y[:, PACK_W:].astype(BF16).astype(F32), jnp.uint32)
    return hi | lax.shift_right_logical(lo, jnp.uint32(16))


def _unpack_rows(w):
    lo = pltpu.bitcast(lax.shift_left(w, jnp.uint32(16)), F32)
    hi = pltpu.bitcast(w & jnp.uint32(0xFFFF0000), F32)
    return jnp.concatenate([lo, hi], axis=1)


def _proj_kernel(x_ref, g_ref, w_ref, u_ref, q_ref, kv_ref):
    h = _rms(x_ref[...], g_ref[...]).astype(BF16)
    proj = _dot(h, w_ref[...])
    u_ref[...] = proj[:, :D_SSM].astype(BF16)
    q_ref[...] = (proj[:, D_SSM:D_SSM + ATT_W] * (HEAD_DIM ** -0.5)).astype(BF16)
    kv_ref[...] = proj[:, D_SSM + ATT_W:].astype(BF16)


def _proj(x2d, g_mix, w_in_b, tm=2048):
    t = x2d.shape[0]
    row = lambda w: pl.BlockSpec((tm, w), lambda i: (i, 0))
    full = lambda a: pl.BlockSpec(a.shape, lambda i: (0,) * a.ndim)
    return pl.pallas_call(
        _proj_kernel,
        grid=(t // tm,),
        in_specs=[row(D_MODEL), full(g_mix), full(w_in_b)],
        out_specs=[row(D_SSM), row(ATT_W), row(KV_W)],
        out_shape=[jax.ShapeDtypeStruct((t, D_SSM), BF16),
                   jax.ShapeDtypeStruct((t, ATT_W), BF16),
                   jax.ShapeDtypeStruct((t, KV_W), BF16)],
        compiler_params=_cparams(("parallel",)),
        name="proj",
    )(x2d, g_mix, w_in_b)


def _ssmprep_kernel(are_ref, aim_ref, ldt_ref, bre_ref, bim_ref, abr_ref, abi_ref, bbr_ref, bbi_ref):
    a_re, a_im = are_ref[...], aim_ref[...]
    dt = jnp.exp(ldt_ref[...])
    mag = jnp.exp(a_re * dt)
    ab_re = mag * jnp.cos(a_im * dt)
    ab_im = mag * jnp.sin(a_im * dt)
    abr_ref[...] = ab_re
    abi_ref[...] = ab_im
    den = a_re * a_re + a_im * a_im
    c_re = ((ab_re - 1.0) * a_re + ab_im * a_im) / den
    c_im = (ab_im * a_re - (ab_re - 1.0) * a_im) / den
    for c in range(SSM_CH):
        b_re, b_im = bre_ref[c], bim_ref[c]
        bbr_ref[c] = c_re * b_re - c_im * b_im
        bbi_ref[c] = c_re * b_im + c_im * b_re


def _ssmprep(a_re, a_im, log_dt, b_re, b_im):
    g, p = a_re.shape
    bt_re = jnp.transpose(b_re, (2, 0, 1))
    bt_im = jnp.transpose(b_im, (2, 0, 1))
    gp = jax.ShapeDtypeStruct((g, p), F32)
    cgp = jax.ShapeDtypeStruct((SSM_CH, g, p), F32)
    return pl.pallas_call(
        _ssmprep_kernel, out_shape=[gp, gp, cgp, cgp], name="ssmprep",
    )(a_re, a_im, log_dt.reshape(g, 1), bt_re, bt_im)


def _ssm_weights(ab_re, ab_im, bb_re, bb_im, c_re, c_im):
    g, p = ab_re.shape
    kb = D_SSM // 256
    gk = g // kb
    eye = jnp.eye(gk, dtype=F32)

    def in_block(bb):
        b4 = jnp.transpose(bb, (1, 0, 2)).reshape(kb, gk, SSM_CH, p)
        return jnp.einsum("kgcp,gh->kgchp", b4, eye).reshape(kb, gk * SSM_CH, gk * p)

    w_in = jnp.concatenate([in_block(bb_re), in_block(bb_im)], axis=2).astype(BF16)
    nj = D_SSM // LANES
    gj = g // nj
    eyej = jnp.eye(gj, dtype=F32)

    def out_block(c):
        c4 = jnp.transpose(c, (0, 2, 1)).reshape(nj, gj, p, SSM_CH)
        return jnp.einsum("jgpc,gh->jgphc", c4, eyej).reshape(nj, gj * p, gj * SSM_CH)

    w_out = jnp.stack([out_block(c_re), out_block(-c_im)], axis=1).astype(BF16)
    return w_in, w_out, ab_re.reshape(1, g * p), ab_im.reshape(1, g * p)


def _gelu_tanh(x):
    return 0.5 * x * (1.0 + jnp.tanh(math.sqrt(2.0 / math.pi) * (x + 0.044715 * (x * x * x))))


def _ssm_kernel(u_ref, ul_ref, win_ref, wout_ref, ar_ref, ai_ref, d_ref, wglu_ref, bglu_ref, y_ref,
                bu0_ref, bu1_ref, xb0_ref, xb1_ref, st_ref, *, nb, lc):
    rows = nb * lc
    ns = N_STATE
    k = pl.program_id(0)
    bu = (bu0_ref, bu1_ref)
    xb = (xb0_ref, xb1_ref)

    @pl.when(k == 0)
    def _():
        bu1_ref[...] = jnp.zeros_like(bu1_ref)
        xb0_ref[...] = jnp.zeros_like(xb0_ref)
        xb1_ref[...] = jnp.zeros_like(xb1_ref)
        st_ref[...] = jnp.zeros_like(st_ref)

    kb = win_ref.shape[0]
    half = ns // kb
    nj = wout_ref.shape[0]
    kw = ns // nj
    ar, ai = ar_ref[...], ai_ref[...]
    xr, xi = st_ref[:, 0:ns], st_ref[:, ns:2 * ns]

    for h in range(2):
        part = slice(h * lc, (h + 1) * lc)
        u_new = pltpu.einshape("btd->(tb)d", u_ref[:, part, :])
        for j in range(kb):
            res = _dot(u_new[:, 256 * j:256 * (j + 1)], win_ref[j])
            bu[h][:, half * j:half * (j + 1)] = res[:, :half]
            bu[h][:, ns + half * j:ns + half * (j + 1)] = res[:, half:]
        src, dst = bu[1 - h], xb[1 - h]
        for t in range(0, lc, 2):
            keep_r, keep_i = [], []
            for s in range(2):
                row = (t + s) * nb
                br = src[row:row + nb, 0:ns]
                bi = src[row:row + nb, ns:2 * ns]
                xr, xi = ar * xr - ai * xi + br, ar * xi + ai * xr + bi
                keep_r.append(xr)
                keep_i.append(xi)
            dst[t * nb:(t + 2) * nb, 0:ns] = jnp.concatenate(keep_r, axis=0).astype(BF16)
            dst[t * nb:(t + 2) * nb, ns:2 * ns] = jnp.concatenate(keep_i, axis=0).astype(BF16)
        ys = []
        for j in range(nj):
            ys.append(_dot(xb[h][:, kw * j:kw * (j + 1)], wout_ref[j, 0])
                      + _dot(xb[h][:, ns + kw * j:ns + kw * (j + 1)], wout_ref[j, 1]))
        y_bt = pltpu.einshape("(tb)d->btd", jnp.concatenate(ys, axis=1), b=nb).reshape(rows, D_SSM)
        y = y_bt + d_ref[...] * ul_ref[:, part, :].reshape(rows, D_SSM).astype(F32)
        y = _gelu_tanh(y)
        z = _dot(y.astype(BF16), wglu_ref[...]) + bglu_ref[...]
        y_ref[:, part, :] = (y * _sigmoid(z)).astype(BF16).reshape(nb, lc, D_SSM)

    st_ref[:, 0:ns] = xr
    st_ref[:, ns:2 * ns] = xi


def _ssm(u3, w_in, w_out, ar, ai, d_skip, w_glu_b, b_glu, lc=64):
    nb, s, _ = u3.shape
    rows = nb * lc
    nk = s // (2 * lc)
    assert nb & (nb - 1) == 0 and nb % 8 == 0 and s % (2 * lc) == 0 and lc % 16 == 0
    full = lambda a: pl.BlockSpec(a.shape, lambda i: (0,) * a.ndim)
    lead = pl.BlockSpec((nb, 2 * lc, D_SSM), lambda i: (0, jnp.minimum(i, nk - 1), 0))
    lag = pl.BlockSpec((nb, 2 * lc, D_SSM), lambda i: (0, jnp.maximum(i - 1, 0), 0))
    return pl.pallas_call(
        functools.partial(_ssm_kernel, nb=nb, lc=lc),
        grid=(nk + 1,),
        in_specs=[lead, lag, full(w_in), full(w_out), full(ar), full(ai), full(d_skip), full(w_glu_b), full(b_glu)],
        out_specs=lag,
        out_shape=jax.ShapeDtypeStruct((nb, s, D_SSM), BF16),
        scratch_shapes=[pltpu.VMEM((rows, 2 * N_STATE), F32), pltpu.VMEM((rows, 2 * N_STATE), F32),
                        pltpu.VMEM((rows, 2 * N_STATE), BF16), pltpu.VMEM((rows, 2 * N_STATE), BF16),
                        pltpu.VMEM((nb, 2 * N_STATE), F32)],
        compiler_params=_cparams(("arbitrary",)),
        name="ssm",
    )(u3, u3, w_in, w_out, ar, ai, d_skip, w_glu_b, b_glu)


def _t5_bucket_np(rel):
    max_exact = NUM_BUCKETS // 2
    relf = np.maximum(rel, 1).astype(np.float32)
    large = max_exact + (np.log(relf / np.float32(max_exact)) / np.float32(math.log(MAX_DIST / max_exact))
                         * np.float32(NUM_BUCKETS - max_exact)).astype(np.int32)
    large = np.minimum(large, NUM_BUCKETS - 1)
    return np.where(rel < max_exact, rel, large)


def _bias_table(rel_bias):
    q_loc = np.arange(BLOCK)[:, None]
    c_loc = np.arange(2 * BLOCK)[None, :]
    rel = q_loc + BLOCK - c_loc
    valid = (rel >= 0) & (rel < WINDOW)
    bucket = _t5_bucket_np(np.maximum(rel, 0)).reshape(-1, 1)
    onehot = (jnp.asarray(bucket) == jnp.arange(NUM_BUCKETS)[None, :]).astype(F32)
    bias = jnp.dot(onehot, rel_bias.astype(F32), precision=lax.Precision.HIGHEST)
    bias = jnp.where(valid.reshape(-1, 1), bias, NEG)
    return jnp.transpose(bias, (1, 0)).reshape(N_HEADS, BLOCK, 2 * BLOCK)


def _attn_kernel(sink_ref, q_ref, kvp_ref, kvc_ref, bias_ref, o_ref, *, nq):
    n = pl.program_id(1)
    kv_all = jnp.concatenate([kvp_ref[0], kvc_ref[0]], axis=0)
    col = lax.broadcasted_iota(jnp.int32, (BLOCK, 2 * BLOCK), 1)
    no_prev = jnp.where(col < BLOCK, jnp.where(n == 0, NEG, 0.0), 0.0)
    for j in range(nq):
        q = q_ref[0, BLOCK * j:BLOCK * (j + 1), :]
        kv = kv_all[BLOCK * j:BLOCK * (j + 2), :]
        outs = []
        for h in range(N_HEADS):
            g = h // GROUP
            qh = q[:, HEAD_DIM * h:HEAD_DIM * (h + 1)]
            kh = kv[:, HEAD_DIM * g:HEAD_DIM * (g + 1)]
            vh = kv[:, N_KV * HEAD_DIM + HEAD_DIM * g:N_KV * HEAD_DIM + HEAD_DIM * (g + 1)]
            s = lax.dot_general(qh, kh, (((1,), (1,)), ((), ())), preferred_element_type=F32)
            s = s + bias_ref[h]
            if j == 0:
                s = s + no_prev
            sink = sink_ref[h]
            m = jnp.maximum(jnp.max(s, axis=-1, keepdims=True), sink)
            e = jnp.exp(s - m)
            den = jnp.sum(e, axis=-1, keepdims=True) + jnp.exp(sink - m)
            outs.append(_dot(e.astype(BF16), vh) / den)
        o_ref[0, BLOCK * j:BLOCK * (j + 1), :] = jnp.concatenate(outs, axis=1).astype(BF16)


def _attn(q3, kv3, bias, sinks, nq=2):
    b, s, _ = q3.shape
    qb = nq * BLOCK
    grid_spec = pltpu.PrefetchScalarGridSpec(
        num_scalar_prefetch=0,
        grid=(b, s // qb),
        in_specs=[pl.BlockSpec(memory_space=pltpu.SMEM),
                  pl.BlockSpec((1, qb, ATT_W), lambda i, n: (i, n, 0)),
                  pl.BlockSpec((1, BLOCK, KV_W), lambda i, n: (i, jnp.maximum(n * nq - 1, 0), 0)),
                  pl.BlockSpec((1, qb, KV_W), lambda i, n: (i, n, 0)),
                  pl.BlockSpec(bias.shape, lambda i, n: (0, 0, 0))],
        out_specs=pl.BlockSpec((1, qb, ATT_W), lambda i, n: (i, n, 0)),
    )
    return pl.pallas_call(
        functools.partial(_attn_kernel, nq=nq),
        grid_spec=grid_spec,
        out_shape=jax.ShapeDtypeStruct((b, s, ATT_W), BF16),
        compiler_params=_cparams(("parallel", "parallel")),
        name="attn",
    )(sinks, q3, kv3, kv3, bias)


EXPERT_ROW0 = 8
assert EXPERTS_PER_GROUP == 8 and N_GROUPS <= EXPERT_ROW0


def _route_t(lt):
    m = lt.shape[1]
    row = lax.broadcasted_iota(jnp.int32, (8, m), 0).astype(F32)
    big = 8.0
    gl = jnp.where(row < N_GROUPS, lt[0:8, :], NEG)
    mg = jnp.max(gl, axis=0, keepdims=True)
    gidx = jnp.min(jnp.where(gl == mg, row, big), axis=0, keepdims=True)
    gp = 1.0 / jnp.sum(jnp.where(row < N_GROUPS, jnp.exp(gl - mg), 0.0), axis=0, keepdims=True)
    el = lt[EXPERT_ROW0:EXPERT_ROW0 + 8, :]
    for g in range(1, N_GROUPS):
        el = jnp.where(gidx == float(g), lt[EXPERT_ROW0 + 8 * g:EXPERT_ROW0 + 8 * (g + 1), :], el)
    m1 = jnp.max(el, axis=0, keepdims=True)
    i1 = jnp.min(jnp.where(el == m1, row, big), axis=0, keepdims=True)
    el2 = jnp.where(row == i1, NEG, el)
    m2 = jnp.max(el2, axis=0, keepdims=True)
    i2 = jnp.min(jnp.where(el2 == m2, row, big), axis=0, keepdims=True)
    t = jnp.exp(m2 - m1)
    first = gidx * float(EXPERTS_PER_GROUP)
    return first + i1, first + i2, gp / (1.0 + t), gp * t / (1.0 + t)


def _row_copy(src_ref, dst_ref, sem, src_row, dst_row):
    return pltpu.make_async_copy(src_ref.at[pl.ds(src_row, 1)], dst_ref.at[pl.ds(dst_row, 1)], sem)


def _start_all(copies):
    for k, c in enumerate(copies):
        c.start(priority=k % 2)


def _merge_kernel(x_ref, ya_ref, yb_ref, gmix_ref, wg_ref, bg_ref, wa_ref, wb_ref, wo_ref, gffn_ref, wr_ref, br_ref,
                  x1_ref, route_ref, dest_ref, alloc_ref, state_ref, xd_hbm,
                  tri_ref, pre_ref, st_ref, hbuf_ref, destv_ref, dests_ref, stv_ref, sts_ref, zblk_ref,
                  hs_ref, gate_ref, sem_rows, sem_idx, sem_pad, *, tm, n_blk):
    i = pl.program_id(0)
    last = pl.num_programs(0) - 1
    slot = lax.rem(i, NBUF)
    prev = lax.rem(i + NBUF - 1, NBUF)
    n = TOP_K * tm
    dump0 = n_blk * MOE_BLK

    def row_copies(s, lo=0, hi=n):
        return [_row_copy(hbuf_ref.at[s], xd_hbm, sem_rows.at[s], r % tm, dests_ref[s, r // tm, r % tm])
                for r in range(lo, hi)]

    def wait_rows(s):
        for _ in range(n):
            _row_copy(hbuf_ref.at[s], xd_hbm, sem_rows.at[s], 0, dump0).wait()

    def index_copy(s):
        return pltpu.make_async_copy(destv_ref.at[s], dests_ref.at[s], sem_idx.at[s])

    @pl.when(i == 0)
    def _():
        r = lax.broadcasted_iota(jnp.int32, (tm, tm), 0)
        c = lax.broadcasted_iota(jnp.int32, (tm, tm), 1)
        tri_ref[...] = jnp.where(r < c, 1.0, 0.0).astype(BF16)
        r = lax.broadcasted_iota(jnp.int32, (LANES, LANES), 0)
        c = lax.broadcasted_iota(jnp.int32, (LANES, LANES), 1)
        pre_ref[...] = jnp.where(r > c, 1.0, 0.0).astype(BF16)
        c = lax.broadcasted_iota(jnp.int32, (N_EXPERTS, LANES), 1)
        st_ref[...] = jnp.where(c == 0, float(MOE_BLK), 0.0)
        hbuf_ref[NBUF - 1] = jnp.zeros((tm, PACK_W), jnp.uint32)

        def spare(t, carry):
            dests_ref[NBUF - 1, 0, t] = dump0 + t
            dests_ref[NBUF - 1, 1, t] = dump0 + tm + t
            return carry

        lax.fori_loop(0, tm, spare, 0)

    @pl.when(i >= NBUF - 1)
    def _():
        wait_rows(slot)

    @pl.when(i >= 1)
    def _():
        index_copy(prev).wait()

    group = n // 6
    once = jnp.minimum(dests_ref[prev, 0, 0], 0) + 1

    def region(body):
        lax.fori_loop(0, once, lambda _, c: (body(), c)[1], 0)

    def gates_lo():
        _start_all(row_copies(prev, 0, group))
        h = _rms(x_ref[...], gmix_ref[...]).astype(BF16)
        hs_ref[...] = h
        gate_ref[0] = _sigmoid(_dot(h, wg_ref[:, 0:D_MODEL]) + bg_ref[:, 0:D_MODEL])

    def gates_hi():
        _start_all(row_copies(prev, group, 2 * group))
        gate_ref[1] = _sigmoid(_dot(hs_ref[...], wg_ref[:, D_MODEL:2 * D_MODEL]) + bg_ref[:, D_MODEL:2 * D_MODEL])

    def branches():
        _start_all(row_copies(prev, 2 * group, 3 * group))
        merged = gate_ref[0] * _dot(ya_ref[...], wa_ref[...]) + gate_ref[1] * _dot(yb_ref[...], wb_ref[...])
        hs_ref[...] = merged.astype(BF16)

    region(gates_lo)
    region(gates_hi)
    region(branches)
    _start_all(row_copies(prev, 3 * group, n))
    x1 = x_ref[...] + _dot(hs_ref[...], wo_ref[...])
    x1_ref[...] = x1
    h2 = _rms(x1, gffn_ref[...])
    hbuf_ref[slot] = _pack_rows(h2)
    lt = lax.dot_general(wr_ref[...], h2.astype(BF16), (((1,), (1,)), ((), ())),
                         preferred_element_type=F32) + br_ref[...]
    e1, e2, w1, w2 = _route_t(lt)
    r8 = lax.broadcasted_iota(jnp.int32, (8, tm), 0)
    rt8 = jnp.where(r8 == 0, e1, jnp.where(r8 == 1, e2, jnp.where(r8 == 2, w1, jnp.where(r8 == 3, w2, 0.0))))
    route_ref[...] = jnp.transpose(jnp.concatenate([rt8, jnp.zeros((LANES - 8, tm), F32)], axis=0), (1, 0))

    erow = lax.broadcasted_iota(jnp.int32, (N_EXPERTS, tm), 0).astype(F32)
    is1, is2 = erow == e1, erow == e2
    member = jnp.where(is1, 1.0, jnp.where(is2, 1.0, 0.0))
    before = _dot(member.astype(BF16), tri_ref[...])
    cnt = jnp.sum(member, axis=1, keepdims=True)
    st = st_ref[...]
    fill, blk, free = st[:, 0:1], st[:, 1:2], st[:, 2:3]
    need = fill + cnt
    new = jnp.floor((need + float(MOE_BLK - 1)) * (1.0 / MOE_BLK)) - 1.0
    new_pad = jnp.concatenate([jnp.broadcast_to(new, (N_EXPERTS, LANES)),
                               jnp.zeros((LANES - N_EXPERTS, LANES), F32)], axis=0).astype(BF16)
    base = free + _dot(pre_ref[...], new_pad)[0:N_EXPERTS, 0:1]
    q = fill + before
    jb = jnp.floor(q * (1.0 / MOE_BLK))
    rowid = jnp.where(jb == 0.0, blk, base + jb - 1.0) * float(MOE_BLK) + (q - jb * float(MOE_BLK))
    d1 = jnp.sum(jnp.where(is1, rowid, 0.0), axis=0, keepdims=True)
    d2 = jnp.sum(jnp.where(is2, rowid, 0.0), axis=0, keepdims=True)
    d8 = jnp.where(r8 == 0, d1, jnp.where(r8 == 1, d2, 0.0)).astype(jnp.int32)
    destv_ref[slot] = d8
    dest_ref[0] = d8
    index_copy(slot).start()

    lane = lax.broadcasted_iota(jnp.int32, (N_EXPERTS, LANES), 1)
    alloc_ref[0] = jnp.where(lane == 0, new, jnp.where(lane == 1, base, 0.0))
    state = jnp.where(lane == 0, need - new * float(MOE_BLK),
                      jnp.where(lane == 1, jnp.where(new > 0.0, base + new - 1.0, blk),
                                jnp.where(lane == 2, free + jnp.sum(new, axis=0, keepdims=True), 0.0)))
    st_ref[...] = state
    state_ref[...] = state

    @pl.when(i == last)
    def _():
        index_copy(slot).wait()
        _start_all(row_copies(slot))
        for s in range(NBUF):
            wait_rows(s)
        stv_ref[...] = state.astype(jnp.int32)
        cp = pltpu.make_async_copy(stv_ref, sts_ref, sem_pad)
        cp.start()
        cp.wait()
        zblk_ref[...] = jnp.zeros_like(zblk_ref)

        def zero_rows(e, carry):
            first = sts_ref[e, 1] * MOE_BLK
            lo = sts_ref[e, 0]
            for g, nxt in ((1, 8), (8, 64), (64, MOE_BLK)):
                hi = jnp.minimum((lo + nxt - 1) // nxt * nxt, MOE_BLK)

                def run(j, g=g):
                    dst = xd_hbm.at[pl.ds(pl.multiple_of(first + j * g, g), g)]
                    return pltpu.make_async_copy(zblk_ref.at[pl.ds(0, g)], dst, sem_pad)

                def start(j, c, run=run):
                    run(j).start()
                    return c

                def wait(j, c, run=run):
                    run(j).wait()
                    return c

                lax.fori_loop(lo // g, hi // g, start, 0)
                lax.fori_loop(lo // g, hi // g, wait, 0)
                lo = hi
            return carry

        lax.fori_loop(0, N_EXPERTS, zero_rows, 0)

        def block_copy(b):
            return pltpu.make_async_copy(zblk_ref, xd_hbm.at[pl.ds(pl.multiple_of(b * MOE_BLK, MOE_BLK), MOE_BLK)],
                                         sem_pad)

        def zero_block(b, c):
            block_copy(b).start()
            block_copy(b).wait()
            return c

        lax.fori_loop(sts_ref[0, 2], n_blk, zero_block, 0)


def _merge(x2d, ya, yb, g_mix, wg, bg, wa, wb, wo, g_ffn, wr, br, n_blk, tm=512):
    t = x2d.shape[0]
    nt = t // tm
    n_rows = n_blk * MOE_BLK + TOP_K * tm
    row = lambda w: pl.BlockSpec((tm, w), lambda i: (i, 0))
    full = lambda a: pl.BlockSpec(a.shape, lambda i: (0,) * a.ndim)
    once = lambda a: pl.BlockSpec(a.shape, lambda i: (0,) * a.ndim, pipeline_mode=pl.Buffered(1))
    return pl.pallas_call(
        functools.partial(_merge_kernel, tm=tm, n_blk=n_blk),
        grid=(nt,),
        in_specs=[row(D_MODEL), row(D_SSM), row(ATT_W), full(g_mix), once(wg), full(bg),
                  once(wa), once(wb), once(wo), full(g_ffn), full(wr), full(br)],
        out_specs=[row(D_MODEL), row(LANES), pl.BlockSpec((1, 8, tm), lambda i: (i, 0, 0)),
                   pl.BlockSpec((1, N_EXPERTS, LANES), lambda i: (i, 0, 0)),
                   pl.BlockSpec((N_EXPERTS, LANES), lambda i: (0, 0)),
                   pl.BlockSpec(memory_space=pl.ANY)],
        out_shape=[jax.ShapeDtypeStruct((t, D_MODEL), F32),
                   jax.ShapeDtypeStruct((t, LANES), F32),
                   jax.ShapeDtypeStruct((nt, 8, tm), jnp.int32),
                   jax.ShapeDtypeStruct((nt, N_EXPERTS, LANES), F32),
                   jax.ShapeDtypeStruct((N_EXPERTS, LANES), F32),
                   jax.ShapeDtypeStruct((n_rows, PACK_W), jnp.uint32)],
        scratch_shapes=[pltpu.VMEM((tm, tm), BF16), pltpu.VMEM((LANES, LANES), BF16), pltpu.VMEM((N_EXPERTS, LANES), F32),
                        pltpu.VMEM((NBUF, tm, PACK_W), jnp.uint32), pltpu.VMEM((NBUF, 8, tm), jnp.int32),
                        pltpu.SMEM((NBUF, 8, tm), jnp.int32), pltpu.VMEM((N_EXPERTS, LANES), jnp.int32),
                        pltpu.SMEM((N_EXPERTS, LANES), jnp.int32), pltpu.VMEM((MOE_BLK, PACK_W), jnp.uint32),
                        pltpu.VMEM((tm, D_MODEL), BF16), pltpu.VMEM((2, tm, D_MODEL), F32),
                        pltpu.SemaphoreType.DMA((NBUF,)), pltpu.SemaphoreType.DMA((NBUF,)),
                        pltpu.SemaphoreType.DMA],
        compiler_params=_cparams(("arbitrary",)),
        name="merge",
    )(x2d, ya, yb, g_mix, wg, bg, wa, wb, wo, g_ffn, wr, br)


def _moe_kernel(be_ref, bn_ref, bi_ref, xd_ref, wg_ref, wu_ref, wd_ref, yd_ref, wgb_ref, wub_ref, wdb_ref):
    del bi_ref
    i = pl.program_id(0)

    @pl.when(bn_ref[i] == 0)
    def _():
        yd_ref[...] = jnp.zeros_like(yd_ref)

    @pl.when(bn_ref[i] > 0)
    def _():
        prev = be_ref[jnp.maximum(i - 1, 0)]

        @pl.when(jnp.logical_or(i == 0, be_ref[i] != prev))
        def _():
            wgb_ref[...] = wg_ref[0].astype(BF16)
            wub_ref[...] = wu_ref[0].astype(BF16)
            wdb_ref[...] = wd_ref[0].astype(BF16)

        x = _unpack_rows(xd_ref[...]).astype(BF16)
        g = _dot(x, wgb_ref[...])
        u = _dot(x, wub_ref[...])
        a = (g * _sigmoid(g) * u).astype(BF16)
        yd_ref[...] = _pack_rows(_dot(a, wdb_ref[...]))


def _moe(blk_e, blk_n, blk_i, xdw, w_eg, w_eu, w_ed):
    n_blk = blk_e.shape[0]
    grid_spec = pltpu.PrefetchScalarGridSpec(
        num_scalar_prefetch=3,
        grid=(n_blk,),
        in_specs=[pl.BlockSpec((MOE_BLK, PACK_W), lambda i, be, bn, bi: (bi[i], 0)),
                  pl.BlockSpec((1, D_MODEL, D_FF_EXPERT), lambda i, be, bn, bi: (be[i], 0, 0)),
                  pl.BlockSpec((1, D_MODEL, D_FF_EXPERT), lambda i, be, bn, bi: (be[i], 0, 0)),
                  pl.BlockSpec((1, D_FF_EXPERT, D_MODEL), lambda i, be, bn, bi: (be[i], 0, 0))],
        out_specs=pl.BlockSpec((MOE_BLK, PACK_W), lambda i, be, bn, bi: (bi[i], 0)),
        scratch_shapes=[pltpu.VMEM((D_MODEL, D_FF_EXPERT), BF16), pltpu.VMEM((D_MODEL, D_FF_EXPERT), BF16),
                        pltpu.VMEM((D_FF_EXPERT, D_MODEL), BF16)],
    )
    return pl.pallas_call(
        _moe_kernel,
        grid_spec=grid_spec,
        out_shape=jax.ShapeDtypeStruct((n_blk * MOE_BLK, PACK_W), jnp.uint32),
        compiler_params=_cparams(("arbitrary",)),
        name="moe",
    )(blk_e, blk_n, blk_i, xdw, w_eg, w_eu, w_ed)


def _final_kernel(dcur_ref, dnxt_ref, x1_ref, p_ref, route_ref, yd_hbm, wpp_ref, wpg_ref, bpg_ref, gple_ref,
                  gfin_ref, o_ref, yw0_ref, yw1_ref, sem, *, tm):
    j = pl.program_id(0)
    last = pl.num_programs(0) - 1
    n = TOP_K * tm
    yw = (yw0_ref, yw1_ref)

    def gather(dref, half, slot):
        return [_row_copy(yd_hbm, yw[slot], sem.at[slot], dref[0, r // tm, half * tm + r % tm], r)
                for r in range(n)]

    def tile(half, slot):
        rows = slice(half * tm, (half + 1) * tm)
        rt = route_ref[rows, :]
        y0 = _unpack_rows(yw[slot][0:tm, :])
        y1 = _unpack_rows(yw[slot][tm:n, :])
        x2 = x1_ref[rows, :] + rt[:, 2:3] * y0 + rt[:, 3:4] * y1
        h3 = _rms(x2, gple_ref[...]).astype(BF16)
        pp = _dot(p_ref[rows, :].astype(BF16), wpp_ref[...])
        x3 = x2 + pp * _sigmoid(_dot(h3, wpg_ref[...]) + bpg_ref[...])
        o_ref[rows, :] = _rms(x3, gfin_ref[...])

    @pl.when(j == 0)
    def _():
        _start_all(gather(dcur_ref, 0, 0))

    for c in gather(dcur_ref, 0, 0):
        c.wait()
    _start_all(gather(dcur_ref, 1, 1))
    tile(0, 0)
    for c in gather(dcur_ref, 1, 1):
        c.wait()
    _start_all(gather(dnxt_ref, 0, 0))
    tile(1, 1)

    @pl.when(j == last)
    def _():
        for c in gather(dnxt_ref, 0, 0):
            c.wait()


def _final(dest, x1, p2d, route, ydw, wpp, wpg, bpg, g_ple, g_final):
    t = x1.shape[0]
    ns, _, two_tm = dest.shape
    tm = two_tm // 2
    n = TOP_K * tm
    row = lambda w: pl.BlockSpec((2 * tm, w), lambda i: (i, 0))
    full = lambda a: pl.BlockSpec(a.shape, lambda i: (0,) * a.ndim)
    return pl.pallas_call(
        functools.partial(_final_kernel, tm=tm),
        grid=(ns,),
        in_specs=[pl.BlockSpec((1, 8, two_tm), lambda i: (i, 0, 0), memory_space=pltpu.SMEM),
                  pl.BlockSpec((1, 8, two_tm), lambda i: (jnp.minimum(i + 1, ns - 1), 0, 0),
                               memory_space=pltpu.SMEM),
                  row(D_MODEL), row(PLE_DIM), row(LANES), pl.BlockSpec(memory_space=pl.ANY),
                  full(wpp), full(wpg), full(bpg), full(g_ple), full(g_final)],
        out_specs=row(D_MODEL),
        out_shape=jax.ShapeDtypeStruct((t, D_MODEL), F32),
        scratch_shapes=[pltpu.VMEM((n, PACK_W), jnp.uint32), pltpu.VMEM((n, PACK_W), jnp.uint32),
                        pltpu.SemaphoreType.DMA((2,))],
        compiler_params=_cparams(("arbitrary",)),
        name="final",
    )(dest, dest, x1, p2d, route, ydw, wpp, wpg, bpg, g_ple, g_final)


def _block_plan(alloc, state, n_blk):
    new = alloc[:, :, 0].astype(jnp.int32).reshape(-1)
    base = alloc[:, :, 1].astype(jnp.int32).reshape(-1)
    expert = jnp.tile(jnp.arange(N_EXPERTS, dtype=jnp.int32), alloc.shape[0])
    fill = state[:, 0].astype(jnp.int32)
    last_blk = state[:, 1].astype(jnp.int32)
    taken = state[0, 2].astype(jnp.int32)
    b = jnp.arange(n_blk, dtype=jnp.int32)
    opened = (base[None, :] <= b[:, None]) & (b[:, None] < (base + new)[None, :])
    e_of = jnp.sum(jnp.where(opened, expert[None, :], 0), axis=1)
    onehot = e_of[:, None] == jnp.arange(N_EXPERTS, dtype=jnp.int32)[None, :]
    pick = lambda v: jnp.sum(jnp.where(onehot, v[None, :], 0), axis=1)
    rows = jnp.where(b < taken, jnp.where(b == pick(last_blk), pick(fill), MOE_BLK), 0)
    key = jnp.where(b < taken, e_of, N_EXPERTS) * n_blk + b
    pos = jnp.sum(key[None, :] < key[:, None], axis=1)
    at = pos[None, :] == b[:, None]
    order = lambda v: jnp.sum(jnp.where(at, v[None, :], 0), axis=1).astype(jnp.int32)
    return order(jnp.where(b < taken, e_of, N_EXPERTS - 1)), order(rows), order(b)


def _row(v):
    return v.reshape(1, -1).astype(F32)


def kernel(x, p, rel_bias, g_mix, w_in, w_gate, b_gate, ssm_a_re, ssm_a_im, ssm_log_dt, ssm_b_re, ssm_b_im,
           ssm_c_re, ssm_c_im, ssm_d, w_glu, b_glu, sinks, w_br_ssm, w_br_attn, w_out, g_ffn, w_router_group,
           b_router_group, w_router_expert, b_router_expert, w_e_gate, w_e_up, w_e_down, g_ple, w_ple_gate,
           b_ple_gate, w_ple_proj, g_final):
    bsz, seq, dm = x.shape
    assert g_mix.shape[0] == 1, "one layer followed by the final norm"
    i = 0
    t = bsz * seq
    x2d = x.reshape(t, dm)
    bias = _bias_table(rel_bias)
    u, q, kv = _proj(x2d, _row(g_mix[i]), w_in[i].astype(BF16))
    ab_re, ab_im, bb_re, bb_im = _ssmprep(ssm_a_re[i], ssm_a_im[i], ssm_log_dt[i], ssm_b_re[i], ssm_b_im[i])
    ws_in, ws_out, ar, ai = _ssm_weights(ab_re, ab_im, bb_re, bb_im, ssm_c_re[i], ssm_c_im[i])
    y_ssm = _ssm(u.reshape(bsz, seq, D_SSM), ws_in, ws_out, ar, ai, _row(ssm_d[i]),
                 w_glu[i].astype(BF16), _row(b_glu[i]))
    y_att = _attn(q.reshape(bsz, seq, ATT_W), kv.reshape(bsz, seq, KV_W), bias, sinks[i].astype(F32))
    w_router = jnp.zeros((LANES, dm), F32)
    w_router = w_router.at[:N_GROUPS].set(w_router_group[i].T)
    w_router = w_router.at[EXPERT_ROW0:EXPERT_ROW0 + N_EXPERTS].set(w_router_expert[i].T)
    b_router = jnp.zeros((LANES, 1), F32)
    b_router = b_router.at[:N_GROUPS, 0].set(b_router_group[i])
    b_router = b_router.at[EXPERT_ROW0:EXPERT_ROW0 + N_EXPERTS, 0].set(b_router_expert[i])
    n_blk = t * TOP_K // MOE_BLK + N_EXPERTS
    x1, route, dest, alloc, state, xdw = _merge(
        x2d, y_ssm.reshape(t, D_SSM), y_att.reshape(t, ATT_W), _row(g_mix[i]), w_gate[i].astype(BF16),
        _row(b_gate[i]), w_br_ssm[i].astype(BF16), w_br_attn[i].astype(BF16), w_out[i].astype(BF16),
        _row(g_ffn[i]), w_router.astype(BF16), b_router, n_blk)
    blk_e, blk_n, blk_i = _block_plan(alloc, state, n_blk)
    ydw = _moe(blk_e, blk_n, blk_i, xdw, w_e_gate[i], w_e_up[i], w_e_down[i])
    out = _final(dest, x1, p[i].reshape(t, PLE_DIM), route, ydw, w_ple_proj[i].astype(BF16),
                 w_ple_gate[i].astype(BF16), _row(b_ple_gate[i]), _row(g_ple[i]), _row(g_final))
    return out.reshape(bsz, seq, dm)
```

```python
import functools
import math

import numpy as np
import jax
import jax.numpy as jnp
from jax import lax
from jax.experimental import pallas as pl
from jax.experimental.pallas import tpu as pltpu

D_MODEL = 1024
D_SSM = 512
SSM_CH = 16
SSM_GROUPS = D_SSM // SSM_CH
SSM_STATE = 64
N_STATE = SSM_GROUPS * SSM_STATE
N_HEADS = 8
N_KV = 2
HEAD_DIM = 64
GROUP = N_HEADS // N_KV
ATT_W = N_HEADS * HEAD_DIM
KV_W = 2 * N_KV * HEAD_DIM
WINDOW = 128
BLOCK = 128
NUM_BUCKETS = 32
MAX_DIST = 128
D_IN = D_SSM + ATT_W + KV_W
N_GROUPS = 4
EXPERTS_PER_GROUP = 8
N_EXPERTS = N_GROUPS * EXPERTS_PER_GROUP
TOP_K = 2
D_FF_EXPERT = 512
MOE_BLK = 512
PLE_DIM = 256
EPS = 1e-6

LANES = 128
NEG = -1e30
PACK_W = D_MODEL // 2

BF16 = jnp.bfloat16
F32 = jnp.float32
VMEM_LIMIT = 56 * 1024 * 1024
NBUF = 3


def _cparams(sem):
    return pltpu.CompilerParams(dimension_semantics=sem, vmem_limit_bytes=VMEM_LIMIT)


def _rms(x, g):
    ms = jnp.mean(x * x, axis=-1, keepdims=True)
    return x * lax.rsqrt(ms + EPS) * g


def _sigmoid(x):
    return 1.0 / (1.0 + jnp.exp(-x))


def _dot(a, b):
    return jnp.dot(a, b, preferred_element_type=F32)


def _pack_rows(y):
    lo = pltpu.bitcast(y[:, :PACK_W].astype(BF16).astype(F32), jnp.uint32)
    hi = pltpu.bitcast(y[:, PACK_W:].astype(BF16).astype(F32), jnp.uint32)
    return hi | lax.shift_right_logical(lo, jnp.uint32(16))


def _unpack_rows(w):
    lo = pltpu.bitcast(lax.shift_left(w, jnp.uint32(16)), F32)
    hi = pltpu.bitcast(w & jnp.uint32(0xFFFF0000), F32)
    return jnp.concatenate([lo, hi], axis=1)


def _proj_kernel(x_ref, g_ref, w_ref, u_ref, q_ref, kv_ref):
    h = _rms(x_ref[...], g_ref[...]).astype(BF16)
    proj = _dot(h, w_ref[...])
    u_ref[...] = proj[:, :D_SSM].astype(BF16)
    q_ref[...] = (proj[:, D_SSM:D_SSM + ATT_W] * (HEAD_DIM ** -0.5)).astype(BF16)
    kv_ref[...] = proj[:, D_SSM + ATT_W:].astype(BF16)


def _proj(x2d, g_mix, w_in_b, tm=2048):
    t = x2d.shape[0]
    row = lambda w: pl.BlockSpec((tm, w), lambda i: (i, 0))
    full = lambda a: pl.BlockSpec(a.shape, lambda i: (0,) * a.ndim)
    return pl.pallas_call(
        _proj_kernel,
        grid=(t // tm,),
        in_specs=[row(D_MODEL), full(g_mix), full(w_in_b)],
        out_specs=[row(D_SSM), row(ATT_W), row(KV_W)],
        out_shape=[jax.ShapeDtypeStruct((t, D_SSM), BF16),
                   jax.ShapeDtypeStruct((t, ATT_W), BF16),
                   jax.ShapeDtypeStruct((t, KV_W), BF16)],
        compiler_params=_cparams(("parallel",)),
        name="proj",
    )(x2d, g_mix, w_in_b)


def _ssmprep_kernel(are_ref, aim_ref, ldt_ref, bre_ref, bim_ref, abr_ref, abi_ref, bbr_ref, bbi_ref):
    a_re, a_im = are_ref[...], aim_ref[...]
    dt = jnp.exp(ldt_ref[...])
    mag = jnp.exp(a_re * dt)
    ab_re = mag * jnp.cos(a_im * dt)
    ab_im = mag * jnp.sin(a_im * dt)
    abr_ref[...] = ab_re
    abi_ref[...] = ab_im
    den = a_re * a_re + a_im * a_im
    c_re = ((ab_re - 1.0) * a_re + ab_im * a_im) / den
    c_im = (ab_im * a_re - (ab_re - 1.0) * a_im) / den
    for c in range(SSM_CH):
        b_re, b_im = bre_ref[c], bim_ref[c]
        bbr_ref[c] = c_re * b_re - c_im * b_im
        bbi_ref[c] = c_re * b_im + c_im * b_re


def _ssmprep(a_re, a_im, log_dt, b_re, b_im):
    g, p = a_re.shape
    bt_re = jnp.transpose(b_re, (2, 0, 1))
    bt_im = jnp.transpose(b_im, (2, 0, 1))
    gp = jax.ShapeDtypeStruct((g, p), F32)
    cgp = jax.ShapeDtypeStruct((SSM_CH, g, p), F32)
    return pl.pallas_call(
        _ssmprep_kernel, out_shape=[gp, gp, cgp, cgp], name="ssmprep",
    )(a_re, a_im, log_dt.reshape(g, 1), bt_re, bt_im)


def _ssm_weights(ab_re, ab_im, bb_re, bb_im, c_re, c_im):
    g, p = ab_re.shape
    kb = D_SSM // 256
    gk = g // kb
    eye = jnp.eye(gk, dtype=F32)

    def in_block(bb):
        b4 = jnp.transpose(bb, (1, 0, 2)).reshape(kb, gk, SSM_CH, p)
        return jnp.einsum("kgcp,gh->kgchp", b4, eye).reshape(kb, gk * SSM_CH, gk * p)

    w_in = jnp.concatenate([in_block(bb_re), in_block(bb_im)], axis=2).astype(BF16)
    nj = D_SSM // LANES
    gj = g // nj
    eyej = jnp.eye(gj, dtype=F32)

    def out_block(c):
        c4 = jnp.transpose(c, (0, 2, 1)).reshape(nj, gj, p, SSM_CH)
        return jnp.einsum("jgpc,gh->jgphc", c4, eyej).reshape(nj, gj * p, gj * SSM_CH)

    w_out = jnp.stack([out_block(c_re), out_block(-c_im)], axis=1).astype(BF16)
    return w_in, w_out, ab_re.reshape(1, g * p), ab_im.reshape(1, g * p)


def _gelu_tanh(x):
    return 0.5 * x * (1.0 + jnp.tanh(math.sqrt(2.0 / math.pi) * (x + 0.044715 * (x * x * x))))


def _ssm_kernel(u_ref, ul_ref, win_ref, wout_ref, ar_ref, ai_ref, d_ref, wglu_ref, bglu_ref, y_ref,
                bu0_ref, bu1_ref, xb0_ref, xb1_ref, st_ref, *, nb, lc):
    rows = nb * lc
    ns = N_STATE
    k = pl.program_id(0)
    bu = (bu0_ref, bu1_ref)
    xb = (xb0_ref, xb1_ref)

    @pl.when(k == 0)
    def _():
        bu1_ref[...] = jnp.zeros_like(bu1_ref)
        xb0_ref[...] = jnp.zeros_like(xb0_ref)
        xb1_ref[...] = jnp.zeros_like(xb1_ref)
        st_ref[...] = jnp.zeros_like(st_ref)

    kb = win_ref.shape[0]
    half = ns // kb
    nj = wout_ref.shape[0]
    kw = ns // nj
    ar, ai = ar_ref[...], ai_ref[...]
    xr, xi = st_ref[:, 0:ns], st_ref[:, ns:2 * ns]

    for h in range(2):
        part = slice(h * lc, (h + 1) * lc)
        u_new = pltpu.einshape("btd->(tb)d", u_ref[:, part, :])
        for j in range(kb):
            res = _dot(u_new[:, 256 * j:256 * (j + 1)], win_ref[j])
            bu[h][:, half * j:half * (j + 1)] = res[:, :half]
            bu[h][:, ns + half * j:ns + half * (j + 1)] = res[:, half:]
        src, dst = bu[1 - h], xb[1 - h]
        for t in range(0, lc, 2):
            keep_r, keep_i = [], []
            for s in range(2):
                row = (t + s) * nb
                br = src[row:row + nb, 0:ns]
                bi = src[row:row + nb, ns:2 * ns]
                xr, xi = ar * xr - ai * xi + br, ar * xi + ai * xr + bi
                keep_r.append(xr)
                keep_i.append(xi)
            dst[t * nb:(t + 2) * nb, 0:ns] = jnp.concatenate(keep_r, axis=0).astype(BF16)
            dst[t * nb:(t + 2) * nb, ns:2 * ns] = jnp.concatenate(keep_i, axis=0).astype(BF16)
        ys = []
        for j in range(nj):
            ys.append(_dot(xb[h][:, kw * j:kw * (j + 1)], wout_ref[j, 0])
                      + _dot(xb[h][:, ns + kw * j:ns + kw * (j + 1)], wout_ref[j, 1]))
        y_bt = pltpu.einshape("(tb)d->btd", jnp.concatenate(ys, axis=1), b=nb).reshape(rows, D_SSM)
        y = y_bt + d_ref[...] * ul_ref[:, part, :].reshape(rows, D_SSM).astype(F32)
        y = _gelu_tanh(y)
        z = _dot(y.astype(BF16), wglu_ref[...]) + bglu_ref[...]
        y_ref[:, part, :] = (y * _sigmoid(z)).astype(BF16).reshape(nb, lc, D_SSM)

    st_ref[:, 0:ns] = xr
    st_ref[:, ns:2 * ns] = xi


def _ssm(u3, w_in, w_out, ar, ai, d_skip, w_glu_b, b_glu, lc=64):
    nb, s, _ = u3.shape
    rows = nb * lc
    nk = s // (2 * lc)
    assert nb & (nb - 1) == 0 and nb % 8 == 0 and s % (2 * lc) == 0 and lc % 16 == 0
    full = lambda a: pl.BlockSpec(a.shape, lambda i: (0,) * a.ndim)
    lead = pl.BlockSpec((nb, 2 * lc, D_SSM), lambda i: (0, jnp.minimum(i, nk - 1), 0))
    lag = pl.BlockSpec((nb, 2 * lc, D_SSM), lambda i: (0, jnp.maximum(i - 1, 0), 0))
    return pl.pallas_call(
        functools.partial(_ssm_kernel, nb=nb, lc=lc),
        grid=(nk + 1,),
        in_specs=[lead, lag, full(w_in), full(w_out), full(ar), full(ai), full(d_skip), full(w_glu_b), full(b_glu)],
        out_specs=lag,
        out_shape=jax.ShapeDtypeStruct((nb, s, D_SSM), BF16),
        scratch_shapes=[pltpu.VMEM((rows, 2 * N_STATE), F32), pltpu.VMEM((rows, 2 * N_STATE), F32),
                        pltpu.VMEM((rows, 2 * N_STATE), BF16), pltpu.VMEM((rows, 2 * N_STATE), BF16),
                        pltpu.VMEM((nb, 2 * N_STATE), F32)],
        compiler_params=_cparams(("arbitrary",)),
        name="ssm",
    )(u3, u3, w_in, w_out, ar, ai, d_skip, w_glu_b, b_glu)


def _t5_bucket_np(rel):
    max_exact = NUM_BUCKETS // 2
    relf = np.maximum(rel, 1).astype(np.float32)
    large = max_exact + (np.log(relf / np.float32(max_exact)) / np.float32(math.log(MAX_DIST / max_exact))
                         * np.float32(NUM_BUCKETS - max_exact)).astype(np.int32)
    large = np.minimum(large, NUM_BUCKETS - 1)
    return np.where(rel < max_exact, rel, large)


def _bias_table(rel_bias):
    q_loc = np.arange(BLOCK)[:, None]
    c_loc = np.arange(2 * BLOCK)[None, :]
    rel = q_loc + BLOCK - c_loc
    valid = (rel >= 0) & (rel < WINDOW)
    bucket = _t5_bucket_np(np.maximum(rel, 0)).reshape(-1, 1)
    onehot = (jnp.asarray(bucket) == jnp.arange(NUM_BUCKETS)[None, :]).astype(F32)
    bias = jnp.dot(onehot, rel_bias.astype(F32), precision=lax.Precision.HIGHEST)
    bias = jnp.where(valid.reshape(-1, 1), bias, NEG)
    return jnp.transpose(bias, (1, 0)).reshape(N_HEADS, BLOCK, 2 * BLOCK)


def _attn_kernel(sink_ref, q_ref, kvp_ref, kvc_ref, bias_ref, o_ref, *, nq):
    n = pl.program_id(1)
    kv_all = jnp.concatenate([kvp_ref[0], kvc_ref[0]], axis=0)
    col = lax.broadcasted_iota(jnp.int32, (BLOCK, 2 * BLOCK), 1)
    no_prev = jnp.where(col < BLOCK, jnp.where(n == 0, NEG, 0.0), 0.0)
    for j in range(nq):
        q = q_ref[0, BLOCK * j:BLOCK * (j + 1), :]
        kv = kv_all[BLOCK * j:BLOCK * (j + 2), :]
        outs = []
        for h in range(N_HEADS):
            g = h // GROUP
            qh = q[:, HEAD_DIM * h:HEAD_DIM * (h + 1)]
            kh = kv[:, HEAD_DIM * g:HEAD_DIM * (g + 1)]
            vh = kv[:, N_KV * HEAD_DIM + HEAD_DIM * g:N_KV * HEAD_DIM + HEAD_DIM * (g + 1)]
            s = lax.dot_general(qh, kh, (((1,), (1,)), ((), ())), preferred_element_type=F32)
            s = s + bias_ref[h]
            if j == 0:
                s = s + no_prev
            sink = sink_ref[h]
            m = jnp.maximum(jnp.max(s, axis=-1, keepdims=True), sink)
            e = jnp.exp(s - m)
            den = jnp.sum(e, axis=-1, keepdims=True) + jnp.exp(sink - m)
            outs.append(_dot(e.astype(BF16), vh) / den)
        o_ref[0, BLOCK * j:BLOCK * (j + 1), :] = jnp.concatenate(outs, axis=1).astype(BF16)


def _attn(q3, kv3, bias, sinks, nq=2):
    b, s, _ = q3.shape
    qb = nq * BLOCK
    grid_spec = pltpu.PrefetchScalarGridSpec(
        num_scalar_prefetch=0,
        grid=(b, s // qb),
        in_specs=[pl.BlockSpec(memory_space=pltpu.SMEM),
                  pl.BlockSpec((1, qb, ATT_W), lambda i, n: (i, n, 0)),
                  pl.BlockSpec((1, BLOCK, KV_W), lambda i, n: (i, jnp.maximum(n * nq - 1, 0), 0)),
                  pl.BlockSpec((1, qb, KV_W), lambda i, n: (i, n, 0)),
                  pl.BlockSpec(bias.shape, lambda i, n: (0, 0, 0))],
        out_specs=pl.BlockSpec((1, qb, ATT_W), lambda i, n: (i, n, 0)),
    )
    return pl.pallas_call(
        functools.partial(_attn_kernel, nq=nq),
        grid_spec=grid_spec,
        out_shape=jax.ShapeDtypeStruct((b, s, ATT_W), BF16),
        compiler_params=_cparams(("parallel", "parallel")),
        name="attn",
    )(sinks, q3, kv3, kv3, bias)


EXPERT_ROW0 = 8
assert EXPERTS_PER_GROUP == 8 and N_GROUPS <= EXPERT_ROW0


def _route_t(lt):
    m = lt.shape[1]
    row = lax.broadcasted_iota(jnp.int32, (8, m), 0).astype(F32)
    big = 8.0
    gl = jnp.where(row < N_GROUPS, lt[0:8, :], NEG)
    mg = jnp.max(gl, axis=0, keepdims=True)
    gidx = jnp.min(jnp.where(gl == mg, row, big), axis=0, keepdims=True)
    gp = 1.0 / jnp.sum(jnp.where(row < N_GROUPS, jnp.exp(gl - mg), 0.0), axis=0, keepdims=True)
    el = lt[EXPERT_ROW0:EXPERT_ROW0 + 8, :]
    for g in range(1, N_GROUPS):
        el = jnp.where(gidx == float(g), lt[EXPERT_ROW0 + 8 * g:EXPERT_ROW0 + 8 * (g + 1), :], el)
    m1 = jnp.max(el, axis=0, keepdims=True)
    i1 = jnp.min(jnp.where(el == m1, row, big), axis=0, keepdims=True)
    el2 = jnp.where(row == i1, NEG, el)
    m2 = jnp.max(el2, axis=0, keepdims=True)
    i2 = jnp.min(jnp.where(el2 == m2, row, big), axis=0, keepdims=True)
    t = jnp.exp(m2 - m1)
    first = gidx * float(EXPERTS_PER_GROUP)
    return first + i1, first + i2, gp / (1.0 + t), gp * t / (1.0 + t)


def _row_copy(src_ref, dst_ref, sem, src_row, dst_row):
    return pltpu.make_async_copy(src_ref.at[pl.ds(src_row, 1)], dst_ref.at[pl.ds(dst_row, 1)], sem)


def _start_all(copies):
    for c in copies:
        c.start(priority=1)


def _merge_kernel(x_ref, ya_ref, yb_ref, gmix_ref, wg_ref, bg_ref, wa_ref, wb_ref, wo_ref, gffn_ref, wr_ref, br_ref,
                  x1_ref, route_ref, dest_ref, alloc_ref, state_ref, xd_hbm,
                  tri_ref, pre_ref, st_ref, hbuf_ref, destv_ref, dests_ref, stv_ref, sts_ref, zblk_ref,
                  hs_ref, gate_ref, sem_rows, sem_idx, sem_pad, *, tm, n_blk):
    i = pl.program_id(0)
    last = pl.num_programs(0) - 1
    slot = lax.rem(i, NBUF)
    prev = lax.rem(i + NBUF - 1, NBUF)
    n = TOP_K * tm
    dump0 = n_blk * MOE_BLK

    def row_copies(s, lo=0, hi=n):
        return [_row_copy(hbuf_ref.at[s], xd_hbm, sem_rows.at[s], r % tm, dests_ref[s, r // tm, r % tm])
                for r in range(lo, hi)]

    def wait_rows(s):
        for _ in range(n):
            _row_copy(hbuf_ref.at[s], xd_hbm, sem_rows.at[s], 0, dump0).wait()

    def index_copy(s):
        return pltpu.make_async_copy(destv_ref.at[s], dests_ref.at[s], sem_idx.at[s])

    @pl.when(i == 0)
    def _():
        r = lax.broadcasted_iota(jnp.int32, (tm, tm), 0)
        c = lax.broadcasted_iota(jnp.int32, (tm, tm), 1)
        tri_ref[...] = jnp.where(r < c, 1.0, 0.0).astype(BF16)
        r = lax.broadcasted_iota(jnp.int32, (LANES, LANES), 0)
        c = lax.broadcasted_iota(jnp.int32, (LANES, LANES), 1)
        pre_ref[...] = jnp.where(r > c, 1.0, 0.0).astype(BF16)
        c = lax.broadcasted_iota(jnp.int32, (N_EXPERTS, LANES), 1)
        st_ref[...] = jnp.where(c == 0, float(MOE_BLK), 0.0)
        hbuf_ref[NBUF - 1] = jnp.zeros((tm, PACK_W), jnp.uint32)

        def spare(t, carry):
            dests_ref[NBUF - 1, 0, t] = dump0 + t
            dests_ref[NBUF - 1, 1, t] = dump0 + tm + t
            return carry

        lax.fori_loop(0, tm, spare, 0)

    @pl.when(i >= NBUF - 1)
    def _():
        wait_rows(slot)

    @pl.when(i >= 1)
    def _():
        index_copy(prev).wait()

    group = n // 6
    once = jnp.minimum(dests_ref[prev, 0, 0], 0) + 1

    def region(body):
        lax.fori_loop(0, once, lambda _, c: (body(), c)[1], 0)

    def gates_lo():
        _start_all(row_copies(prev, 0, group))
        h = _rms(x_ref[...], gmix_ref[...]).astype(BF16)
        hs_ref[...] = h
        gate_ref[0] = _sigmoid(_dot(h, wg_ref[:, 0:D_MODEL]) + bg_ref[:, 0:D_MODEL])

    def gates_hi():
        _start_all(row_copies(prev, group, 2 * group))
        gate_ref[1] = _sigmoid(_dot(hs_ref[...], wg_ref[:, D_MODEL:2 * D_MODEL]) + bg_ref[:, D_MODEL:2 * D_MODEL])

    def branches():
        _start_all(row_copies(prev, 2 * group, 3 * group))
        merged = gate_ref[0] * _dot(ya_ref[...], wa_ref[...]) + gate_ref[1] * _dot(yb_ref[...], wb_ref[...])
        hs_ref[...] = merged.astype(BF16)

    region(gates_lo)
    region(gates_hi)
    region(branches)
    _start_all(row_copies(prev, 3 * group, n))
    x1 = x_ref[...] + _dot(hs_ref[...], wo_ref[...])
    x1_ref[...] = x1
    h2 = _rms(x1, gffn_ref[...])
    hbuf_ref[slot] = _pack_rows(h2)
    lt = lax.dot_general(wr_ref[...], h2.astype(BF16), (((1,), (1,)), ((), ())),
                         preferred_element_type=F32) + br_ref[...]
    e1, e2, w1, w2 = _route_t(lt)
    r8 = lax.broadcasted_iota(jnp.int32, (8, tm), 0)
    rt8 = jnp.where(r8 == 0, e1, jnp.where(r8 == 1, e2, jnp.where(r8 == 2, w1, jnp.where(r8 == 3, w2, 0.0))))
    route_ref[...] = jnp.transpose(jnp.concatenate([rt8, jnp.zeros((LANES - 8, tm), F32)], axis=0), (1, 0))

    erow = lax.broadcasted_iota(jnp.int32, (N_EXPERTS, tm), 0).astype(F32)
    is1, is2 = erow == e1, erow == e2
    member = jnp.where(is1, 1.0, jnp.where(is2, 1.0, 0.0))
    before = _dot(member.astype(BF16), tri_ref[...])
    cnt = jnp.sum(member, axis=1, keepdims=True)
    st = st_ref[...]
    fill, blk, free = st[:, 0:1], st[:, 1:2], st[:, 2:3]
    need = fill + cnt
    new = jnp.floor((need + float(MOE_BLK - 1)) * (1.0 / MOE_BLK)) - 1.0
    new_pad = jnp.concatenate([jnp.broadcast_to(new, (N_EXPERTS, LANES)),
                               jnp.zeros((LANES - N_EXPERTS, LANES), F32)], axis=0).astype(BF16)
    base = free + _dot(pre_ref[...], new_pad)[0:N_EXPERTS, 0:1]
    q = fill + before
    jb = jnp.floor(q * (1.0 / MOE_BLK))
    rowid = jnp.where(jb == 0.0, blk, base + jb - 1.0) * float(MOE_BLK) + (q - jb * float(MOE_BLK))
    d1 = jnp.sum(jnp.where(is1, rowid, 0.0), axis=0, keepdims=True)
    d2 = jnp.sum(jnp.where(is2, rowid, 0.0), axis=0, keepdims=True)
    d8 = jnp.where(r8 == 0, d1, jnp.where(r8 == 1, d2, 0.0)).astype(jnp.int32)
    destv_ref[slot] = d8
    dest_ref[0] = d8
    index_copy(slot).start()

    lane = lax.broadcasted_iota(jnp.int32, (N_EXPERTS, LANES), 1)
    alloc_ref[0] = jnp.where(lane == 0, new, jnp.where(lane == 1, base, 0.0))
    state = jnp.where(lane == 0, need - new * float(MOE_BLK),
                      jnp.where(lane == 1, jnp.where(new > 0.0, base + new - 1.0, blk),
                                jnp.where(lane == 2, free + jnp.sum(new, axis=0, keepdims=True), 0.0)))
    st_ref[...] = state
    state_ref[...] = state

    @pl.when(i == last)
    def _():
        index_copy(slot).wait()
        _start_all(row_copies(slot))
        for s in range(NBUF):
            wait_rows(s)
        stv_ref[...] = state.astype(jnp.int32)
        cp = pltpu.make_async_copy(stv_ref, sts_ref, sem_pad)
        cp.start()
        cp.wait()
        zblk_ref[...] = jnp.zeros_like(zblk_ref)

        def zero_rows(e, carry):
            first = sts_ref[e, 1] * MOE_BLK
            lo = sts_ref[e, 0]
            for g, nxt in ((1, 8), (8, 64), (64, MOE_BLK)):
                hi = jnp.minimum((lo + nxt - 1) // nxt * nxt, MOE_BLK)

                def run(j, g=g):
                    dst = xd_hbm.at[pl.ds(pl.multiple_of(first + j * g, g), g)]
                    return pltpu.make_async_copy(zblk_ref.at[pl.ds(0, g)], dst, sem_pad)

                def start(j, c, run=run):
                    run(j).start()
                    return c

                def wait(j, c, run=run):
                    run(j).wait()
                    return c

                lax.fori_loop(lo // g, hi // g, start, 0)
                lax.fori_loop(lo // g, hi // g, wait, 0)
                lo = hi
            return carry

        lax.fori_loop(0, N_EXPERTS, zero_rows, 0)

        def block_copy(b):
            return pltpu.make_async_copy(zblk_ref, xd_hbm.at[pl.ds(pl.multiple_of(b * MOE_BLK, MOE_BLK), MOE_BLK)],
                                         sem_pad)

        def zero_block(b, c):
            block_copy(b).start()
            block_copy(b).wait()
            return c

        lax.fori_loop(sts_ref[0, 2], n_blk, zero_block, 0)


def _merge(x2d, ya, yb, g_mix, wg, bg, wa, wb, wo, g_ffn, wr, br, n_blk, tm=512):
    t = x2d.shape[0]
    nt = t // tm
    n_rows = n_blk * MOE_BLK + TOP_K * tm
    row = lambda w: pl.BlockSpec((tm, w), lambda i: (i, 0))
    full = lambda a: pl.BlockSpec(a.shape, lambda i: (0,) * a.ndim)
    once = lambda a: pl.BlockSpec(a.shape, lambda i: (0,) * a.ndim, pipeline_mode=pl.Buffered(1))
    return pl.pallas_call(
        functools.partial(_merge_kernel, tm=tm, n_blk=n_blk),
        grid=(nt,),
        in_specs=[row(D_MODEL), row(D_SSM), row(ATT_W), full(g_mix), once(wg), full(bg),
                  once(wa), once(wb), once(wo), full(g_ffn), full(wr), full(br)],
        out_specs=[row(D_MODEL), row(LANES), pl.BlockSpec((1, 8, tm), lambda i: (i, 0, 0)),
                   pl.BlockSpec((1, N_EXPERTS, LANES), lambda i: (i, 0, 0)),
                   pl.BlockSpec((N_EXPERTS, LANES), lambda i: (0, 0)),
                   pl.BlockSpec(memory_space=pl.ANY)],
        out_shape=[jax.ShapeDtypeStruct((t, D_MODEL), F32),
                   jax.ShapeDtypeStruct((t, LANES), F32),
                   jax.ShapeDtypeStruct((nt, 8, tm), jnp.int32),
                   jax.ShapeDtypeStruct((nt, N_EXPERTS, LANES), F32),
                   jax.ShapeDtypeStruct((N_EXPERTS, LANES), F32),
                   jax.ShapeDtypeStruct((n_rows, PACK_W), jnp.uint32)],
        scratch_shapes=[pltpu.VMEM((tm, tm), BF16), pltpu.VMEM((LANES, LANES), BF16), pltpu.VMEM((N_EXPERTS, LANES), F32),
                        pltpu.VMEM((NBUF, tm, PACK_W), jnp.uint32), pltpu.VMEM((NBUF, 8, tm), jnp.int32),
                        pltpu.SMEM((NBUF, 8, tm), jnp.int32), pltpu.VMEM((N_EXPERTS, LANES), jnp.int32),
                        pltpu.SMEM((N_EXPERTS, LANES), jnp.int32), pltpu.VMEM((MOE_BLK, PACK_W), jnp.uint32),
                        pltpu.VMEM((tm, D_MODEL), BF16), pltpu.VMEM((2, tm, D_MODEL), F32),
                        pltpu.SemaphoreType.DMA((NBUF,)), pltpu.SemaphoreType.DMA((NBUF,)),
                        pltpu.SemaphoreType.DMA],
        compiler_params=_cparams(("arbitrary",)),
        name="merge",
    )(x2d, ya, yb, g_mix, wg, bg, wa, wb, wo, g_ffn, wr, br)


def _moe_kernel(be_ref, bn_ref, bi_ref, xd_ref, wg_ref, wu_ref, wd_ref, yd_ref, wgb_ref, wub_ref, wdb_ref):
    del bi_ref
    i = pl.program_id(0)

    @pl.when(bn_ref[i] == 0)
    def _():
        yd_ref[...] = jnp.zeros_like(yd_ref)

    @pl.when(bn_ref[i] > 0)
    def _():
        prev = be_ref[jnp.maximum(i - 1, 0)]

        @pl.when(jnp.logical_or(i == 0, be_ref[i] != prev))
        def _():
            wgb_ref[...] = wg_ref[0].astype(BF16)
            wub_ref[...] = wu_ref[0].astype(BF16)
            wdb_ref[...] = wd_ref[0].astype(BF16)

        x = _unpack_rows(xd_ref[...]).astype(BF16)
        g = _dot(x, wgb_ref[...])
        u = _dot(x, wub_ref[...])
        a = (g * _sigmoid(g) * u).astype(BF16)
        yd_ref[...] = _pack_rows(_dot(a, wdb_ref[...]))


def _moe(blk_e, blk_n, blk_i, xdw, w_eg, w_eu, w_ed):
    n_blk = blk_e.shape[0]
    grid_spec = pltpu.PrefetchScalarGridSpec(
        num_scalar_prefetch=3,
        grid=(n_blk,),
        in_specs=[pl.BlockSpec((MOE_BLK, PACK_W), lambda i, be, bn, bi: (bi[i], 0)),
                  pl.BlockSpec((1, D_MODEL, D_FF_EXPERT), lambda i, be, bn, bi: (be[i], 0, 0)),
                  pl.BlockSpec((1, D_MODEL, D_FF_EXPERT), lambda i, be, bn, bi: (be[i], 0, 0)),
                  pl.BlockSpec((1, D_FF_EXPERT, D_MODEL), lambda i, be, bn, bi: (be[i], 0, 0))],
        out_specs=pl.BlockSpec((MOE_BLK, PACK_W), lambda i, be, bn, bi: (bi[i], 0)),
        scratch_shapes=[pltpu.VMEM((D_MODEL, D_FF_EXPERT), BF16), pltpu.VMEM((D_MODEL, D_FF_EXPERT), BF16),
                        pltpu.VMEM((D_FF_EXPERT, D_MODEL), BF16)],
    )
    return pl.pallas_call(
        _moe_kernel,
        grid_spec=grid_spec,
        out_shape=jax.ShapeDtypeStruct((n_blk * MOE_BLK, PACK_W), jnp.uint32),
        compiler_params=_cparams(("arbitrary",)),
        name="moe",
    )(blk_e, blk_n, blk_i, xdw, w_eg, w_eu, w_ed)


def _final_kernel(dcur_ref, dnxt_ref, x1_ref, p_ref, route_ref, yd_hbm, wpp_ref, wpg_ref, bpg_ref, gple_ref,
                  gfin_ref, o_ref, yw0_ref, yw1_ref, sem, *, tm):
    j = pl.program_id(0)
    last = pl.num_programs(0) - 1
    n = TOP_K * tm
    yw = (yw0_ref, yw1_ref)

    def gather(dref, half, slot):
        return [_row_copy(yd_hbm, yw[slot], sem.at[slot], dref[0, r // tm, half * tm + r % tm], r)
                for r in range(n)]

    def tile(half, slot):
        rows = slice(half * tm, (half + 1) * tm)
        rt = route_ref[rows, :]
        y0 = _unpack_rows(yw[slot][0:tm, :])
        y1 = _unpack_rows(yw[slot][tm:n, :])
        x2 = x1_ref[rows, :] + rt[:, 2:3] * y0 + rt[:, 3:4] * y1
        h3 = _rms(x2, gple_ref[...]).astype(BF16)
        pp = _dot(p_ref[rows, :].astype(BF16), wpp_ref[...])
        x3 = x2 + pp * _sigmoid(_dot(h3, wpg_ref[...]) + bpg_ref[...])
        o_ref[rows, :] = _rms(x3, gfin_ref[...])

    @pl.when(j == 0)
    def _():
        _start_all(gather(dcur_ref, 0, 0))

    for c in gather(dcur_ref, 0, 0):
        c.wait()
    _start_all(gather(dcur_ref, 1, 1))
    tile(0, 0)
    for c in gather(dcur_ref, 1, 1):
        c.wait()
    _start_all(gather(dnxt_ref, 0, 0))
    tile(1, 1)

    @pl.when(j == last)
    def _():
        for c in gather(dnxt_ref, 0, 0):
            c.wait()


def _final(dest, x1, p2d, route, ydw, wpp, wpg, bpg, g_ple, g_final):
    t = x1.shape[0]
    ns, _, two_tm = dest.shape
    tm = two_tm // 2
    n = TOP_K * tm
    row = lambda w: pl.BlockSpec((2 * tm, w), lambda i: (i, 0))
    full = lambda a: pl.BlockSpec(a.shape, lambda i: (0,) * a.ndim)
    return pl.pallas_call(
        functools.partial(_final_kernel, tm=tm),
        grid=(ns,),
        in_specs=[pl.BlockSpec((1, 8, two_tm), lambda i: (i, 0, 0), memory_space=pltpu.SMEM),
                  pl.BlockSpec((1, 8, two_tm), lambda i: (jnp.minimum(i + 1, ns - 1), 0, 0),
                               memory_space=pltpu.SMEM),
                  row(D_MODEL), row(PLE_DIM), row(LANES), pl.BlockSpec(memory_space=pl.ANY),
                  full(wpp), full(wpg), full(bpg), full(g_ple), full(g_final)],
        out_specs=row(D_MODEL),
        out_shape=jax.ShapeDtypeStruct((t, D_MODEL), F32),
        scratch_shapes=[pltpu.VMEM((n, PACK_W), jnp.uint32), pltpu.VMEM((n, PACK_W), jnp.uint32),
                        pltpu.SemaphoreType.DMA((2,))],
        compiler_params=_cparams(("arbitrary",)),
        name="final",
    )(dest, dest, x1, p2d, route, ydw, wpp, wpg, bpg, g_ple, g_final)


def _block_plan(alloc, state, n_blk):
    new = alloc[:, :, 0].astype(jnp.int32).reshape(-1)
    base = alloc[:, :, 1].astype(jnp.int32).reshape(-1)
    expert = jnp.tile(jnp.arange(N_EXPERTS, dtype=jnp.int32), alloc.shape[0])
    fill = state[:, 0].astype(jnp.int32)
    last_blk = state[:, 1].astype(jnp.int32)
    taken = state[0, 2].astype(jnp.int32)
    b = jnp.arange(n_blk, dtype=jnp.int32)
    opened = (base[None, :] <= b[:, None]) & (b[:, None] < (base + new)[None, :])
    e_of = jnp.sum(jnp.where(opened, expert[None, :], 0), axis=1)
    onehot = e_of[:, None] == jnp.arange(N_EXPERTS, dtype=jnp.int32)[None, :]
    pick = lambda v: jnp.sum(jnp.where(onehot, v[None, :], 0), axis=1)
    rows = jnp.where(b < taken, jnp.where(b == pick(last_blk), pick(fill), MOE_BLK), 0)
    key = jnp.where(b < taken, e_of, N_EXPERTS) * n_blk + b
    pos = jnp.sum(key[None, :] < key[:, None], axis=1)
    at = pos[None, :] == b[:, None]
    order = lambda v: jnp.sum(jnp.where(at, v[None, :], 0), axis=1).astype(jnp.int32)
    return order(jnp.where(b < taken, e_of, N_EXPERTS - 1)), order(rows), order(b)


def _row(v):
    return v.reshape(1, -1).astype(F32)


def kernel(x, p, rel_bias, g_mix, w_in, w_gate, b_gate, ssm_a_re, ssm_a_im, ssm_log_dt, ssm_b_re, ssm_b_im,
           ssm_c_re, ssm_c_im, ssm_d, w_glu, b_glu, sinks, w_br_ssm, w_br_attn, w_out, g_ffn, w_router_group,
           b_router_group, w_router_expert, b_router_expert, w_e_gate, w_e_up, w_e_down, g_ple, w_ple_gate,
           b_ple_gate, w_ple_proj, g_final):
    bsz, seq, dm = x.shape
    assert g_mix.shape[0] == 1, "one layer followed by the final norm"
    i = 0
    t = bsz * seq
    x2d = x.reshape(t, dm)
    bias = _bias_table(rel_bias)
    u, q, kv = _proj(x2d, _row(g_mix[i]), w_in[i].astype(BF16))
    ab_re, ab_im, bb_re, bb_im = _ssmprep(ssm_a_re[i], ssm_a_im[i], ssm_log_dt[i], ssm_b_re[i], ssm_b_im[i])
    ws_in, ws_out, ar, ai = _ssm_weights(ab_re, ab_im, bb_re, bb_im, ssm_c_re[i], ssm_c_im[i])
    y_ssm = _ssm(u.reshape(bsz, seq, D_SSM), ws_in, ws_out, ar, ai, _row(ssm_d[i]),
                 w_glu[i].astype(BF16), _row(b_glu[i]))
    y_att = _attn(q.reshape(bsz, seq, ATT_W), kv.reshape(bsz, seq, KV_W), bias, sinks[i].astype(F32))
    w_router = jnp.zeros((LANES, dm), F32)
    w_router = w_router.at[:N_GROUPS].set(w_router_group[i].T)
    w_router = w_router.at[EXPERT_ROW0:EXPERT_ROW0 + N_EXPERTS].set(w_router_expert[i].T)
    b_router = jnp.zeros((LANES, 1), F32)
    b_router = b_router.at[:N_GROUPS, 0].set(b_router_group[i])
    b_router = b_router.at[EXPERT_ROW0:EXPERT_ROW0 + N_EXPERTS, 0].set(b_router_expert[i])
    n_blk = t * TOP_K // MOE_BLK + N_EXPERTS
    x1, route, dest, alloc, state, xdw = _merge(
        x2d, y_ssm.reshape(t, D_SSM), y_att.reshape(t, ATT_W), _row(g_mix[i]), w_gate[i].astype(BF16),
        _row(b_gate[i]), w_br_ssm[i].astype(BF16), w_br_attn[i].astype(BF16), w_out[i].astype(BF16),
        _row(g_ffn[i]), w_router.astype(BF16), b_router, n_blk)
    blk_e, blk_n, blk_i = _block_plan(alloc, state, n_blk)
    ydw = _moe(blk_e, blk_n, blk_i, xdw, w_e_gate[i], w_e_up[i], w_e_down[i])
    out = _final(dest, x1, p[i].reshape(t, PLE_DIM), route, ydw, w_ple_proj[i].astype(BF16),
                 w_ple_gate[i].astype(BF16), _row(b_ple_gate[i]), _row(g_ple[i]), _row(g_final))
    return out.reshape(bsz, seq, dm)
```
